```python
import jax, jax.numpy as jnp
from jax import lax
import numpy as np

D_MODEL = 1024
BATCH = 2
SEQ = 16384
DEPTH = 2

PLE_DIM = 256
N_MIXERS = 2
D_FF = 2816
FFN_HALF = 0.5
MLA_HEADS = 16
MLA_Q_RANK = 512
MLA_KV_RANK = 256
MLA_NOPE = 64
MLA_ROPE = 32
MLA_V = 64
ROPE_THETA = 10000.0
FOX_HEADS = 16
FOX_HEAD_DIM = 64
Q_BLOCK = 128
EPS = 1e-6
N_MLA_LAYERS = (DEPTH + N_MIXERS - 1) // N_MIXERS
N_FOX_LAYERS = DEPTH // N_MIXERS

kernel_name = "hybrid_mla_fox_macaron_ple"


def rms_norm(x, g):
    xf = x.astype(jnp.float32)
    y = xf * lax.rsqrt(jnp.mean(xf * xf, axis=-1, keepdims=True) + EPS)
    return (y * g.astype(jnp.float32)).astype(x.dtype)


def swiglu(x, w_in, w_out):
    gate, up = jnp.split(x @ w_in, 2, axis=-1)
    return (jax.nn.silu(gate) * up) @ w_out


def rope(x, pos):
    r = x.shape[-1]
    inv_freq = ROPE_THETA ** (-jnp.arange(0, r, 2, dtype=jnp.float32) / r)
    ang = pos.astype(jnp.float32)[:, :, None, None] * inv_freq
    cos = jnp.cos(ang).astype(x.dtype)
    sin = jnp.sin(ang).astype(x.dtype)
    x1, x2 = jnp.split(x, 2, axis=-1)
    return jnp.concatenate([x1 * cos - x2 * sin, x1 * sin + x2 * cos], axis=-1)


def block_causal_attention(q, k, v, decay=None):
    b, s, h, dk = q.shape
    dv = v.shape[-1]
    nb = s // Q_BLOCK
    scale = dk ** -0.5
    qb = q.reshape(b, nb, Q_BLOCK, h, dk).transpose(1, 0, 2, 3, 4)
    blk = jnp.arange(nb)
    k_pos = jnp.arange(s)
    c_k = None if decay is None else decay.transpose(0, 2, 1)

    def attend(qi, i, c_qi):
        logits = jnp.einsum('bqhd,bkhd->bhqk', qi, k).astype(jnp.float32) * scale
        if c_qi is not None:
            logits = logits + c_qi[:, :, :, None] - c_k[:, :, None, :]
        q_pos = i * Q_BLOCK + jnp.arange(Q_BLOCK)
        mask = k_pos[None, :] <= q_pos[:, None]
        logits = jnp.where(mask[None, None], logits, -jnp.inf)
        probs = jax.nn.softmax(logits, axis=-1)
        return jnp.einsum('bhqk,bkhd->bqhd', probs.astype(v.dtype), v)

    if decay is None:
        out = lax.map(lambda a: attend(a[0], a[1], None), (qb, blk))
    else:
        c_qb = decay.reshape(b, nb, Q_BLOCK, h).transpose(1, 0, 3, 2)
        out = lax.map(lambda a: attend(a[0], a[1], a[2]), (qb, blk, c_qb))
    return out.transpose(1, 0, 2, 3, 4).reshape(b, s, h, dv)


def mla_mixer(hn, pos, w_in, g_q_lat, w_uq, g_kv_lat, w_ukv, g_qn, g_kn, w_o):
    b, s, _ = hn.shape
    z = hn @ w_in
    c_q = z[..., :MLA_Q_RANK]
    c_kv = z[..., MLA_Q_RANK:MLA_Q_RANK + MLA_KV_RANK]
    k_pe = z[..., MLA_Q_RANK + MLA_KV_RANK:]
    q = (rms_norm(c_q, g_q_lat) @ w_uq).reshape(b, s, MLA_HEADS, MLA_NOPE + MLA_ROPE)
    kv = (rms_norm(c_kv, g_kv_lat) @ w_ukv).reshape(b, s, MLA_HEADS, MLA_NOPE + MLA_V)
    k_nope, v = kv[..., :MLA_NOPE], kv[..., MLA_NOPE:]
    k_pe_h = jnp.broadcast_to(k_pe[:, :, None, :], (b, s, MLA_HEADS, MLA_ROPE))
    k = jnp.concatenate([k_nope, k_pe_h], axis=-1)
    q = rms_norm(q, g_qn)
    k = rms_norm(k, g_kn)
    q = jnp.concatenate([q[..., :MLA_NOPE], rope(q[..., MLA_NOPE:], pos)], axis=-1)
    k = jnp.concatenate([k[..., :MLA_NOPE], rope(k[..., MLA_NOPE:], pos)], axis=-1)
    o = block_causal_attention(q, k, v)
    return o.reshape(b, s, MLA_HEADS * MLA_V) @ w_o


def fox_mixer(hn, w_in, b_f, g_qn, g_kn, w_o):
    b, s, _ = hn.shape
    hd = FOX_HEADS * FOX_HEAD_DIM
    z = hn @ w_in
    q = z[..., :hd].reshape(b, s, FOX_HEADS, FOX_HEAD_DIM)
    k = z[..., hd:2 * hd].reshape(b, s, FOX_HEADS, FOX_HEAD_DIM)
    v = z[..., 2 * hd:3 * hd].reshape(b, s, FOX_HEADS, FOX_HEAD_DIM)
    f_logit = z[..., 3 * hd:].astype(jnp.float32) + b_f.astype(jnp.float32)
    log_f = jax.nn.log_sigmoid(f_logit)
    decay = jnp.cumsum(log_f, axis=1)
    q = rms_norm(q, g_qn)
    k = rms_norm(k, g_kn)
    o = block_causal_attention(q, k, v, decay)
    return o.reshape(b, s, hd) @ w_o


def setup_inputs(seed: int = 0) -> dict:
    key = jax.random.key(seed)
    ks = iter(jax.random.split(key, 40))

    def dense(shape, fan_in):
        return jax.random.normal(next(ks), shape, jnp.float32) * (fan_in ** -0.5)

    def gain(shape):
        return 1.0 + 0.1 * jax.random.normal(next(ks), shape, jnp.float32)

    x = jax.random.normal(next(ks), (BATCH, SEQ, D_MODEL), jnp.float32)
    p = jax.random.normal(next(ks), (DEPTH, BATCH, SEQ, PLE_DIM), jnp.float32)
    positions = jnp.broadcast_to(jnp.arange(SEQ, dtype=jnp.int32), (BATCH, SEQ))
    na, nf = N_MLA_LAYERS, N_FOX_LAYERS
    fox_in = 3 * FOX_HEADS * FOX_HEAD_DIM + FOX_HEADS
    return {
        "x": x,
        "p": p,
        "positions": positions,
        "g_ffn1": gain((DEPTH, D_MODEL)),
        "g_mix": gain((DEPTH, D_MODEL)),
        "g_ffn2": gain((DEPTH, D_MODEL)),
        "g_ple": gain((DEPTH, D_MODEL)),
        "ffn1_w_in": dense((DEPTH, D_MODEL, 2 * D_FF), D_MODEL),
        "ffn1_w_out": dense((DEPTH, D_FF, D_MODEL), D_FF),
        "ffn2_w_in": dense((DEPTH, D_MODEL, 2 * D_FF), D_MODEL),
        "ffn2_w_out": dense((DEPTH, D_FF, D_MODEL), D_FF),
        "ple_w_proj": dense((DEPTH, PLE_DIM, D_MODEL), PLE_DIM),
        "ple_w_gate": dense((DEPTH, D_MODEL, D_MODEL), D_MODEL),
        "mla_w_in": dense((na, D_MODEL, MLA_Q_RANK + MLA_KV_RANK + MLA_ROPE), D_MODEL),
        "mla_g_q_lat": gain((na, MLA_Q_RANK)),
        "mla_w_uq": dense((na, MLA_Q_RANK, MLA_HEADS * (MLA_NOPE + MLA_ROPE)), MLA_Q_RANK),
        "mla_g_kv_lat": gain((na, MLA_KV_RANK)),
        "mla_w_ukv": dense((na, MLA_KV_RANK, MLA_HEADS * (MLA_NOPE + MLA_V)), MLA_KV_RANK),
        "mla_g_qn": gain((na, MLA_NOPE + MLA_ROPE)),
        "mla_g_kn": gain((na, MLA_NOPE + MLA_ROPE)),
        "mla_w_o": dense((na, MLA_HEADS * MLA_V, D_MODEL), MLA_HEADS * MLA_V),
        "fox_w_in": dense((nf, D_MODEL, fox_in), D_MODEL),
        "fox_b_f": jax.random.uniform(next(ks), (nf, FOX_HEADS), jnp.float32, 1.0, 4.0),
        "fox_g_qn": gain((nf, FOX_HEAD_DIM)),
        "fox_g_kn": gain((nf, FOX_HEAD_DIM)),
        "fox_w_o": dense((nf, FOX_HEADS * FOX_HEAD_DIM, D_MODEL), FOX_HEADS * FOX_HEAD_DIM),
    }


def reference(x, p, positions, g_ffn1, g_mix, g_ffn2, g_ple,
              ffn1_w_in, ffn1_w_out, ffn2_w_in, ffn2_w_out, ple_w_proj, ple_w_gate,
              mla_w_in, mla_g_q_lat, mla_w_uq, mla_g_kv_lat, mla_w_ukv,
              mla_g_qn, mla_g_kn, mla_w_o,
              fox_w_in, fox_b_f, fox_g_qn, fox_g_kn, fox_w_o):
    h = x
    for i in range(DEPTH):
        h = h + FFN_HALF * swiglu(rms_norm(h, g_ffn1[i]), ffn1_w_in[i], ffn1_w_out[i])
        hn = rms_norm(h, g_mix[i])
        j = i // N_MIXERS
        if i % N_MIXERS == 0:
            h = h + mla_mixer(hn, positions, mla_w_in[j], mla_g_q_lat[j], mla_w_uq[j],
                              mla_g_kv_lat[j], mla_w_ukv[j], mla_g_qn[j], mla_g_kn[j],
                              mla_w_o[j])
        else:
            h = h + fox_mixer(hn, fox_w_in[j], fox_b_f[j], fox_g_qn[j], fox_g_kn[j],
                              fox_w_o[j])
        h = h + FFN_HALF * swiglu(rms_norm(h, g_ffn2[i]), ffn2_w_in[i], ffn2_w_out[i])
        gate = jax.nn.sigmoid(rms_norm(h, g_ple[i]) @ ple_w_gate[i])
        h = h + gate * (p[i] @ ple_w_proj[i])
    return h
```

```python
import functools
import math

import jax
import jax.numpy as jnp
from jax import lax
from jax.experimental import pallas as pl
from jax.experimental.pallas import tpu as pltpu

F32 = jnp.float32
BF16 = jnp.bfloat16

D_MODEL = 1024
D_FF = 2816
PLE_DIM = 256
N_MIXERS = 2
FFN_HALF = 0.5
HEADS = 16
MLA_Q_RANK = 512
MLA_KV_RANK = 256
MLA_NOPE = 64
MLA_ROPE = 32
MLA_QK = MLA_NOPE + MLA_ROPE
HEAD_V = 64
FOX_HEAD_DIM = 64
ROPE_THETA = 10000.0
EPS = 1e-6
LOG2E = math.log2(math.e)

LANES = 128
MXU_DIM = 256
BF16_SUBLANES = 16
QK_PAD = 128
V_ROWS = HEAD_V + BF16_SUBLANES
VMEM_LIMIT = 52 * 1024 * 1024

ROW_TILE = 256
PROJ_TILE = 512
ATT_TQ = 1024
ATT_CW = MXU_DIM
ATT_TK = 512
MASK_VALUE = -jnp.inf
M_INIT = -1e30


def _rms(x, g):
    return x * lax.rsqrt(jnp.mean(x * x, axis=-1, keepdims=True) + EPS) * g


def _resident(shape):
    return pl.BlockSpec(shape, lambda *_: (0,) * len(shape), pipeline_mode=pl.Buffered(1))


def _ffn_kernel(x_ref, g_ref, w_in_ref, w_out_ref, o_ref):
    x = x_ref[...]
    xn = _rms(x, g_ref[...]).astype(BF16)
    gu = jnp.dot(xn, w_in_ref[...], preferred_element_type=F32)
    gate = gu[:, :D_FF]
    up = gu[:, D_FF:]
    act = (gate * jax.nn.sigmoid(gate) * up).astype(BF16)
    y = jnp.dot(act, w_out_ref[...], preferred_element_type=F32)
    o_ref[...] = x + FFN_HALF * y


def _ffn(h2d, g, w_in, w_out):
    n = h2d.shape[0]
    return pl.pallas_call(
        _ffn_kernel,
        grid=(n // ROW_TILE,),
        in_specs=[
            pl.BlockSpec((ROW_TILE, D_MODEL), lambda i: (i, 0)),
            _resident((1, D_MODEL)),
            _resident((D_MODEL, 2 * D_FF)),
            _resident((D_FF, D_MODEL)),
        ],
        out_specs=pl.BlockSpec((ROW_TILE, D_MODEL), lambda i: (i, 0)),
        out_shape=jax.ShapeDtypeStruct((n, D_MODEL), F32),
        compiler_params=pltpu.CompilerParams(
            dimension_semantics=("parallel",), vmem_limit_bytes=VMEM_LIMIT),
        name="ffn",
    )(h2d, g, w_in, w_out)


def _oproj_kernel(x_ref, ot_ref, w_ref, o_ref):
    y = lax.dot_general(ot_ref[...], w_ref[...], (((0,), (0,)), ((), ())),
                        preferred_element_type=F32)
    o_ref[...] = x_ref[...] + y


def _oproj(h, ot, w_o):
    b, s, _ = h.shape
    hd = ot.shape[1]
    return pl.pallas_call(
        _oproj_kernel,
        grid=(b, s // PROJ_TILE),
        in_specs=[
            pl.BlockSpec((None, PROJ_TILE, D_MODEL), lambda bi, i: (bi, i, 0)),
            pl.BlockSpec((None, hd, PROJ_TILE), lambda bi, i: (bi, 0, i)),
            _resident((hd, D_MODEL)),
        ],
        out_specs=pl.BlockSpec((None, PROJ_TILE, D_MODEL), lambda bi, i: (bi, i, 0)),
        out_shape=jax.ShapeDtypeStruct(h.shape, F32),
        compiler_params=pltpu.CompilerParams(
            dimension_semantics=("parallel", "parallel"), vmem_limit_bytes=VMEM_LIMIT),
        name="oproj",
    )(h, ot, w_o)


def _ple_kernel(x_ref, p_ref, g_ref, wg_ref, wp_ref, o_ref):
    x = x_ref[...]
    xn = _rms(x, g_ref[...]).astype(BF16)
    gate = jax.nn.sigmoid(jnp.dot(xn, wg_ref[...], preferred_element_type=F32))
    pp = jnp.dot(p_ref[...].astype(BF16), wp_ref[...], preferred_element_type=F32)
    o_ref[...] = x + gate * pp


def _ple(h2d, p2d, g, w_gate, w_proj):
    n = h2d.shape[0]
    return pl.pallas_call(
        _ple_kernel,
        grid=(n // ROW_TILE,),
        in_specs=[
            pl.BlockSpec((ROW_TILE, D_MODEL), lambda i: (i, 0)),
            pl.BlockSpec((ROW_TILE, PLE_DIM), lambda i: (i, 0)),
            _resident((1, D_MODEL)),
            _resident((D_MODEL, D_MODEL)),
            _resident((PLE_DIM, D_MODEL)),
        ],
        out_specs=pl.BlockSpec((ROW_TILE, D_MODEL), lambda i: (i, 0)),
        out_shape=jax.ShapeDtypeStruct((n, D_MODEL), F32),
        compiler_params=pltpu.CompilerParams(
            dimension_semantics=("parallel",), vmem_limit_bytes=VMEM_LIMIT),
        name="ple",
    )(h2d, p2d, g, w_gate, w_proj)


def _dot_nt(w, x):
    return lax.dot_general(w, x, (((1,), (1,)), ((), ())), preferred_element_type=F32)


def _ones_row_group(width):
    row = lax.broadcasted_iota(jnp.int32, (BF16_SUBLANES, width), 0)
    return jnp.where(row == 0, 1.0, 0.0).astype(F32)


def _rope_rows(x1, x2, cos, sin):
    return x1 * cos - x2 * sin, x1 * sin + x2 * cos


def _mla_proj_kernel(x_ref, pos_ref, g_ref, w_in_ref, w_pe_ref, gq_lat_ref, gkv_lat_ref,
                     w_uq_ref, w_ukv_ref, gqn_ref, gkn_ref, inv_freq_ref,
                     qt_ref, k_ref, vt_ref):
    tm = x_ref.shape[0]
    xn = _rms(x_ref[...], g_ref[...]).astype(BF16)
    z = jnp.dot(xn, w_in_ref[...], preferred_element_type=F32)
    cq = _rms(z[:, :MLA_Q_RANK], gq_lat_ref[...]).astype(BF16)
    ckv = _rms(z[:, MLA_Q_RANK:], gkv_lat_ref[...]).astype(BF16)
    qt_all = _dot_nt(w_uq_ref[...], cq)
    kvt_all = _dot_nt(w_ukv_ref[...], ckv)
    kpe_t = _dot_nt(w_pe_ref[...], xn)

    ang = pos_ref[...].astype(F32) * inv_freq_ref[...]
    cos = jnp.cos(ang)
    sin = jnp.sin(ang)
    half = MLA_ROPE // 2
    q_scale = (MLA_QK ** -0.5) * LOG2E
    gqn = gqn_ref[...] * q_scale
    gkn = gkn_ref[...]
    kpe_sq = jnp.sum(kpe_t * kpe_t, axis=0, keepdims=True)
    ones_rows = _ones_row_group(tm).astype(BF16)
    zeros_tail = jnp.zeros((QK_PAD - MLA_QK, tm), F32)

    for h in range(HEADS):
        qh = qt_all[h * MLA_QK:(h + 1) * MLA_QK]
        qn = qh * lax.rsqrt(jnp.mean(qh * qh, axis=0, keepdims=True) + EPS) * gqn
        q1, q2 = _rope_rows(qn[MLA_NOPE:MLA_NOPE + half], qn[MLA_NOPE + half:], cos, sin)
        qt_ref[h, 0:MLA_NOPE, :] = qn[:MLA_NOPE].astype(BF16)
        qt_ref[h, MLA_NOPE:MLA_NOPE + half, :] = q1.astype(BF16)
        qt_ref[h, MLA_NOPE + half:MLA_QK, :] = q2.astype(BF16)
        qt_ref[h, MLA_QK:QK_PAD, :] = zeros_tail.astype(BF16)

        base = h * (MLA_NOPE + HEAD_V)
        kn = kvt_all[base:base + MLA_NOPE]
        ms = (jnp.sum(kn * kn, axis=0, keepdims=True) + kpe_sq) * (1.0 / MLA_QK)
        r = lax.rsqrt(ms + EPS)
        kn = kn * r * gkn[:MLA_NOPE]
        kp = kpe_t * r * gkn[MLA_NOPE:]
        k1, k2 = _rope_rows(kp[:half], kp[half:], cos, sin)
        kt = jnp.concatenate([kn, k1, k2, zeros_tail], axis=0)
        k_ref[h] = kt.T.astype(BF16)

        vt_ref[h, 0:HEAD_V, :] = kvt_all[base + MLA_NOPE:base + MLA_NOPE + HEAD_V].astype(BF16)
        vt_ref[h, HEAD_V:V_ROWS, :] = ones_rows


def _qkv_out(b, s):
    specs = [
        pl.BlockSpec((None, HEADS, QK_PAD, PROJ_TILE), lambda bi, i: (bi, 0, 0, i)),
        pl.BlockSpec((None, HEADS, PROJ_TILE, QK_PAD), lambda bi, i: (bi, 0, i, 0)),
        pl.BlockSpec((None, HEADS, V_ROWS, PROJ_TILE), lambda bi, i: (bi, 0, 0, i)),
    ]
    shapes = [
        jax.ShapeDtypeStruct((b, HEADS, QK_PAD, s), BF16),
        jax.ShapeDtypeStruct((b, HEADS, s, QK_PAD), BF16),
        jax.ShapeDtypeStruct((b, HEADS, V_ROWS, s), BF16),
    ]
    return specs, shapes


def _mla_proj(h, pos3, g, w_in, w_pe_t, gq_lat, gkv_lat, w_uq_t, w_ukv_t, gqn, gkn, inv_freq):
    b, s, _ = h.shape
    out_specs, out_shapes = _qkv_out(b, s)
    return pl.pallas_call(
        _mla_proj_kernel,
        grid=(b, s // PROJ_TILE),
        in_specs=[
            pl.BlockSpec((None, PROJ_TILE, D_MODEL), lambda bi, i: (bi, i, 0)),
            pl.BlockSpec((None, 1, PROJ_TILE), lambda bi, i: (bi, 0, i)),
            _resident(g.shape), _resident(w_in.shape), _resident(w_pe_t.shape),
            _resident(gq_lat.shape), _resident(gkv_lat.shape),
            _resident(w_uq_t.shape), _resident(w_ukv_t.shape),
            _resident(gqn.shape), _resident(gkn.shape), _resident(inv_freq.shape),
        ],
        out_specs=out_specs,
        out_shape=out_shapes,
        compiler_params=pltpu.CompilerParams(
            dimension_semantics=("parallel", "parallel"), vmem_limit_bytes=VMEM_LIMIT),
        name="mla_proj",
    )(h, pos3, g, w_in, w_pe_t, gq_lat, gkv_lat, w_uq_t, w_ukv_t, gqn, gkn, inv_freq)


def _split3(c):
    hi = c.astype(BF16).astype(F32)
    mid = (c - hi).astype(BF16).astype(F32)
    lo = (c - hi - mid).astype(BF16).astype(F32)
    return hi, mid, lo


def _fox_proj_kernel(x_ref, g_ref, w_ref, bf_ref, gqn_ref, gkn_ref,
                     qt_ref, k_ref, vt_ref, carry_ref):
    tm = x_ref.shape[0]
    hd = HEADS * FOX_HEAD_DIM

    @pl.when(pl.program_id(1) == 0)
    def _():
        carry_ref[...] = jnp.zeros_like(carry_ref)

    xn = _rms(x_ref[...], g_ref[...]).astype(BF16)
    qt_all = _dot_nt(w_ref[0:hd, :], xn)
    kt_all = _dot_nt(w_ref[hd:2 * hd, :], xn)
    vt_all = _dot_nt(w_ref[2 * hd:3 * hd, :], xn)
    f_t = _dot_nt(w_ref[3 * hd:3 * hd + HEADS, :], xn) + bf_ref[...]

    logf = (jnp.minimum(f_t, 0.0) - jnp.log1p(jnp.exp(-jnp.abs(f_t)))) * LOG2E
    hi, mid, lo = _split3(logf)
    parts = jnp.concatenate([hi, mid, lo], axis=0).astype(BF16)
    src = lax.broadcasted_iota(jnp.int32, (tm, tm), 0)
    dst = lax.broadcasted_iota(jnp.int32, (tm, tm), 1)
    upper = jnp.where(src <= dst, 1.0, 0.0).astype(BF16)
    sums = jnp.dot(parts, upper, preferred_element_type=F32)
    local = sums[2 * HEADS:] + sums[HEADS:2 * HEADS] + sums[:HEADS]
    c = local + carry_ref[:, 0:1]
    carry_ref[...] = jnp.broadcast_to(c[:, tm - 1:tm], carry_ref.shape)

    q_scale = (FOX_HEAD_DIM ** -0.5) * LOG2E
    gqn = gqn_ref[...] * q_scale
    gkn = gkn_ref[...]
    row = lax.broadcasted_iota(jnp.int32, (BF16_SUBLANES, tm), 0)
    ones_rows = _ones_row_group(tm).astype(BF16)
    zeros_tail = jnp.zeros((QK_PAD - FOX_HEAD_DIM - BF16_SUBLANES, tm), F32)

    def bias_rows(first, second):
        out = jnp.zeros((BF16_SUBLANES, tm), F32)
        for j in range(3):
            out = jnp.where(row == j, first[j], out)
            out = jnp.where(row == 8 + j, second[j], out)
        return out

    one3 = (1.0, 1.0, 1.0)
    for h in range(HEADS):
        sl = slice(h * FOX_HEAD_DIM, (h + 1) * FOX_HEAD_DIM)
        c3 = _split3(c[h:h + 1])
        qh = qt_all[sl]
        qn = qh * lax.rsqrt(jnp.mean(qh * qh, axis=0, keepdims=True) + EPS) * gqn
        qt_ref[h, 0:FOX_HEAD_DIM, :] = qn.astype(BF16)
        qt_ref[h, FOX_HEAD_DIM:FOX_HEAD_DIM + BF16_SUBLANES, :] = bias_rows(one3, c3).astype(BF16)
        qt_ref[h, FOX_HEAD_DIM + BF16_SUBLANES:QK_PAD, :] = zeros_tail.astype(BF16)

        kh = kt_all[sl]
        kn = kh * lax.rsqrt(jnp.mean(kh * kh, axis=0, keepdims=True) + EPS) * gkn
        neg3 = tuple(-p for p in c3)
        kt = jnp.concatenate([kn, bias_rows(neg3, one3), zeros_tail], axis=0)
        k_ref[h] = kt.T.astype(BF16)

        vt_ref[h, 0:HEAD_V, :] = vt_all[sl].astype(BF16)
        vt_ref[h, HEAD_V:V_ROWS, :] = ones_rows


def _fox_proj(h, g, w_t, b_f, gqn, gkn):
    b, s, _ = h.shape
    out_specs, out_shapes = _qkv_out(b, s)
    return pl.pallas_call(
        _fox_proj_kernel,
        grid=(b, s // PROJ_TILE),
        in_specs=[
            pl.BlockSpec((None, PROJ_TILE, D_MODEL), lambda bi, i: (bi, i, 0)),
            _resident(g.shape), _resident(w_t.shape), _resident(b_f.shape),
            _resident(gqn.shape), _resident(gkn.shape),
        ],
        out_specs=out_specs,
        out_shape=out_shapes,
        scratch_shapes=[pltpu.VMEM((HEADS, LANES), F32)],
        compiler_params=pltpu.CompilerParams(
            dimension_semantics=("arbitrary", "arbitrary"), vmem_limit_bytes=VMEM_LIMIT),
        name="fox_proj",
    )(h, g, w_t, b_f, gqn, gkn)


def _attn_kernel(qt_ref, k_ref, vt_ref, o_ref, acc_ref, m_ref, *, seq, tq, tk, cw):
    ncol = tq // cw

    def chain(q0, k0, c, nk, masked):
        cols = pl.ds(c * cw, cw)
        qt = qt_ref[:, pl.ds(q0 + c * cw, cw)]
        s = jnp.dot(k_ref[pl.ds(k0, nk), :], qt, preferred_element_type=F32)
        if masked:
            kpos = k0 + lax.broadcasted_iota(jnp.int32, (nk, cw), 0)
            qpos = q0 + c * cw + lax.broadcasted_iota(jnp.int32, (nk, cw), 1)
            s = jnp.where(kpos <= qpos, s, MASK_VALUE)
        m_old = m_ref[:, cols]
        m_new = jnp.maximum(m_old, jnp.max(s, axis=0, keepdims=True))
        alpha = jnp.exp2(m_old - m_new)
        p = jnp.exp2(s - m_new).astype(BF16)
        pv = jnp.dot(vt_ref[:, pl.ds(k0, nk)], p, preferred_element_type=F32)
        acc_ref[:, cols] = alpha * acc_ref[:, cols] + pv
        m_ref[:, cols] = m_new

    def q_tile(qi, carry):
        q0 = pl.multiple_of(qi * tq, tq)
        acc_ref[...] = jnp.zeros_like(acc_ref)
        m_ref[...] = jnp.full_like(m_ref, M_INIT)

        def below(kj, carry2):
            k0 = pl.multiple_of(kj * tk, tk)
            for c in range(ncol):
                chain(q0, k0, c, tk, False)
            return carry2

        lax.fori_loop(0, qi * (tq // tk), below, 0)
        for kb in range(ncol):
            k0 = q0 + kb * cw
            for c in range(kb, ncol):
                chain(q0, k0, c, cw, c == kb)

        acc = acc_ref[...]
        o = acc[:HEAD_V] / acc[HEAD_V:HEAD_V + 1]
        o_ref[:, pl.ds(q0, tq)] = o.astype(o_ref.dtype)
        return carry

    lax.fori_loop(0, seq // tq, q_tile, 0)


def _attention(qt, k, vt):
    b, heads, _, s = qt.shape
    tq = min(ATT_TQ, s)
    tk = min(ATT_TK, tq)
    kern = functools.partial(_attn_kernel, seq=s, tq=tq, tk=tk, cw=ATT_CW)
    return pl.pallas_call(
        kern,
        grid=(b, heads),
        in_specs=[
            pl.BlockSpec((None, None, QK_PAD, s), lambda bi, hi: (bi, hi, 0, 0)),
            pl.BlockSpec((None, None, s, QK_PAD), lambda bi, hi: (bi, hi, 0, 0)),
            pl.BlockSpec((None, None, V_ROWS, s), lambda bi, hi: (bi, hi, 0, 0)),
        ],
        out_specs=pl.BlockSpec((None, HEAD_V, s), lambda bi, hi: (bi, hi, 0)),
        out_shape=jax.ShapeDtypeStruct((b, heads * HEAD_V, s), BF16),
        scratch_shapes=[pltpu.VMEM((V_ROWS, tq), F32), pltpu.VMEM((1, tq), F32)],
        compiler_params=pltpu.CompilerParams(
            dimension_semantics=("parallel", "parallel"), vmem_limit_bytes=VMEM_LIMIT),
        name="attention",
    )(qt, k, vt)


def kernel(x, p, positions, g_ffn1, g_mix, g_ffn2, g_ple, ffn1_w_in, ffn1_w_out, ffn2_w_in, ffn2_w_out, ple_w_proj, ple_w_gate, mla_w_in, mla_g_q_lat, mla_w_uq, mla_g_kv_lat, mla_w_ukv, mla_g_qn, mla_g_kn, mla_w_o, fox_w_in, fox_b_f, fox_g_qn, fox_g_kn, fox_w_o):
    b, s, d = x.shape
    depth = g_ffn1.shape[0]
    assert d == D_MODEL and s % PROJ_TILE == 0 and (b * s) % ROW_TILE == 0
    n = b * s
    bf = lambda w: w.astype(BF16)
    row = lambda g: g.reshape(1, -1)
    col = lambda g: g.reshape(-1, 1)

    pos3 = positions.reshape(b, 1, s)
    half = MLA_ROPE // 2
    inv_freq = (ROPE_THETA ** (-jnp.arange(0, MLA_ROPE, 2, dtype=F32) / MLA_ROPE)).reshape(half, 1)
    lat = MLA_Q_RANK + MLA_KV_RANK

    h = x
    for i in range(depth):
        j = i // N_MIXERS
        h = _ffn(h.reshape(n, d), row(g_ffn1[i]), bf(ffn1_w_in[i]), bf(ffn1_w_out[i])).reshape(b, s, d)
        if i % N_MIXERS == 0:
            qt, k, vt = _mla_proj(
                h, pos3, row(g_mix[i]), bf(mla_w_in[j][:, :lat]), bf(mla_w_in[j][:, lat:].T),
                row(mla_g_q_lat[j]), row(mla_g_kv_lat[j]), bf(mla_w_uq[j].T), bf(mla_w_ukv[j].T),
                col(mla_g_qn[j]), col(mla_g_kn[j]), inv_freq)
            w_o = mla_w_o[j]
        else:
            qt, k, vt = _fox_proj(h, row(g_mix[i]), bf(fox_w_in[j].T), col(fox_b_f[j]),
                                  col(fox_g_qn[j]), col(fox_g_kn[j]))
            w_o = fox_w_o[j]
        ot = _attention(qt, k, vt)
        h = _oproj(h, ot, bf(w_o))
        h = _ffn(h.reshape(n, d), row(g_ffn2[i]), bf(ffn2_w_in[i]), bf(ffn2_w_out[i]))
        h = _ple(h, p[i].reshape(n, PLE_DIM), row(g_ple[i]), bf(ple_w_gate[i]), bf(ple_w_proj[i])).reshape(b, s, d)
    return h
```

```python
import functools
import math

import jax
import jax.numpy as jnp
from jax import lax
from jax.experimental import pallas as pl
from jax.experimental.pallas import tpu as pltpu

F32 = jnp.float32
BF16 = jnp.bfloat16

D_MODEL = 1024
D_FF = 2816
PLE_DIM = 256
N_MIXERS = 2
FFN_HALF = 0.5
HEADS = 16
MLA_Q_RANK = 512
MLA_KV_RANK = 256
MLA_NOPE = 64
MLA_ROPE = 32
MLA_QK = MLA_NOPE + MLA_ROPE
HEAD_V = 64
FOX_HEAD_DIM = 64
ROPE_THETA = 10000.0
EPS = 1e-6
LOG2E = math.log2(math.e)

LANES = 128
MXU_DIM = 256
BF16_SUBLANES = 16
QK_PAD = 128
V_ROWS = HEAD_V + BF16_SUBLANES
VMEM_LIMIT = 52 * 1024 * 1024

ROW_TILE = 256
PROJ_TILE = 512
ATT_TQ = 1024
ATT_CW = MXU_DIM
MASK_VALUE = -jnp.inf
M_INIT = -1e30


def _rms(x, g):
    return x * lax.rsqrt(jnp.mean(x * x, axis=-1, keepdims=True) + EPS) * g


def _resident(shape):
    return pl.BlockSpec(shape, lambda *_: (0,) * len(shape), pipeline_mode=pl.Buffered(1))


def _ffn_kernel(x_ref, g_ref, w_in_ref, w_out_ref, o_ref):
    x = x_ref[...]
    xn = _rms(x, g_ref[...]).astype(BF16)
    gu = jnp.dot(xn, w_in_ref[...], preferred_element_type=F32)
    gate = gu[:, :D_FF]
    up = gu[:, D_FF:]
    act = (gate * jax.nn.sigmoid(gate) * up).astype(BF16)
    y = jnp.dot(act, w_out_ref[...], preferred_element_type=F32)
    o_ref[...] = x + FFN_HALF * y


def _ffn(h2d, g, w_in, w_out):
    n = h2d.shape[0]
    return pl.pallas_call(
        _ffn_kernel,
        grid=(n // ROW_TILE,),
        in_specs=[
            pl.BlockSpec((ROW_TILE, D_MODEL), lambda i: (i, 0)),
            _resident((1, D_MODEL)),
            _resident((D_MODEL, 2 * D_FF)),
            _resident((D_FF, D_MODEL)),
        ],
        out_specs=pl.BlockSpec((ROW_TILE, D_MODEL), lambda i: (i, 0)),
        out_shape=jax.ShapeDtypeStruct((n, D_MODEL), F32),
        compiler_params=pltpu.CompilerParams(
            dimension_semantics=("parallel",), vmem_limit_bytes=VMEM_LIMIT),
        name="ffn",
    )(h2d, g, w_in, w_out)


def _oproj_kernel(x_ref, ot_ref, w_ref, o_ref):
    y = lax.dot_general(ot_ref[...], w_ref[...], (((0,), (0,)), ((), ())),
                        preferred_element_type=F32)
    o_ref[...] = x_ref[...] + y


def _oproj(h, ot, w_o):
    b, s, _ = h.shape
    hd = ot.shape[1]
    return pl.pallas_call(
        _oproj_kernel,
        grid=(b, s // PROJ_TILE),
        in_specs=[
            pl.BlockSpec((None, PROJ_TILE, D_MODEL), lambda bi, i: (bi, i, 0)),
            pl.BlockSpec((None, hd, PROJ_TILE), lambda bi, i: (bi, 0, i)),
            _resident((hd, D_MODEL)),
        ],
        out_specs=pl.BlockSpec((None, PROJ_TILE, D_MODEL), lambda bi, i: (bi, i, 0)),
        out_shape=jax.ShapeDtypeStruct(h.shape, F32),
        compiler_params=pltpu.CompilerParams(
            dimension_semantics=("parallel", "parallel"), vmem_limit_bytes=VMEM_LIMIT),
        name="oproj",
    )(h, ot, w_o)


def _ple_kernel(x_ref, p_ref, g_ref, wg_ref, wp_ref, o_ref):
    x = x_ref[...]
    xn = _rms(x, g_ref[...]).astype(BF16)
    gate = jax.nn.sigmoid(jnp.dot(xn, wg_ref[...], preferred_element_type=F32))
    pp = jnp.dot(p_ref[...].astype(BF16), wp_ref[...], preferred_element_type=F32)
    o_ref[...] = x + gate * pp


def _ple(h2d, p2d, g, w_gate, w_proj):
    n = h2d.shape[0]
    return pl.pallas_call(
        _ple_kernel,
        grid=(n // ROW_TILE,),
        in_specs=[
            pl.BlockSpec((ROW_TILE, D_MODEL), lambda i: (i, 0)),
            pl.BlockSpec((ROW_TILE, PLE_DIM), lambda i: (i, 0)),
            _resident((1, D_MODEL)),
            _resident((D_MODEL, D_MODEL)),
            _resident((PLE_DIM, D_MODEL)),
        ],
        out_specs=pl.BlockSpec((ROW_TILE, D_MODEL), lambda i: (i, 0)),
        out_shape=jax.ShapeDtypeStruct((n, D_MODEL), F32),
        compiler_params=pltpu.CompilerParams(
            dimension_semantics=("parallel",), vmem_limit_bytes=VMEM_LIMIT),
        name="ple",
    )(h2d, p2d, g, w_gate, w_proj)


def _dot_nt(w, x):
    return lax.dot_general(w, x, (((1,), (1,)), ((), ())), preferred_element_type=F32)


def _ones_row_group(width):
    row = lax.broadcasted_iota(jnp.int32, (BF16_SUBLANES, width), 0)
    return jnp.where(row == 0, 1.0, 0.0).astype(F32)


def _rope_rows(x1, x2, cos, sin):
    return x1 * cos - x2 * sin, x1 * sin + x2 * cos


def _mla_proj_kernel(x_ref, pos_ref, g_ref, w_in_ref, w_pe_ref, gq_lat_ref, gkv_lat_ref,
                     w_uq_ref, w_ukv_ref, gqn_ref, gkn_ref, inv_freq_ref,
                     qt_ref, k_ref, vt_ref):
    tm = x_ref.shape[0]
    xn = _rms(x_ref[...], g_ref[...]).astype(BF16)
    z = jnp.dot(xn, w_in_ref[...], preferred_element_type=F32)
    cq = _rms(z[:, :MLA_Q_RANK], gq_lat_ref[...]).astype(BF16)
    ckv = _rms(z[:, MLA_Q_RANK:], gkv_lat_ref[...]).astype(BF16)
    qt_all = _dot_nt(w_uq_ref[...], cq)
    kvt_all = _dot_nt(w_ukv_ref[...], ckv)
    kpe_t = _dot_nt(w_pe_ref[...], xn)

    ang = pos_ref[...].astype(F32) * inv_freq_ref[...]
    cos = jnp.cos(ang)
    sin = jnp.sin(ang)
    half = MLA_ROPE // 2
    q_scale = (MLA_QK ** -0.5) * LOG2E
    gqn = gqn_ref[...] * q_scale
    gkn = gkn_ref[...]
    kpe_sq = jnp.sum(kpe_t * kpe_t, axis=0, keepdims=True)
    ones_rows = _ones_row_group(tm).astype(BF16)
    zeros_tail = jnp.zeros((QK_PAD - MLA_QK, tm), F32)

    for h in range(HEADS):
        qh = qt_all[h * MLA_QK:(h + 1) * MLA_QK]
        qn = qh * lax.rsqrt(jnp.mean(qh * qh, axis=0, keepdims=True) + EPS) * gqn
        q1, q2 = _rope_rows(qn[MLA_NOPE:MLA_NOPE + half], qn[MLA_NOPE + half:], cos, sin)
        qt_ref[h, 0:MLA_NOPE, :] = qn[:MLA_NOPE].astype(BF16)
        qt_ref[h, MLA_NOPE:MLA_NOPE + half, :] = q1.astype(BF16)
        qt_ref[h, MLA_NOPE + half:MLA_QK, :] = q2.astype(BF16)
        qt_ref[h, MLA_QK:QK_PAD, :] = zeros_tail.astype(BF16)

        base = h * (MLA_NOPE + HEAD_V)
        kn = kvt_all[base:base + MLA_NOPE]
        ms = (jnp.sum(kn * kn, axis=0, keepdims=True) + kpe_sq) * (1.0 / MLA_QK)
        r = lax.rsqrt(ms + EPS)
        kn = kn * r * gkn[:MLA_NOPE]
        kp = kpe_t * r * gkn[MLA_NOPE:]
        k1, k2 = _rope_rows(kp[:half], kp[half:], cos, sin)
        kt = jnp.concatenate([kn, k1, k2, zeros_tail], axis=0)
        k_ref[h] = kt.T.astype(BF16)

        vt_ref[h, 0:HEAD_V, :] = kvt_all[base + MLA_NOPE:base + MLA_NOPE + HEAD_V].astype(BF16)
        vt_ref[h, HEAD_V:V_ROWS, :] = ones_rows


def _qkv_out(b, s):
    specs = [
        pl.BlockSpec((None, HEADS, QK_PAD, PROJ_TILE), lambda bi, i: (bi, 0, 0, i)),
        pl.BlockSpec((None, HEADS, PROJ_TILE, QK_PAD), lambda bi, i: (bi, 0, i, 0)),
        pl.BlockSpec((None, HEADS, V_ROWS, PROJ_TILE), lambda bi, i: (bi, 0, 0, i)),
    ]
    shapes = [
        jax.ShapeDtypeStruct((b, HEADS, QK_PAD, s), BF16),
        jax.ShapeDtypeStruct((b, HEADS, s, QK_PAD), BF16),
        jax.ShapeDtypeStruct((b, HEADS, V_ROWS, s), BF16),
    ]
    return specs, shapes


def _mla_proj(h, pos3, g, w_in, w_pe_t, gq_lat, gkv_lat, w_uq_t, w_ukv_t, gqn, gkn, inv_freq):
    b, s, _ = h.shape
    out_specs, out_shapes = _qkv_out(b, s)
    return pl.pallas_call(
        _mla_proj_kernel,
        grid=(b, s // PROJ_TILE),
        in_specs=[
            pl.BlockSpec((None, PROJ_TILE, D_MODEL), lambda bi, i: (bi, i, 0)),
            pl.BlockSpec((None, 1, PROJ_TILE), lambda bi, i: (bi, 0, i)),
            _resident(g.shape), _resident(w_in.shape), _resident(w_pe_t.shape),
            _resident(gq_lat.shape), _resident(gkv_lat.shape),
            _resident(w_uq_t.shape), _resident(w_ukv_t.shape),
            _resident(gqn.shape), _resident(gkn.shape), _resident(inv_freq.shape),
        ],
        out_specs=out_specs,
        out_shape=out_shapes,
        compiler_params=pltpu.CompilerParams(
            dimension_semantics=("parallel", "parallel"), vmem_limit_bytes=VMEM_LIMIT),
        name="mla_proj",
    )(h, pos3, g, w_in, w_pe_t, gq_lat, gkv_lat, w_uq_t, w_ukv_t, gqn, gkn, inv_freq)


def _split3(c):
    hi = c.astype(BF16).astype(F32)
    mid = (c - hi).astype(BF16).astype(F32)
    lo = (c - hi - mid).astype(BF16).astype(F32)
    return hi, mid, lo


def _fox_proj_kernel(x_ref, g_ref, w_ref, bf_ref, gqn_ref, gkn_ref,
                     qt_ref, k_ref, vt_ref, carry_ref):
    tm = x_ref.shape[0]
    hd = HEADS * FOX_HEAD_DIM

    @pl.when(pl.program_id(1) == 0)
    def _():
        carry_ref[...] = jnp.zeros_like(carry_ref)

    xn = _rms(x_ref[...], g_ref[...]).astype(BF16)
    qt_all = _dot_nt(w_ref[0:hd, :], xn)
    kt_all = _dot_nt(w_ref[hd:2 * hd, :], xn)
    vt_all = _dot_nt(w_ref[2 * hd:3 * hd, :], xn)
    f_t = _dot_nt(w_ref[3 * hd:3 * hd + HEADS, :], xn) + bf_ref[...]

    logf = (jnp.minimum(f_t, 0.0) - jnp.log1p(jnp.exp(-jnp.abs(f_t)))) * LOG2E
    hi, mid, lo = _split3(logf)
    parts = jnp.concatenate([hi, mid, lo], axis=0).astype(BF16)
    src = lax.broadcasted_iota(jnp.int32, (tm, tm), 0)
    dst = lax.broadcasted_iota(jnp.int32, (tm, tm), 1)
    upper = jnp.where(src <= dst, 1.0, 0.0).astype(BF16)
    sums = jnp.dot(parts, upper, preferred_element_type=F32)
    local = sums[2 * HEADS:] + sums[HEADS:2 * HEADS] + sums[:HEADS]
    c = local + carry_ref[:, 0:1]
    carry_ref[...] = jnp.broadcast_to(c[:, tm - 1:tm], carry_ref.shape)

    q_scale = (FOX_HEAD_DIM ** -0.5) * LOG2E
    gqn = gqn_ref[...] * q_scale
    gkn = gkn_ref[...]
    row = lax.broadcasted_iota(jnp.int32, (BF16_SUBLANES, tm), 0)
    ones_rows = _ones_row_group(tm).astype(BF16)
    zeros_tail = jnp.zeros((QK_PAD - FOX_HEAD_DIM - BF16_SUBLANES, tm), F32)

    def bias_rows(first, second):
        out = jnp.zeros((BF16_SUBLANES, tm), F32)
        for j in range(3):
            out = jnp.where(row == j, first[j], out)
            out = jnp.where(row == 8 + j, second[j], out)
        return out

    one3 = (1.0, 1.0, 1.0)
    for h in range(HEADS):
        sl = slice(h * FOX_HEAD_DIM, (h + 1) * FOX_HEAD_DIM)
        c3 = _split3(c[h:h + 1])
        qh = qt_all[sl]
        qn = qh * lax.rsqrt(jnp.mean(qh * qh, axis=0, keepdims=True) + EPS) * gqn
        qt_ref[h, 0:FOX_HEAD_DIM, :] = qn.astype(BF16)
        qt_ref[h, FOX_HEAD_DIM:FOX_HEAD_DIM + BF16_SUBLANES, :] = bias_rows(one3, c3).astype(BF16)
        qt_ref[h, FOX_HEAD_DIM + BF16_SUBLANES:QK_PAD, :] = zeros_tail.astype(BF16)

        kh = kt_all[sl]
        kn = kh * lax.rsqrt(jnp.mean(kh * kh, axis=0, keepdims=True) + EPS) * gkn
        neg3 = tuple(-p for p in c3)
        kt = jnp.concatenate([kn, bias_rows(neg3, one3), zeros_tail], axis=0)
        k_ref[h] = kt.T.astype(BF16)

        vt_ref[h, 0:HEAD_V, :] = vt_all[sl].astype(BF16)
        vt_ref[h, HEAD_V:V_ROWS, :] = ones_rows


def _fox_proj(h, g, w_t, b_f, gqn, gkn):
    b, s, _ = h.shape
    out_specs, out_shapes = _qkv_out(b, s)
    return pl.pallas_call(
        _fox_proj_kernel,
        grid=(b, s // PROJ_TILE),
        in_specs=[
            pl.BlockSpec((None, PROJ_TILE, D_MODEL), lambda bi, i: (bi, i, 0)),
            _resident(g.shape), _resident(w_t.shape), _resident(b_f.shape),
            _resident(gqn.shape), _resident(gkn.shape),
        ],
        out_specs=out_specs,
        out_shape=out_shapes,
        scratch_shapes=[pltpu.VMEM((HEADS, LANES), F32)],
        compiler_params=pltpu.CompilerParams(
            dimension_semantics=("arbitrary", "arbitrary"), vmem_limit_bytes=VMEM_LIMIT),
        name="fox_proj",
    )(h, g, w_t, b_f, gqn, gkn)


def _attn_kernel(qt_ref, k_ref, vt_ref, o_ref, acc_ref, m_ref, s0_ref, s1_ref, bm0_ref, bm1_ref,
                 *, seq, tq, tk, cw):
    ncol = tq // cw
    all_chains = tuple(range(ncol))
    s_refs = (s0_ref, s1_ref)
    bm_refs = (bm0_ref, bm1_ref)

    def stage_a(q0, k0, slot, masked, chains):
        kb = k_ref[pl.ds(k0, tk), :]
        for c in chains:
            cols = slice(c * cw, (c + 1) * cw)
            qt = qt_ref[:, pl.ds(q0 + c * cw, cw)]
            s = jnp.dot(kb, qt, preferred_element_type=F32)
            if masked:
                kpos = k0 + lax.broadcasted_iota(jnp.int32, (tk, cw), 0)
                qpos = q0 + c * cw + lax.broadcasted_iota(jnp.int32, (tk, cw), 1)
                s = jnp.where(kpos <= qpos, s, MASK_VALUE)
            s_refs[slot][:, cols] = s
            bm_refs[slot][:, cols] = jnp.max(s, axis=0, keepdims=True)

    def stage_b(k0, slot, chains):
        vb = vt_ref[:, pl.ds(k0, tk)]
        for c in chains:
            cols = slice(c * cw, (c + 1) * cw)
            m_old = m_ref[:, cols]
            m_new = jnp.maximum(m_old, bm_refs[slot][:, cols])
            alpha = jnp.exp2(m_old - m_new)
            p = jnp.exp2(s_refs[slot][:, cols] - m_new).astype(BF16)
            pv = jnp.dot(vb, p, preferred_element_type=F32)
            acc_ref[:, cols] = alpha * acc_ref[:, cols] + pv
            m_ref[:, cols] = m_new

    diag1_chains = tuple(c for c in all_chains if (c + 1) * cw > tk)

    def q_tile(qi, carry):
        q0 = pl.multiple_of(qi * tq, tq)
        acc_ref[...] = jnp.zeros_like(acc_ref)
        m_ref[...] = jnp.full_like(m_ref, M_INIT)
        stage_a(q0, 0, 0, True, all_chains)

        def pair(t, carry2):
            k0 = pl.multiple_of(2 * t * tk, 2 * tk)
            stage_a(q0, k0 + tk, 1, False, all_chains)
            stage_b(k0, 0, all_chains)
            stage_a(q0, k0 + 2 * tk, 0, True, all_chains)
            stage_b(k0 + tk, 1, all_chains)
            return carry2

        lax.fori_loop(0, qi, pair, 0)
        stage_a(q0, q0 + tk, 1, True, diag1_chains)
        stage_b(q0, 0, all_chains)
        stage_b(q0 + tk, 1, diag1_chains)

        acc = acc_ref[...]
        o = acc[:HEAD_V] / acc[HEAD_V:HEAD_V + 1]
        o_ref[:, pl.ds(q0, tq)] = o.astype(o_ref.dtype)
        return carry

    lax.fori_loop(0, seq // tq, q_tile, 0)


def _attention(qt, k, vt):
    b, heads, _, s = qt.shape
    tq = min(ATT_TQ, s)
    tk = tq // 2
    kern = functools.partial(_attn_kernel, seq=s, tq=tq, tk=tk, cw=ATT_CW)
    return pl.pallas_call(
        kern,
        grid=(b, heads),
        in_specs=[
            pl.BlockSpec((None, None, QK_PAD, s), lambda bi, hi: (bi, hi, 0, 0)),
            pl.BlockSpec((None, None, s, QK_PAD), lambda bi, hi: (bi, hi, 0, 0)),
            pl.BlockSpec((None, None, V_ROWS, s), lambda bi, hi: (bi, hi, 0, 0)),
        ],
        out_specs=pl.BlockSpec((None, HEAD_V, s), lambda bi, hi: (bi, hi, 0)),
        out_shape=jax.ShapeDtypeStruct((b, heads * HEAD_V, s), BF16),
        scratch_shapes=[pltpu.VMEM((V_ROWS, tq), F32), pltpu.VMEM((1, tq), F32),
                        pltpu.VMEM((tk, tq), F32), pltpu.VMEM((tk, tq), F32),
                        pltpu.VMEM((1, tq), F32), pltpu.VMEM((1, tq), F32)],
        compiler_params=pltpu.CompilerParams(
            dimension_semantics=("parallel", "parallel"), vmem_limit_bytes=VMEM_LIMIT),
        name="attention",
    )(qt, k, vt)


def kernel(x, p, positions, g_ffn1, g_mix, g_ffn2, g_ple, ffn1_w_in, ffn1_w_out, ffn2_w_in, ffn2_w_out, ple_w_proj, ple_w_gate, mla_w_in, mla_g_q_lat, mla_w_uq, mla_g_kv_lat, mla_w_ukv, mla_g_qn, mla_g_kn, mla_w_o, fox_w_in, fox_b_f, fox_g_qn, fox_g_kn, fox_w_o):
    b, s, d = x.shape
    depth = g_ffn1.shape[0]
    assert d == D_MODEL and s % PROJ_TILE == 0 and (b * s) % ROW_TILE == 0
    n = b * s
    bf = lambda w: w.astype(BF16)
    row = lambda g: g.reshape(1, -1)
    col = lambda g: g.reshape(-1, 1)

    pos3 = positions.reshape(b, 1, s)
    half = MLA_ROPE // 2
    inv_freq = (ROPE_THETA ** (-jnp.arange(0, MLA_ROPE, 2, dtype=F32) / MLA_ROPE)).reshape(half, 1)
    lat = MLA_Q_RANK + MLA_KV_RANK

    h = x
    for i in range(depth):
        j = i // N_MIXERS
        h = _ffn(h.reshape(n, d), row(g_ffn1[i]), bf(ffn1_w_in[i]), bf(ffn1_w_out[i])).reshape(b, s, d)
        if i % N_MIXERS == 0:
            qt, k, vt = _mla_proj(
                h, pos3, row(g_mix[i]), bf(mla_w_in[j][:, :lat]), bf(mla_w_in[j][:, lat:].T),
                row(mla_g_q_lat[j]), row(mla_g_kv_lat[j]), bf(mla_w_uq[j].T), bf(mla_w_ukv[j].T),
                col(mla_g_qn[j]), col(mla_g_kn[j]), inv_freq)
            w_o = mla_w_o[j]
        else:
            qt, k, vt = _fox_proj(h, row(g_mix[i]), bf(fox_w_in[j].T), col(fox_b_f[j]),
                                  col(fox_g_qn[j]), col(fox_g_kn[j]))
            w_o = fox_w_o[j]
        ot = _attention(qt, k, vt)
        h = _oproj(h, ot, bf(w_o))
        h = _ffn(h.reshape(n, d), row(g_ffn2[i]), bf(ffn2_w_in[i]), bf(ffn2_w_out[i]))
        h = _ple(h, p[i].reshape(n, PLE_DIM), row(g_ple[i]), bf(ple_w_gate[i]), bf(ple_w_proj[i])).reshape(b, s, d)
    return h
```

```python
import functools
import math

import jax
import jax.numpy as jnp
from jax import lax
from jax.experimental import pallas as pl
from jax.experimental.pallas import tpu as pltpu

F32 = jnp.float32
BF16 = jnp.bfloat16

D_MODEL = 1024
D_FF = 2816
PLE_DIM = 256
N_MIXERS = 2
FFN_HALF = 0.5
HEADS = 16
MLA_Q_RANK = 512
MLA_KV_RANK = 256
MLA_NOPE = 64
MLA_ROPE = 32
MLA_QK = MLA_NOPE + MLA_ROPE
HEAD_V = 64
FOX_HEAD_DIM = 64
ROPE_THETA = 10000.0
EPS = 1e-6
LOG2E = math.log2(math.e)

LANES = 128
MXU_DIM = 256
BF16_SUBLANES = 16
QK_PAD = 128
V_ROWS = HEAD_V + BF16_SUBLANES
VMEM_LIMIT = 52 * 1024 * 1024

ROW_TILE = 256
PROJ_TILE = 512
ATT_TQ = 1024
ATT_CW = MXU_DIM
MASK_VALUE = -jnp.inf
M_INIT = -1e30


def _rms(x, g):
    return x * lax.rsqrt(jnp.mean(x * x, axis=-1, keepdims=True) + EPS) * g


def _resident(shape):
    return pl.BlockSpec(shape, lambda *_: (0,) * len(shape), pipeline_mode=pl.Buffered(1))


def _ffn_kernel(x_ref, g_ref, w_in_ref, w_out_ref, o_ref):
    x = x_ref[...]
    xn = _rms(x, g_ref[...]).astype(BF16)
    gu = jnp.dot(xn, w_in_ref[...], preferred_element_type=F32)
    gate = gu[:, :D_FF]
    up = gu[:, D_FF:]
    act = (gate * jax.nn.sigmoid(gate) * up).astype(BF16)
    y = jnp.dot(act, w_out_ref[...], preferred_element_type=F32)
    o_ref[...] = x + FFN_HALF * y


def _ffn(h2d, g, w_in, w_out):
    n = h2d.shape[0]
    return pl.pallas_call(
        _ffn_kernel,
        grid=(n // ROW_TILE,),
        in_specs=[
            pl.BlockSpec((ROW_TILE, D_MODEL), lambda i: (i, 0)),
            _resident((1, D_MODEL)),
            _resident((D_MODEL, 2 * D_FF)),
            _resident((D_FF, D_MODEL)),
        ],
        out_specs=pl.BlockSpec((ROW_TILE, D_MODEL), lambda i: (i, 0)),
        out_shape=jax.ShapeDtypeStruct((n, D_MODEL), F32),
        compiler_params=pltpu.CompilerParams(
            dimension_semantics=("parallel",), vmem_limit_bytes=VMEM_LIMIT),
        name="ffn",
    )(h2d, g, w_in, w_out)


def _oproj_kernel(x_ref, ot_ref, w_ref, o_ref):
    y = lax.dot_general(ot_ref[...], w_ref[...], (((0,), (0,)), ((), ())),
                        preferred_element_type=F32)
    o_ref[...] = x_ref[...] + y


def _oproj(h, ot, w_o):
    b, s, _ = h.shape
    hd = ot.shape[1]
    return pl.pallas_call(
        _oproj_kernel,
        grid=(b, s // PROJ_TILE),
        in_specs=[
            pl.BlockSpec((None, PROJ_TILE, D_MODEL), lambda bi, i: (bi, i, 0)),
            pl.BlockSpec((None, hd, PROJ_TILE), lambda bi, i: (bi, 0, i)),
            _resident((hd, D_MODEL)),
        ],
        out_specs=pl.BlockSpec((None, PROJ_TILE, D_MODEL), lambda bi, i: (bi, i, 0)),
        out_shape=jax.ShapeDtypeStruct(h.shape, F32),
        compiler_params=pltpu.CompilerParams(
            dimension_semantics=("parallel", "parallel"), vmem_limit_bytes=VMEM_LIMIT),
        name="oproj",
    )(h, ot, w_o)


def _ple_kernel(x_ref, p_ref, g_ref, wg_ref, wp_ref, o_ref):
    x = x_ref[...]
    xn = _rms(x, g_ref[...]).astype(BF16)
    gate = jax.nn.sigmoid(jnp.dot(xn, wg_ref[...], preferred_element_type=F32))
    pp = jnp.dot(p_ref[...].astype(BF16), wp_ref[...], preferred_element_type=F32)
    o_ref[...] = x + gate * pp


def _ple(h2d, p2d, g, w_gate, w_proj):
    n = h2d.shape[0]
    return pl.pallas_call(
        _ple_kernel,
        grid=(n // ROW_TILE,),
        in_specs=[
            pl.BlockSpec((ROW_TILE, D_MODEL), lambda i: (i, 0)),
            pl.BlockSpec((ROW_TILE, PLE_DIM), lambda i: (i, 0)),
            _resident((1, D_MODEL)),
            _resident((D_MODEL, D_MODEL)),
            _resident((PLE_DIM, D_MODEL)),
        ],
        out_specs=pl.BlockSpec((ROW_TILE, D_MODEL), lambda i: (i, 0)),
        out_shape=jax.ShapeDtypeStruct((n, D_MODEL), F32),
        compiler_params=pltpu.CompilerParams(
            dimension_semantics=("parallel",), vmem_limit_bytes=VMEM_LIMIT),
        name="ple",
    )(h2d, p2d, g, w_gate, w_proj)


def _dot_nt(w, x):
    return lax.dot_general(w, x, (((1,), (1,)), ((), ())), preferred_element_type=F32)


def _ones_row_group(width):
    row = lax.broadcasted_iota(jnp.int32, (BF16_SUBLANES, width), 0)
    return jnp.where(row == 0, 1.0, 0.0).astype(F32)


def _rope_rows(x1, x2, cos, sin):
    return x1 * cos - x2 * sin, x1 * sin + x2 * cos


def _mla_proj_kernel(x_ref, pos_ref, g_ref, w_in_ref, w_pe_ref, gq_lat_ref, gkv_lat_ref,
                     w_uq_ref, w_ukv_ref, gqn_ref, gkn_ref, inv_freq_ref,
                     qt_ref, k_ref, vt_ref):
    tm = x_ref.shape[0]
    xn = _rms(x_ref[...], g_ref[...]).astype(BF16)
    z = jnp.dot(xn, w_in_ref[...], preferred_element_type=F32)
    cq = _rms(z[:, :MLA_Q_RANK], gq_lat_ref[...]).astype(BF16)
    ckv = _rms(z[:, MLA_Q_RANK:], gkv_lat_ref[...]).astype(BF16)
    qt_all = _dot_nt(w_uq_ref[...], cq)
    kvt_all = _dot_nt(w_ukv_ref[...], ckv)
    kpe_t = _dot_nt(w_pe_ref[...], xn)

    ang = pos_ref[...].astype(F32) * inv_freq_ref[...]
    cos = jnp.cos(ang)
    sin = jnp.sin(ang)
    half = MLA_ROPE // 2
    q_scale = (MLA_QK ** -0.5) * LOG2E
    gqn = gqn_ref[...] * q_scale
    gkn = gkn_ref[...]
    kpe_sq = jnp.sum(kpe_t * kpe_t, axis=0, keepdims=True)
    ones_rows = _ones_row_group(tm).astype(BF16)
    zeros_tail = jnp.zeros((QK_PAD - MLA_QK, tm), F32)

    for h in range(HEADS):
        qh = qt_all[h * MLA_QK:(h + 1) * MLA_QK]
        qn = qh * lax.rsqrt(jnp.mean(qh * qh, axis=0, keepdims=True) + EPS) * gqn
        q1, q2 = _rope_rows(qn[MLA_NOPE:MLA_NOPE + half], qn[MLA_NOPE + half:], cos, sin)
        qt_ref[h, 0:MLA_NOPE, :] = qn[:MLA_NOPE].astype(BF16)
        qt_ref[h, MLA_NOPE:MLA_NOPE + half, :] = q1.astype(BF16)
        qt_ref[h, MLA_NOPE + half:MLA_QK, :] = q2.astype(BF16)
        qt_ref[h, MLA_QK:QK_PAD, :] = zeros_tail.astype(BF16)

        base = h * (MLA_NOPE + HEAD_V)
        kn = kvt_all[base:base + MLA_NOPE]
        ms = (jnp.sum(kn * kn, axis=0, keepdims=True) + kpe_sq) * (1.0 / MLA_QK)
        r = lax.rsqrt(ms + EPS)
        kn = kn * r * gkn[:MLA_NOPE]
        kp = kpe_t * r * gkn[MLA_NOPE:]
        k1, k2 = _rope_rows(kp[:half], kp[half:], cos, sin)
        kt = jnp.concatenate([kn, k1, k2, zeros_tail], axis=0)
        k_ref[h] = kt.T.astype(BF16)

        vt_ref[h, 0:HEAD_V, :] = kvt_all[base + MLA_NOPE:base + MLA_NOPE + HEAD_V].astype(BF16)
        vt_ref[h, HEAD_V:V_ROWS, :] = ones_rows


def _qkv_out(b, s):
    specs = [
        pl.BlockSpec((None, HEADS, QK_PAD, PROJ_TILE), lambda bi, i: (bi, 0, 0, i)),
        pl.BlockSpec((None, HEADS, PROJ_TILE, QK_PAD), lambda bi, i: (bi, 0, i, 0)),
        pl.BlockSpec((None, HEADS, V_ROWS, PROJ_TILE), lambda bi, i: (bi, 0, 0, i)),
    ]
    shapes = [
        jax.ShapeDtypeStruct((b, HEADS, QK_PAD, s), BF16),
        jax.ShapeDtypeStruct((b, HEADS, s, QK_PAD), BF16),
        jax.ShapeDtypeStruct((b, HEADS, V_ROWS, s), BF16),
    ]
    return specs, shapes


def _mla_proj(h, pos3, g, w_in, w_pe_t, gq_lat, gkv_lat, w_uq_t, w_ukv_t, gqn, gkn, inv_freq):
    b, s, _ = h.shape
    out_specs, out_shapes = _qkv_out(b, s)
    return pl.pallas_call(
        _mla_proj_kernel,
        grid=(b, s // PROJ_TILE),
        in_specs=[
            pl.BlockSpec((None, PROJ_TILE, D_MODEL), lambda bi, i: (bi, i, 0)),
            pl.BlockSpec((None, 1, PROJ_TILE), lambda bi, i: (bi, 0, i)),
            _resident(g.shape), _resident(w_in.shape), _resident(w_pe_t.shape),
            _resident(gq_lat.shape), _resident(gkv_lat.shape),
            _resident(w_uq_t.shape), _resident(w_ukv_t.shape),
            _resident(gqn.shape), _resident(gkn.shape), _resident(inv_freq.shape),
        ],
        out_specs=out_specs,
        out_shape=out_shapes,
        compiler_params=pltpu.CompilerParams(
            dimension_semantics=("parallel", "parallel"), vmem_limit_bytes=VMEM_LIMIT),
        name="mla_proj",
    )(h, pos3, g, w_in, w_pe_t, gq_lat, gkv_lat, w_uq_t, w_ukv_t, gqn, gkn, inv_freq)


def _split3(c):
    hi = c.astype(BF16).astype(F32)
    mid = (c - hi).astype(BF16).astype(F32)
    lo = (c - hi - mid).astype(BF16).astype(F32)
    return hi, mid, lo


def _fox_proj_kernel(x_ref, g_ref, w_ref, bf_ref, gqn_ref, gkn_ref,
                     qt_ref, k_ref, vt_ref, carry_ref):
    tm = x_ref.shape[0]
    hd = HEADS * FOX_HEAD_DIM

    @pl.when(pl.program_id(1) == 0)
    def _():
        carry_ref[...] = jnp.zeros_like(carry_ref)

    xn = _rms(x_ref[...], g_ref[...]).astype(BF16)
    qt_all = _dot_nt(w_ref[0:hd, :], xn)
    kt_all = _dot_nt(w_ref[hd:2 * hd, :], xn)
    vt_all = _dot_nt(w_ref[2 * hd:3 * hd, :], xn)
    f_t = _dot_nt(w_ref[3 * hd:3 * hd + HEADS, :], xn) + bf_ref[...]

    logf = (jnp.minimum(f_t, 0.0) - jnp.log1p(jnp.exp(-jnp.abs(f_t)))) * LOG2E
    hi, mid, lo = _split3(logf)
    parts = jnp.concatenate([hi, mid, lo], axis=0).astype(BF16)
    src = lax.broadcasted_iota(jnp.int32, (tm, tm), 0)
    dst = lax.broadcasted_iota(jnp.int32, (tm, tm), 1)
    upper = jnp.where(src <= dst, 1.0, 0.0).astype(BF16)
    sums = jnp.dot(parts, upper, preferred_element_type=F32)
    local = sums[2 * HEADS:] + sums[HEADS:2 * HEADS] + sums[:HEADS]
    c = local + carry_ref[:, 0:1]
    carry_ref[...] = jnp.broadcast_to(c[:, tm - 1:tm], carry_ref.shape)

    q_scale = (FOX_HEAD_DIM ** -0.5) * LOG2E
    gqn = gqn_ref[...] * q_scale
    gkn = gkn_ref[...]
    row = lax.broadcasted_iota(jnp.int32, (BF16_SUBLANES, tm), 0)
    ones_rows = _ones_row_group(tm).astype(BF16)
    zeros_tail = jnp.zeros((QK_PAD - FOX_HEAD_DIM - BF16_SUBLANES, tm), F32)

    def bias_rows(first, second):
        out = jnp.zeros((BF16_SUBLANES, tm), F32)
        for j in range(3):
            out = jnp.where(row == j, first[j], out)
            out = jnp.where(row == 8 + j, second[j], out)
        return out

    one3 = (1.0, 1.0, 1.0)
    for h in range(HEADS):
        sl = slice(h * FOX_HEAD_DIM, (h + 1) * FOX_HEAD_DIM)
        c3 = _split3(c[h:h + 1])
        qh = qt_all[sl]
        qn = qh * lax.rsqrt(jnp.mean(qh * qh, axis=0, keepdims=True) + EPS) * gqn
        qt_ref[h, 0:FOX_HEAD_DIM, :] = qn.astype(BF16)
        qt_ref[h, FOX_HEAD_DIM:FOX_HEAD_DIM + BF16_SUBLANES, :] = bias_rows(one3, c3).astype(BF16)
        qt_ref[h, FOX_HEAD_DIM + BF16_SUBLANES:QK_PAD, :] = zeros_tail.astype(BF16)

        kh = kt_all[sl]
        kn = kh * lax.rsqrt(jnp.mean(kh * kh, axis=0, keepdims=True) + EPS) * gkn
        neg3 = tuple(-p for p in c3)
        kt = jnp.concatenate([kn, bias_rows(neg3, one3), zeros_tail], axis=0)
        k_ref[h] = kt.T.astype(BF16)

        vt_ref[h, 0:HEAD_V, :] = vt_all[sl].astype(BF16)
        vt_ref[h, HEAD_V:V_ROWS, :] = ones_rows


def _fox_proj(h, g, w_t, b_f, gqn, gkn):
    b, s, _ = h.shape
    out_specs, out_shapes = _qkv_out(b, s)
    return pl.pallas_call(
        _fox_proj_kernel,
        grid=(b, s // PROJ_TILE),
        in_specs=[
            pl.BlockSpec((None, PROJ_TILE, D_MODEL), lambda bi, i: (bi, i, 0)),
            _resident(g.shape), _resident(w_t.shape), _resident(b_f.shape),
            _resident(gqn.shape), _resident(gkn.shape),
        ],
        out_specs=out_specs,
        out_shape=out_shapes,
        scratch_shapes=[pltpu.VMEM((HEADS, LANES), F32)],
        compiler_params=pltpu.CompilerParams(
            dimension_semantics=("arbitrary", "arbitrary"), vmem_limit_bytes=VMEM_LIMIT),
        name="fox_proj",
    )(h, g, w_t, b_f, gqn, gkn)


def _attn_kernel(qt_ref, k_ref, vt_ref, o_ref, acc_ref, m_ref, s0_ref, s1_ref, bm0_ref, bm1_ref,
                 *, seq, tq, tk, cw):
    ncol = tq // cw
    nq = seq // tq
    all_chains = tuple(range(ncol))
    s_refs = (s0_ref, s1_ref)
    bm_refs = (bm0_ref, bm1_ref)

    def stage_a(q0, k0, slot, chains, masked_chains=()):
        kb = k_ref[pl.ds(k0, tk), :]
        for c in chains:
            cols = slice(c * cw, (c + 1) * cw)
            qt = qt_ref[:, pl.ds(q0 + c * cw, cw)]
            s = jnp.dot(kb, qt, preferred_element_type=F32)
            if c in masked_chains:
                kpos = k0 + lax.broadcasted_iota(jnp.int32, (tk, cw), 0)
                qpos = q0 + c * cw + lax.broadcasted_iota(jnp.int32, (tk, cw), 1)
                s = jnp.where(kpos <= qpos, s, MASK_VALUE)
            s_refs[slot][:, cols] = s
            bm_refs[slot][:, cols] = jnp.max(s, axis=0, keepdims=True)

    def stage_b(k0, slot, chains):
        vb = vt_ref[:, pl.ds(k0, tk)]
        for c in chains:
            cols = slice(c * cw, (c + 1) * cw)
            m_old = m_ref[:, cols]
            m_new = jnp.maximum(m_old, bm_refs[slot][:, cols])
            alpha = jnp.exp2(m_old - m_new)
            p = jnp.exp2(s_refs[slot][:, cols] - m_new).astype(BF16)
            pv = jnp.dot(vb, p, preferred_element_type=F32)
            acc_ref[:, cols] = alpha * acc_ref[:, cols] + pv
            m_ref[:, cols] = m_new

    def visible(c, d):
        return d * tk <= (c + 1) * cw - 1

    def needs_mask(c, d):
        return visible(c, d) and (d + 1) * tk - 1 > c * cw

    diag_chains = [tuple(c for c in all_chains if visible(c, d)) for d in (0, 1)]
    diag_masked = [tuple(c for c in all_chains if needs_mask(c, d)) for d in (0, 1)]
    assert diag_chains[0] == all_chains

    def step(q0, k0, rslot, nxt_q0, nxt_k0, a_chains=all_chains, a_masked=(), b_chains=all_chains):
        stage_a(nxt_q0, nxt_k0, 1 - rslot, a_chains, a_masked)
        stage_b(k0, rslot, b_chains)

    def q_tile(qi, carry):
        q0 = pl.multiple_of(qi * tq, tq)
        acc_ref[...] = jnp.zeros_like(acc_ref)
        m_ref[...] = jnp.full_like(m_ref, M_INIT)

        def quad(t, carry2):
            k0 = pl.multiple_of(t * 4 * tk, 4 * tk)
            for j in range(4):
                step(q0, k0 + j * tk, j % 2, q0, k0 + (j + 1) * tk)
            return carry2

        n_quads = lax.shift_right_logical(jnp.maximum(qi - 1, 0), 1)
        lax.fori_loop(0, n_quads, quad, 0)

        @pl.when(jnp.logical_and(qi >= 1, (qi & 1) == 0))
        def _():
            k0 = pl.multiple_of(q0 - 2 * tq, tq)
            step(q0, k0, 0, q0, k0 + tk)
            step(q0, k0 + tk, 1, q0, k0 + 2 * tk)

        @pl.when(qi >= 1)
        def _():
            k0 = pl.multiple_of(q0 - tq, tq)
            step(q0, k0, 0, q0, k0 + tk)
            step(q0, k0 + tk, 1, q0, q0, a_masked=diag_masked[0])

        nxt_q0 = pl.multiple_of(jnp.minimum(q0 + tq, seq - tq), tq)
        step(q0, q0, 0, q0, q0 + tk, a_chains=diag_chains[1], a_masked=diag_masked[1])
        step(q0, q0 + tk, 1, nxt_q0, 0, b_chains=diag_chains[1])

        acc = acc_ref[...]
        o = acc[:HEAD_V] / acc[HEAD_V:HEAD_V + 1]
        o_ref[:, pl.ds(q0, tq)] = o.astype(o_ref.dtype)
        return carry

    stage_a(0, 0, 0, all_chains, diag_masked[0])
    lax.fori_loop(0, nq, q_tile, 0)


def _attention(qt, k, vt):
    b, heads, _, s = qt.shape
    tq = min(ATT_TQ, s)
    tk = tq // 2
    kern = functools.partial(_attn_kernel, seq=s, tq=tq, tk=tk, cw=ATT_CW)
    return pl.pallas_call(
        kern,
        grid=(b, heads),
        in_specs=[
            pl.BlockSpec((None, None, QK_PAD, s), lambda bi, hi: (bi, hi, 0, 0)),
            pl.BlockSpec((None, None, s, QK_PAD), lambda bi, hi: (bi, hi, 0, 0)),
            pl.BlockSpec((None, None, V_ROWS, s), lambda bi, hi: (bi, hi, 0, 0)),
        ],
        out_specs=pl.BlockSpec((None, HEAD_V, s), lambda bi, hi: (bi, hi, 0)),
        out_shape=jax.ShapeDtypeStruct((b, heads * HEAD_V, s), BF16),
        scratch_shapes=[pltpu.VMEM((V_ROWS, tq), F32), pltpu.VMEM((1, tq), F32),
                        pltpu.VMEM((tk, tq), F32), pltpu.VMEM((tk, tq), F32),
                        pltpu.VMEM((1, tq), F32), pltpu.VMEM((1, tq), F32)],
        compiler_params=pltpu.CompilerParams(
            dimension_semantics=("parallel", "parallel"), vmem_limit_bytes=VMEM_LIMIT),
        name="attention",
    )(qt, k, vt)


def kernel(x, p, positions, g_ffn1, g_mix, g_ffn2, g_ple, ffn1_w_in, ffn1_w_out, ffn2_w_in, ffn2_w_out, ple_w_proj, ple_w_gate, mla_w_in, mla_g_q_lat, mla_w_uq, mla_g_kv_lat, mla_w_ukv, mla_g_qn, mla_g_kn, mla_w_o, fox_w_in, fox_b_f, fox_g_qn, fox_g_kn, fox_w_o):
    b, s, d = x.shape
    depth = g_ffn1.shape[0]
    assert d == D_MODEL and s % PROJ_TILE == 0 and (b * s) % ROW_TILE == 0
    n = b * s
    bf = lambda w: w.astype(BF16)
    row = lambda g: g.reshape(1, -1)
    col = lambda g: g.reshape(-1, 1)

    pos3 = positions.reshape(b, 1, s)
    half = MLA_ROPE // 2
    inv_freq = (ROPE_THETA ** (-jnp.arange(0, MLA_ROPE, 2, dtype=F32) / MLA_ROPE)).reshape(half, 1)
    lat = MLA_Q_RANK + MLA_KV_RANK

    h = x
    for i in range(depth):
        j = i // N_MIXERS
        h = _ffn(h.reshape(n, d), row(g_ffn1[i]), bf(ffn1_w_in[i]), bf(ffn1_w_out[i])).reshape(b, s, d)
        if i % N_MIXERS == 0:
            qt, k, vt = _mla_proj(
                h, pos3, row(g_mix[i]), bf(mla_w_in[j][:, :lat]), bf(mla_w_in[j][:, lat:].T),
                row(mla_g_q_lat[j]), row(mla_g_kv_lat[j]), bf(mla_w_uq[j].T), bf(mla_w_ukv[j].T),
                col(mla_g_qn[j]), col(mla_g_kn[j]), inv_freq)
            w_o = mla_w_o[j]
        else:
            qt, k, vt = _fox_proj(h, row(g_mix[i]), bf(fox_w_in[j].T), col(fox_b_f[j]),
                                  col(fox_g_qn[j]), col(fox_g_kn[j]))
            w_o = fox_w_o[j]
        ot = _attention(qt, k, vt)
        h = _oproj(h, ot, bf(w_o))
        h = _ffn(h.reshape(n, d), row(g_ffn2[i]), bf(ffn2_w_in[i]), bf(ffn2_w_out[i]))
        h = _ple(h, p[i].reshape(n, PLE_DIM), row(g_ple[i]), bf(ple_w_gate[i]), bf(ple_w_proj[i])).reshape(b, s, d)
    return h
```

```python
import functools
import math

import jax
import jax.numpy as jnp
from jax import lax
from jax.experimental import pallas as pl
from jax.experimental.pallas import tpu as pltpu

F32 = jnp.float32
BF16 = jnp.bfloat16

D_MODEL = 1024
D_FF = 2816
PLE_DIM = 256
N_MIXERS = 2
FFN_HALF = 0.5
HEADS = 16
MLA_Q_RANK = 512
MLA_KV_RANK = 256
MLA_NOPE = 64
MLA_ROPE = 32
MLA_QK = MLA_NOPE + MLA_ROPE
HEAD_V = 64
FOX_HEAD_DIM = 64
ROPE_THETA = 10000.0
EPS = 1e-6
LOG2E = math.log2(math.e)

LANES = 128
MXU_DIM = 256
BF16_SUBLANES = 16
QK_PAD = 128
V_ROWS = HEAD_V + BF16_SUBLANES
VMEM_LIMIT = 52 * 1024 * 1024

ROW_TILE = 256
PROJ_TILE = 512
ATT_TQ = 1024
ATT_CW = MXU_DIM
QK_ROWS = 128
PV_KEYS = MXU_DIM
MASK_VALUE = -jnp.inf
M_INIT = -1e30


def _rms(x, g):
    return x * lax.rsqrt(jnp.mean(x * x, axis=-1, keepdims=True) + EPS) * g


def _resident(shape):
    return pl.BlockSpec(shape, lambda *_: (0,) * len(shape), pipeline_mode=pl.Buffered(1))


def _ffn_kernel(x_ref, g_ref, w_in_ref, w_out_ref, o_ref):
    x = x_ref[...]
    xn = _rms(x, g_ref[...]).astype(BF16)
    gu = jnp.dot(xn, w_in_ref[...], preferred_element_type=F32)
    gate = gu[:, :D_FF]
    up = gu[:, D_FF:]
    act = (gate * jax.nn.sigmoid(gate) * up).astype(BF16)
    y = jnp.dot(act, w_out_ref[...], preferred_element_type=F32)
    o_ref[...] = x + FFN_HALF * y


def _ffn(h2d, g, w_in, w_out):
    n = h2d.shape[0]
    return pl.pallas_call(
        _ffn_kernel,
        grid=(n // ROW_TILE,),
        in_specs=[
            pl.BlockSpec((ROW_TILE, D_MODEL), lambda i: (i, 0)),
            _resident((1, D_MODEL)),
            _resident((D_MODEL, 2 * D_FF)),
            _resident((D_FF, D_MODEL)),
        ],
        out_specs=pl.BlockSpec((ROW_TILE, D_MODEL), lambda i: (i, 0)),
        out_shape=jax.ShapeDtypeStruct((n, D_MODEL), F32),
        compiler_params=pltpu.CompilerParams(
            dimension_semantics=("parallel",), vmem_limit_bytes=VMEM_LIMIT),
        name="ffn",
    )(h2d, g, w_in, w_out)


def _oproj_kernel(x_ref, ot_ref, w_ref, o_ref):
    y = lax.dot_general(ot_ref[...], w_ref[...], (((0,), (0,)), ((), ())),
                        preferred_element_type=F32)
    o_ref[...] = x_ref[...] + y


def _oproj(h, ot, w_o):
    b, s, _ = h.shape
    hd = ot.shape[1]
    return pl.pallas_call(
        _oproj_kernel,
        grid=(b, s // PROJ_TILE),
        in_specs=[
            pl.BlockSpec((None, PROJ_TILE, D_MODEL), lambda bi, i: (bi, i, 0)),
            pl.BlockSpec((None, hd, PROJ_TILE), lambda bi, i: (bi, 0, i)),
            _resident((hd, D_MODEL)),
        ],
        out_specs=pl.BlockSpec((None, PROJ_TILE, D_MODEL), lambda bi, i: (bi, i, 0)),
        out_shape=jax.ShapeDtypeStruct(h.shape, F32),
        compiler_params=pltpu.CompilerParams(
            dimension_semantics=("parallel", "parallel"), vmem_limit_bytes=VMEM_LIMIT),
        name="oproj",
    )(h, ot, w_o)


def _ple_kernel(x_ref, p_ref, g_ref, wg_ref, wp_ref, o_ref):
    x = x_ref[...]
    xn = _rms(x, g_ref[...]).astype(BF16)
    gate = jax.nn.sigmoid(jnp.dot(xn, wg_ref[...], preferred_element_type=F32))
    pp = jnp.dot(p_ref[...].astype(BF16), wp_ref[...], preferred_element_type=F32)
    o_ref[...] = x + gate * pp


def _ple(h2d, p2d, g, w_gate, w_proj):
    n = h2d.shape[0]
    return pl.pallas_call(
        _ple_kernel,
        grid=(n // ROW_TILE,),
        in_specs=[
            pl.BlockSpec((ROW_TILE, D_MODEL), lambda i: (i, 0)),
            pl.BlockSpec((ROW_TILE, PLE_DIM), lambda i: (i, 0)),
            _resident((1, D_MODEL)),
            _resident((D_MODEL, D_MODEL)),
            _resident((PLE_DIM, D_MODEL)),
        ],
        out_specs=pl.BlockSpec((ROW_TILE, D_MODEL), lambda i: (i, 0)),
        out_shape=jax.ShapeDtypeStruct((n, D_MODEL), F32),
        compiler_params=pltpu.CompilerParams(
            dimension_semantics=("parallel",), vmem_limit_bytes=VMEM_LIMIT),
        name="ple",
    )(h2d, p2d, g, w_gate, w_proj)


def _dot_nt(w, x):
    return lax.dot_general(w, x, (((1,), (1,)), ((), ())), preferred_element_type=F32)


def _ones_row_group(width):
    row = lax.broadcasted_iota(jnp.int32, (BF16_SUBLANES, width), 0)
    return jnp.where(row == 0, 1.0, 0.0).astype(F32)


def _rope_rows(x1, x2, cos, sin):
    return x1 * cos - x2 * sin, x1 * sin + x2 * cos


def _mla_proj_kernel(x_ref, pos_ref, g_ref, w_in_ref, w_pe_ref, gq_lat_ref, gkv_lat_ref,
                     w_uq_ref, w_ukv_ref, gqn_ref, gkn_ref, inv_freq_ref,
                     qt_ref, k_ref, vt_ref):
    tm = x_ref.shape[0]
    xn = _rms(x_ref[...], g_ref[...]).astype(BF16)
    z = jnp.dot(xn, w_in_ref[...], preferred_element_type=F32)
    cq = _rms(z[:, :MLA_Q_RANK], gq_lat_ref[...]).astype(BF16)
    ckv = _rms(z[:, MLA_Q_RANK:], gkv_lat_ref[...]).astype(BF16)
    qt_all = _dot_nt(w_uq_ref[...], cq)
    kvt_all = _dot_nt(w_ukv_ref[...], ckv)
    kpe_t = _dot_nt(w_pe_ref[...], xn)

    ang = pos_ref[...].astype(F32) * inv_freq_ref[...]
    cos = jnp.cos(ang)
    sin = jnp.sin(ang)
    half = MLA_ROPE // 2
    q_scale = (MLA_QK ** -0.5) * LOG2E
    gqn = gqn_ref[...] * q_scale
    gkn = gkn_ref[...]
    kpe_sq = jnp.sum(kpe_t * kpe_t, axis=0, keepdims=True)
    ones_rows = _ones_row_group(tm).astype(BF16)
    zeros_tail = jnp.zeros((QK_PAD - MLA_QK, tm), F32)

    for h in range(HEADS):
        qh = qt_all[h * MLA_QK:(h + 1) * MLA_QK]
        qn = qh * lax.rsqrt(jnp.mean(qh * qh, axis=0, keepdims=True) + EPS) * gqn
        q1, q2 = _rope_rows(qn[MLA_NOPE:MLA_NOPE + half], qn[MLA_NOPE + half:], cos, sin)
        qt_ref[h, 0:MLA_NOPE, :] = qn[:MLA_NOPE].astype(BF16)
        qt_ref[h, MLA_NOPE:MLA_NOPE + half, :] = q1.astype(BF16)
        qt_ref[h, MLA_NOPE + half:MLA_QK, :] = q2.astype(BF16)
        qt_ref[h, MLA_QK:QK_PAD, :] = zeros_tail.astype(BF16)

        base = h * (MLA_NOPE + HEAD_V)
        kn = kvt_all[base:base + MLA_NOPE]
        ms = (jnp.sum(kn * kn, axis=0, keepdims=True) + kpe_sq) * (1.0 / MLA_QK)
        r = lax.rsqrt(ms + EPS)
        kn = kn * r * gkn[:MLA_NOPE]
        kp = kpe_t * r * gkn[MLA_NOPE:]
        k1, k2 = _rope_rows(kp[:half], kp[half:], cos, sin)
        kt = jnp.concatenate([kn, k1, k2, zeros_tail], axis=0)
        k_ref[h] = kt.T.astype(BF16)

        vt_ref[h, 0:HEAD_V, :] = kvt_all[base + MLA_NOPE:base + MLA_NOPE + HEAD_V].astype(BF16)
        vt_ref[h, HEAD_V:V_ROWS, :] = ones_rows


def _qkv_out(b, s):
    specs = [
        pl.BlockSpec((None, HEADS, QK_PAD, PROJ_TILE), lambda bi, i: (bi, 0, 0, i)),
        pl.BlockSpec((None, HEADS, PROJ_TILE, QK_PAD), lambda bi, i: (bi, 0, i, 0)),
        pl.BlockSpec((None, HEADS, V_ROWS, PROJ_TILE), lambda bi, i: (bi, 0, 0, i)),
    ]
    shapes = [
        jax.ShapeDtypeStruct((b, HEADS, QK_PAD, s), BF16),
        jax.ShapeDtypeStruct((b, HEADS, s, QK_PAD), BF16),
        jax.ShapeDtypeStruct((b, HEADS, V_ROWS, s), BF16),
    ]
    return specs, shapes


def _mla_proj(h, pos3, g, w_in, w_pe_t, gq_lat, gkv_lat, w_uq_t, w_ukv_t, gqn, gkn, inv_freq):
    b, s, _ = h.shape
    out_specs, out_shapes = _qkv_out(b, s)
    return pl.pallas_call(
        _mla_proj_kernel,
        grid=(b, s // PROJ_TILE),
        in_specs=[
            pl.BlockSpec((None, PROJ_TILE, D_MODEL), lambda bi, i: (bi, i, 0)),
            pl.BlockSpec((None, 1, PROJ_TILE), lambda bi, i: (bi, 0, i)),
            _resident(g.shape), _resident(w_in.shape), _resident(w_pe_t.shape),
            _resident(gq_lat.shape), _resident(gkv_lat.shape),
            _resident(w_uq_t.shape), _resident(w_ukv_t.shape),
            _resident(gqn.shape), _resident(gkn.shape), _resident(inv_freq.shape),
        ],
        out_specs=out_specs,
        out_shape=out_shapes,
        compiler_params=pltpu.CompilerParams(
            dimension_semantics=("parallel", "parallel"), vmem_limit_bytes=VMEM_LIMIT),
        name="mla_proj",
    )(h, pos3, g, w_in, w_pe_t, gq_lat, gkv_lat, w_uq_t, w_ukv_t, gqn, gkn, inv_freq)


def _split3(c):
    hi = c.astype(BF16).astype(F32)
    mid = (c - hi).astype(BF16).astype(F32)
    lo = (c - hi - mid).astype(BF16).astype(F32)
    return hi, mid, lo


def _fox_proj_kernel(x_ref, g_ref, w_ref, bf_ref, gqn_ref, gkn_ref,
                     qt_ref, k_ref, vt_ref, carry_ref):
    tm = x_ref.shape[0]
    hd = HEADS * FOX_HEAD_DIM

    @pl.when(pl.program_id(1) == 0)
    def _():
        carry_ref[...] = jnp.zeros_like(carry_ref)

    xn = _rms(x_ref[...], g_ref[...]).astype(BF16)
    qt_all = _dot_nt(w_ref[0:hd, :], xn)
    kt_all = _dot_nt(w_ref[hd:2 * hd, :], xn)
    vt_all = _dot_nt(w_ref[2 * hd:3 * hd, :], xn)
    f_t = _dot_nt(w_ref[3 * hd:3 * hd + HEADS, :], xn) + bf_ref[...]

    logf = (jnp.minimum(f_t, 0.0) - jnp.log1p(jnp.exp(-jnp.abs(f_t)))) * LOG2E
    hi, mid, lo = _split3(logf)
    parts = jnp.concatenate([hi, mid, lo], axis=0).astype(BF16)
    src = lax.broadcasted_iota(jnp.int32, (tm, tm), 0)
    dst = lax.broadcasted_iota(jnp.int32, (tm, tm), 1)
    upper = jnp.where(src <= dst, 1.0, 0.0).astype(BF16)
    sums = jnp.dot(parts, upper, preferred_element_type=F32)
    local = sums[2 * HEADS:] + sums[HEADS:2 * HEADS] + sums[:HEADS]
    c = local + carry_ref[:, 0:1]
    carry_ref[...] = jnp.broadcast_to(c[:, tm - 1:tm], carry_ref.shape)

    q_scale = (FOX_HEAD_DIM ** -0.5) * LOG2E
    gqn = gqn_ref[...] * q_scale
    gkn = gkn_ref[...]
    row = lax.broadcasted_iota(jnp.int32, (BF16_SUBLANES, tm), 0)
    ones_rows = _ones_row_group(tm).astype(BF16)
    zeros_tail = jnp.zeros((QK_PAD - FOX_HEAD_DIM - BF16_SUBLANES, tm), F32)

    def bias_rows(first, second):
        out = jnp.zeros((BF16_SUBLANES, tm), F32)
        for j in range(3):
            out = jnp.where(row == j, first[j], out)
            out = jnp.where(row == 8 + j, second[j], out)
        return out

    one3 = (1.0, 1.0, 1.0)
    for h in range(HEADS):
        sl = slice(h * FOX_HEAD_DIM, (h + 1) * FOX_HEAD_DIM)
        c3 = _split3(c[h:h + 1])
        qh = qt_all[sl]
        qn = qh * lax.rsqrt(jnp.mean(qh * qh, axis=0, keepdims=True) + EPS) * gqn
        qt_ref[h, 0:FOX_HEAD_DIM, :] = qn.astype(BF16)
        qt_ref[h, FOX_HEAD_DIM:FOX_HEAD_DIM + BF16_SUBLANES, :] = bias_rows(one3, c3).astype(BF16)
        qt_ref[h, FOX_HEAD_DIM + BF16_SUBLANES:QK_PAD, :] = zeros_tail.astype(BF16)

        kh = kt_all[sl]
        kn = kh * lax.rsqrt(jnp.mean(kh * kh, axis=0, keepdims=True) + EPS) * gkn
        neg3 = tuple(-p for p in c3)
        kt = jnp.concatenate([kn, bias_rows(neg3, one3), zeros_tail], axis=0)
        k_ref[h] = kt.T.astype(BF16)

        vt_ref[h, 0:HEAD_V, :] = vt_all[sl].astype(BF16)
        vt_ref[h, HEAD_V:V_ROWS, :] = ones_rows


def _fox_proj(h, g, w_t, b_f, gqn, gkn):
    b, s, _ = h.shape
    out_specs, out_shapes = _qkv_out(b, s)
    return pl.pallas_call(
        _fox_proj_kernel,
        grid=(b, s // PROJ_TILE),
        in_specs=[
            pl.BlockSpec((None, PROJ_TILE, D_MODEL), lambda bi, i: (bi, i, 0)),
            _resident(g.shape), _resident(w_t.shape), _resident(b_f.shape),
            _resident(gqn.shape), _resident(gkn.shape),
        ],
        out_specs=out_specs,
        out_shape=out_shapes,
        scratch_shapes=[pltpu.VMEM((HEADS, LANES), F32)],
        compiler_params=pltpu.CompilerParams(
            dimension_semantics=("arbitrary", "arbitrary"), vmem_limit_bytes=VMEM_LIMIT),
        name="fox_proj",
    )(h, g, w_t, b_f, gqn, gkn)


def _attn_kernel(qt_ref, k_ref, vt_ref, o_ref, acc_ref, m_ref, s0_ref, s1_ref, bm0_ref, bm1_ref,
                 *, seq, tq, tk, cw):
    ncol = tq // cw
    nq = seq // tq
    all_chains = tuple(range(ncol))
    s_refs = (s0_ref, s1_ref)
    bm_refs = (bm0_ref, bm1_ref)

    n_parts = tk // PV_KEYS

    def stage_a_parts(q0, k0, slot, c, masked):
        cols = slice(c * cw, (c + 1) * cw)
        qt = qt_ref[:, pl.ds(q0 + c * cw, cw)]
        bm = None
        for r in range(n_parts):
            rows = slice(r * PV_KEYS, (r + 1) * PV_KEYS)
            kb = k_ref[pl.ds(k0 + r * PV_KEYS, PV_KEYS), :]
            s = jnp.dot(kb, qt, preferred_element_type=F32)
            if masked:
                kpos = k0 + r * PV_KEYS + lax.broadcasted_iota(jnp.int32, (PV_KEYS, cw), 0)
                qpos = q0 + c * cw + lax.broadcasted_iota(jnp.int32, (PV_KEYS, cw), 1)
                s = jnp.where(kpos <= qpos, s, MASK_VALUE)
            s_refs[slot][c, rows, :] = s
            pm = jnp.max(s, axis=0, keepdims=True)
            bm = pm if bm is None else jnp.maximum(bm, pm)
            if r == n_parts - 1:
                bm_refs[slot][:, cols] = bm
            yield

    def stage_b_parts(k0, slot, c):
        cols = slice(c * cw, (c + 1) * cw)
        m_old = m_ref[:, cols]
        m_new = jnp.maximum(m_old, bm_refs[slot][:, cols])
        alpha = jnp.exp2(m_old - m_new)
        pv = None
        for r in range(n_parts):
            rows = slice(r * PV_KEYS, (r + 1) * PV_KEYS)
            p = jnp.exp2(s_refs[slot][c, rows, :] - m_new).astype(BF16)
            vb = vt_ref[:, pl.ds(k0 + r * PV_KEYS, PV_KEYS)]
            t = jnp.dot(vb, p, preferred_element_type=F32)
            pv = t if pv is None else pv + t
            if r == n_parts - 1:
                acc_ref[c] = alpha * acc_ref[c] + pv
                m_ref[:, cols] = m_new
            yield

    def stage_a(q0, k0, slot, chains, masked_chains=()):
        for c in chains:
            for _ in stage_a_parts(q0, k0, slot, c, c in masked_chains):
                pass

    def visible(c, d):
        return d * tk <= (c + 1) * cw - 1

    def needs_mask(c, d):
        return visible(c, d) and (d + 1) * tk - 1 > c * cw

    diag_chains = [tuple(c for c in all_chains if visible(c, d)) for d in (0, 1)]
    diag_masked = [tuple(c for c in all_chains if needs_mask(c, d)) for d in (0, 1)]
    assert diag_chains[0] == all_chains

    def step(q0, k0, rslot, nxt_q0, nxt_k0, a_chains=all_chains, a_masked=(), b_chains=all_chains):
        for c in all_chains:
            gens = []
            if c in a_chains:
                gens.append(stage_a_parts(nxt_q0, nxt_k0, 1 - rslot, c, c in a_masked))
            if c in b_chains:
                gens.append(stage_b_parts(k0, rslot, c))
            for _ in range(n_parts):
                for g in gens:
                    next(g)

    def q_tile(qi, carry):
        q0 = pl.multiple_of(qi * tq, tq)
        acc_ref[...] = jnp.zeros_like(acc_ref)
        m_ref[...] = jnp.full_like(m_ref, M_INIT)

        def quad(t, carry2):
            k0 = pl.multiple_of(t * 4 * tk, 4 * tk)
            for j in range(4):
                step(q0, k0 + j * tk, j % 2, q0, k0 + (j + 1) * tk)
            return carry2

        n_quads = lax.shift_right_logical(jnp.maximum(qi - 1, 0), 1)
        lax.fori_loop(0, n_quads, quad, 0)

        @pl.when(jnp.logical_and(qi >= 1, (qi & 1) == 0))
        def _():
            k0 = pl.multiple_of(q0 - 2 * tq, tq)
            step(q0, k0, 0, q0, k0 + tk)
            step(q0, k0 + tk, 1, q0, k0 + 2 * tk)

        @pl.when(qi >= 1)
        def _():
            k0 = pl.multiple_of(q0 - tq, tq)
            step(q0, k0, 0, q0, k0 + tk)
            step(q0, k0 + tk, 1, q0, q0, a_masked=diag_masked[0])

        nxt_q0 = pl.multiple_of(jnp.minimum(q0 + tq, seq - tq), tq)
        step(q0, q0, 0, q0, q0 + tk, a_chains=diag_chains[1], a_masked=diag_masked[1])
        step(q0, q0 + tk, 1, nxt_q0, 0, b_chains=diag_chains[1])

        for c in all_chains:
            acc = acc_ref[c]
            o = acc[:HEAD_V] / acc[HEAD_V:HEAD_V + 1]
            o_ref[:, pl.ds(q0 + c * cw, cw)] = o.astype(o_ref.dtype)
        return carry

    stage_a(0, 0, 0, all_chains, diag_masked[0])
    lax.fori_loop(0, nq, q_tile, 0)


def _attention(qt, k, vt):
    b, heads, _, s = qt.shape
    tq = min(ATT_TQ, s)
    tk = tq // 2
    kern = functools.partial(_attn_kernel, seq=s, tq=tq, tk=tk, cw=ATT_CW)
    return pl.pallas_call(
        kern,
        grid=(b, heads),
        in_specs=[
            pl.BlockSpec((None, None, QK_PAD, s), lambda bi, hi: (bi, hi, 0, 0)),
            pl.BlockSpec((None, None, s, QK_PAD), lambda bi, hi: (bi, hi, 0, 0)),
            pl.BlockSpec((None, None, V_ROWS, s), lambda bi, hi: (bi, hi, 0, 0)),
        ],
        out_specs=pl.BlockSpec((None, HEAD_V, s), lambda bi, hi: (bi, hi, 0)),
        out_shape=jax.ShapeDtypeStruct((b, heads * HEAD_V, s), BF16),
        scratch_shapes=[pltpu.VMEM((tq // ATT_CW, V_ROWS, ATT_CW), F32), pltpu.VMEM((1, tq), F32),
                        pltpu.VMEM((tq // ATT_CW, tk, ATT_CW), F32),
                        pltpu.VMEM((tq // ATT_CW, tk, ATT_CW), F32),
                        pltpu.VMEM((1, tq), F32), pltpu.VMEM((1, tq), F32)],
        compiler_params=pltpu.CompilerParams(
            dimension_semantics=("parallel", "parallel"), vmem_limit_bytes=VMEM_LIMIT),
        name="attention",
    )(qt, k, vt)


def kernel(x, p, positions, g_ffn1, g_mix, g_ffn2, g_ple, ffn1_w_in, ffn1_w_out, ffn2_w_in, ffn2_w_out, ple_w_proj, ple_w_gate, mla_w_in, mla_g_q_lat, mla_w_uq, mla_g_kv_lat, mla_w_ukv, mla_g_qn, mla_g_kn, mla_w_o, fox_w_in, fox_b_f, fox_g_qn, fox_g_kn, fox_w_o):
    b, s, d = x.shape
    depth = g_ffn1.shape[0]
    assert d == D_MODEL and s % PROJ_TILE == 0 and (b * s) % ROW_TILE == 0
    n = b * s
    bf = lambda w: w.astype(BF16)
    row = lambda g: g.reshape(1, -1)
    col = lambda g: g.reshape(-1, 1)

    pos3 = positions.reshape(b, 1, s)
    half = MLA_ROPE // 2
    inv_freq = (ROPE_THETA ** (-jnp.arange(0, MLA_ROPE, 2, dtype=F32) / MLA_ROPE)).reshape(half, 1)
    lat = MLA_Q_RANK + MLA_KV_RANK

    h = x
    for i in range(depth):
        j = i // N_MIXERS
        h = _ffn(h.reshape(n, d), row(g_ffn1[i]), bf(ffn1_w_in[i]), bf(ffn1_w_out[i])).reshape(b, s, d)
        if i % N_MIXERS == 0:
            qt, k, vt = _mla_proj(
                h, pos3, row(g_mix[i]), bf(mla_w_in[j][:, :lat]), bf(mla_w_in[j][:, lat:].T),
                row(mla_g_q_lat[j]), row(mla_g_kv_lat[j]), bf(mla_w_uq[j].T), bf(mla_w_ukv[j].T),
                col(mla_g_qn[j]), col(mla_g_kn[j]), inv_freq)
            w_o = mla_w_o[j]
        else:
            qt, k, vt = _fox_proj(h, row(g_mix[i]), bf(fox_w_in[j].T), col(fox_b_f[j]),
                                  col(fox_g_qn[j]), col(fox_g_kn[j]))
            w_o = fox_w_o[j]
        ot = _attention(qt, k, vt)
        h = _oproj(h, ot, bf(w_o))
        h = _ffn(h.reshape(n, d), row(g_ffn2[i]), bf(ffn2_w_in[i]), bf(ffn2_w_out[i]))
        h = _ple(h, p[i].reshape(n, PLE_DIM), row(g_ple[i]), bf(ple_w_gate[i]), bf(ple_w_proj[i])).reshape(b, s, d)
    return h
```

```python
import functools
import math

import jax
import jax.numpy as jnp
from jax import lax
from jax.experimental import pallas as pl
from jax.experimental.pallas import tpu as pltpu

F32 = jnp.float32
BF16 = jnp.bfloat16

D_MODEL = 1024
D_FF = 2816
PLE_DIM = 256
N_MIXERS = 2
FFN_HALF = 0.5
HEADS = 16
MLA_Q_RANK = 512
MLA_KV_RANK = 256
MLA_NOPE = 64
MLA_ROPE = 32
MLA_QK = MLA_NOPE + MLA_ROPE
HEAD_V = 64
FOX_HEAD_DIM = 64
ROPE_THETA = 10000.0
EPS = 1e-6
LOG2E = math.log2(math.e)

LANES = 128
MXU_DIM = 256
BF16_SUBLANES = 16
QK_PAD = 128
V_ROWS = HEAD_V + BF16_SUBLANES
VMEM_LIMIT = 52 * 1024 * 1024

ROW_TILE = 256
PROJ_TILE = 512
ATT_TQ = 1024
ATT_CW = MXU_DIM
MAIN_UNROLL = 8
PV_KEYS = MXU_DIM
MASK_VALUE = -jnp.inf
M_INIT = -1e30


def _rms(x, g):
    return x * lax.rsqrt(jnp.mean(x * x, axis=-1, keepdims=True) + EPS) * g


def _resident(shape):
    return pl.BlockSpec(shape, lambda *_: (0,) * len(shape), pipeline_mode=pl.Buffered(1))


def _ffn_kernel(x_ref, g_ref, w_in_ref, w_out_ref, o_ref):
    x = x_ref[...]
    xn = _rms(x, g_ref[...]).astype(BF16)
    gu = jnp.dot(xn, w_in_ref[...], preferred_element_type=F32)
    gate = gu[:, :D_FF]
    up = gu[:, D_FF:]
    act = (gate * jax.nn.sigmoid(gate) * up).astype(BF16)
    y = jnp.dot(act, w_out_ref[...], preferred_element_type=F32)
    o_ref[...] = x + FFN_HALF * y


def _ffn(h2d, g, w_in, w_out):
    n = h2d.shape[0]
    return pl.pallas_call(
        _ffn_kernel,
        grid=(n // ROW_TILE,),
        in_specs=[
            pl.BlockSpec((ROW_TILE, D_MODEL), lambda i: (i, 0)),
            _resident((1, D_MODEL)),
            _resident((D_MODEL, 2 * D_FF)),
            _resident((D_FF, D_MODEL)),
        ],
        out_specs=pl.BlockSpec((ROW_TILE, D_MODEL), lambda i: (i, 0)),
        out_shape=jax.ShapeDtypeStruct((n, D_MODEL), F32),
        compiler_params=pltpu.CompilerParams(
            dimension_semantics=("parallel",), vmem_limit_bytes=VMEM_LIMIT),
        name="ffn",
    )(h2d, g, w_in, w_out)


def _oproj_kernel(x_ref, ot_ref, w_ref, o_ref):
    y = lax.dot_general(ot_ref[...], w_ref[...], (((0,), (0,)), ((), ())),
                        preferred_element_type=F32)
    o_ref[...] = x_ref[...] + y


def _oproj(h, ot, w_o):
    b, s, _ = h.shape
    hd = ot.shape[1]
    return pl.pallas_call(
        _oproj_kernel,
        grid=(b, s // PROJ_TILE),
        in_specs=[
            pl.BlockSpec((None, PROJ_TILE, D_MODEL), lambda bi, i: (bi, i, 0)),
            pl.BlockSpec((None, hd, PROJ_TILE), lambda bi, i: (bi, 0, i)),
            _resident((hd, D_MODEL)),
        ],
        out_specs=pl.BlockSpec((None, PROJ_TILE, D_MODEL), lambda bi, i: (bi, i, 0)),
        out_shape=jax.ShapeDtypeStruct(h.shape, F32),
        compiler_params=pltpu.CompilerParams(
            dimension_semantics=("parallel", "parallel"), vmem_limit_bytes=VMEM_LIMIT),
        name="oproj",
    )(h, ot, w_o)


def _ple_kernel(x_ref, p_ref, g_ref, wg_ref, wp_ref, o_ref):
    x = x_ref[...]
    xn = _rms(x, g_ref[...]).astype(BF16)
    gate = jax.nn.sigmoid(jnp.dot(xn, wg_ref[...], preferred_element_type=F32))
    pp = jnp.dot(p_ref[...].astype(BF16), wp_ref[...], preferred_element_type=F32)
    o_ref[...] = x + gate * pp


def _ple(h2d, p2d, g, w_gate, w_proj):
    n = h2d.shape[0]
    return pl.pallas_call(
        _ple_kernel,
        grid=(n // ROW_TILE,),
        in_specs=[
            pl.BlockSpec((ROW_TILE, D_MODEL), lambda i: (i, 0)),
            pl.BlockSpec((ROW_TILE, PLE_DIM), lambda i: (i, 0)),
            _resident((1, D_MODEL)),
            _resident((D_MODEL, D_MODEL)),
            _resident((PLE_DIM, D_MODEL)),
        ],
        out_specs=pl.BlockSpec((ROW_TILE, D_MODEL), lambda i: (i, 0)),
        out_shape=jax.ShapeDtypeStruct((n, D_MODEL), F32),
        compiler_params=pltpu.CompilerParams(
            dimension_semantics=("parallel",), vmem_limit_bytes=VMEM_LIMIT),
        name="ple",
    )(h2d, p2d, g, w_gate, w_proj)


def _dot_nt(w, x):
    return lax.dot_general(w, x, (((1,), (1,)), ((), ())), preferred_element_type=F32)


def _ones_row_group(width):
    row = lax.broadcasted_iota(jnp.int32, (BF16_SUBLANES, width), 0)
    return jnp.where(row == 0, 1.0, 0.0).astype(F32)


def _rope_rows(x1, x2, cos, sin):
    return x1 * cos - x2 * sin, x1 * sin + x2 * cos


def _mla_proj_kernel(x_ref, pos_ref, g_ref, w_in_ref, w_pe_ref, gq_lat_ref, gkv_lat_ref,
                     w_uq_ref, w_ukv_ref, gqn_ref, gkn_ref, inv_freq_ref,
                     qt_ref, k_ref, vt_ref):
    tm = x_ref.shape[0]
    xn = _rms(x_ref[...], g_ref[...]).astype(BF16)
    z = jnp.dot(xn, w_in_ref[...], preferred_element_type=F32)
    cq = _rms(z[:, :MLA_Q_RANK], gq_lat_ref[...]).astype(BF16)
    ckv = _rms(z[:, MLA_Q_RANK:], gkv_lat_ref[...]).astype(BF16)
    qt_all = _dot_nt(w_uq_ref[...], cq)
    kvt_all = _dot_nt(w_ukv_ref[...], ckv)
    kpe_t = _dot_nt(w_pe_ref[...], xn)

    ang = pos_ref[...].astype(F32) * inv_freq_ref[...]
    cos = jnp.cos(ang)
    sin = jnp.sin(ang)
    half = MLA_ROPE // 2
    q_scale = (MLA_QK ** -0.5) * LOG2E
    gqn = gqn_ref[...] * q_scale
    gkn = gkn_ref[...]
    kpe_sq = jnp.sum(kpe_t * kpe_t, axis=0, keepdims=True)
    ones_rows = _ones_row_group(tm).astype(BF16)
    zeros_tail = jnp.zeros((QK_PAD - MLA_QK, tm), F32)

    for h in range(HEADS):
        qh = qt_all[h * MLA_QK:(h + 1) * MLA_QK]
        qn = qh * lax.rsqrt(jnp.mean(qh * qh, axis=0, keepdims=True) + EPS) * gqn
        q1, q2 = _rope_rows(qn[MLA_NOPE:MLA_NOPE + half], qn[MLA_NOPE + half:], cos, sin)
        qt_ref[h, 0:MLA_NOPE, :] = qn[:MLA_NOPE].astype(BF16)
        qt_ref[h, MLA_NOPE:MLA_NOPE + half, :] = q1.astype(BF16)
        qt_ref[h, MLA_NOPE + half:MLA_QK, :] = q2.astype(BF16)
        qt_ref[h, MLA_QK:QK_PAD, :] = zeros_tail.astype(BF16)

        base = h * (MLA_NOPE + HEAD_V)
        kn = kvt_all[base:base + MLA_NOPE]
        ms = (jnp.sum(kn * kn, axis=0, keepdims=True) + kpe_sq) * (1.0 / MLA_QK)
        r = lax.rsqrt(ms + EPS)
        kn = kn * r * gkn[:MLA_NOPE]
        kp = kpe_t * r * gkn[MLA_NOPE:]
        k1, k2 = _rope_rows(kp[:half], kp[half:], cos, sin)
        kt = jnp.concatenate([kn, k1, k2, zeros_tail], axis=0)
        k_ref[h] = kt.T.astype(BF16)

        vt_ref[h, 0:HEAD_V, :] = kvt_all[base + MLA_NOPE:base + MLA_NOPE + HEAD_V].astype(BF16)
        vt_ref[h, HEAD_V:V_ROWS, :] = ones_rows


def _qkv_out(b, s):
    specs = [
        pl.BlockSpec((None, HEADS, QK_PAD, PROJ_TILE), lambda bi, i: (bi, 0, 0, i)),
        pl.BlockSpec((None, HEADS, PROJ_TILE, QK_PAD), lambda bi, i: (bi, 0, i, 0)),
        pl.BlockSpec((None, HEADS, V_ROWS, PROJ_TILE), lambda bi, i: (bi, 0, 0, i)),
    ]
    shapes = [
        jax.ShapeDtypeStruct((b, HEADS, QK_PAD, s), BF16),
        jax.ShapeDtypeStruct((b, HEADS, s, QK_PAD), BF16),
        jax.ShapeDtypeStruct((b, HEADS, V_ROWS, s), BF16),
    ]
    return specs, shapes


def _mla_proj(h, pos3, g, w_in, w_pe_t, gq_lat, gkv_lat, w_uq_t, w_ukv_t, gqn, gkn, inv_freq):
    b, s, _ = h.shape
    out_specs, out_shapes = _qkv_out(b, s)
    return pl.pallas_call(
        _mla_proj_kernel,
        grid=(b, s // PROJ_TILE),
        in_specs=[
            pl.BlockSpec((None, PROJ_TILE, D_MODEL), lambda bi, i: (bi, i, 0)),
            pl.BlockSpec((None, 1, PROJ_TILE), lambda bi, i: (bi, 0, i)),
            _resident(g.shape), _resident(w_in.shape), _resident(w_pe_t.shape),
            _resident(gq_lat.shape), _resident(gkv_lat.shape),
            _resident(w_uq_t.shape), _resident(w_ukv_t.shape),
            _resident(gqn.shape), _resident(gkn.shape), _resident(inv_freq.shape),
        ],
        out_specs=out_specs,
        out_shape=out_shapes,
        compiler_params=pltpu.CompilerParams(
            dimension_semantics=("parallel", "parallel"), vmem_limit_bytes=VMEM_LIMIT),
        name="mla_proj",
    )(h, pos3, g, w_in, w_pe_t, gq_lat, gkv_lat, w_uq_t, w_ukv_t, gqn, gkn, inv_freq)


def _split3(c):
    hi = c.astype(BF16).astype(F32)
    mid = (c - hi).astype(BF16).astype(F32)
    lo = (c - hi - mid).astype(BF16).astype(F32)
    return hi, mid, lo


def _fox_proj_kernel(x_ref, g_ref, w_ref, bf_ref, gqn_ref, gkn_ref,
                     qt_ref, k_ref, vt_ref, carry_ref):
    tm = x_ref.shape[0]
    hd = HEADS * FOX_HEAD_DIM

    @pl.when(pl.program_id(1) == 0)
    def _():
        carry_ref[...] = jnp.zeros_like(carry_ref)

    xn = _rms(x_ref[...], g_ref[...]).astype(BF16)
    qt_all = _dot_nt(w_ref[0:hd, :], xn)
    kt_all = _dot_nt(w_ref[hd:2 * hd, :], xn)
    vt_all = _dot_nt(w_ref[2 * hd:3 * hd, :], xn)
    f_t = _dot_nt(w_ref[3 * hd:3 * hd + HEADS, :], xn) + bf_ref[...]

    logf = (jnp.minimum(f_t, 0.0) - jnp.log1p(jnp.exp(-jnp.abs(f_t)))) * LOG2E
    hi, mid, lo = _split3(logf)
    parts = jnp.concatenate([hi, mid, lo], axis=0).astype(BF16)
    src = lax.broadcasted_iota(jnp.int32, (tm, tm), 0)
    dst = lax.broadcasted_iota(jnp.int32, (tm, tm), 1)
    upper = jnp.where(src <= dst, 1.0, 0.0).astype(BF16)
    sums = jnp.dot(parts, upper, preferred_element_type=F32)
    local = sums[2 * HEADS:] + sums[HEADS:2 * HEADS] + sums[:HEADS]
    c = local + carry_ref[:, 0:1]
    carry_ref[...] = jnp.broadcast_to(c[:, tm - 1:tm], carry_ref.shape)

    q_scale = (FOX_HEAD_DIM ** -0.5) * LOG2E
    gqn = gqn_ref[...] * q_scale
    gkn = gkn_ref[...]
    row = lax.broadcasted_iota(jnp.int32, (BF16_SUBLANES, tm), 0)
    ones_rows = _ones_row_group(tm).astype(BF16)
    zeros_tail = jnp.zeros((QK_PAD - FOX_HEAD_DIM - BF16_SUBLANES, tm), F32)

    def bias_rows(first, second):
        out = jnp.zeros((BF16_SUBLANES, tm), F32)
        for j in range(3):
            out = jnp.where(row == j, first[j], out)
            out = jnp.where(row == 8 + j, second[j], out)
        return out

    one3 = (1.0, 1.0, 1.0)
    for h in range(HEADS):
        sl = slice(h * FOX_HEAD_DIM, (h + 1) * FOX_HEAD_DIM)
        c3 = _split3(c[h:h + 1])
        qh = qt_all[sl]
        qn = qh * lax.rsqrt(jnp.mean(qh * qh, axis=0, keepdims=True) + EPS) * gqn
        qt_ref[h, 0:FOX_HEAD_DIM, :] = qn.astype(BF16)
        qt_ref[h, FOX_HEAD_DIM:FOX_HEAD_DIM + BF16_SUBLANES, :] = bias_rows(one3, c3).astype(BF16)
        qt_ref[h, FOX_HEAD_DIM + BF16_SUBLANES:QK_PAD, :] = zeros_tail.astype(BF16)

        kh = kt_all[sl]
        kn = kh * lax.rsqrt(jnp.mean(kh * kh, axis=0, keepdims=True) + EPS) * gkn
        neg3 = tuple(-p for p in c3)
        kt = jnp.concatenate([kn, bias_rows(neg3, one3), zeros_tail], axis=0)
        k_ref[h] = kt.T.astype(BF16)

        vt_ref[h, 0:HEAD_V, :] = vt_all[sl].astype(BF16)
        vt_ref[h, HEAD_V:V_ROWS, :] = ones_rows


def _fox_proj(h, g, w_t, b_f, gqn, gkn):
    b, s, _ = h.shape
    out_specs, out_shapes = _qkv_out(b, s)
    return pl.pallas_call(
        _fox_proj_kernel,
        grid=(b, s // PROJ_TILE),
        in_specs=[
            pl.BlockSpec((None, PROJ_TILE, D_MODEL), lambda bi, i: (bi, i, 0)),
            _resident(g.shape), _resident(w_t.shape), _resident(b_f.shape),
            _resident(gqn.shape), _resident(gkn.shape),
        ],
        out_specs=out_specs,
        out_shape=out_shapes,
        scratch_shapes=[pltpu.VMEM((HEADS, LANES), F32)],
        compiler_params=pltpu.CompilerParams(
            dimension_semantics=("arbitrary", "arbitrary"), vmem_limit_bytes=VMEM_LIMIT),
        name="fox_proj",
    )(h, g, w_t, b_f, gqn, gkn)


def _attn_kernel(qt_ref, k_ref, vt_ref, o_ref, acc_ref, m_ref, s0_ref, s1_ref, bm0_ref, bm1_ref,
                 *, seq, tq, tk, cw):
    ncol = tq // cw
    nq = seq // tq
    all_chains = tuple(range(ncol))
    s_refs = (s0_ref, s1_ref)
    bm_refs = (bm0_ref, bm1_ref)

    n_parts = tk // PV_KEYS
    full_plan = {c: tuple((r, False) for r in range(n_parts)) for c in all_chains}

    def diag_plan(d):
        plan = {}
        for c in all_chains:
            parts = []
            for r in range(n_parts):
                key_lo = d * tk + r * PV_KEYS
                if key_lo <= (c + 1) * cw - 1:
                    parts.append((r, key_lo + PV_KEYS - 1 > c * cw))
            if parts:
                plan[c] = tuple(parts)
        return plan

    diag_plans = (diag_plan(0), diag_plan(1))

    def stage_a_parts(q0, k0, slot, c, parts):
        cols = slice(c * cw, (c + 1) * cw)
        qt = qt_ref[:, pl.ds(q0 + c * cw, cw)]
        bm = None
        for r, masked in parts:
            rows = slice(r * PV_KEYS, (r + 1) * PV_KEYS)
            kb = k_ref[pl.ds(k0 + r * PV_KEYS, PV_KEYS), :]
            s = jnp.dot(kb, qt, preferred_element_type=F32)
            if masked:
                kpos = k0 + r * PV_KEYS + lax.broadcasted_iota(jnp.int32, (PV_KEYS, cw), 0)
                qpos = q0 + c * cw + lax.broadcasted_iota(jnp.int32, (PV_KEYS, cw), 1)
                s = jnp.where(kpos <= qpos, s, MASK_VALUE)
            s_refs[slot][c, rows, :] = s
            pm = jnp.max(s, axis=0, keepdims=True)
            bm = pm if bm is None else jnp.maximum(bm, pm)
            if r == parts[-1][0]:
                bm_refs[slot][:, cols] = bm
            yield

    def stage_b_parts(k0, slot, c, parts):
        cols = slice(c * cw, (c + 1) * cw)
        m_old = m_ref[:, cols]
        m_new = jnp.maximum(m_old, bm_refs[slot][:, cols])
        alpha = jnp.exp2(m_old - m_new)
        pv = None
        for r, _ in parts:
            rows = slice(r * PV_KEYS, (r + 1) * PV_KEYS)
            p = jnp.exp2(s_refs[slot][c, rows, :] - m_new).astype(BF16)
            vb = vt_ref[:, pl.ds(k0 + r * PV_KEYS, PV_KEYS)]
            t = jnp.dot(vb, p, preferred_element_type=F32)
            pv = t if pv is None else pv + t
            if r == parts[-1][0]:
                acc_ref[c] = alpha * acc_ref[c] + pv
                m_ref[:, cols] = m_new
            yield

    def step(q0, k0, rslot, nxt_q0, nxt_k0, a_plan=full_plan, b_plan=full_plan):
        for c in all_chains:
            gens = []
            if a_plan is not None and c in a_plan:
                gens.append(stage_a_parts(nxt_q0, nxt_k0, 1 - rslot, c, a_plan[c]))
            if b_plan is not None and c in b_plan:
                gens.append(stage_b_parts(k0, rslot, c, b_plan[c]))
            while gens:
                gens = [g for g in gens if next(g, gens) is not gens]

    def q_tile(qi, carry):
        q0 = pl.multiple_of(qi * tq, tq)
        acc_ref[...] = jnp.zeros_like(acc_ref)
        m_ref[...] = jnp.full_like(m_ref, M_INIT)

        def run_steps(k0, count):
            for j in range(count):
                step(q0, k0 + j * tk, j % 2, q0, k0 + (j + 1) * tk)

        n_plain = jnp.maximum(2 * qi - 1, 0)
        rem = n_plain & (MAIN_UNROLL - 1)

        def main(t, carry2):
            run_steps(pl.multiple_of(t * MAIN_UNROLL * tk, MAIN_UNROLL * tk), MAIN_UNROLL)
            return carry2

        lax.fori_loop(0, lax.shift_right_logical(n_plain, MAIN_UNROLL.bit_length() - 1), main, 0)
        k_single = q0 - tq
        sub = MAIN_UNROLL // 2
        while sub >= 2:
            after = (rem & (sub - 1)) - 1 + sub
            k_sub = pl.multiple_of(k_single - after * tk, tk)

            @pl.when((rem & sub) != 0)
            def _(k_sub=k_sub, sub=sub):
                run_steps(k_sub, sub)

            sub //= 2

        @pl.when(qi >= 1)
        def _():
            k0 = pl.multiple_of(k_single, tq)
            step(q0, k0, 0, q0, k0 + tk)
            step(q0, k0 + tk, 1, q0, q0, a_plan=diag_plans[0])

        nxt_q0 = pl.multiple_of(jnp.minimum(q0 + tq, seq - tq), tq)
        step(q0, q0, 0, q0, q0 + tk, a_plan=diag_plans[1], b_plan=diag_plans[0])
        step(q0, q0 + tk, 1, nxt_q0, 0, b_plan=diag_plans[1])

        for c in all_chains:
            acc = acc_ref[c]
            o = acc[:HEAD_V] / acc[HEAD_V:HEAD_V + 1]
            o_ref[:, pl.ds(q0 + c * cw, cw)] = o.astype(o_ref.dtype)
        return carry

    step(0, 0, 1, 0, 0, a_plan=diag_plans[0], b_plan=None)
    lax.fori_loop(0, nq, q_tile, 0)


def _attention(qt, k, vt):
    b, heads, _, s = qt.shape
    tq = min(ATT_TQ, s)
    tk = tq // 2
    kern = functools.partial(_attn_kernel, seq=s, tq=tq, tk=tk, cw=ATT_CW)
    return pl.pallas_call(
        kern,
        grid=(b, heads),
        in_specs=[
            pl.BlockSpec((None, None, QK_PAD, s), lambda bi, hi: (bi, hi, 0, 0)),
            pl.BlockSpec((None, None, s, QK_PAD), lambda bi, hi: (bi, hi, 0, 0)),
            pl.BlockSpec((None, None, V_ROWS, s), lambda bi, hi: (bi, hi, 0, 0)),
        ],
        out_specs=pl.BlockSpec((None, HEAD_V, s), lambda bi, hi: (bi, hi, 0)),
        out_shape=jax.ShapeDtypeStruct((b, heads * HEAD_V, s), BF16),
        scratch_shapes=[pltpu.VMEM((tq // ATT_CW, V_ROWS, ATT_CW), F32), pltpu.VMEM((1, tq), F32),
                        pltpu.VMEM((tq // ATT_CW, tk, ATT_CW), F32),
                        pltpu.VMEM((tq // ATT_CW, tk, ATT_CW), F32),
                        pltpu.VMEM((1, tq), F32), pltpu.VMEM((1, tq), F32)],
        compiler_params=pltpu.CompilerParams(
            dimension_semantics=("parallel", "parallel"), vmem_limit_bytes=VMEM_LIMIT),
        name="attention",
    )(qt, k, vt)


def kernel(x, p, positions, g_ffn1, g_mix, g_ffn2, g_ple, ffn1_w_in, ffn1_w_out, ffn2_w_in, ffn2_w_out, ple_w_proj, ple_w_gate, mla_w_in, mla_g_q_lat, mla_w_uq, mla_g_kv_lat, mla_w_ukv, mla_g_qn, mla_g_kn, mla_w_o, fox_w_in, fox_b_f, fox_g_qn, fox_g_kn, fox_w_o):
    b, s, d = x.shape
    depth = g_ffn1.shape[0]
    assert d == D_MODEL and s % PROJ_TILE == 0 and (b * s) % ROW_TILE == 0
    n = b * s
    bf = lambda w: w.astype(BF16)
    row = lambda g: g.reshape(1, -1)
    col = lambda g: g.reshape(-1, 1)

    pos3 = positions.reshape(b, 1, s)
    half = MLA_ROPE // 2
    inv_freq = (ROPE_THETA ** (-jnp.arange(0, MLA_ROPE, 2, dtype=F32) / MLA_ROPE)).reshape(half, 1)
    lat = MLA_Q_RANK + MLA_KV_RANK

    h = x
    for i in range(depth):
        j = i // N_MIXERS
        h = _ffn(h.reshape(n, d), row(g_ffn1[i]), bf(ffn1_w_in[i]), bf(ffn1_w_out[i])).reshape(b, s, d)
        if i % N_MIXERS == 0:
            qt, k, vt = _mla_proj(
                h, pos3, row(g_mix[i]), bf(mla_w_in[j][:, :lat]), bf(mla_w_in[j][:, lat:].T),
                row(mla_g_q_lat[j]), row(mla_g_kv_lat[j]), bf(mla_w_uq[j].T), bf(mla_w_ukv[j].T),
                col(mla_g_qn[j]), col(mla_g_kn[j]), inv_freq)
            w_o = mla_w_o[j]
        else:
            qt, k, vt = _fox_proj(h, row(g_mix[i]), bf(fox_w_in[j].T), col(fox_b_f[j]),
                                  col(fox_g_qn[j]), col(fox_g_kn[j]))
            w_o = fox_w_o[j]
        ot = _attention(qt, k, vt)
        h = _oproj(h, ot, bf(w_o))
        h = _ffn(h.reshape(n, d), row(g_ffn2[i]), bf(ffn2_w_in[i]), bf(ffn2_w_out[i]))
        h = _ple(h, p[i].reshape(n, PLE_DIM), row(g_ple[i]), bf(ple_w_gate[i]), bf(ple_w_proj[i])).reshape(b, s, d)
    return h
```

```python
import functools
import math

import jax
import jax.numpy as jnp
from jax import lax
from jax.experimental import pallas as pl
from jax.experimental.pallas import tpu as pltpu

F32 = jnp.float32
BF16 = jnp.bfloat16

D_MODEL = 1024
D_FF = 2816
PLE_DIM = 256
N_MIXERS = 2
FFN_HALF = 0.5
HEADS = 16
MLA_Q_RANK = 512
MLA_KV_RANK = 256
MLA_NOPE = 64
MLA_ROPE = 32
MLA_QK = MLA_NOPE + MLA_ROPE
HEAD_V = 64
FOX_HEAD_DIM = 64
ROPE_THETA = 10000.0
EPS = 1e-6
LOG2E = math.log2(math.e)

LANES = 128
MXU_DIM = 256
BF16_SUBLANES = 16
QK_PAD = 128
V_ROWS = HEAD_V + BF16_SUBLANES
VMEM_LIMIT = 52 * 1024 * 1024

ROW_TILE = 256
PROJ_TILE = 512
ATT_TQ = 1024
ATT_CW = MXU_DIM
ATT_TK = MXU_DIM
MAIN_UNROLL = 16
NSLOT = 4
LAG = 2
PV_KEYS = MXU_DIM
MASK_VALUE = -jnp.inf
M_INIT = -1e30


def _rms(x, g):
    return x * lax.rsqrt(jnp.mean(x * x, axis=-1, keepdims=True) + EPS) * g


def _resident(shape):
    return pl.BlockSpec(shape, lambda *_: (0,) * len(shape), pipeline_mode=pl.Buffered(1))


def _ffn_kernel(x_ref, g_ref, w_in_ref, w_out_ref, o_ref):
    x = x_ref[...]
    xn = _rms(x, g_ref[...]).astype(BF16)
    gu = jnp.dot(xn, w_in_ref[...], preferred_element_type=F32)
    gate = gu[:, :D_FF]
    up = gu[:, D_FF:]
    act = (gate * jax.nn.sigmoid(gate) * up).astype(BF16)
    y = jnp.dot(act, w_out_ref[...], preferred_element_type=F32)
    o_ref[...] = x + FFN_HALF * y


def _ffn(h2d, g, w_in, w_out):
    n = h2d.shape[0]
    return pl.pallas_call(
        _ffn_kernel,
        grid=(n // ROW_TILE,),
        in_specs=[
            pl.BlockSpec((ROW_TILE, D_MODEL), lambda i: (i, 0)),
            _resident((1, D_MODEL)),
            _resident((D_MODEL, 2 * D_FF)),
            _resident((D_FF, D_MODEL)),
        ],
        out_specs=pl.BlockSpec((ROW_TILE, D_MODEL), lambda i: (i, 0)),
        out_shape=jax.ShapeDtypeStruct((n, D_MODEL), F32),
        compiler_params=pltpu.CompilerParams(
            dimension_semantics=("parallel",), vmem_limit_bytes=VMEM_LIMIT),
        name="ffn",
    )(h2d, g, w_in, w_out)


def _oproj_kernel(x_ref, ot_ref, w_ref, o_ref):
    y = lax.dot_general(ot_ref[...], w_ref[...], (((0,), (0,)), ((), ())),
                        preferred_element_type=F32)
    o_ref[...] = x_ref[...] + y


def _oproj(h, ot, w_o):
    b, s, _ = h.shape
    hd = ot.shape[1]
    return pl.pallas_call(
        _oproj_kernel,
        grid=(b, s // PROJ_TILE),
        in_specs=[
            pl.BlockSpec((None, PROJ_TILE, D_MODEL), lambda bi, i: (bi, i, 0)),
            pl.BlockSpec((None, hd, PROJ_TILE), lambda bi, i: (bi, 0, i)),
            _resident((hd, D_MODEL)),
        ],
        out_specs=pl.BlockSpec((None, PROJ_TILE, D_MODEL), lambda bi, i: (bi, i, 0)),
        out_shape=jax.ShapeDtypeStruct(h.shape, F32),
        compiler_params=pltpu.CompilerParams(
            dimension_semantics=("parallel", "parallel"), vmem_limit_bytes=VMEM_LIMIT),
        name="oproj",
    )(h, ot, w_o)


def _ple_kernel(x_ref, p_ref, g_ref, wg_ref, wp_ref, o_ref):
    x = x_ref[...]
    xn = _rms(x, g_ref[...]).astype(BF16)
    gate = jax.nn.sigmoid(jnp.dot(xn, wg_ref[...], preferred_element_type=F32))
    pp = jnp.dot(p_ref[...].astype(BF16), wp_ref[...], preferred_element_type=F32)
    o_ref[...] = x + gate * pp


def _ple(h2d, p2d, g, w_gate, w_proj):
    n = h2d.shape[0]
    return pl.pallas_call(
        _ple_kernel,
        grid=(n // ROW_TILE,),
        in_specs=[
            pl.BlockSpec((ROW_TILE, D_MODEL), lambda i: (i, 0)),
            pl.BlockSpec((ROW_TILE, PLE_DIM), lambda i: (i, 0)),
            _resident((1, D_MODEL)),
            _resident((D_MODEL, D_MODEL)),
            _resident((PLE_DIM, D_MODEL)),
        ],
        out_specs=pl.BlockSpec((ROW_TILE, D_MODEL), lambda i: (i, 0)),
        out_shape=jax.ShapeDtypeStruct((n, D_MODEL), F32),
        compiler_params=pltpu.CompilerParams(
            dimension_semantics=("parallel",), vmem_limit_bytes=VMEM_LIMIT),
        name="ple",
    )(h2d, p2d, g, w_gate, w_proj)


def _dot_nt(w, x):
    return lax.dot_general(w, x, (((1,), (1,)), ((), ())), preferred_element_type=F32)


def _ones_row_group(width):
    row = lax.broadcasted_iota(jnp.int32, (BF16_SUBLANES, width), 0)
    return jnp.where(row == 0, 1.0, 0.0).astype(F32)


def _rope_rows(x1, x2, cos, sin):
    return x1 * cos - x2 * sin, x1 * sin + x2 * cos


def _mla_proj_kernel(x_ref, pos_ref, g_ref, w_in_ref, w_pe_ref, gq_lat_ref, gkv_lat_ref,
                     w_uq_ref, w_ukv_ref, gqn_ref, gkn_ref, inv_freq_ref,
                     qt_ref, k_ref, vt_ref):
    tm = x_ref.shape[0]
    xn = _rms(x_ref[...], g_ref[...]).astype(BF16)
    z = jnp.dot(xn, w_in_ref[...], preferred_element_type=F32)
    cq = _rms(z[:, :MLA_Q_RANK], gq_lat_ref[...]).astype(BF16)
    ckv = _rms(z[:, MLA_Q_RANK:], gkv_lat_ref[...]).astype(BF16)
    qt_all = _dot_nt(w_uq_ref[...], cq)
    kvt_all = _dot_nt(w_ukv_ref[...], ckv)
    kpe_t = _dot_nt(w_pe_ref[...], xn)

    ang = pos_ref[...].astype(F32) * inv_freq_ref[...]
    cos = jnp.cos(ang)
    sin = jnp.sin(ang)
    half = MLA_ROPE // 2
    q_scale = (MLA_QK ** -0.5) * LOG2E
    gqn = gqn_ref[...] * q_scale
    gkn = gkn_ref[...]
    kpe_sq = jnp.sum(kpe_t * kpe_t, axis=0, keepdims=True)
    ones_rows = _ones_row_group(tm).astype(BF16)
    zeros_tail = jnp.zeros((QK_PAD - MLA_QK, tm), F32)

    for h in range(HEADS):
        qh = qt_all[h * MLA_QK:(h + 1) * MLA_QK]
        qn = qh * lax.rsqrt(jnp.mean(qh * qh, axis=0, keepdims=True) + EPS) * gqn
        q1, q2 = _rope_rows(qn[MLA_NOPE:MLA_NOPE + half], qn[MLA_NOPE + half:], cos, sin)
        qt_ref[h, 0:MLA_NOPE, :] = qn[:MLA_NOPE].astype(BF16)
        qt_ref[h, MLA_NOPE:MLA_NOPE + half, :] = q1.astype(BF16)
        qt_ref[h, MLA_NOPE + half:MLA_QK, :] = q2.astype(BF16)
        qt_ref[h, MLA_QK:QK_PAD, :] = zeros_tail.astype(BF16)

        base = h * (MLA_NOPE + HEAD_V)
        kn = kvt_all[base:base + MLA_NOPE]
        ms = (jnp.sum(kn * kn, axis=0, keepdims=True) + kpe_sq) * (1.0 / MLA_QK)
        r = lax.rsqrt(ms + EPS)
        kn = kn * r * gkn[:MLA_NOPE]
        kp = kpe_t * r * gkn[MLA_NOPE:]
        k1, k2 = _rope_rows(kp[:half], kp[half:], cos, sin)
        kt = jnp.concatenate([kn, k1, k2, zeros_tail], axis=0)
        k_ref[h] = kt.T.astype(BF16)

        vt_ref[h, 0:HEAD_V, :] = kvt_all[base + MLA_NOPE:base + MLA_NOPE + HEAD_V].astype(BF16)
        vt_ref[h, HEAD_V:V_ROWS, :] = ones_rows


def _qkv_out(b, s):
    specs = [
        pl.BlockSpec((None, HEADS, QK_PAD, PROJ_TILE), lambda bi, i: (bi, 0, 0, i)),
        pl.BlockSpec((None, HEADS, PROJ_TILE, QK_PAD), lambda bi, i: (bi, 0, i, 0)),
        pl.BlockSpec((None, HEADS, V_ROWS, PROJ_TILE), lambda bi, i: (bi, 0, 0, i)),
    ]
    shapes = [
        jax.ShapeDtypeStruct((b, HEADS, QK_PAD, s), BF16),
        jax.ShapeDtypeStruct((b, HEADS, s, QK_PAD), BF16),
        jax.ShapeDtypeStruct((b, HEADS, V_ROWS, s), BF16),
    ]
    return specs, shapes


def _mla_proj(h, pos3, g, w_in, w_pe_t, gq_lat, gkv_lat, w_uq_t, w_ukv_t, gqn, gkn, inv_freq):
    b, s, _ = h.shape
    out_specs, out_shapes = _qkv_out(b, s)
    return pl.pallas_call(
        _mla_proj_kernel,
        grid=(b, s // PROJ_TILE),
        in_specs=[
            pl.BlockSpec((None, PROJ_TILE, D_MODEL), lambda bi, i: (bi, i, 0)),
            pl.BlockSpec((None, 1, PROJ_TILE), lambda bi, i: (bi, 0, i)),
            _resident(g.shape), _resident(w_in.shape), _resident(w_pe_t.shape),
            _resident(gq_lat.shape), _resident(gkv_lat.shape),
            _resident(w_uq_t.shape), _resident(w_ukv_t.shape),
            _resident(gqn.shape), _resident(gkn.shape), _resident(inv_freq.shape),
        ],
        out_specs=out_specs,
        out_shape=out_shapes,
        compiler_params=pltpu.CompilerParams(
            dimension_semantics=("parallel", "parallel"), vmem_limit_bytes=VMEM_LIMIT),
        name="mla_proj",
    )(h, pos3, g, w_in, w_pe_t, gq_lat, gkv_lat, w_uq_t, w_ukv_t, gqn, gkn, inv_freq)


def _split3(c):
    hi = c.astype(BF16).astype(F32)
    mid = (c - hi).astype(BF16).astype(F32)
    lo = (c - hi - mid).astype(BF16).astype(F32)
    return hi, mid, lo


def _fox_proj_kernel(x_ref, g_ref, w_ref, bf_ref, gqn_ref, gkn_ref,
                     qt_ref, k_ref, vt_ref, carry_ref):
    tm = x_ref.shape[0]
    hd = HEADS * FOX_HEAD_DIM

    @pl.when(pl.program_id(1) == 0)
    def _():
        carry_ref[...] = jnp.zeros_like(carry_ref)

    xn = _rms(x_ref[...], g_ref[...]).astype(BF16)
    qt_all = _dot_nt(w_ref[0:hd, :], xn)
    kt_all = _dot_nt(w_ref[hd:2 * hd, :], xn)
    vt_all = _dot_nt(w_ref[2 * hd:3 * hd, :], xn)
    f_t = _dot_nt(w_ref[3 * hd:3 * hd + HEADS, :], xn) + bf_ref[...]

    logf = (jnp.minimum(f_t, 0.0) - jnp.log1p(jnp.exp(-jnp.abs(f_t)))) * LOG2E
    hi, mid, lo = _split3(logf)
    parts = jnp.concatenate([hi, mid, lo], axis=0).astype(BF16)
    src = lax.broadcasted_iota(jnp.int32, (tm, tm), 0)
    dst = lax.broadcasted_iota(jnp.int32, (tm, tm), 1)
    upper = jnp.where(src <= dst, 1.0, 0.0).astype(BF16)
    sums = jnp.dot(parts, upper, preferred_element_type=F32)
    local = sums[2 * HEADS:] + sums[HEADS:2 * HEADS] + sums[:HEADS]
    c = local + carry_ref[:, 0:1]
    carry_ref[...] = jnp.broadcast_to(c[:, tm - 1:tm], carry_ref.shape)

    q_scale = (FOX_HEAD_DIM ** -0.5) * LOG2E
    gqn = gqn_ref[...] * q_scale
    gkn = gkn_ref[...]
    row = lax.broadcasted_iota(jnp.int32, (BF16_SUBLANES, tm), 0)
    ones_rows = _ones_row_group(tm).astype(BF16)
    zeros_tail = jnp.zeros((QK_PAD - FOX_HEAD_DIM - BF16_SUBLANES, tm), F32)

    def bias_rows(first, second):
        out = jnp.zeros((BF16_SUBLANES, tm), F32)
        for j in range(3):
            out = jnp.where(row == j, first[j], out)
            out = jnp.where(row == 8 + j, second[j], out)
        return out

    one3 = (1.0, 1.0, 1.0)
    for h in range(HEADS):
        sl = slice(h * FOX_HEAD_DIM, (h + 1) * FOX_HEAD_DIM)
        c3 = _split3(c[h:h + 1])
        qh = qt_all[sl]
        qn = qh * lax.rsqrt(jnp.mean(qh * qh, axis=0, keepdims=True) + EPS) * gqn
        qt_ref[h, 0:FOX_HEAD_DIM, :] = qn.astype(BF16)
        qt_ref[h, FOX_HEAD_DIM:FOX_HEAD_DIM + BF16_SUBLANES, :] = bias_rows(one3, c3).astype(BF16)
        qt_ref[h, FOX_HEAD_DIM + BF16_SUBLANES:QK_PAD, :] = zeros_tail.astype(BF16)

        kh = kt_all[sl]
        kn = kh * lax.rsqrt(jnp.mean(kh * kh, axis=0, keepdims=True) + EPS) * gkn
        neg3 = tuple(-p for p in c3)
        kt = jnp.concatenate([kn, bias_rows(neg3, one3), zeros_tail], axis=0)
        k_ref[h] = kt.T.astype(BF16)

        vt_ref[h, 0:HEAD_V, :] = vt_all[sl].astype(BF16)
        vt_ref[h, HEAD_V:V_ROWS, :] = ones_rows


def _fox_proj(h, g, w_t, b_f, gqn, gkn):
    b, s, _ = h.shape
    out_specs, out_shapes = _qkv_out(b, s)
    return pl.pallas_call(
        _fox_proj_kernel,
        grid=(b, s // PROJ_TILE),
        in_specs=[
            pl.BlockSpec((None, PROJ_TILE, D_MODEL), lambda bi, i: (bi, i, 0)),
            _resident(g.shape), _resident(w_t.shape), _resident(b_f.shape),
            _resident(gqn.shape), _resident(gkn.shape),
        ],
        out_specs=out_specs,
        out_shape=out_shapes,
        scratch_shapes=[pltpu.VMEM((HEADS, LANES), F32)],
        compiler_params=pltpu.CompilerParams(
            dimension_semantics=("arbitrary", "arbitrary"), vmem_limit_bytes=VMEM_LIMIT),
        name="fox_proj",
    )(h, g, w_t, b_f, gqn, gkn)


def _attn_kernel(qt_ref, k_ref, vt_ref, o_ref, acc_ref, m_ref, *bufs, seq, tq, tk, cw):
    ncol = tq // cw
    nq = seq // tq
    all_chains = tuple(range(ncol))
    s_refs = bufs[:NSLOT]
    bm_refs = bufs[NSLOT:]

    n_parts = tk // PV_KEYS
    full_plan = {c: tuple((r, False) for r in range(n_parts)) for c in all_chains}

    nd = tq // tk
    assert nd % NSLOT == 0 and MAIN_UNROLL % NSLOT == 0 and LAG < NSLOT

    def diag_plan(d):
        plan = {}
        for c in all_chains:
            parts = []
            for r in range(n_parts):
                key_lo = d * tk + r * PV_KEYS
                if key_lo <= (c + 1) * cw - 1:
                    parts.append((r, key_lo + PV_KEYS - 1 > c * cw))
            if parts:
                plan[c] = tuple(parts)
        return plan

    diag_plans = tuple(diag_plan(d) for d in range(nd))

    def stage_a_parts(q0, k0, slot, c, parts):
        cols = slice(c * cw, (c + 1) * cw)
        qt = qt_ref[:, pl.ds(q0 + c * cw, cw)]
        bm = None
        for r, masked in parts:
            rows = slice(r * PV_KEYS, (r + 1) * PV_KEYS)
            kb = k_ref[pl.ds(k0 + r * PV_KEYS, PV_KEYS), :]
            s = jnp.dot(kb, qt, preferred_element_type=F32)
            if masked:
                kpos = k0 + r * PV_KEYS + lax.broadcasted_iota(jnp.int32, (PV_KEYS, cw), 0)
                qpos = q0 + c * cw + lax.broadcasted_iota(jnp.int32, (PV_KEYS, cw), 1)
                s = jnp.where(kpos <= qpos, s, MASK_VALUE)
            s_refs[slot][c, rows, :] = s
            pm = jnp.max(s, axis=0, keepdims=True)
            bm = pm if bm is None else jnp.maximum(bm, pm)
            if r == parts[-1][0]:
                bm_refs[slot][:, cols] = bm
            yield

    def stage_b_parts(k0, slot, c, parts):
        cols = slice(c * cw, (c + 1) * cw)
        m_old = m_ref[:, cols]
        m_new = jnp.maximum(m_old, bm_refs[slot][:, cols])
        alpha = jnp.exp2(m_old - m_new)
        pv = None
        for r, _ in parts:
            rows = slice(r * PV_KEYS, (r + 1) * PV_KEYS)
            p = jnp.exp2(s_refs[slot][c, rows, :] - m_new).astype(BF16)
            vb = vt_ref[:, pl.ds(k0 + r * PV_KEYS, PV_KEYS)]
            t = jnp.dot(vb, p, preferred_element_type=F32)
            pv = t if pv is None else pv + t
            if r == parts[-1][0]:
                acc_ref[c] = alpha * acc_ref[c] + pv
                m_ref[:, cols] = m_new
            yield

    def step(q0, k0, rslot, nxt_q0, nxt_k0, a_plan=full_plan, b_plan=full_plan):
        for c in all_chains:
            gens = []
            if a_plan is not None and c in a_plan:
                gens.append(stage_a_parts(nxt_q0, nxt_k0, (rslot + LAG) % NSLOT, c, a_plan[c]))
            if b_plan is not None and c in b_plan:
                gens.append(stage_b_parts(k0, rslot, c, b_plan[c]))
            while gens:
                gens = [g for g in gens if next(g, gens) is not gens]

    def q_tile(qi, carry):
        q0 = pl.multiple_of(qi * tq, tq)
        acc_ref[...] = jnp.zeros_like(acc_ref)
        m_ref[...] = jnp.full_like(m_ref, M_INIT)

        def run_steps(k0, count):
            for j in range(count):
                step(q0, k0 + j * tk, j % NSLOT, q0, k0 + (j + LAG) * tk)

        n_main = jnp.maximum(nd * qi - NSLOT, 0)
        rem = n_main & (MAIN_UNROLL - 1)

        def main(t, carry2):
            run_steps(pl.multiple_of(t * MAIN_UNROLL * tk, MAIN_UNROLL * tk), MAIN_UNROLL)
            return carry2

        lax.fori_loop(0, lax.shift_right_logical(n_main, MAIN_UNROLL.bit_length() - 1), main, 0)
        k_last = q0 - NSLOT * tk
        sub = MAIN_UNROLL // 2
        while sub >= NSLOT:
            k_sub = pl.multiple_of(k_last - ((rem & (sub - 1)) + sub) * tk, NSLOT * tk)

            @pl.when((rem & sub) != 0)
            def _(k_sub=k_sub, sub=sub):
                run_steps(k_sub, sub)

            sub //= 2

        nxt_q0 = pl.multiple_of(jnp.minimum(q0 + tq, seq - tq), tq)

        def last_below():
            k0 = pl.multiple_of(k_last, NSLOT * tk)
            for j in range(NSLOT):
                tgt = j + LAG
                if tgt < NSLOT:
                    step(q0, k0 + j * tk, j, q0, k0 + tgt * tk)
                else:
                    step(q0, k0 + j * tk, j, q0, q0 + (tgt - NSLOT) * tk,
                         a_plan=diag_plans[tgt - NSLOT])

        def diagonal():
            for d in range(nd):
                tgt = d + LAG
                if tgt < nd:
                    step(q0, q0 + d * tk, d % NSLOT, q0, q0 + tgt * tk,
                         a_plan=diag_plans[tgt], b_plan=diag_plans[d])
                else:
                    step(q0, q0 + d * tk, d % NSLOT, nxt_q0, (tgt - nd) * tk,
                         b_plan=diag_plans[d])

        @pl.when(qi >= 1)
        def _():
            last_below()
            diagonal()

        @pl.when(qi == 0)
        def _():
            diagonal()

        for c in all_chains:
            acc = acc_ref[c]
            o = acc[:HEAD_V] / acc[HEAD_V:HEAD_V + 1]
            o_ref[:, pl.ds(q0 + c * cw, cw)] = o.astype(o_ref.dtype)
        return carry

    for j in range(LAG):
        step(0, 0, (j - LAG) % NSLOT, 0, j * tk, a_plan=diag_plans[j], b_plan=None)
    lax.fori_loop(0, nq, q_tile, 0)


def _attention(qt, k, vt):
    b, heads, _, s = qt.shape
    tq = min(ATT_TQ, s)
    tk = min(ATT_TK, tq // NSLOT)
    kern = functools.partial(_attn_kernel, seq=s, tq=tq, tk=tk, cw=ATT_CW)
    return pl.pallas_call(
        kern,
        grid=(b, heads),
        in_specs=[
            pl.BlockSpec((None, None, QK_PAD, s), lambda bi, hi: (bi, hi, 0, 0)),
            pl.BlockSpec((None, None, s, QK_PAD), lambda bi, hi: (bi, hi, 0, 0)),
            pl.BlockSpec((None, None, V_ROWS, s), lambda bi, hi: (bi, hi, 0, 0)),
        ],
        out_specs=pl.BlockSpec((None, HEAD_V, s), lambda bi, hi: (bi, hi, 0)),
        out_shape=jax.ShapeDtypeStruct((b, heads * HEAD_V, s), BF16),
        scratch_shapes=([pltpu.VMEM((tq // ATT_CW, V_ROWS, ATT_CW), F32), pltpu.VMEM((1, tq), F32)]
                        + [pltpu.VMEM((tq // ATT_CW, tk, ATT_CW), F32)] * NSLOT
                        + [pltpu.VMEM((1, tq), F32)] * NSLOT),
        compiler_params=pltpu.CompilerParams(
            dimension_semantics=("parallel", "parallel"), vmem_limit_bytes=VMEM_LIMIT),
        name="attention",
    )(qt, k, vt)


def kernel(x, p, positions, g_ffn1, g_mix, g_ffn2, g_ple, ffn1_w_in, ffn1_w_out, ffn2_w_in, ffn2_w_out, ple_w_proj, ple_w_gate, mla_w_in, mla_g_q_lat, mla_w_uq, mla_g_kv_lat, mla_w_ukv, mla_g_qn, mla_g_kn, mla_w_o, fox_w_in, fox_b_f, fox_g_qn, fox_g_kn, fox_w_o):
    b, s, d = x.shape
    depth = g_ffn1.shape[0]
    assert d == D_MODEL and s % PROJ_TILE == 0 and (b * s) % ROW_TILE == 0
    n = b * s
    bf = lambda w: w.astype(BF16)
    row = lambda g: g.reshape(1, -1)
    col = lambda g: g.reshape(-1, 1)

    pos3 = positions.reshape(b, 1, s)
    half = MLA_ROPE // 2
    inv_freq = (ROPE_THETA ** (-jnp.arange(0, MLA_ROPE, 2, dtype=F32) / MLA_ROPE)).reshape(half, 1)
    lat = MLA_Q_RANK + MLA_KV_RANK

    h = x
    for i in range(depth):
        j = i // N_MIXERS
        h = _ffn(h.reshape(n, d), row(g_ffn1[i]), bf(ffn1_w_in[i]), bf(ffn1_w_out[i])).reshape(b, s, d)
        if i % N_MIXERS == 0:
            qt, k, vt = _mla_proj(
                h, pos3, row(g_mix[i]), bf(mla_w_in[j][:, :lat]), bf(mla_w_in[j][:, lat:].T),
                row(mla_g_q_lat[j]), row(mla_g_kv_lat[j]), bf(mla_w_uq[j].T), bf(mla_w_ukv[j].T),
                col(mla_g_qn[j]), col(mla_g_kn[j]), inv_freq)
            w_o = mla_w_o[j]
        else:
            qt, k, vt = _fox_proj(h, row(g_mix[i]), bf(fox_w_in[j].T), col(fox_b_f[j]),
                                  col(fox_g_qn[j]), col(fox_g_kn[j]))
            w_o = fox_w_o[j]
        ot = _attention(qt, k, vt)
        h = _oproj(h, ot, bf(w_o))
        h = _ffn(h.reshape(n, d), row(g_ffn2[i]), bf(ffn2_w_in[i]), bf(ffn2_w_out[i]))
        h = _ple(h, p[i].reshape(n, PLE_DIM), row(g_ple[i]), bf(ple_w_gate[i]), bf(ple_w_proj[i])).reshape(b, s, d)
    return h
```

```python
import functools
import math

import jax
import jax.numpy as jnp
from jax import lax
from jax.experimental import pallas as pl
from jax.experimental.pallas import tpu as pltpu

F32 = jnp.float32
BF16 = jnp.bfloat16

D_MODEL = 1024
D_FF = 2816
PLE_DIM = 256
N_MIXERS = 2
FFN_HALF = 0.5
HEADS = 16
MLA_Q_RANK = 512
MLA_KV_RANK = 256
MLA_NOPE = 64
MLA_ROPE = 32
MLA_QK = MLA_NOPE + MLA_ROPE
HEAD_V = 64
FOX_HEAD_DIM = 64
ROPE_THETA = 10000.0
EPS = 1e-6
LOG2E = math.log2(math.e)

LANES = 128
MXU_DIM = 256
BF16_SUBLANES = 16
QK_PAD = 128
V_ROWS = HEAD_V + BF16_SUBLANES
VMEM_LIMIT = 52 * 1024 * 1024

ROW_TILE = 512
PROJ_TILE = 512
ATT_TQ = 1024
ATT_CW = MXU_DIM
ATT_TK = MXU_DIM
MAIN_UNROLL = 16
NSLOT = 4
LAG = 2
PV_KEYS = MXU_DIM
MASK_VALUE = -jnp.inf
M_INIT = -1e30


def _rms(x, g):
    return x * lax.rsqrt(jnp.mean(x * x, axis=-1, keepdims=True) + EPS) * g


def _resident(shape):
    return pl.BlockSpec(shape, lambda *_: (0,) * len(shape), pipeline_mode=pl.Buffered(1))


def _swiglu_half_step(x, g, w_in_ref, w_out_ref):
    xn = _rms(x, g).astype(BF16)
    gu = jnp.dot(xn, w_in_ref[...], preferred_element_type=F32)
    gate = gu[:, :D_FF]
    up = gu[:, D_FF:]
    act = (gate * jax.nn.sigmoid(gate) * up).astype(BF16)
    return x + FFN_HALF * jnp.dot(act, w_out_ref[...], preferred_element_type=F32)


def _ffn_kernel(x_ref, g_ref, w_in_ref, w_out_ref, o_ref):
    o_ref[...] = _swiglu_half_step(x_ref[...], g_ref[...], w_in_ref, w_out_ref)


def _ffn(h2d, g, w_in, w_out):
    n = h2d.shape[0]
    return pl.pallas_call(
        _ffn_kernel,
        grid=(n // ROW_TILE,),
        in_specs=[
            pl.BlockSpec((ROW_TILE, D_MODEL), lambda i: (i, 0)),
            _resident((1, D_MODEL)),
            _resident((D_MODEL, 2 * D_FF)),
            _resident((D_FF, D_MODEL)),
        ],
        out_specs=pl.BlockSpec((ROW_TILE, D_MODEL), lambda i: (i, 0)),
        out_shape=jax.ShapeDtypeStruct((n, D_MODEL), F32),
        compiler_params=pltpu.CompilerParams(
            dimension_semantics=("parallel",), vmem_limit_bytes=VMEM_LIMIT),
        name="ffn",
    )(h2d, g, w_in, w_out)


def _post_kernel(x_ref, ot_ref, wo_ref, g2_ref, w_in_ref, w_out_ref,
                 p_ref, gp_ref, wg_ref, wp_ref, o_ref):
    x = x_ref[...] + lax.dot_general(ot_ref[...], wo_ref[...], (((0,), (0,)), ((), ())),
                                     preferred_element_type=F32)
    x = _swiglu_half_step(x, g2_ref[...], w_in_ref, w_out_ref)
    xn = _rms(x, gp_ref[...]).astype(BF16)
    gate = jax.nn.sigmoid(jnp.dot(xn, wg_ref[...], preferred_element_type=F32))
    pp = jnp.dot(p_ref[...].astype(BF16), wp_ref[...], preferred_element_type=F32)
    o_ref[...] = x + gate * pp


def _post(h2d, ot, w_o, g2, w_in, w_out, p2d, g_ple, w_gate, w_proj):
    n = h2d.shape[0]
    hd, s = ot.shape[1], ot.shape[2]
    tiles = s // ROW_TILE
    rows = pl.BlockSpec((ROW_TILE, D_MODEL), lambda i: (i, 0))
    return pl.pallas_call(
        _post_kernel,
        grid=(n // ROW_TILE,),
        in_specs=[
            rows,
            pl.BlockSpec((None, hd, ROW_TILE), lambda i: (i // tiles, 0, i % tiles)),
            _resident((hd, D_MODEL)),
            _resident((1, D_MODEL)),
            _resident((D_MODEL, 2 * D_FF)),
            _resident((D_FF, D_MODEL)),
            pl.BlockSpec((ROW_TILE, PLE_DIM), lambda i: (i, 0)),
            _resident((1, D_MODEL)),
            _resident((D_MODEL, D_MODEL)),
            _resident((PLE_DIM, D_MODEL)),
        ],
        out_specs=rows,
        out_shape=jax.ShapeDtypeStruct((n, D_MODEL), F32),
        compiler_params=pltpu.CompilerParams(
            dimension_semantics=("parallel",), vmem_limit_bytes=VMEM_LIMIT),
        name="post",
    )(h2d, ot, w_o, g2, w_in, w_out, p2d, g_ple, w_gate, w_proj)


def _dot_nt(w, x):
    return lax.dot_general(w, x, (((1,), (1,)), ((), ())), preferred_element_type=F32)


def _ones_row_group(width):
    row = lax.broadcasted_iota(jnp.int32, (BF16_SUBLANES, width), 0)
    return jnp.where(row == 0, 1.0, 0.0).astype(F32)


def _rope_rows(x1, x2, cos, sin):
    return x1 * cos - x2 * sin, x1 * sin + x2 * cos


def _mla_proj_kernel(x_ref, pos_ref, g_ref, w_in_ref, w_pe_ref, gq_lat_ref, gkv_lat_ref,
                     w_uq_ref, w_ukv_ref, gqn_ref, gkn_ref, inv_freq_ref,
                     qt_ref, k_ref, vt_ref):
    tm = x_ref.shape[0]
    xn = _rms(x_ref[...], g_ref[...]).astype(BF16)
    z = jnp.dot(xn, w_in_ref[...], preferred_element_type=F32)
    cq = _rms(z[:, :MLA_Q_RANK], gq_lat_ref[...]).astype(BF16)
    ckv = _rms(z[:, MLA_Q_RANK:], gkv_lat_ref[...]).astype(BF16)
    qt_all = _dot_nt(w_uq_ref[...], cq)
    kvt_all = _dot_nt(w_ukv_ref[...], ckv)
    kpe_t = _dot_nt(w_pe_ref[...], xn)

    ang = pos_ref[...].astype(F32) * inv_freq_ref[...]
    cos = jnp.cos(ang)
    sin = jnp.sin(ang)
    half = MLA_ROPE // 2
    q_scale = (MLA_QK ** -0.5) * LOG2E
    gqn = gqn_ref[...] * q_scale
    gkn = gkn_ref[...]
    kpe_sq = jnp.sum(kpe_t * kpe_t, axis=0, keepdims=True)
    ones_rows = _ones_row_group(tm).astype(BF16)
    zeros_tail = jnp.zeros((QK_PAD - MLA_QK, tm), F32)

    for h in range(HEADS):
        qh = qt_all[h * MLA_QK:(h + 1) * MLA_QK]
        qn = qh * lax.rsqrt(jnp.mean(qh * qh, axis=0, keepdims=True) + EPS) * gqn
        q1, q2 = _rope_rows(qn[MLA_NOPE:MLA_NOPE + half], qn[MLA_NOPE + half:], cos, sin)
        qt_ref[h, 0:MLA_NOPE, :] = qn[:MLA_NOPE].astype(BF16)
        qt_ref[h, MLA_NOPE:MLA_NOPE + half, :] = q1.astype(BF16)
        qt_ref[h, MLA_NOPE + half:MLA_QK, :] = q2.astype(BF16)
        qt_ref[h, MLA_QK:QK_PAD, :] = zeros_tail.astype(BF16)

        base = h * (MLA_NOPE + HEAD_V)
        kn = kvt_all[base:base + MLA_NOPE]
        ms = (jnp.sum(kn * kn, axis=0, keepdims=True) + kpe_sq) * (1.0 / MLA_QK)
        r = lax.rsqrt(ms + EPS)
        kn = kn * r * gkn[:MLA_NOPE]
        kp = kpe_t * r * gkn[MLA_NOPE:]
        k1, k2 = _rope_rows(kp[:half], kp[half:], cos, sin)
        kt = jnp.concatenate([kn, k1, k2, zeros_tail], axis=0)
        k_ref[h] = kt.T.astype(BF16)

        vt_ref[h, 0:HEAD_V, :] = kvt_all[base + MLA_NOPE:base + MLA_NOPE + HEAD_V].astype(BF16)
        vt_ref[h, HEAD_V:V_ROWS, :] = ones_rows


def _qkv_out(b, s):
    specs = [
        pl.BlockSpec((None, HEADS, QK_PAD, PROJ_TILE), lambda bi, i: (bi, 0, 0, i)),
        pl.BlockSpec((None, HEADS, PROJ_TILE, QK_PAD), lambda bi, i: (bi, 0, i, 0)),
        pl.BlockSpec((None, HEADS, V_ROWS, PROJ_TILE), lambda bi, i: (bi, 0, 0, i)),
    ]
    shapes = [
        jax.ShapeDtypeStruct((b, HEADS, QK_PAD, s), BF16),
        jax.ShapeDtypeStruct((b, HEADS, s, QK_PAD), BF16),
        jax.ShapeDtypeStruct((b, HEADS, V_ROWS, s), BF16),
    ]
    return specs, shapes


def _mla_proj(h, pos3, g, w_in, w_pe_t, gq_lat, gkv_lat, w_uq_t, w_ukv_t, gqn, gkn, inv_freq):
    b, s, _ = h.shape
    out_specs, out_shapes = _qkv_out(b, s)
    return pl.pallas_call(
        _mla_proj_kernel,
        grid=(b, s // PROJ_TILE),
        in_specs=[
            pl.BlockSpec((None, PROJ_TILE, D_MODEL), lambda bi, i: (bi, i, 0)),
            pl.BlockSpec((None, 1, PROJ_TILE), lambda bi, i: (bi, 0, i)),
            _resident(g.shape), _resident(w_in.shape), _resident(w_pe_t.shape),
            _resident(gq_lat.shape), _resident(gkv_lat.shape),
            _resident(w_uq_t.shape), _resident(w_ukv_t.shape),
            _resident(gqn.shape), _resident(gkn.shape), _resident(inv_freq.shape),
        ],
        out_specs=out_specs,
        out_shape=out_shapes,
        compiler_params=pltpu.CompilerParams(
            dimension_semantics=("parallel", "parallel"), vmem_limit_bytes=VMEM_LIMIT),
        name="mla_proj",
    )(h, pos3, g, w_in, w_pe_t, gq_lat, gkv_lat, w_uq_t, w_ukv_t, gqn, gkn, inv_freq)


def _split3(c):
    hi = c.astype(BF16).astype(F32)
    mid = (c - hi).astype(BF16).astype(F32)
    lo = (c - hi - mid).astype(BF16).astype(F32)
    return hi, mid, lo


def _fox_proj_kernel(x_ref, g_ref, w_ref, bf_ref, gqn_ref, gkn_ref,
                     qt_ref, k_ref, vt_ref, carry_ref):
    tm = x_ref.shape[0]
    hd = HEADS * FOX_HEAD_DIM

    @pl.when(pl.program_id(1) == 0)
    def _():
        carry_ref[...] = jnp.zeros_like(carry_ref)

    xn = _rms(x_ref[...], g_ref[...]).astype(BF16)
    qt_all = _dot_nt(w_ref[0:hd, :], xn)
    kt_all = _dot_nt(w_ref[hd:2 * hd, :], xn)
    vt_all = _dot_nt(w_ref[2 * hd:3 * hd, :], xn)
    f_t = _dot_nt(w_ref[3 * hd:3 * hd + HEADS, :], xn) + bf_ref[...]

    logf = (jnp.minimum(f_t, 0.0) - jnp.log1p(jnp.exp(-jnp.abs(f_t)))) * LOG2E
    hi, mid, lo = _split3(logf)
    parts = jnp.concatenate([hi, mid, lo], axis=0).astype(BF16)
    src = lax.broadcasted_iota(jnp.int32, (tm, tm), 0)
    dst = lax.broadcasted_iota(jnp.int32, (tm, tm), 1)
    upper = jnp.where(src <= dst, 1.0, 0.0).astype(BF16)
    sums = jnp.dot(parts, upper, preferred_element_type=F32)
    local = sums[2 * HEADS:] + sums[HEADS:2 * HEADS] + sums[:HEADS]
    c = local + carry_ref[:, 0:1]
    carry_ref[...] = jnp.broadcast_to(c[:, tm - 1:tm], carry_ref.shape)

    q_scale = (FOX_HEAD_DIM ** -0.5) * LOG2E
    gqn = gqn_ref[...] * q_scale
    gkn = gkn_ref[...]
    row = lax.broadcasted_iota(jnp.int32, (BF16_SUBLANES, tm), 0)
    ones_rows = _ones_row_group(tm).astype(BF16)
    zeros_tail = jnp.zeros((QK_PAD - FOX_HEAD_DIM - BF16_SUBLANES, tm), F32)

    def bias_rows(first, second):
        out = jnp.zeros((BF16_SUBLANES, tm), F32)
        for j in range(3):
            out = jnp.where(row == j, first[j], out)
            out = jnp.where(row == 8 + j, second[j], out)
        return out

    one3 = (1.0, 1.0, 1.0)
    for h in range(HEADS):
        sl = slice(h * FOX_HEAD_DIM, (h + 1) * FOX_HEAD_DIM)
        c3 = _split3(c[h:h + 1])
        qh = qt_all[sl]
        qn = qh * lax.rsqrt(jnp.mean(qh * qh, axis=0, keepdims=True) + EPS) * gqn
        qt_ref[h, 0:FOX_HEAD_DIM, :] = qn.astype(BF16)
        qt_ref[h, FOX_HEAD_DIM:FOX_HEAD_DIM + BF16_SUBLANES, :] = bias_rows(one3, c3).astype(BF16)
        qt_ref[h, FOX_HEAD_DIM + BF16_SUBLANES:QK_PAD, :] = zeros_tail.astype(BF16)

        kh = kt_all[sl]
        kn = kh * lax.rsqrt(jnp.mean(kh * kh, axis=0, keepdims=True) + EPS) * gkn
        neg3 = tuple(-p for p in c3)
        kt = jnp.concatenate([kn, bias_rows(neg3, one3), zeros_tail], axis=0)
        k_ref[h] = kt.T.astype(BF16)

        vt_ref[h, 0:HEAD_V, :] = vt_all[sl].astype(BF16)
        vt_ref[h, HEAD_V:V_ROWS, :] = ones_rows


def _fox_proj(h, g, w_t, b_f, gqn, gkn):
    b, s, _ = h.shape
    out_specs, out_shapes = _qkv_out(b, s)
    return pl.pallas_call(
        _fox_proj_kernel,
        grid=(b, s // PROJ_TILE),
        in_specs=[
            pl.BlockSpec((None, PROJ_TILE, D_MODEL), lambda bi, i: (bi, i, 0)),
            _resident(g.shape), _resident(w_t.shape), _resident(b_f.shape),
            _resident(gqn.shape), _resident(gkn.shape),
        ],
        out_specs=out_specs,
        out_shape=out_shapes,
        scratch_shapes=[pltpu.VMEM((HEADS, LANES), F32)],
        compiler_params=pltpu.CompilerParams(
            dimension_semantics=("arbitrary", "arbitrary"), vmem_limit_bytes=VMEM_LIMIT),
        name="fox_proj",
    )(h, g, w_t, b_f, gqn, gkn)


def _attn_kernel(qt_ref, k_ref, vt_ref, o_ref, acc_ref, m_ref, *bufs, seq, tq, tk, cw):
    ncol = tq // cw
    nq = seq // tq
    all_chains = tuple(range(ncol))
    s_refs = bufs[:NSLOT]
    bm_refs = bufs[NSLOT:]

    n_parts = tk // PV_KEYS
    full_plan = {c: tuple((r, False) for r in range(n_parts)) for c in all_chains}

    nd = tq // tk
    assert nd % NSLOT == 0 and MAIN_UNROLL % NSLOT == 0 and LAG < NSLOT

    def diag_plan(d):
        plan = {}
        for c in all_chains:
            parts = []
            for r in range(n_parts):
                key_lo = d * tk + r * PV_KEYS
                if key_lo <= (c + 1) * cw - 1:
                    parts.append((r, key_lo + PV_KEYS - 1 > c * cw))
            if parts:
                plan[c] = tuple(parts)
        return plan

    diag_plans = tuple(diag_plan(d) for d in range(nd))

    def stage_a_parts(q0, k0, slot, c, parts):
        cols = slice(c * cw, (c + 1) * cw)
        qt = qt_ref[:, pl.ds(q0 + c * cw, cw)]
        bm = None
        for r, masked in parts:
            rows = slice(r * PV_KEYS, (r + 1) * PV_KEYS)
            kb = k_ref[pl.ds(k0 + r * PV_KEYS, PV_KEYS), :]
            s = jnp.dot(kb, qt, preferred_element_type=F32)
            if masked:
                kpos = k0 + r * PV_KEYS + lax.broadcasted_iota(jnp.int32, (PV_KEYS, cw), 0)
                qpos = q0 + c * cw + lax.broadcasted_iota(jnp.int32, (PV_KEYS, cw), 1)
                s = jnp.where(kpos <= qpos, s, MASK_VALUE)
            s_refs[slot][c, rows, :] = s
            pm = jnp.max(s, axis=0, keepdims=True)
            bm = pm if bm is None else jnp.maximum(bm, pm)
            if r == parts[-1][0]:
                bm_refs[slot][:, cols] = bm
            yield

    def stage_b_parts(k0, slot, c, parts):
        cols = slice(c * cw, (c + 1) * cw)
        m_old = m_ref[:, cols]
        m_new = jnp.maximum(m_old, bm_refs[slot][:, cols])
        alpha = jnp.exp2(m_old - m_new)
        pv = None
        for r, _ in parts:
            rows = slice(r * PV_KEYS, (r + 1) * PV_KEYS)
            p = jnp.exp2(s_refs[slot][c, rows, :] - m_new).astype(BF16)
            vb = vt_ref[:, pl.ds(k0 + r * PV_KEYS, PV_KEYS)]
            t = jnp.dot(vb, p, preferred_element_type=F32)
            pv = t if pv is None else pv + t
            if r == parts[-1][0]:
                acc_ref[c] = alpha * acc_ref[c] + pv
                m_ref[:, cols] = m_new
            yield

    def step(q0, k0, rslot, nxt_q0, nxt_k0, a_plan=full_plan, b_plan=full_plan):
        for c in all_chains:
            gens = []
            if a_plan is not None and c in a_plan:
                gens.append(stage_a_parts(nxt_q0, nxt_k0, (rslot + LAG) % NSLOT, c, a_plan[c]))
            if b_plan is not None and c in b_plan:
                gens.append(stage_b_parts(k0, rslot, c, b_plan[c]))
            while gens:
                gens = [g for g in gens if next(g, gens) is not gens]

    def q_tile(qi, carry):
        q0 = pl.multiple_of(qi * tq, tq)
        acc_ref[...] = jnp.zeros_like(acc_ref)
        m_ref[...] = jnp.full_like(m_ref, M_INIT)

        def run_steps(k0, count):
            for j in range(count):
                step(q0, k0 + j * tk, j % NSLOT, q0, k0 + (j + LAG) * tk)

        n_main = jnp.maximum(nd * qi - NSLOT, 0)
        rem = n_main & (MAIN_UNROLL - 1)

        def main(t, carry2):
            run_steps(pl.multiple_of(t * MAIN_UNROLL * tk, MAIN_UNROLL * tk), MAIN_UNROLL)
            return carry2

        lax.fori_loop(0, lax.shift_right_logical(n_main, MAIN_UNROLL.bit_length() - 1), main, 0)
        k_last = q0 - NSLOT * tk
        sub = MAIN_UNROLL // 2
        while sub >= NSLOT:
            k_sub = pl.multiple_of(k_last - ((rem & (sub - 1)) + sub) * tk, NSLOT * tk)

            @pl.when((rem & sub) != 0)
            def _(k_sub=k_sub, sub=sub):
                run_steps(k_sub, sub)

            sub //= 2

        nxt_q0 = pl.multiple_of(jnp.minimum(q0 + tq, seq - tq), tq)

        def last_below():
            k0 = pl.multiple_of(k_last, NSLOT * tk)
            for j in range(NSLOT):
                tgt = j + LAG
                if tgt < NSLOT:
                    step(q0, k0 + j * tk, j, q0, k0 + tgt * tk)
                else:
                    step(q0, k0 + j * tk, j, q0, q0 + (tgt - NSLOT) * tk,
                         a_plan=diag_plans[tgt - NSLOT])

        def diagonal():
            for d in range(nd):
                tgt = d + LAG
                if tgt < nd:
                    step(q0, q0 + d * tk, d % NSLOT, q0, q0 + tgt * tk,
                         a_plan=diag_plans[tgt], b_plan=diag_plans[d])
                else:
                    step(q0, q0 + d * tk, d % NSLOT, nxt_q0, (tgt - nd) * tk,
                         b_plan=diag_plans[d])

        @pl.when(qi >= 1)
        def _():
            last_below()
            diagonal()

        @pl.when(qi == 0)
        def _():
            diagonal()

        for c in all_chains:
            acc = acc_ref[c]
            o = acc[:HEAD_V] / acc[HEAD_V:HEAD_V + 1]
            o_ref[:, pl.ds(q0 + c * cw, cw)] = o.astype(o_ref.dtype)
        return carry

    for j in range(LAG):
        step(0, 0, (j - LAG) % NSLOT, 0, j * tk, a_plan=diag_plans[j], b_plan=None)
    lax.fori_loop(0, nq, q_tile, 0)


def _attention(qt, k, vt):
    b, heads, _, s = qt.shape
    tq = min(ATT_TQ, s)
    tk = min(ATT_TK, tq // NSLOT)
    kern = functools.partial(_attn_kernel, seq=s, tq=tq, tk=tk, cw=ATT_CW)
    return pl.pallas_call(
        kern,
        grid=(b, heads),
        in_specs=[
            pl.BlockSpec((None, None, QK_PAD, s), lambda bi, hi: (bi, hi, 0, 0)),
            pl.BlockSpec((None, None, s, QK_PAD), lambda bi, hi: (bi, hi, 0, 0)),
            pl.BlockSpec((None, None, V_ROWS, s), lambda bi, hi: (bi, hi, 0, 0)),
        ],
        out_specs=pl.BlockSpec((None, HEAD_V, s), lambda bi, hi: (bi, hi, 0)),
        out_shape=jax.ShapeDtypeStruct((b, heads * HEAD_V, s), BF16),
        scratch_shapes=([pltpu.VMEM((tq // ATT_CW, V_ROWS, ATT_CW), F32), pltpu.VMEM((1, tq), F32)]
                        + [pltpu.VMEM((tq // ATT_CW, tk, ATT_CW), F32)] * NSLOT
                        + [pltpu.VMEM((1, tq), F32)] * NSLOT),
        compiler_params=pltpu.CompilerParams(
            dimension_semantics=("parallel", "parallel"), vmem_limit_bytes=VMEM_LIMIT),
        name="attention",
    )(qt, k, vt)


def kernel(x, p, positions, g_ffn1, g_mix, g_ffn2, g_ple, ffn1_w_in, ffn1_w_out, ffn2_w_in, ffn2_w_out, ple_w_proj, ple_w_gate, mla_w_in, mla_g_q_lat, mla_w_uq, mla_g_kv_lat, mla_w_ukv, mla_g_qn, mla_g_kn, mla_w_o, fox_w_in, fox_b_f, fox_g_qn, fox_g_kn, fox_w_o):
    b, s, d = x.shape
    depth = g_ffn1.shape[0]
    assert d == D_MODEL and s % PROJ_TILE == 0 and s % ROW_TILE == 0
    n = b * s
    bf = lambda w: w.astype(BF16)
    row = lambda g: g.reshape(1, -1)
    col = lambda g: g.reshape(-1, 1)

    pos3 = positions.reshape(b, 1, s)
    half = MLA_ROPE // 2
    inv_freq = (ROPE_THETA ** (-jnp.arange(0, MLA_ROPE, 2, dtype=F32) / MLA_ROPE)).reshape(half, 1)
    lat = MLA_Q_RANK + MLA_KV_RANK

    h = x
    for i in range(depth):
        j = i // N_MIXERS
        h = _ffn(h.reshape(n, d), row(g_ffn1[i]), bf(ffn1_w_in[i]), bf(ffn1_w_out[i])).reshape(b, s, d)
        if i % N_MIXERS == 0:
            qt, k, vt = _mla_proj(
                h, pos3, row(g_mix[i]), bf(mla_w_in[j][:, :lat]), bf(mla_w_in[j][:, lat:].T),
                row(mla_g_q_lat[j]), row(mla_g_kv_lat[j]), bf(mla_w_uq[j].T), bf(mla_w_ukv[j].T),
                col(mla_g_qn[j]), col(mla_g_kn[j]), inv_freq)
            w_o = mla_w_o[j]
        else:
            qt, k, vt = _fox_proj(h, row(g_mix[i]), bf(fox_w_in[j].T), col(fox_b_f[j]),
                                  col(fox_g_qn[j]), col(fox_g_kn[j]))
            w_o = fox_w_o[j]
        ot = _attention(qt, k, vt)
        h = _post(h.reshape(n, d), ot, bf(w_o), row(g_ffn2[i]), bf(ffn2_w_in[i]), bf(ffn2_w_out[i]),
                  p[i].reshape(n, PLE_DIM), row(g_ple[i]), bf(ple_w_gate[i]),
                  bf(ple_w_proj[i])).reshape(b, s, d)
    return h
```

```python
import functools
import math

import jax
import jax.numpy as jnp
from jax import lax
from jax.experimental import pallas as pl
from jax.experimental.pallas import tpu as pltpu

F32 = jnp.float32
BF16 = jnp.bfloat16

D_MODEL = 1024
D_FF = 2816
PLE_DIM = 256
N_MIXERS = 2
FFN_HALF = 0.5
HEADS = 16
MLA_Q_RANK = 512
MLA_KV_RANK = 256
MLA_NOPE = 64
MLA_ROPE = 32
MLA_QK = MLA_NOPE + MLA_ROPE
HEAD_V = 64
FOX_HEAD_DIM = 64
ROPE_THETA = 10000.0
EPS = 1e-6
LOG2E = math.log2(math.e)

LANES = 128
MXU_DIM = 256
BF16_SUBLANES = 16
QK_PAD = 128
V_ROWS = HEAD_V + BF16_SUBLANES
VMEM_LIMIT = 52 * 1024 * 1024

ROW_TILE = 512
PROJ_TILE = 512
ATT_TQ = 1024
ATT_CW = MXU_DIM
ATT_TK = MXU_DIM
MAIN_UNROLL = 16
NSLOT = 4
LAG = 2
PV_KEYS = MXU_DIM
MASK_VALUE = -jnp.inf
M_INIT = -1e30
ZERO_WEIGHT_LOG2 = 170.0
BF16_NORM_SLACK = 1.02


def _rms(x, g):
    return x * lax.rsqrt(jnp.mean(x * x, axis=-1, keepdims=True) + EPS) * g


def _resident(shape):
    return pl.BlockSpec(shape, lambda *_: (0,) * len(shape), pipeline_mode=pl.Buffered(1))


def _swiglu_half_step(x, g, w_in_ref, w_out_ref):
    xn = _rms(x, g).astype(BF16)
    gu = jnp.dot(xn, w_in_ref[...], preferred_element_type=F32)
    gate = gu[:, :D_FF]
    up = gu[:, D_FF:]
    act = (gate * jax.nn.sigmoid(gate) * up).astype(BF16)
    return x + FFN_HALF * jnp.dot(act, w_out_ref[...], preferred_element_type=F32)


def _ffn_kernel(x_ref, g_ref, w_in_ref, w_out_ref, o_ref):
    o_ref[...] = _swiglu_half_step(x_ref[...], g_ref[...], w_in_ref, w_out_ref)


def _ffn(h2d, g, w_in, w_out):
    n = h2d.shape[0]
    return pl.pallas_call(
        _ffn_kernel,
        grid=(n // ROW_TILE,),
        in_specs=[
            pl.BlockSpec((ROW_TILE, D_MODEL), lambda i: (i, 0)),
            _resident((1, D_MODEL)),
            _resident((D_MODEL, 2 * D_FF)),
            _resident((D_FF, D_MODEL)),
        ],
        out_specs=pl.BlockSpec((ROW_TILE, D_MODEL), lambda i: (i, 0)),
        out_shape=jax.ShapeDtypeStruct((n, D_MODEL), F32),
        compiler_params=pltpu.CompilerParams(
            dimension_semantics=("parallel",), vmem_limit_bytes=VMEM_LIMIT),
        name="ffn",
    )(h2d, g, w_in, w_out)


def _post_kernel(x_ref, ot_ref, wo_ref, g2_ref, w_in_ref, w_out_ref,
                 p_ref, gp_ref, wg_ref, wp_ref, o_ref):
    x = x_ref[...] + lax.dot_general(ot_ref[...], wo_ref[...], (((0,), (0,)), ((), ())),
                                     preferred_element_type=F32)
    x = _swiglu_half_step(x, g2_ref[...], w_in_ref, w_out_ref)
    xn = _rms(x, gp_ref[...]).astype(BF16)
    gate = jax.nn.sigmoid(jnp.dot(xn, wg_ref[...], preferred_element_type=F32))
    pp = jnp.dot(p_ref[...].astype(BF16), wp_ref[...], preferred_element_type=F32)
    o_ref[...] = x + gate * pp


def _post(h2d, ot, w_o, g2, w_in, w_out, p2d, g_ple, w_gate, w_proj):
    n = h2d.shape[0]
    hd, s = ot.shape[1], ot.shape[2]
    tiles = s // ROW_TILE
    rows = pl.BlockSpec((ROW_TILE, D_MODEL), lambda i: (i, 0))
    return pl.pallas_call(
        _post_kernel,
        grid=(n // ROW_TILE,),
        in_specs=[
            rows,
            pl.BlockSpec((None, hd, ROW_TILE), lambda i: (i // tiles, 0, i % tiles)),
            _resident((hd, D_MODEL)),
            _resident((1, D_MODEL)),
            _resident((D_MODEL, 2 * D_FF)),
            _resident((D_FF, D_MODEL)),
            pl.BlockSpec((ROW_TILE, PLE_DIM), lambda i: (i, 0)),
            _resident((1, D_MODEL)),
            _resident((D_MODEL, D_MODEL)),
            _resident((PLE_DIM, D_MODEL)),
        ],
        out_specs=rows,
        out_shape=jax.ShapeDtypeStruct((n, D_MODEL), F32),
        compiler_params=pltpu.CompilerParams(
            dimension_semantics=("parallel",), vmem_limit_bytes=VMEM_LIMIT),
        name="post",
    )(h2d, ot, w_o, g2, w_in, w_out, p2d, g_ple, w_gate, w_proj)


def _dot_nt(w, x):
    return lax.dot_general(w, x, (((1,), (1,)), ((), ())), preferred_element_type=F32)


def _ones_row_group(width):
    row = lax.broadcasted_iota(jnp.int32, (BF16_SUBLANES, width), 0)
    return jnp.where(row == 0, 1.0, 0.0).astype(F32)


def _rope_rows(x1, x2, cos, sin):
    return x1 * cos - x2 * sin, x1 * sin + x2 * cos


def _mla_proj_kernel(x_ref, pos_ref, g_ref, w_in_ref, w_pe_ref, gq_lat_ref, gkv_lat_ref,
                     w_uq_ref, w_ukv_ref, gqn_ref, gkn_ref, inv_freq_ref,
                     qt_ref, k_ref, vt_ref):
    tm = x_ref.shape[0]
    xn = _rms(x_ref[...], g_ref[...]).astype(BF16)
    z = jnp.dot(xn, w_in_ref[...], preferred_element_type=F32)
    cq = _rms(z[:, :MLA_Q_RANK], gq_lat_ref[...]).astype(BF16)
    ckv = _rms(z[:, MLA_Q_RANK:], gkv_lat_ref[...]).astype(BF16)
    qt_all = _dot_nt(w_uq_ref[...], cq)
    kvt_all = _dot_nt(w_ukv_ref[...], ckv)
    kpe_t = _dot_nt(w_pe_ref[...], xn)

    ang = pos_ref[...].astype(F32) * inv_freq_ref[...]
    cos = jnp.cos(ang)
    sin = jnp.sin(ang)
    half = MLA_ROPE // 2
    q_scale = (MLA_QK ** -0.5) * LOG2E
    gqn = gqn_ref[...] * q_scale
    gkn = gkn_ref[...]
    kpe_sq = jnp.sum(kpe_t * kpe_t, axis=0, keepdims=True)
    ones_rows = _ones_row_group(tm).astype(BF16)
    zeros_tail = jnp.zeros((QK_PAD - MLA_QK, tm), F32)

    for h in range(HEADS):
        qh = qt_all[h * MLA_QK:(h + 1) * MLA_QK]
        qn = qh * lax.rsqrt(jnp.mean(qh * qh, axis=0, keepdims=True) + EPS) * gqn
        q1, q2 = _rope_rows(qn[MLA_NOPE:MLA_NOPE + half], qn[MLA_NOPE + half:], cos, sin)
        qt_ref[h, 0:MLA_NOPE, :] = qn[:MLA_NOPE].astype(BF16)
        qt_ref[h, MLA_NOPE:MLA_NOPE + half, :] = q1.astype(BF16)
        qt_ref[h, MLA_NOPE + half:MLA_QK, :] = q2.astype(BF16)
        qt_ref[h, MLA_QK:QK_PAD, :] = zeros_tail.astype(BF16)

        base = h * (MLA_NOPE + HEAD_V)
        kn = kvt_all[base:base + MLA_NOPE]
        ms = (jnp.sum(kn * kn, axis=0, keepdims=True) + kpe_sq) * (1.0 / MLA_QK)
        r = lax.rsqrt(ms + EPS)
        kn = kn * r * gkn[:MLA_NOPE]
        kp = kpe_t * r * gkn[MLA_NOPE:]
        k1, k2 = _rope_rows(kp[:half], kp[half:], cos, sin)
        kt = jnp.concatenate([kn, k1, k2, zeros_tail], axis=0)
        k_ref[h] = kt.T.astype(BF16)

        vt_ref[h, 0:HEAD_V, :] = kvt_all[base + MLA_NOPE:base + MLA_NOPE + HEAD_V].astype(BF16)
        vt_ref[h, HEAD_V:V_ROWS, :] = ones_rows


def _qkv_out(b, s):
    specs = [
        pl.BlockSpec((None, HEADS, QK_PAD, PROJ_TILE), lambda bi, i: (bi, 0, 0, i)),
        pl.BlockSpec((None, HEADS, PROJ_TILE, QK_PAD), lambda bi, i: (bi, 0, i, 0)),
        pl.BlockSpec((None, HEADS, V_ROWS, PROJ_TILE), lambda bi, i: (bi, 0, 0, i)),
    ]
    shapes = [
        jax.ShapeDtypeStruct((b, HEADS, QK_PAD, s), BF16),
        jax.ShapeDtypeStruct((b, HEADS, s, QK_PAD), BF16),
        jax.ShapeDtypeStruct((b, HEADS, V_ROWS, s), BF16),
    ]
    return specs, shapes


def _mla_proj(h, pos3, g, w_in, w_pe_t, gq_lat, gkv_lat, w_uq_t, w_ukv_t, gqn, gkn, inv_freq):
    b, s, _ = h.shape
    out_specs, out_shapes = _qkv_out(b, s)
    return pl.pallas_call(
        _mla_proj_kernel,
        grid=(b, s // PROJ_TILE),
        in_specs=[
            pl.BlockSpec((None, PROJ_TILE, D_MODEL), lambda bi, i: (bi, i, 0)),
            pl.BlockSpec((None, 1, PROJ_TILE), lambda bi, i: (bi, 0, i)),
            _resident(g.shape), _resident(w_in.shape), _resident(w_pe_t.shape),
            _resident(gq_lat.shape), _resident(gkv_lat.shape),
            _resident(w_uq_t.shape), _resident(w_ukv_t.shape),
            _resident(gqn.shape), _resident(gkn.shape), _resident(inv_freq.shape),
        ],
        out_specs=out_specs,
        out_shape=out_shapes,
        compiler_params=pltpu.CompilerParams(
            dimension_semantics=("parallel", "parallel"), vmem_limit_bytes=VMEM_LIMIT),
        name="mla_proj",
    )(h, pos3, g, w_in, w_pe_t, gq_lat, gkv_lat, w_uq_t, w_ukv_t, gqn, gkn, inv_freq)


def _split3(c):
    hi = c.astype(BF16).astype(F32)
    mid = (c - hi).astype(BF16).astype(F32)
    lo = (c - hi - mid).astype(BF16).astype(F32)
    return hi, mid, lo


def _fox_proj_kernel(x_ref, g_ref, w_ref, bf_ref, gqn_ref, gkn_ref,
                     qt_ref, k_ref, vt_ref, c_ref, carry_ref):
    tm = x_ref.shape[0]
    hd = HEADS * FOX_HEAD_DIM

    @pl.when(pl.program_id(1) == 0)
    def _():
        carry_ref[...] = jnp.zeros_like(carry_ref)

    xn = _rms(x_ref[...], g_ref[...]).astype(BF16)
    qt_all = _dot_nt(w_ref[0:hd, :], xn)
    kt_all = _dot_nt(w_ref[hd:2 * hd, :], xn)
    vt_all = _dot_nt(w_ref[2 * hd:3 * hd, :], xn)
    f_t = _dot_nt(w_ref[3 * hd:3 * hd + HEADS, :], xn) + bf_ref[...]

    logf = (jnp.minimum(f_t, 0.0) - jnp.log1p(jnp.exp(-jnp.abs(f_t)))) * LOG2E
    hi, mid, lo = _split3(logf)
    parts = jnp.concatenate([hi, mid, lo], axis=0).astype(BF16)
    src = lax.broadcasted_iota(jnp.int32, (tm, tm), 0)
    dst = lax.broadcasted_iota(jnp.int32, (tm, tm), 1)
    upper = jnp.where(src <= dst, 1.0, 0.0).astype(BF16)
    sums = jnp.dot(parts, upper, preferred_element_type=F32)
    local = sums[2 * HEADS:] + sums[HEADS:2 * HEADS] + sums[:HEADS]
    c = local + carry_ref[:, 0:1]
    carry_ref[...] = jnp.broadcast_to(c[:, tm - 1:tm], carry_ref.shape)
    c_ref[...] = c

    q_scale = (FOX_HEAD_DIM ** -0.5) * LOG2E
    gqn = gqn_ref[...] * q_scale
    gkn = gkn_ref[...]
    row = lax.broadcasted_iota(jnp.int32, (BF16_SUBLANES, tm), 0)
    ones_rows = _ones_row_group(tm).astype(BF16)
    zeros_tail = jnp.zeros((QK_PAD - FOX_HEAD_DIM - BF16_SUBLANES, tm), F32)

    def bias_rows(first, second):
        out = jnp.zeros((BF16_SUBLANES, tm), F32)
        for j in range(3):
            out = jnp.where(row == j, first[j], out)
            out = jnp.where(row == 8 + j, second[j], out)
        return out

    one3 = (1.0, 1.0, 1.0)
    for h in range(HEADS):
        sl = slice(h * FOX_HEAD_DIM, (h + 1) * FOX_HEAD_DIM)
        c3 = _split3(c[h:h + 1])
        qh = qt_all[sl]
        qn = qh * lax.rsqrt(jnp.mean(qh * qh, axis=0, keepdims=True) + EPS) * gqn
        qt_ref[h, 0:FOX_HEAD_DIM, :] = qn.astype(BF16)
        qt_ref[h, FOX_HEAD_DIM:FOX_HEAD_DIM + BF16_SUBLANES, :] = bias_rows(one3, c3).astype(BF16)
        qt_ref[h, FOX_HEAD_DIM + BF16_SUBLANES:QK_PAD, :] = zeros_tail.astype(BF16)

        kh = kt_all[sl]
        kn = kh * lax.rsqrt(jnp.mean(kh * kh, axis=0, keepdims=True) + EPS) * gkn
        neg3 = tuple(-p for p in c3)
        kt = jnp.concatenate([kn, bias_rows(neg3, one3), zeros_tail], axis=0)
        k_ref[h] = kt.T.astype(BF16)

        vt_ref[h, 0:HEAD_V, :] = vt_all[sl].astype(BF16)
        vt_ref[h, HEAD_V:V_ROWS, :] = ones_rows


def _fox_proj(h, g, w_t, b_f, gqn, gkn):
    b, s, _ = h.shape
    out_specs, out_shapes = _qkv_out(b, s)
    out_specs = out_specs + [pl.BlockSpec((None, HEADS, PROJ_TILE), lambda bi, i: (bi, 0, i))]
    out_shapes = out_shapes + [jax.ShapeDtypeStruct((b, HEADS, s), F32)]
    return pl.pallas_call(
        _fox_proj_kernel,
        grid=(b, s // PROJ_TILE),
        in_specs=[
            pl.BlockSpec((None, PROJ_TILE, D_MODEL), lambda bi, i: (bi, i, 0)),
            _resident(g.shape), _resident(w_t.shape), _resident(b_f.shape),
            _resident(gqn.shape), _resident(gkn.shape),
        ],
        out_specs=out_specs,
        out_shape=out_shapes,
        scratch_shapes=[pltpu.VMEM((HEADS, LANES), F32)],
        compiler_params=pltpu.CompilerParams(
            dimension_semantics=("arbitrary", "arbitrary"), vmem_limit_bytes=VMEM_LIMIT),
        name="fox_proj",
    )(h, g, w_t, b_f, gqn, gkn)


def _attn_kernel(cq_ref, cg_ref, thr_ref, qt_ref, k_ref, vt_ref, o_ref, acc_ref, m_ref, *bufs,
                 seq, tq, tk, cw):
    ncol = tq // cw
    nq = seq // tq
    all_chains = tuple(range(ncol))
    s_refs = bufs[:NSLOT]
    bm_refs = bufs[NSLOT:]
    bias_base = (pl.program_id(0) * pl.num_programs(1) + pl.program_id(1)) * nq

    def first_group(t):
        def dead(g):
            return jnp.logical_and(
                g < t - 1, cq_ref[bias_base + t] - cg_ref[bias_base + g] < -thr_ref[0])
        return lax.while_loop(dead, lambda g: g + 1, jnp.int32(0))

    n_parts = tk // PV_KEYS
    full_plan = {c: tuple((r, False) for r in range(n_parts)) for c in all_chains}

    nd = tq // tk
    assert nd == NSLOT and MAIN_UNROLL % NSLOT == 0 and LAG < NSLOT

    def diag_plan(d):
        plan = {}
        for c in all_chains:
            parts = []
            for r in range(n_parts):
                key_lo = d * tk + r * PV_KEYS
                if key_lo <= (c + 1) * cw - 1:
                    parts.append((r, key_lo + PV_KEYS - 1 > c * cw))
            if parts:
                plan[c] = tuple(parts)
        return plan

    diag_plans = tuple(diag_plan(d) for d in range(nd))

    def stage_a_parts(q0, k0, slot, c, parts):
        cols = slice(c * cw, (c + 1) * cw)
        qt = qt_ref[:, pl.ds(q0 + c * cw, cw)]
        bm = None
        for r, masked in parts:
            rows = slice(r * PV_KEYS, (r + 1) * PV_KEYS)
            kb = k_ref[pl.ds(k0 + r * PV_KEYS, PV_KEYS), :]
            s = jnp.dot(kb, qt, preferred_element_type=F32)
            if masked:
                kpos = k0 + r * PV_KEYS + lax.broadcasted_iota(jnp.int32, (PV_KEYS, cw), 0)
                qpos = q0 + c * cw + lax.broadcasted_iota(jnp.int32, (PV_KEYS, cw), 1)
                s = jnp.where(kpos <= qpos, s, MASK_VALUE)
            s_refs[slot][c, rows, :] = s
            pm = jnp.max(s, axis=0, keepdims=True)
            bm = pm if bm is None else jnp.maximum(bm, pm)
            if r == parts[-1][0]:
                bm_refs[slot][:, cols] = bm
            yield

    def stage_b_parts(k0, slot, c, parts):
        cols = slice(c * cw, (c + 1) * cw)
        m_old = m_ref[:, cols]
        m_new = jnp.maximum(m_old, bm_refs[slot][:, cols])
        alpha = jnp.exp2(m_old - m_new)
        pv = None
        for r, _ in parts:
            rows = slice(r * PV_KEYS, (r + 1) * PV_KEYS)
            p = jnp.exp2(s_refs[slot][c, rows, :] - m_new).astype(BF16)
            vb = vt_ref[:, pl.ds(k0 + r * PV_KEYS, PV_KEYS)]
            t = jnp.dot(vb, p, preferred_element_type=F32)
            pv = t if pv is None else pv + t
            if r == parts[-1][0]:
                acc_ref[c] = alpha * acc_ref[c] + pv
                m_ref[:, cols] = m_new
            yield

    def step(q0, k0, rslot, nxt_q0, nxt_k0, a_plan=full_plan, b_plan=full_plan):
        for c in all_chains:
            gens = []
            if a_plan is not None and c in a_plan:
                gens.append(stage_a_parts(nxt_q0, nxt_k0, (rslot + LAG) % NSLOT, c, a_plan[c]))
            if b_plan is not None and c in b_plan:
                gens.append(stage_b_parts(k0, rslot, c, b_plan[c]))
            while gens:
                gens = [g for g in gens if next(g, gens) is not gens]

    def q_tile(qi, carry):
        q0 = pl.multiple_of(qi * tq, tq)
        acc_ref[...] = jnp.zeros_like(acc_ref)
        m_ref[...] = jnp.full_like(m_ref, M_INIT)

        def run_steps(k0, count):
            for j in range(count):
                step(q0, k0 + j * tk, j % NSLOT, q0, k0 + (j + LAG) * tk)

        g_first = first_group(qi)
        k_first = pl.multiple_of(g_first * tq, tq)
        n_main = NSLOT * (jnp.maximum(qi - 1, 0) - g_first)
        rem = n_main & (MAIN_UNROLL - 1)

        def main(t, carry2):
            run_steps(pl.multiple_of(k_first + t * MAIN_UNROLL * tk, tq), MAIN_UNROLL)
            return carry2

        lax.fori_loop(0, lax.shift_right_logical(n_main, MAIN_UNROLL.bit_length() - 1), main, 0)
        k_last = q0 - NSLOT * tk
        sub = MAIN_UNROLL // 2
        while sub >= NSLOT:
            k_sub = pl.multiple_of(k_last - ((rem & (sub - 1)) + sub) * tk, NSLOT * tk)

            @pl.when((rem & sub) != 0)
            def _(k_sub=k_sub, sub=sub):
                run_steps(k_sub, sub)

            sub //= 2

        nxt_qi = jnp.minimum(qi + 1, nq - 1)
        nxt_q0 = pl.multiple_of(nxt_qi * tq, tq)
        nxt_k_first = pl.multiple_of(first_group(nxt_qi) * tq, tq)

        def last_below():
            k0 = pl.multiple_of(k_last, NSLOT * tk)
            for j in range(NSLOT):
                tgt = j + LAG
                if tgt < NSLOT:
                    step(q0, k0 + j * tk, j, q0, k0 + tgt * tk)
                else:
                    step(q0, k0 + j * tk, j, q0, q0 + (tgt - NSLOT) * tk,
                         a_plan=diag_plans[tgt - NSLOT])

        def diagonal():
            for d in range(nd):
                tgt = d + LAG
                if tgt < nd:
                    step(q0, q0 + d * tk, d % NSLOT, q0, q0 + tgt * tk,
                         a_plan=diag_plans[tgt], b_plan=diag_plans[d])
                else:
                    step(q0, q0 + d * tk, d % NSLOT, nxt_q0, nxt_k_first + (tgt - nd) * tk,
                         b_plan=diag_plans[d])

        @pl.when(qi >= 1)
        def _():
            last_below()
            diagonal()

        @pl.when(qi == 0)
        def _():
            diagonal()

        for c in all_chains:
            acc = acc_ref[c]
            o = acc[:HEAD_V] / acc[HEAD_V:HEAD_V + 1]
            o_ref[:, pl.ds(q0 + c * cw, cw)] = o.astype(o_ref.dtype)
        return carry

    for j in range(LAG):
        step(0, 0, (j - LAG) % NSLOT, 0, j * tk, a_plan=diag_plans[j], b_plan=None)
    lax.fori_loop(0, nq, q_tile, 0)


def _attention(qt, k, vt, bias=None, bound=0.0):
    b, heads, _, s = qt.shape
    tq = min(ATT_TQ, s)
    tk = min(ATT_TK, tq // NSLOT)
    nq = s // tq
    if bias is None:
        cq = cg = jnp.zeros((b * heads * nq,), F32)
    else:
        cq = bias[:, :, ::tq].reshape(-1)
        cg = bias[:, :, tq - 1::tq].reshape(-1)
    thr = jnp.reshape(2.0 * bound + ZERO_WEIGHT_LOG2, (1,)).astype(F32)
    kern = functools.partial(_attn_kernel, seq=s, tq=tq, tk=tk, cw=ATT_CW)
    return pl.pallas_call(
        kern,
        grid_spec=pltpu.PrefetchScalarGridSpec(
            num_scalar_prefetch=3,
            grid=(b, heads),
            in_specs=[
                pl.BlockSpec((None, None, QK_PAD, s), lambda bi, hi, *_: (bi, hi, 0, 0)),
                pl.BlockSpec((None, None, s, QK_PAD), lambda bi, hi, *_: (bi, hi, 0, 0)),
                pl.BlockSpec((None, None, V_ROWS, s), lambda bi, hi, *_: (bi, hi, 0, 0)),
            ],
            out_specs=pl.BlockSpec((None, HEAD_V, s), lambda bi, hi, *_: (bi, hi, 0)),
            scratch_shapes=([pltpu.VMEM((tq // ATT_CW, V_ROWS, ATT_CW), F32),
                             pltpu.VMEM((1, tq), F32)]
                            + [pltpu.VMEM((tq // ATT_CW, tk, ATT_CW), F32)] * NSLOT
                            + [pltpu.VMEM((1, tq), F32)] * NSLOT),
        ),
        out_shape=jax.ShapeDtypeStruct((b, heads * HEAD_V, s), BF16),
        compiler_params=pltpu.CompilerParams(
            dimension_semantics=("parallel", "parallel"), vmem_limit_bytes=VMEM_LIMIT),
        name="attention",
    )(cq, cg, thr, qt, k, vt)


def kernel(x, p, positions, g_ffn1, g_mix, g_ffn2, g_ple, ffn1_w_in, ffn1_w_out, ffn2_w_in, ffn2_w_out, ple_w_proj, ple_w_gate, mla_w_in, mla_g_q_lat, mla_w_uq, mla_g_kv_lat, mla_w_ukv, mla_g_qn, mla_g_kn, mla_w_o, fox_w_in, fox_b_f, fox_g_qn, fox_g_kn, fox_w_o):
    b, s, d = x.shape
    depth = g_ffn1.shape[0]
    assert d == D_MODEL and s % PROJ_TILE == 0 and s % ROW_TILE == 0
    n = b * s
    bf = lambda w: w.astype(BF16)
    row = lambda g: g.reshape(1, -1)
    col = lambda g: g.reshape(-1, 1)

    pos3 = positions.reshape(b, 1, s)
    half = MLA_ROPE // 2
    inv_freq = (ROPE_THETA ** (-jnp.arange(0, MLA_ROPE, 2, dtype=F32) / MLA_ROPE)).reshape(half, 1)
    lat = MLA_Q_RANK + MLA_KV_RANK

    h = x
    for i in range(depth):
        j = i // N_MIXERS
        h = _ffn(h.reshape(n, d), row(g_ffn1[i]), bf(ffn1_w_in[i]), bf(ffn1_w_out[i])).reshape(b, s, d)
        if i % N_MIXERS == 0:
            qt, k, vt = _mla_proj(
                h, pos3, row(g_mix[i]), bf(mla_w_in[j][:, :lat]), bf(mla_w_in[j][:, lat:].T),
                row(mla_g_q_lat[j]), row(mla_g_kv_lat[j]), bf(mla_w_uq[j].T), bf(mla_w_ukv[j].T),
                col(mla_g_qn[j]), col(mla_g_kn[j]), inv_freq)
            w_o = mla_w_o[j]
            ot = _attention(qt, k, vt)
        else:
            qt, k, vt, c = _fox_proj(h, row(g_mix[i]), bf(fox_w_in[j].T), col(fox_b_f[j]),
                                     col(fox_g_qn[j]), col(fox_g_kn[j]))
            w_o = fox_w_o[j]
            bound = (jnp.max(jnp.abs(fox_g_qn[j])) * LOG2E
                     * jnp.max(jnp.abs(fox_g_kn[j])) * FOX_HEAD_DIM ** 0.5 * BF16_NORM_SLACK)
            ot = _attention(qt, k, vt, bias=c, bound=bound)
        h = _post(h.reshape(n, d), ot, bf(w_o), row(g_ffn2[i]), bf(ffn2_w_in[i]), bf(ffn2_w_out[i]),
                  p[i].reshape(n, PLE_DIM), row(g_ple[i]), bf(ple_w_gate[i]),
                  bf(ple_w_proj[i])).reshape(b, s, d)
    return h
```

```python
import functools
import math

import jax
import jax.numpy as jnp
from jax import lax
from jax.experimental import pallas as pl
from jax.experimental.pallas import tpu as pltpu

F32 = jnp.float32
BF16 = jnp.bfloat16

D_MODEL = 1024
D_FF = 2816
PLE_DIM = 256
N_MIXERS = 2
FFN_HALF = 0.5
HEADS = 16
MLA_Q_RANK = 512
MLA_KV_RANK = 256
MLA_NOPE = 64
MLA_ROPE = 32
MLA_QK = MLA_NOPE + MLA_ROPE
HEAD_V = 64
FOX_HEAD_DIM = 64
ROPE_THETA = 10000.0
EPS = 1e-6
LOG2E = math.log2(math.e)

LANES = 128
MXU_DIM = 256
BF16_SUBLANES = 16
QK_PAD = 128
V_ROWS = HEAD_V + BF16_SUBLANES
VMEM_LIMIT = 52 * 1024 * 1024

ROW_TILE = 512
PROJ_TILE = 512
ATT_TQ = 1024
ATT_CW = MXU_DIM
ATT_TK = MXU_DIM
MAIN_UNROLL = 16
NSLOT = 4
LAG = 2
PV_KEYS = MXU_DIM
MASK_VALUE = -jnp.inf
M_INIT = -1e30
ZERO_WEIGHT_LOG2 = 170.0
BF16_NORM_SLACK = 1.02


def _rms(x, g):
    return x * lax.rsqrt(jnp.mean(x * x, axis=-1, keepdims=True) + EPS) * g


def _resident(shape):
    return pl.BlockSpec(shape, lambda *_: (0,) * len(shape), pipeline_mode=pl.Buffered(1))


def _swiglu_half_step(x, g, w_in_ref, w_out_ref):
    xn = _rms(x, g).astype(BF16)
    gu = jnp.dot(xn, w_in_ref[...], preferred_element_type=F32)
    gate = gu[:, :D_FF]
    up = gu[:, D_FF:]
    act = (gate * jax.nn.sigmoid(gate) * up).astype(BF16)
    return x + FFN_HALF * jnp.dot(act, w_out_ref[...], preferred_element_type=F32)


def _ffn_kernel(x_ref, g_ref, w_in_ref, w_out_ref, o_ref):
    o_ref[...] = _swiglu_half_step(x_ref[...], g_ref[...], w_in_ref, w_out_ref)


def _ffn(h2d, g, w_in, w_out):
    n = h2d.shape[0]
    return pl.pallas_call(
        _ffn_kernel,
        grid=(n // ROW_TILE,),
        in_specs=[
            pl.BlockSpec((ROW_TILE, D_MODEL), lambda i: (i, 0)),
            _resident((1, D_MODEL)),
            _resident((D_MODEL, 2 * D_FF)),
            _resident((D_FF, D_MODEL)),
        ],
        out_specs=pl.BlockSpec((ROW_TILE, D_MODEL), lambda i: (i, 0)),
        out_shape=jax.ShapeDtypeStruct((n, D_MODEL), F32),
        compiler_params=pltpu.CompilerParams(
            dimension_semantics=("parallel",), vmem_limit_bytes=VMEM_LIMIT),
        name="ffn",
    )(h2d, g, w_in, w_out)


def _post_kernel(x_ref, ot_ref, wo_ref, g2_ref, w_in_ref, w_out_ref,
                 p_ref, gp_ref, wg_ref, wp_ref, o_ref):
    x = x_ref[...] + lax.dot_general(ot_ref[...], wo_ref[...], (((0,), (0,)), ((), ())),
                                     preferred_element_type=F32)
    x = _swiglu_half_step(x, g2_ref[...], w_in_ref, w_out_ref)
    xn = _rms(x, gp_ref[...]).astype(BF16)
    gate = jax.nn.sigmoid(jnp.dot(xn, wg_ref[...], preferred_element_type=F32))
    pp = jnp.dot(p_ref[...].astype(BF16), wp_ref[...], preferred_element_type=F32)
    o_ref[...] = x + gate * pp


def _post(h2d, ot, w_o, g2, w_in, w_out, p2d, g_ple, w_gate, w_proj):
    n = h2d.shape[0]
    hd, s = ot.shape[1], ot.shape[2]
    tiles = s // ROW_TILE
    rows = pl.BlockSpec((ROW_TILE, D_MODEL), lambda i: (i, 0))
    return pl.pallas_call(
        _post_kernel,
        grid=(n // ROW_TILE,),
        in_specs=[
            rows,
            pl.BlockSpec((None, hd, ROW_TILE), lambda i: (i // tiles, 0, i % tiles)),
            _resident((hd, D_MODEL)),
            _resident((1, D_MODEL)),
            _resident((D_MODEL, 2 * D_FF)),
            _resident((D_FF, D_MODEL)),
            pl.BlockSpec((ROW_TILE, PLE_DIM), lambda i: (i, 0)),
            _resident((1, D_MODEL)),
            _resident((D_MODEL, D_MODEL)),
            _resident((PLE_DIM, D_MODEL)),
        ],
        out_specs=rows,
        out_shape=jax.ShapeDtypeStruct((n, D_MODEL), F32),
        compiler_params=pltpu.CompilerParams(
            dimension_semantics=("parallel",), vmem_limit_bytes=VMEM_LIMIT),
        name="post",
    )(h2d, ot, w_o, g2, w_in, w_out, p2d, g_ple, w_gate, w_proj)


def _dot_nt(w, x):
    return lax.dot_general(w, x, (((1,), (1,)), ((), ())), preferred_element_type=F32)


def _ones_row_group(width):
    row = lax.broadcasted_iota(jnp.int32, (BF16_SUBLANES, width), 0)
    return jnp.where(row == 0, 1.0, 0.0).astype(F32)


def _rope_rows(x1, x2, cos, sin):
    return x1 * cos - x2 * sin, x1 * sin + x2 * cos


def _mla_proj_kernel(x_ref, pos_ref, g_ref, w_in_ref, w_pe_ref, gq_lat_ref, gkv_lat_ref,
                     w_uq_ref, w_ukv_ref, gqn_ref, gkn_ref, inv_freq_ref,
                     qt_ref, k_ref, vt_ref):
    tm = x_ref.shape[0]
    xn = _rms(x_ref[...], g_ref[...]).astype(BF16)
    z = jnp.dot(xn, w_in_ref[...], preferred_element_type=F32)
    cq = _rms(z[:, :MLA_Q_RANK], gq_lat_ref[...]).astype(BF16)
    ckv = _rms(z[:, MLA_Q_RANK:], gkv_lat_ref[...]).astype(BF16)
    qt_all = _dot_nt(w_uq_ref[...], cq)
    kvt_all = _dot_nt(w_ukv_ref[...], ckv)
    kpe_t = _dot_nt(w_pe_ref[...], xn)

    ang = pos_ref[...].astype(F32) * inv_freq_ref[...]
    cos = jnp.cos(ang)
    sin = jnp.sin(ang)
    half = MLA_ROPE // 2
    q_scale = (MLA_QK ** -0.5) * LOG2E
    gqn = gqn_ref[...] * q_scale
    gkn = gkn_ref[...]
    kpe_sq = jnp.sum(kpe_t * kpe_t, axis=0, keepdims=True)
    ones_rows = _ones_row_group(tm).astype(BF16)
    zeros_tail = jnp.zeros((QK_PAD - MLA_QK, tm), F32)

    for h in range(HEADS):
        qh = qt_all[h * MLA_QK:(h + 1) * MLA_QK]
        qn = qh * lax.rsqrt(jnp.mean(qh * qh, axis=0, keepdims=True) + EPS) * gqn
        q1, q2 = _rope_rows(qn[MLA_NOPE:MLA_NOPE + half], qn[MLA_NOPE + half:], cos, sin)
        qt_ref[h, 0:MLA_NOPE, :] = qn[:MLA_NOPE].astype(BF16)
        qt_ref[h, MLA_NOPE:MLA_NOPE + half, :] = q1.astype(BF16)
        qt_ref[h, MLA_NOPE + half:MLA_QK, :] = q2.astype(BF16)
        qt_ref[h, MLA_QK:QK_PAD, :] = zeros_tail.astype(BF16)

        base = h * (MLA_NOPE + HEAD_V)
        kn = kvt_all[base:base + MLA_NOPE]
        ms = (jnp.sum(kn * kn, axis=0, keepdims=True) + kpe_sq) * (1.0 / MLA_QK)
        r = lax.rsqrt(ms + EPS)
        kn = kn * r * gkn[:MLA_NOPE]
        kp = kpe_t * r * gkn[MLA_NOPE:]
        k1, k2 = _rope_rows(kp[:half], kp[half:], cos, sin)
        kt = jnp.concatenate([kn, k1, k2, zeros_tail], axis=0)
        k_ref[h] = kt.T.astype(BF16)

        vt_ref[h, 0:HEAD_V, :] = kvt_all[base + MLA_NOPE:base + MLA_NOPE + HEAD_V].astype(BF16)
        vt_ref[h, HEAD_V:V_ROWS, :] = ones_rows


def _qkv_out(b, s):
    specs = [
        pl.BlockSpec((None, HEADS, QK_PAD, PROJ_TILE), lambda bi, i: (bi, 0, 0, i)),
        pl.BlockSpec((None, HEADS, PROJ_TILE, QK_PAD), lambda bi, i: (bi, 0, i, 0)),
        pl.BlockSpec((None, HEADS, V_ROWS, PROJ_TILE), lambda bi, i: (bi, 0, 0, i)),
    ]
    shapes = [
        jax.ShapeDtypeStruct((b, HEADS, QK_PAD, s), BF16),
        jax.ShapeDtypeStruct((b, HEADS, s, QK_PAD), BF16),
        jax.ShapeDtypeStruct((b, HEADS, V_ROWS, s), BF16),
    ]
    return specs, shapes


def _mla_proj(h, pos3, g, w_in, w_pe_t, gq_lat, gkv_lat, w_uq_t, w_ukv_t, gqn, gkn, inv_freq):
    b, s, _ = h.shape
    out_specs, out_shapes = _qkv_out(b, s)
    return pl.pallas_call(
        _mla_proj_kernel,
        grid=(b, s // PROJ_TILE),
        in_specs=[
            pl.BlockSpec((None, PROJ_TILE, D_MODEL), lambda bi, i: (bi, i, 0)),
            pl.BlockSpec((None, 1, PROJ_TILE), lambda bi, i: (bi, 0, i)),
            _resident(g.shape), _resident(w_in.shape), _resident(w_pe_t.shape),
            _resident(gq_lat.shape), _resident(gkv_lat.shape),
            _resident(w_uq_t.shape), _resident(w_ukv_t.shape),
            _resident(gqn.shape), _resident(gkn.shape), _resident(inv_freq.shape),
        ],
        out_specs=out_specs,
        out_shape=out_shapes,
        compiler_params=pltpu.CompilerParams(
            dimension_semantics=("parallel", "parallel"), vmem_limit_bytes=VMEM_LIMIT),
        name="mla_proj",
    )(h, pos3, g, w_in, w_pe_t, gq_lat, gkv_lat, w_uq_t, w_ukv_t, gqn, gkn, inv_freq)


def _split3(c):
    hi = c.astype(BF16).astype(F32)
    mid = (c - hi).astype(BF16).astype(F32)
    lo = (c - hi - mid).astype(BF16).astype(F32)
    return hi, mid, lo


def _fox_proj_kernel(x_ref, g_ref, w_ref, bf_ref, gqn_ref, gkn_ref,
                     qt_ref, k_ref, vt_ref, c_ref, carry_ref):
    tm = x_ref.shape[0]
    hd = HEADS * FOX_HEAD_DIM

    @pl.when(pl.program_id(1) == 0)
    def _():
        carry_ref[...] = jnp.zeros_like(carry_ref)

    xn = _rms(x_ref[...], g_ref[...]).astype(BF16)
    qt_all = _dot_nt(w_ref[0:hd, :], xn)
    kt_all = _dot_nt(w_ref[hd:2 * hd, :], xn)
    vt_all = _dot_nt(w_ref[2 * hd:3 * hd, :], xn)
    f_t = _dot_nt(w_ref[3 * hd:3 * hd + HEADS, :], xn) + bf_ref[...]

    logf = (jnp.minimum(f_t, 0.0) - jnp.log1p(jnp.exp(-jnp.abs(f_t)))) * LOG2E
    hi, mid, lo = _split3(logf)
    parts = jnp.concatenate([hi, mid, lo], axis=0).astype(BF16)
    src = lax.broadcasted_iota(jnp.int32, (tm, tm), 0)
    dst = lax.broadcasted_iota(jnp.int32, (tm, tm), 1)
    upper = jnp.where(src <= dst, 1.0, 0.0).astype(BF16)
    sums = jnp.dot(parts, upper, preferred_element_type=F32)
    local = sums[2 * HEADS:] + sums[HEADS:2 * HEADS] + sums[:HEADS]
    c = local + carry_ref[:, 0:1]
    carry_ref[...] = jnp.broadcast_to(c[:, tm - 1:tm], carry_ref.shape)
    c_ref[...] = c

    q_scale = (FOX_HEAD_DIM ** -0.5) * LOG2E
    gqn = gqn_ref[...] * q_scale
    gkn = gkn_ref[...]
    row = lax.broadcasted_iota(jnp.int32, (BF16_SUBLANES, tm), 0)
    ones_rows = _ones_row_group(tm).astype(BF16)
    zeros_tail = jnp.zeros((QK_PAD - FOX_HEAD_DIM - BF16_SUBLANES, tm), F32)

    def bias_rows(first, second):
        out = jnp.zeros((BF16_SUBLANES, tm), F32)
        for j in range(3):
            out = jnp.where(row == j, first[j], out)
            out = jnp.where(row == 8 + j, second[j], out)
        return out

    one3 = (1.0, 1.0, 1.0)
    for h in range(HEADS):
        sl = slice(h * FOX_HEAD_DIM, (h + 1) * FOX_HEAD_DIM)
        c3 = _split3(c[h:h + 1])
        qh = qt_all[sl]
        qn = qh * lax.rsqrt(jnp.mean(qh * qh, axis=0, keepdims=True) + EPS) * gqn
        qt_ref[h, 0:FOX_HEAD_DIM, :] = qn.astype(BF16)
        qt_ref[h, FOX_HEAD_DIM:FOX_HEAD_DIM + BF16_SUBLANES, :] = bias_rows(one3, c3).astype(BF16)
        qt_ref[h, FOX_HEAD_DIM + BF16_SUBLANES:QK_PAD, :] = zeros_tail.astype(BF16)

        kh = kt_all[sl]
        kn = kh * lax.rsqrt(jnp.mean(kh * kh, axis=0, keepdims=True) + EPS) * gkn
        neg3 = tuple(-p for p in c3)
        kt = jnp.concatenate([kn, bias_rows(neg3, one3), zeros_tail], axis=0)
        k_ref[h] = kt.T.astype(BF16)

        vt_ref[h, 0:HEAD_V, :] = vt_all[sl].astype(BF16)
        vt_ref[h, HEAD_V:V_ROWS, :] = ones_rows


def _fox_proj(h, g, w_t, b_f, gqn, gkn):
    b, s, _ = h.shape
    out_specs, out_shapes = _qkv_out(b, s)
    out_specs = out_specs + [pl.BlockSpec((None, HEADS, PROJ_TILE), lambda bi, i: (bi, 0, i))]
    out_shapes = out_shapes + [jax.ShapeDtypeStruct((b, HEADS, s), F32)]
    return pl.pallas_call(
        _fox_proj_kernel,
        grid=(b, s // PROJ_TILE),
        in_specs=[
            pl.BlockSpec((None, PROJ_TILE, D_MODEL), lambda bi, i: (bi, i, 0)),
            _resident(g.shape), _resident(w_t.shape), _resident(b_f.shape),
            _resident(gqn.shape), _resident(gkn.shape),
        ],
        out_specs=out_specs,
        out_shape=out_shapes,
        scratch_shapes=[pltpu.VMEM((HEADS, LANES), F32)],
        compiler_params=pltpu.CompilerParams(
            dimension_semantics=("arbitrary", "arbitrary"), vmem_limit_bytes=VMEM_LIMIT),
        name="fox_proj",
    )(h, g, w_t, b_f, gqn, gkn)


def _attn_kernel(cq_ref, cg_ref, thr_ref, qt_ref, k_ref, vt_ref, o_ref, acc_ref, m_ref, *bufs,
                 seq, tq, tk, cw):
    ncol = tq // cw
    nq = seq // tq
    all_chains = tuple(range(ncol))
    s_refs = bufs[:NSLOT]
    bm_refs = bufs[NSLOT:]
    bias_base = (pl.program_id(0) * pl.num_programs(1) + pl.program_id(1)) * nq

    def first_group(t):
        def dead(g):
            return jnp.logical_and(
                g < t - 1, cq_ref[bias_base + t] - cg_ref[bias_base + g] < -thr_ref[0])
        return lax.while_loop(dead, lambda g: g + 1, jnp.int32(0))

    n_parts = tk // PV_KEYS
    full_plan = {c: tuple((r, False) for r in range(n_parts)) for c in all_chains}

    nd = tq // tk
    assert nd == NSLOT and MAIN_UNROLL % NSLOT == 0 and LAG < NSLOT

    def diag_plan(d):
        plan = {}
        for c in all_chains:
            parts = []
            for r in range(n_parts):
                key_lo = d * tk + r * PV_KEYS
                if key_lo <= (c + 1) * cw - 1:
                    parts.append((r, key_lo + PV_KEYS - 1 > c * cw))
            if parts:
                plan[c] = tuple(parts)
        return plan

    diag_plans = tuple(diag_plan(d) for d in range(nd))

    def stage_a_parts(q0, k0, slot, c, parts):
        cols = slice(c * cw, (c + 1) * cw)
        qt = qt_ref[:, pl.ds(q0 + c * cw, cw)]
        bm = None
        for r, masked in parts:
            rows = slice(r * PV_KEYS, (r + 1) * PV_KEYS)
            kb = k_ref[pl.ds(k0 + r * PV_KEYS, PV_KEYS), :]
            s = jnp.dot(kb, qt, preferred_element_type=F32)
            if masked:
                kpos = k0 + r * PV_KEYS + lax.broadcasted_iota(jnp.int32, (PV_KEYS, cw), 0)
                qpos = q0 + c * cw + lax.broadcasted_iota(jnp.int32, (PV_KEYS, cw), 1)
                s = jnp.where(kpos <= qpos, s, MASK_VALUE)
            s_refs[slot][c, rows, :] = s
            pm = jnp.max(s, axis=0, keepdims=True)
            bm = pm if bm is None else jnp.maximum(bm, pm)
            if r == parts[-1][0]:
                bm_refs[slot][:, cols] = bm
            yield

    def stage_b_parts(k0, slot, c, parts):
        cols = slice(c * cw, (c + 1) * cw)
        m_old = m_ref[:, cols]
        m_new = jnp.maximum(m_old, bm_refs[slot][:, cols])
        alpha = jnp.exp2(m_old - m_new)
        pv = None
        for r, _ in parts:
            rows = slice(r * PV_KEYS, (r + 1) * PV_KEYS)
            p = jnp.exp2((s_refs[slot][c, rows, :] - m_new).astype(BF16))
            vb = vt_ref[:, pl.ds(k0 + r * PV_KEYS, PV_KEYS)]
            t = jnp.dot(vb, p, preferred_element_type=F32)
            pv = t if pv is None else pv + t
            if r == parts[-1][0]:
                acc_ref[c] = alpha * acc_ref[c] + pv
                m_ref[:, cols] = m_new
            yield

    def step(q0, k0, rslot, nxt_q0, nxt_k0, a_plan=full_plan, b_plan=full_plan):
        for c in all_chains:
            gens = []
            if a_plan is not None and c in a_plan:
                gens.append(stage_a_parts(nxt_q0, nxt_k0, (rslot + LAG) % NSLOT, c, a_plan[c]))
            if b_plan is not None and c in b_plan:
                gens.append(stage_b_parts(k0, rslot, c, b_plan[c]))
            while gens:
                gens = [g for g in gens if next(g, gens) is not gens]

    def q_tile(qi, carry):
        q0 = pl.multiple_of(qi * tq, tq)
        acc_ref[...] = jnp.zeros_like(acc_ref)
        m_ref[...] = jnp.full_like(m_ref, M_INIT)

        def run_steps(k0, count):
            for j in range(count):
                step(q0, k0 + j * tk, j % NSLOT, q0, k0 + (j + LAG) * tk)

        g_first = first_group(qi)
        k_first = pl.multiple_of(g_first * tq, tq)
        n_main = NSLOT * (jnp.maximum(qi - 1, 0) - g_first)
        rem = n_main & (MAIN_UNROLL - 1)

        def main(t, carry2):
            run_steps(pl.multiple_of(k_first + t * MAIN_UNROLL * tk, tq), MAIN_UNROLL)
            return carry2

        lax.fori_loop(0, lax.shift_right_logical(n_main, MAIN_UNROLL.bit_length() - 1), main, 0)
        k_last = q0 - NSLOT * tk
        sub = MAIN_UNROLL // 2
        while sub >= NSLOT:
            k_sub = pl.multiple_of(k_last - ((rem & (sub - 1)) + sub) * tk, NSLOT * tk)

            @pl.when((rem & sub) != 0)
            def _(k_sub=k_sub, sub=sub):
                run_steps(k_sub, sub)

            sub //= 2

        nxt_qi = jnp.minimum(qi + 1, nq - 1)
        nxt_q0 = pl.multiple_of(nxt_qi * tq, tq)
        nxt_k_first = pl.multiple_of(first_group(nxt_qi) * tq, tq)

        def last_below():
            k0 = pl.multiple_of(k_last, NSLOT * tk)
            for j in range(NSLOT):
                tgt = j + LAG
                if tgt < NSLOT:
                    step(q0, k0 + j * tk, j, q0, k0 + tgt * tk)
                else:
                    step(q0, k0 + j * tk, j, q0, q0 + (tgt - NSLOT) * tk,
                         a_plan=diag_plans[tgt - NSLOT])

        def diagonal():
            for d in range(nd):
                tgt = d + LAG
                if tgt < nd:
                    step(q0, q0 + d * tk, d % NSLOT, q0, q0 + tgt * tk,
                         a_plan=diag_plans[tgt], b_plan=diag_plans[d])
                else:
                    step(q0, q0 + d * tk, d % NSLOT, nxt_q0, nxt_k_first + (tgt - nd) * tk,
                         b_plan=diag_plans[d])

        @pl.when(qi >= 1)
        def _():
            last_below()
            diagonal()

        @pl.when(qi == 0)
        def _():
            diagonal()

        for c in all_chains:
            acc = acc_ref[c]
            o = acc[:HEAD_V] / acc[HEAD_V:HEAD_V + 1]
            o_ref[:, pl.ds(q0 + c * cw, cw)] = o.astype(o_ref.dtype)
        return carry

    for j in range(LAG):
        step(0, 0, (j - LAG) % NSLOT, 0, j * tk, a_plan=diag_plans[j], b_plan=None)
    lax.fori_loop(0, nq, q_tile, 0)


def _attention(qt, k, vt, bias=None, bound=0.0):
    b, heads, _, s = qt.shape
    tq = min(ATT_TQ, s)
    tk = min(ATT_TK, tq // NSLOT)
    nq = s // tq
    if bias is None:
        cq = cg = jnp.zeros((b * heads * nq,), F32)
    else:
        cq = bias[:, :, ::tq].reshape(-1)
        cg = bias[:, :, tq - 1::tq].reshape(-1)
    thr = jnp.reshape(2.0 * bound + ZERO_WEIGHT_LOG2, (1,)).astype(F32)
    kern = functools.partial(_attn_kernel, seq=s, tq=tq, tk=tk, cw=ATT_CW)
    return pl.pallas_call(
        kern,
        grid_spec=pltpu.PrefetchScalarGridSpec(
            num_scalar_prefetch=3,
            grid=(b, heads),
            in_specs=[
                pl.BlockSpec((None, None, QK_PAD, s), lambda bi, hi, *_: (bi, hi, 0, 0)),
                pl.BlockSpec((None, None, s, QK_PAD), lambda bi, hi, *_: (bi, hi, 0, 0)),
                pl.BlockSpec((None, None, V_ROWS, s), lambda bi, hi, *_: (bi, hi, 0, 0)),
            ],
            out_specs=pl.BlockSpec((None, HEAD_V, s), lambda bi, hi, *_: (bi, hi, 0)),
            scratch_shapes=([pltpu.VMEM((tq // ATT_CW, V_ROWS, ATT_CW), F32),
                             pltpu.VMEM((1, tq), F32)]
                            + [pltpu.VMEM((tq // ATT_CW, tk, ATT_CW), F32)] * NSLOT
                            + [pltpu.VMEM((1, tq), F32)] * NSLOT),
        ),
        out_shape=jax.ShapeDtypeStruct((b, heads * HEAD_V, s), BF16),
        compiler_params=pltpu.CompilerParams(
            dimension_semantics=("parallel", "parallel"), vmem_limit_bytes=VMEM_LIMIT),
        name="attention",
    )(cq, cg, thr, qt, k, vt)


def kernel(x, p, positions, g_ffn1, g_mix, g_ffn2, g_ple, ffn1_w_in, ffn1_w_out, ffn2_w_in, ffn2_w_out, ple_w_proj, ple_w_gate, mla_w_in, mla_g_q_lat, mla_w_uq, mla_g_kv_lat, mla_w_ukv, mla_g_qn, mla_g_kn, mla_w_o, fox_w_in, fox_b_f, fox_g_qn, fox_g_kn, fox_w_o):
    b, s, d = x.shape
    depth = g_ffn1.shape[0]
    assert d == D_MODEL and s % PROJ_TILE == 0 and s % ROW_TILE == 0
    n = b * s
    bf = lambda w: w.astype(BF16)
    row = lambda g: g.reshape(1, -1)
    col = lambda g: g.reshape(-1, 1)

    pos3 = positions.reshape(b, 1, s)
    half = MLA_ROPE // 2
    inv_freq = (ROPE_THETA ** (-jnp.arange(0, MLA_ROPE, 2, dtype=F32) / MLA_ROPE)).reshape(half, 1)
    lat = MLA_Q_RANK + MLA_KV_RANK

    h = x
    for i in range(depth):
        j = i // N_MIXERS
        h = _ffn(h.reshape(n, d), row(g_ffn1[i]), bf(ffn1_w_in[i]), bf(ffn1_w_out[i])).reshape(b, s, d)
        if i % N_MIXERS == 0:
            qt, k, vt = _mla_proj(
                h, pos3, row(g_mix[i]), bf(mla_w_in[j][:, :lat]), bf(mla_w_in[j][:, lat:].T),
                row(mla_g_q_lat[j]), row(mla_g_kv_lat[j]), bf(mla_w_uq[j].T), bf(mla_w_ukv[j].T),
                col(mla_g_qn[j]), col(mla_g_kn[j]), inv_freq)
            w_o = mla_w_o[j]
            ot = _attention(qt, k, vt)
        else:
            qt, k, vt, c = _fox_proj(h, row(g_mix[i]), bf(fox_w_in[j].T), col(fox_b_f[j]),
                                     col(fox_g_qn[j]), col(fox_g_kn[j]))
            w_o = fox_w_o[j]
            bound = (jnp.max(jnp.abs(fox_g_qn[j])) * LOG2E
                     * jnp.max(jnp.abs(fox_g_kn[j])) * FOX_HEAD_DIM ** 0.5 * BF16_NORM_SLACK)
            ot = _attention(qt, k, vt, bias=c, bound=bound)
        h = _post(h.reshape(n, d), ot, bf(w_o), row(g_ffn2[i]), bf(ffn2_w_in[i]), bf(ffn2_w_out[i]),
                  p[i].reshape(n, PLE_DIM), row(g_ple[i]), bf(ple_w_gate[i]),
                  bf(ple_w_proj[i])).reshape(b, s, d)
    return h
```

```python
import functools
import math

import jax
import jax.numpy as jnp
from jax import lax
from jax.experimental import pallas as pl
from jax.experimental.pallas import tpu as pltpu

F32 = jnp.float32
BF16 = jnp.bfloat16

D_MODEL = 1024
D_FF = 2816
PLE_DIM = 256
N_MIXERS = 2
FFN_HALF = 0.5
HEADS = 16
MLA_Q_RANK = 512
MLA_KV_RANK = 256
MLA_NOPE = 64
MLA_ROPE = 32
MLA_QK = MLA_NOPE + MLA_ROPE
HEAD_V = 64
FOX_HEAD_DIM = 64
ROPE_THETA = 10000.0
EPS = 1e-6
LOG2E = math.log2(math.e)

LANES = 128
MXU_DIM = 256
BF16_SUBLANES = 16
QK_PAD = 128
V_ROWS = HEAD_V + BF16_SUBLANES
VMEM_LIMIT = 52 * 1024 * 1024

ROW_TILE = 512
PROJ_TILE = 512
ATT_TQ = 1024
ATT_CW = MXU_DIM
ATT_TK = MXU_DIM
MAIN_UNROLL = 8
NSLOT = 4
LAG = 2
PV_KEYS = MXU_DIM
MASK_VALUE = -jnp.inf
M_INIT = -1e30
ZERO_WEIGHT_LOG2 = 170.0
BF16_NORM_SLACK = 1.02


def _rms(x, g):
    return x * lax.rsqrt(jnp.mean(x * x, axis=-1, keepdims=True) + EPS) * g


def _resident(shape):
    return pl.BlockSpec(shape, lambda *_: (0,) * len(shape), pipeline_mode=pl.Buffered(1))


def _swiglu_half_step(x, g, w_in_ref, w_out_ref):
    xn = _rms(x, g).astype(BF16)
    gu = jnp.dot(xn, w_in_ref[...], preferred_element_type=F32)
    gate = gu[:, :D_FF]
    up = gu[:, D_FF:]
    act = (gate * jax.nn.sigmoid(gate) * up).astype(BF16)
    return x + FFN_HALF * jnp.dot(act, w_out_ref[...], preferred_element_type=F32)


def _ffn_kernel(x_ref, g_ref, w_in_ref, w_out_ref, o_ref):
    o_ref[...] = _swiglu_half_step(x_ref[...], g_ref[...], w_in_ref, w_out_ref)


def _ffn(h2d, g, w_in, w_out):
    n = h2d.shape[0]
    return pl.pallas_call(
        _ffn_kernel,
        grid=(n // ROW_TILE,),
        in_specs=[
            pl.BlockSpec((ROW_TILE, D_MODEL), lambda i: (i, 0)),
            _resident((1, D_MODEL)),
            _resident((D_MODEL, 2 * D_FF)),
            _resident((D_FF, D_MODEL)),
        ],
        out_specs=pl.BlockSpec((ROW_TILE, D_MODEL), lambda i: (i, 0)),
        out_shape=jax.ShapeDtypeStruct((n, D_MODEL), F32),
        compiler_params=pltpu.CompilerParams(
            dimension_semantics=("parallel",), vmem_limit_bytes=VMEM_LIMIT),
        name="ffn",
    )(h2d, g, w_in, w_out)


def _post_kernel(x_ref, ot_ref, wo_ref, g2_ref, w_in_ref, w_out_ref,
                 p_ref, gp_ref, wg_ref, wp_ref, o_ref):
    x = x_ref[...] + lax.dot_general(ot_ref[...], wo_ref[...], (((0,), (0,)), ((), ())),
                                     preferred_element_type=F32)
    x = _swiglu_half_step(x, g2_ref[...], w_in_ref, w_out_ref)
    xn = _rms(x, gp_ref[...]).astype(BF16)
    gate = jax.nn.sigmoid(jnp.dot(xn, wg_ref[...], preferred_element_type=F32))
    pp = jnp.dot(p_ref[...].astype(BF16), wp_ref[...], preferred_element_type=F32)
    o_ref[...] = x + gate * pp


def _post(h2d, ot, w_o, g2, w_in, w_out, p2d, g_ple, w_gate, w_proj):
    n = h2d.shape[0]
    hd, s = ot.shape[1], ot.shape[2]
    tiles = s // ROW_TILE
    rows = pl.BlockSpec((ROW_TILE, D_MODEL), lambda i: (i, 0))
    return pl.pallas_call(
        _post_kernel,
        grid=(n // ROW_TILE,),
        in_specs=[
            rows,
            pl.BlockSpec((None, hd, ROW_TILE), lambda i: (i // tiles, 0, i % tiles)),
            _resident((hd, D_MODEL)),
            _resident((1, D_MODEL)),
            _resident((D_MODEL, 2 * D_FF)),
            _resident((D_FF, D_MODEL)),
            pl.BlockSpec((ROW_TILE, PLE_DIM), lambda i: (i, 0)),
            _resident((1, D_MODEL)),
            _resident((D_MODEL, D_MODEL)),
            _resident((PLE_DIM, D_MODEL)),
        ],
        out_specs=rows,
        out_shape=jax.ShapeDtypeStruct((n, D_MODEL), F32),
        compiler_params=pltpu.CompilerParams(
            dimension_semantics=("parallel",), vmem_limit_bytes=VMEM_LIMIT),
        name="post",
    )(h2d, ot, w_o, g2, w_in, w_out, p2d, g_ple, w_gate, w_proj)


def _dot_nt(w, x):
    return lax.dot_general(w, x, (((1,), (1,)), ((), ())), preferred_element_type=F32)


def _ones_row_group(width):
    row = lax.broadcasted_iota(jnp.int32, (BF16_SUBLANES, width), 0)
    return jnp.where(row == 0, 1.0, 0.0).astype(F32)


def _rope_rows(x1, x2, cos, sin):
    return x1 * cos - x2 * sin, x1 * sin + x2 * cos


def _mla_proj_kernel(x_ref, pos_ref, g_ref, w_in_ref, w_pe_ref, gq_lat_ref, gkv_lat_ref,
                     w_uq_ref, w_ukv_ref, gqn_ref, gkn_ref, inv_freq_ref,
                     qt_ref, k_ref, vt_ref):
    tm = x_ref.shape[0]
    xn = _rms(x_ref[...], g_ref[...]).astype(BF16)
    z = jnp.dot(xn, w_in_ref[...], preferred_element_type=F32)
    cq = _rms(z[:, :MLA_Q_RANK], gq_lat_ref[...]).astype(BF16)
    ckv = _rms(z[:, MLA_Q_RANK:], gkv_lat_ref[...]).astype(BF16)
    qt_all = _dot_nt(w_uq_ref[...], cq)
    kvt_all = _dot_nt(w_ukv_ref[...], ckv)
    kpe_t = _dot_nt(w_pe_ref[...], xn)

    ang = pos_ref[...].astype(F32) * inv_freq_ref[...]
    cos = jnp.cos(ang)
    sin = jnp.sin(ang)
    half = MLA_ROPE // 2
    q_scale = (MLA_QK ** -0.5) * LOG2E
    gqn = gqn_ref[...] * q_scale
    gkn = gkn_ref[...]
    kpe_sq = jnp.sum(kpe_t * kpe_t, axis=0, keepdims=True)
    ones_rows = _ones_row_group(tm).astype(BF16)
    zeros_tail = jnp.zeros((QK_PAD - MLA_QK, tm), F32)

    for h in range(HEADS):
        qh = qt_all[h * MLA_QK:(h + 1) * MLA_QK]
        qn = qh * lax.rsqrt(jnp.mean(qh * qh, axis=0, keepdims=True) + EPS) * gqn
        q1, q2 = _rope_rows(qn[MLA_NOPE:MLA_NOPE + half], qn[MLA_NOPE + half:], cos, sin)
        qt_ref[h, 0:MLA_NOPE, :] = qn[:MLA_NOPE].astype(BF16)
        qt_ref[h, MLA_NOPE:MLA_NOPE + half, :] = q1.astype(BF16)
        qt_ref[h, MLA_NOPE + half:MLA_QK, :] = q2.astype(BF16)
        qt_ref[h, MLA_QK:QK_PAD, :] = zeros_tail.astype(BF16)

        base = h * (MLA_NOPE + HEAD_V)
        kn = kvt_all[base:base + MLA_NOPE]
        ms = (jnp.sum(kn * kn, axis=0, keepdims=True) + kpe_sq) * (1.0 / MLA_QK)
        r = lax.rsqrt(ms + EPS)
        kn = kn * r * gkn[:MLA_NOPE]
        kp = kpe_t * r * gkn[MLA_NOPE:]
        k1, k2 = _rope_rows(kp[:half], kp[half:], cos, sin)
        kt = jnp.concatenate([kn, k1, k2, zeros_tail], axis=0)
        k_ref[h] = kt.T.astype(BF16)

        vt_ref[h, 0:HEAD_V, :] = kvt_all[base + MLA_NOPE:base + MLA_NOPE + HEAD_V].astype(BF16)
        vt_ref[h, HEAD_V:V_ROWS, :] = ones_rows


def _qkv_out(b, s):
    specs = [
        pl.BlockSpec((None, HEADS, QK_PAD, PROJ_TILE), lambda bi, i: (bi, 0, 0, i)),
        pl.BlockSpec((None, HEADS, PROJ_TILE, QK_PAD), lambda bi, i: (bi, 0, i, 0)),
        pl.BlockSpec((None, HEADS, V_ROWS, PROJ_TILE), lambda bi, i: (bi, 0, 0, i)),
    ]
    shapes = [
        jax.ShapeDtypeStruct((b, HEADS, QK_PAD, s), BF16),
        jax.ShapeDtypeStruct((b, HEADS, s, QK_PAD), BF16),
        jax.ShapeDtypeStruct((b, HEADS, V_ROWS, s), BF16),
    ]
    return specs, shapes


def _mla_proj(h, pos3, g, w_in, w_pe_t, gq_lat, gkv_lat, w_uq_t, w_ukv_t, gqn, gkn, inv_freq):
    b, s, _ = h.shape
    out_specs, out_shapes = _qkv_out(b, s)
    return pl.pallas_call(
        _mla_proj_kernel,
        grid=(b, s // PROJ_TILE),
        in_specs=[
            pl.BlockSpec((None, PROJ_TILE, D_MODEL), lambda bi, i: (bi, i, 0)),
            pl.BlockSpec((None, 1, PROJ_TILE), lambda bi, i: (bi, 0, i)),
            _resident(g.shape), _resident(w_in.shape), _resident(w_pe_t.shape),
            _resident(gq_lat.shape), _resident(gkv_lat.shape),
            _resident(w_uq_t.shape), _resident(w_ukv_t.shape),
            _resident(gqn.shape), _resident(gkn.shape), _resident(inv_freq.shape),
        ],
        out_specs=out_specs,
        out_shape=out_shapes,
        compiler_params=pltpu.CompilerParams(
            dimension_semantics=("parallel", "parallel"), vmem_limit_bytes=VMEM_LIMIT),
        name="mla_proj",
    )(h, pos3, g, w_in, w_pe_t, gq_lat, gkv_lat, w_uq_t, w_ukv_t, gqn, gkn, inv_freq)


def _split3(c):
    hi = c.astype(BF16).astype(F32)
    mid = (c - hi).astype(BF16).astype(F32)
    lo = (c - hi - mid).astype(BF16).astype(F32)
    return hi, mid, lo


def _fox_proj_kernel(x_ref, g_ref, w_ref, bf_ref, gqn_ref, gkn_ref,
                     qt_ref, k_ref, vt_ref, c_ref, carry_ref):
    tm = x_ref.shape[0]
    hd = HEADS * FOX_HEAD_DIM

    @pl.when(pl.program_id(1) == 0)
    def _():
        carry_ref[...] = jnp.zeros_like(carry_ref)

    xn = _rms(x_ref[...], g_ref[...]).astype(BF16)
    qt_all = _dot_nt(w_ref[0:hd, :], xn)
    kt_all = _dot_nt(w_ref[hd:2 * hd, :], xn)
    vt_all = _dot_nt(w_ref[2 * hd:3 * hd, :], xn)
    f_t = _dot_nt(w_ref[3 * hd:3 * hd + HEADS, :], xn) + bf_ref[...]

    logf = (jnp.minimum(f_t, 0.0) - jnp.log1p(jnp.exp(-jnp.abs(f_t)))) * LOG2E
    hi, mid, lo = _split3(logf)
    parts = jnp.concatenate([hi, mid, lo], axis=0).astype(BF16)
    src = lax.broadcasted_iota(jnp.int32, (tm, tm), 0)
    dst = lax.broadcasted_iota(jnp.int32, (tm, tm), 1)
    upper = jnp.where(src <= dst, 1.0, 0.0).astype(BF16)
    sums = jnp.dot(parts, upper, preferred_element_type=F32)
    local = sums[2 * HEADS:] + sums[HEADS:2 * HEADS] + sums[:HEADS]
    c = local + carry_ref[:, 0:1]
    carry_ref[...] = jnp.broadcast_to(c[:, tm - 1:tm], carry_ref.shape)
    c_ref[...] = c

    q_scale = (FOX_HEAD_DIM ** -0.5) * LOG2E
    gqn = gqn_ref[...] * q_scale
    gkn = gkn_ref[...]
    row = lax.broadcasted_iota(jnp.int32, (BF16_SUBLANES, tm), 0)
    ones_rows = _ones_row_group(tm).astype(BF16)
    zeros_tail = jnp.zeros((QK_PAD - FOX_HEAD_DIM - BF16_SUBLANES, tm), F32)

    def bias_rows(first, second):
        out = jnp.zeros((BF16_SUBLANES, tm), F32)
        for j in range(3):
            out = jnp.where(row == j, first[j], out)
            out = jnp.where(row == 8 + j, second[j], out)
        return out

    one3 = (1.0, 1.0, 1.0)
    for h in range(HEADS):
        sl = slice(h * FOX_HEAD_DIM, (h + 1) * FOX_HEAD_DIM)
        c3 = _split3(c[h:h + 1])
        qh = qt_all[sl]
        qn = qh * lax.rsqrt(jnp.mean(qh * qh, axis=0, keepdims=True) + EPS) * gqn
        qt_ref[h, 0:FOX_HEAD_DIM, :] = qn.astype(BF16)
        qt_ref[h, FOX_HEAD_DIM:FOX_HEAD_DIM + BF16_SUBLANES, :] = bias_rows(one3, c3).astype(BF16)
        qt_ref[h, FOX_HEAD_DIM + BF16_SUBLANES:QK_PAD, :] = zeros_tail.astype(BF16)

        kh = kt_all[sl]
        kn = kh * lax.rsqrt(jnp.mean(kh * kh, axis=0, keepdims=True) + EPS) * gkn
        neg3 = tuple(-p for p in c3)
        kt = jnp.concatenate([kn, bias_rows(neg3, one3), zeros_tail], axis=0)
        k_ref[h] = kt.T.astype(BF16)

        vt_ref[h, 0:HEAD_V, :] = vt_all[sl].astype(BF16)
        vt_ref[h, HEAD_V:V_ROWS, :] = ones_rows


def _fox_proj(h, g, w_t, b_f, gqn, gkn):
    b, s, _ = h.shape
    out_specs, out_shapes = _qkv_out(b, s)
    out_specs = out_specs + [pl.BlockSpec((None, HEADS, PROJ_TILE), lambda bi, i: (bi, 0, i))]
    out_shapes = out_shapes + [jax.ShapeDtypeStruct((b, HEADS, s), F32)]
    return pl.pallas_call(
        _fox_proj_kernel,
        grid=(b, s // PROJ_TILE),
        in_specs=[
            pl.BlockSpec((None, PROJ_TILE, D_MODEL), lambda bi, i: (bi, i, 0)),
            _resident(g.shape), _resident(w_t.shape), _resident(b_f.shape),
            _resident(gqn.shape), _resident(gkn.shape),
        ],
        out_specs=out_specs,
        out_shape=out_shapes,
        scratch_shapes=[pltpu.VMEM((HEADS, LANES), F32)],
        compiler_params=pltpu.CompilerParams(
            dimension_semantics=("arbitrary", "arbitrary"), vmem_limit_bytes=VMEM_LIMIT),
        name="fox_proj",
    )(h, g, w_t, b_f, gqn, gkn)


def _attn_kernel(cq_ref, cg_ref, thr_ref, qt_ref, k_ref, vt_ref, o_ref, acc_ref, m_ref, *bufs,
                 seq, tq, tk, cw):
    ncol = tq // cw
    nq = seq // tq
    all_chains = tuple(range(ncol))
    s_refs = bufs[:NSLOT]
    bm_refs = bufs[NSLOT:]
    bias_base = (pl.program_id(0) * pl.num_programs(1) + pl.program_id(1)) * nq

    def first_group(t):
        def dead(g):
            return jnp.logical_and(
                g < t - 1, cq_ref[bias_base + t] - cg_ref[bias_base + g] < -thr_ref[0])
        return lax.while_loop(dead, lambda g: g + 1, jnp.int32(0))

    n_parts = tk // PV_KEYS
    full_plan = {c: tuple((r, False) for r in range(n_parts)) for c in all_chains}

    nd = tq // tk
    assert nd == NSLOT and MAIN_UNROLL % NSLOT == 0 and LAG < NSLOT

    def diag_plan(d):
        plan = {}
        for c in all_chains:
            parts = []
            for r in range(n_parts):
                key_lo = d * tk + r * PV_KEYS
                if key_lo <= (c + 1) * cw - 1:
                    parts.append((r, key_lo + PV_KEYS - 1 > c * cw))
            if parts:
                plan[c] = tuple(parts)
        return plan

    diag_plans = tuple(diag_plan(d) for d in range(nd))

    def stage_a_parts(q0, k0, slot, c, parts):
        cols = slice(c * cw, (c + 1) * cw)
        qt = qt_ref[:, pl.ds(q0 + c * cw, cw)]
        bm = None
        for r, masked in parts:
            rows = slice(r * PV_KEYS, (r + 1) * PV_KEYS)
            kb = k_ref[pl.ds(k0 + r * PV_KEYS, PV_KEYS), :]
            s = jnp.dot(kb, qt, preferred_element_type=F32)
            if masked:
                kpos = k0 + r * PV_KEYS + lax.broadcasted_iota(jnp.int32, (PV_KEYS, cw), 0)
                qpos = q0 + c * cw + lax.broadcasted_iota(jnp.int32, (PV_KEYS, cw), 1)
                s = jnp.where(kpos <= qpos, s, MASK_VALUE)
            s_refs[slot][c, rows, :] = s
            pm = jnp.max(s, axis=0, keepdims=True)
            bm = pm if bm is None else jnp.maximum(bm, pm)
            if r == parts[-1][0]:
                bm_refs[slot][:, cols] = bm
            yield

    def stage_b_parts(k0, slot, c, parts):
        cols = slice(c * cw, (c + 1) * cw)
        m_old = m_ref[:, cols]
        m_new = jnp.maximum(m_old, bm_refs[slot][:, cols])
        alpha = jnp.exp2(m_old - m_new)
        pv = None
        for r, _ in parts:
            rows = slice(r * PV_KEYS, (r + 1) * PV_KEYS)
            p = jnp.exp2(s_refs[slot][c, rows, :] - m_new).astype(BF16)
            vb = vt_ref[:, pl.ds(k0 + r * PV_KEYS, PV_KEYS)]
            t = jnp.dot(vb, p, preferred_element_type=F32)
            pv = t if pv is None else pv + t
            if r == parts[-1][0]:
                acc_ref[c] = alpha * acc_ref[c] + pv
                m_ref[:, cols] = m_new
            yield

    def step(q0, k0, rslot, nxt_q0, nxt_k0, a_plan=full_plan, b_plan=full_plan):
        for c in all_chains:
            gens = []
            if a_plan is not None and c in a_plan:
                gens.append(stage_a_parts(nxt_q0, nxt_k0, (rslot + LAG) % NSLOT, c, a_plan[c]))
            if b_plan is not None and c in b_plan:
                gens.append(stage_b_parts(k0, rslot, c, b_plan[c]))
            while gens:
                gens = [g for g in gens if next(g, gens) is not gens]

    def q_tile(qi, carry):
        q0 = pl.multiple_of(qi * tq, tq)
        acc_ref[...] = jnp.zeros_like(acc_ref)
        m_ref[...] = jnp.full_like(m_ref, M_INIT)

        def run_steps(k0, count):
            for j in range(count):
                step(q0, k0 + j * tk, j % NSLOT, q0, k0 + (j + LAG) * tk)

        g_first = first_group(qi)
        k_first = pl.multiple_of(g_first * tq, tq)
        n_main = NSLOT * (jnp.maximum(qi - 1, 0) - g_first)
        rem = n_main & (MAIN_UNROLL - 1)

        def main(t, carry2):
            run_steps(pl.multiple_of(k_first + t * MAIN_UNROLL * tk, tq), MAIN_UNROLL)
            return carry2

        lax.fori_loop(0, lax.shift_right_logical(n_main, MAIN_UNROLL.bit_length() - 1), main, 0)
        k_last = q0 - NSLOT * tk
        sub = MAIN_UNROLL // 2
        while sub >= NSLOT:
            k_sub = pl.multiple_of(k_last - ((rem & (sub - 1)) + sub) * tk, NSLOT * tk)

            @pl.when((rem & sub) != 0)
            def _(k_sub=k_sub, sub=sub):
                run_steps(k_sub, sub)

            sub //= 2

        nxt_qi = jnp.minimum(qi + 1, nq - 1)
        nxt_q0 = pl.multiple_of(nxt_qi * tq, tq)
        nxt_k_first = pl.multiple_of(first_group(nxt_qi) * tq, tq)

        def last_below():
            k0 = pl.multiple_of(k_last, NSLOT * tk)
            for j in range(NSLOT):
                tgt = j + LAG
                if tgt < NSLOT:
                    step(q0, k0 + j * tk, j, q0, k0 + tgt * tk)
                else:
                    step(q0, k0 + j * tk, j, q0, q0 + (tgt - NSLOT) * tk,
                         a_plan=diag_plans[tgt - NSLOT])

        def diagonal():
            for d in range(nd):
                tgt = d + LAG
                if tgt < nd:
                    step(q0, q0 + d * tk, d % NSLOT, q0, q0 + tgt * tk,
                         a_plan=diag_plans[tgt], b_plan=diag_plans[d])
                else:
                    step(q0, q0 + d * tk, d % NSLOT, nxt_q0, nxt_k_first + (tgt - nd) * tk,
                         b_plan=diag_plans[d])

        @pl.when(qi >= 1)
        def _():
            last_below()
            diagonal()

        @pl.when(qi == 0)
        def _():
            diagonal()

        for c in all_chains:
            acc = acc_ref[c]
            o = acc[:HEAD_V] / acc[HEAD_V:HEAD_V + 1]
            o_ref[:, pl.ds(q0 + c * cw, cw)] = o.astype(o_ref.dtype)
        return carry

    for j in range(LAG):
        step(0, 0, (j - LAG) % NSLOT, 0, j * tk, a_plan=diag_plans[j], b_plan=None)
    lax.fori_loop(0, nq, q_tile, 0)


def _attention(qt, k, vt, bias=None, bound=0.0):
    b, heads, _, s = qt.shape
    tq = min(ATT_TQ, s)
    tk = min(ATT_TK, tq // NSLOT)
    nq = s // tq
    if bias is None:
        cq = cg = jnp.zeros((b * heads * nq,), F32)
    else:
        cq = bias[:, :, ::tq].reshape(-1)
        cg = bias[:, :, tq - 1::tq].reshape(-1)
    thr = jnp.reshape(2.0 * bound + ZERO_WEIGHT_LOG2, (1,)).astype(F32)
    kern = functools.partial(_attn_kernel, seq=s, tq=tq, tk=tk, cw=ATT_CW)
    return pl.pallas_call(
        kern,
        grid_spec=pltpu.PrefetchScalarGridSpec(
            num_scalar_prefetch=3,
            grid=(b, heads),
            in_specs=[
                pl.BlockSpec((None, None, QK_PAD, s), lambda bi, hi, *_: (bi, hi, 0, 0)),
                pl.BlockSpec((None, None, s, QK_PAD), lambda bi, hi, *_: (bi, hi, 0, 0)),
                pl.BlockSpec((None, None, V_ROWS, s), lambda bi, hi, *_: (bi, hi, 0, 0)),
            ],
            out_specs=pl.BlockSpec((None, HEAD_V, s), lambda bi, hi, *_: (bi, hi, 0)),
            scratch_shapes=([pltpu.VMEM((tq // ATT_CW, V_ROWS, ATT_CW), F32),
                             pltpu.VMEM((1, tq), F32)]
                            + [pltpu.VMEM((tq // ATT_CW, tk, ATT_CW), F32)] * NSLOT
                            + [pltpu.VMEM((1, tq), F32)] * NSLOT),
        ),
        out_shape=jax.ShapeDtypeStruct((b, heads * HEAD_V, s), BF16),
        compiler_params=pltpu.CompilerParams(
            dimension_semantics=("parallel", "parallel"), vmem_limit_bytes=VMEM_LIMIT),
        name="attention",
    )(cq, cg, thr, qt, k, vt)


def kernel(x, p, positions, g_ffn1, g_mix, g_ffn2, g_ple, ffn1_w_in, ffn1_w_out, ffn2_w_in, ffn2_w_out, ple_w_proj, ple_w_gate, mla_w_in, mla_g_q_lat, mla_w_uq, mla_g_kv_lat, mla_w_ukv, mla_g_qn, mla_g_kn, mla_w_o, fox_w_in, fox_b_f, fox_g_qn, fox_g_kn, fox_w_o):
    b, s, d = x.shape
    depth = g_ffn1.shape[0]
    assert d == D_MODEL and s % PROJ_TILE == 0 and s % ROW_TILE == 0
    n = b * s
    bf = lambda w: w.astype(BF16)
    row = lambda g: g.reshape(1, -1)
    col = lambda g: g.reshape(-1, 1)

    pos3 = positions.reshape(b, 1, s)
    half = MLA_ROPE // 2
    inv_freq = (ROPE_THETA ** (-jnp.arange(0, MLA_ROPE, 2, dtype=F32) / MLA_ROPE)).reshape(half, 1)
    lat = MLA_Q_RANK + MLA_KV_RANK

    h = x
    for i in range(depth):
        j = i // N_MIXERS
        h = _ffn(h.reshape(n, d), row(g_ffn1[i]), bf(ffn1_w_in[i]), bf(ffn1_w_out[i])).reshape(b, s, d)
        if i % N_MIXERS == 0:
            qt, k, vt = _mla_proj(
                h, pos3, row(g_mix[i]), bf(mla_w_in[j][:, :lat]), bf(mla_w_in[j][:, lat:].T),
                row(mla_g_q_lat[j]), row(mla_g_kv_lat[j]), bf(mla_w_uq[j].T), bf(mla_w_ukv[j].T),
                col(mla_g_qn[j]), col(mla_g_kn[j]), inv_freq)
            w_o = mla_w_o[j]
            ot = _attention(qt, k, vt)
        else:
            qt, k, vt, c = _fox_proj(h, row(g_mix[i]), bf(fox_w_in[j].T), col(fox_b_f[j]),
                                     col(fox_g_qn[j]), col(fox_g_kn[j]))
            w_o = fox_w_o[j]
            bound = (jnp.max(jnp.abs(fox_g_qn[j])) * LOG2E
                     * jnp.max(jnp.abs(fox_g_kn[j])) * FOX_HEAD_DIM ** 0.5 * BF16_NORM_SLACK)
            ot = _attention(qt, k, vt, bias=c, bound=bound)
        h = _post(h.reshape(n, d), ot, bf(w_o), row(g_ffn2[i]), bf(ffn2_w_in[i]), bf(ffn2_w_out[i]),
                  p[i].reshape(n, PLE_DIM), row(g_ple[i]), bf(ple_w_gate[i]),
                  bf(ple_w_proj[i])).reshape(b, s, d)
    return h
```

```python
import functools
import math

import jax
import jax.numpy as jnp
from jax import lax
from jax.experimental import pallas as pl
from jax.experimental.pallas import tpu as pltpu

F32 = jnp.float32
BF16 = jnp.bfloat16

D_MODEL = 1024
D_FF = 2816
PLE_DIM = 256
N_MIXERS = 2
FFN_HALF = 0.5
HEADS = 16
MLA_Q_RANK = 512
MLA_KV_RANK = 256
MLA_NOPE = 64
MLA_ROPE = 32
MLA_QK = MLA_NOPE + MLA_ROPE
HEAD_V = 64
FOX_HEAD_DIM = 64
ROPE_THETA = 10000.0
EPS = 1e-6
LOG2E = math.log2(math.e)

LANES = 128
MXU_DIM = 256
BF16_SUBLANES = 16
QK_PAD = 128
V_ROWS = HEAD_V + BF16_SUBLANES
VMEM_LIMIT = 52 * 1024 * 1024

ROW_TILE = 512
PROJ_TILE = 512
ATT_TQ = 1024
ATT_TQ_DECAY = 512
ATT_CW = MXU_DIM
ATT_TK = MXU_DIM
MAIN_UNROLL = 16
NSLOT = 4
LAG = 2
PV_KEYS = MXU_DIM
MASK_VALUE = -jnp.inf
M_INIT = -1e30
ZERO_WEIGHT_LOG2 = 170.0
BF16_NORM_SLACK = 1.02


def _rms(x, g):
    return x * lax.rsqrt(jnp.mean(x * x, axis=-1, keepdims=True) + EPS) * g


def _resident(shape):
    return pl.BlockSpec(shape, lambda *_: (0,) * len(shape), pipeline_mode=pl.Buffered(1))


def _swiglu_half_step(x, g, w_in_ref, w_out_ref):
    xn = _rms(x, g).astype(BF16)
    gu = jnp.dot(xn, w_in_ref[...], preferred_element_type=F32)
    gate = gu[:, :D_FF]
    up = gu[:, D_FF:]
    act = (gate * jax.nn.sigmoid(gate) * up).astype(BF16)
    return x + FFN_HALF * jnp.dot(act, w_out_ref[...], preferred_element_type=F32)


def _ffn_kernel(x_ref, g_ref, w_in_ref, w_out_ref, o_ref):
    o_ref[...] = _swiglu_half_step(x_ref[...], g_ref[...], w_in_ref, w_out_ref)


def _ffn(h2d, g, w_in, w_out):
    n = h2d.shape[0]
    return pl.pallas_call(
        _ffn_kernel,
        grid=(n // ROW_TILE,),
        in_specs=[
            pl.BlockSpec((ROW_TILE, D_MODEL), lambda i: (i, 0)),
            _resident((1, D_MODEL)),
            _resident((D_MODEL, 2 * D_FF)),
            _resident((D_FF, D_MODEL)),
        ],
        out_specs=pl.BlockSpec((ROW_TILE, D_MODEL), lambda i: (i, 0)),
        out_shape=jax.ShapeDtypeStruct((n, D_MODEL), F32),
        compiler_params=pltpu.CompilerParams(
            dimension_semantics=("parallel",), vmem_limit_bytes=VMEM_LIMIT),
        name="ffn",
    )(h2d, g, w_in, w_out)


def _post_kernel(x_ref, ot_ref, wo_ref, g2_ref, w_in_ref, w_out_ref,
                 p_ref, gp_ref, wg_ref, wp_ref, o_ref):
    x = x_ref[...] + lax.dot_general(ot_ref[...], wo_ref[...], (((0,), (0,)), ((), ())),
                                     preferred_element_type=F32)
    x = _swiglu_half_step(x, g2_ref[...], w_in_ref, w_out_ref)
    xn = _rms(x, gp_ref[...]).astype(BF16)
    gate = jax.nn.sigmoid(jnp.dot(xn, wg_ref[...], preferred_element_type=F32))
    pp = jnp.dot(p_ref[...].astype(BF16), wp_ref[...], preferred_element_type=F32)
    o_ref[...] = x + gate * pp


def _post(h2d, ot, w_o, g2, w_in, w_out, p2d, g_ple, w_gate, w_proj):
    n = h2d.shape[0]
    hd, s = ot.shape[1], ot.shape[2]
    tiles = s // ROW_TILE
    rows = pl.BlockSpec((ROW_TILE, D_MODEL), lambda i: (i, 0))
    return pl.pallas_call(
        _post_kernel,
        grid=(n // ROW_TILE,),
        in_specs=[
            rows,
            pl.BlockSpec((None, hd, ROW_TILE), lambda i: (i // tiles, 0, i % tiles)),
            _resident((hd, D_MODEL)),
            _resident((1, D_MODEL)),
            _resident((D_MODEL, 2 * D_FF)),
            _resident((D_FF, D_MODEL)),
            pl.BlockSpec((ROW_TILE, PLE_DIM), lambda i: (i, 0)),
            _resident((1, D_MODEL)),
            _resident((D_MODEL, D_MODEL)),
            _resident((PLE_DIM, D_MODEL)),
        ],
        out_specs=rows,
        out_shape=jax.ShapeDtypeStruct((n, D_MODEL), F32),
        compiler_params=pltpu.CompilerParams(
            dimension_semantics=("parallel",), vmem_limit_bytes=VMEM_LIMIT),
        name="post",
    )(h2d, ot, w_o, g2, w_in, w_out, p2d, g_ple, w_gate, w_proj)


def _dot_nt(w, x):
    return lax.dot_general(w, x, (((1,), (1,)), ((), ())), preferred_element_type=F32)


def _ones_row_group(width):
    row = lax.broadcasted_iota(jnp.int32, (BF16_SUBLANES, width), 0)
    return jnp.where(row == 0, 1.0, 0.0).astype(F32)


def _rope_rows(x1, x2, cos, sin):
    return x1 * cos - x2 * sin, x1 * sin + x2 * cos


def _mla_proj_kernel(x_ref, pos_ref, g_ref, w_in_ref, w_pe_ref, gq_lat_ref, gkv_lat_ref,
                     w_uq_ref, w_ukv_ref, gqn_ref, gkn_ref, inv_freq_ref,
                     qt_ref, k_ref, vt_ref):
    tm = x_ref.shape[0]
    xn = _rms(x_ref[...], g_ref[...]).astype(BF16)
    z = jnp.dot(xn, w_in_ref[...], preferred_element_type=F32)
    cq = _rms(z[:, :MLA_Q_RANK], gq_lat_ref[...]).astype(BF16)
    ckv = _rms(z[:, MLA_Q_RANK:], gkv_lat_ref[...]).astype(BF16)
    qt_all = _dot_nt(w_uq_ref[...], cq)
    kvt_all = _dot_nt(w_ukv_ref[...], ckv)
    kpe_t = _dot_nt(w_pe_ref[...], xn)

    ang = pos_ref[...].astype(F32) * inv_freq_ref[...]
    cos = jnp.cos(ang)
    sin = jnp.sin(ang)
    half = MLA_ROPE // 2
    q_scale = (MLA_QK ** -0.5) * LOG2E
    gqn = gqn_ref[...] * q_scale
    gkn = gkn_ref[...]
    kpe_sq = jnp.sum(kpe_t * kpe_t, axis=0, keepdims=True)
    ones_rows = _ones_row_group(tm).astype(BF16)
    zeros_tail = jnp.zeros((QK_PAD - MLA_QK, tm), F32)

    for h in range(HEADS):
        qh = qt_all[h * MLA_QK:(h + 1) * MLA_QK]
        qn = qh * lax.rsqrt(jnp.mean(qh * qh, axis=0, keepdims=True) + EPS) * gqn
        q1, q2 = _rope_rows(qn[MLA_NOPE:MLA_NOPE + half], qn[MLA_NOPE + half:], cos, sin)
        qt_ref[h, 0:MLA_NOPE, :] = qn[:MLA_NOPE].astype(BF16)
        qt_ref[h, MLA_NOPE:MLA_NOPE + half, :] = q1.astype(BF16)
        qt_ref[h, MLA_NOPE + half:MLA_QK, :] = q2.astype(BF16)
        qt_ref[h, MLA_QK:QK_PAD, :] = zeros_tail.astype(BF16)

        base = h * (MLA_NOPE + HEAD_V)
        kn = kvt_all[base:base + MLA_NOPE]
        ms = (jnp.sum(kn * kn, axis=0, keepdims=True) + kpe_sq) * (1.0 / MLA_QK)
        r = lax.rsqrt(ms + EPS)
        kn = kn * r * gkn[:MLA_NOPE]
        kp = kpe_t * r * gkn[MLA_NOPE:]
        k1, k2 = _rope_rows(kp[:half], kp[half:], cos, sin)
        kt = jnp.concatenate([kn, k1, k2, zeros_tail], axis=0)
        k_ref[h] = kt.T.astype(BF16)

        vt_ref[h, 0:HEAD_V, :] = kvt_all[base + MLA_NOPE:base + MLA_NOPE + HEAD_V].astype(BF16)
        vt_ref[h, HEAD_V:V_ROWS, :] = ones_rows


def _qkv_out(b, s):
    specs = [
        pl.BlockSpec((None, HEADS, QK_PAD, PROJ_TILE), lambda bi, i: (bi, 0, 0, i)),
        pl.BlockSpec((None, HEADS, PROJ_TILE, QK_PAD), lambda bi, i: (bi, 0, i, 0)),
        pl.BlockSpec((None, HEADS, V_ROWS, PROJ_TILE), lambda bi, i: (bi, 0, 0, i)),
    ]
    shapes = [
        jax.ShapeDtypeStruct((b, HEADS, QK_PAD, s), BF16),
        jax.ShapeDtypeStruct((b, HEADS, s, QK_PAD), BF16),
        jax.ShapeDtypeStruct((b, HEADS, V_ROWS, s), BF16),
    ]
    return specs, shapes


def _mla_proj(h, pos3, g, w_in, w_pe_t, gq_lat, gkv_lat, w_uq_t, w_ukv_t, gqn, gkn, inv_freq):
    b, s, _ = h.shape
    out_specs, out_shapes = _qkv_out(b, s)
    return pl.pallas_call(
        _mla_proj_kernel,
        grid=(b, s // PROJ_TILE),
        in_specs=[
            pl.BlockSpec((None, PROJ_TILE, D_MODEL), lambda bi, i: (bi, i, 0)),
            pl.BlockSpec((None, 1, PROJ_TILE), lambda bi, i: (bi, 0, i)),
            _resident(g.shape), _resident(w_in.shape), _resident(w_pe_t.shape),
            _resident(gq_lat.shape), _resident(gkv_lat.shape),
            _resident(w_uq_t.shape), _resident(w_ukv_t.shape),
            _resident(gqn.shape), _resident(gkn.shape), _resident(inv_freq.shape),
        ],
        out_specs=out_specs,
        out_shape=out_shapes,
        compiler_params=pltpu.CompilerParams(
            dimension_semantics=("parallel", "parallel"), vmem_limit_bytes=VMEM_LIMIT),
        name="mla_proj",
    )(h, pos3, g, w_in, w_pe_t, gq_lat, gkv_lat, w_uq_t, w_ukv_t, gqn, gkn, inv_freq)


def _split3(c):
    hi = c.astype(BF16).astype(F32)
    mid = (c - hi).astype(BF16).astype(F32)
    lo = (c - hi - mid).astype(BF16).astype(F32)
    return hi, mid, lo


def _fox_proj_kernel(x_ref, g_ref, w_ref, bf_ref, gqn_ref, gkn_ref,
                     qt_ref, k_ref, vt_ref, c_ref, carry_ref):
    tm = x_ref.shape[0]
    hd = HEADS * FOX_HEAD_DIM

    @pl.when(pl.program_id(1) == 0)
    def _():
        carry_ref[...] = jnp.zeros_like(carry_ref)

    xn = _rms(x_ref[...], g_ref[...]).astype(BF16)
    qt_all = _dot_nt(w_ref[0:hd, :], xn)
    kt_all = _dot_nt(w_ref[hd:2 * hd, :], xn)
    vt_all = _dot_nt(w_ref[2 * hd:3 * hd, :], xn)
    f_t = _dot_nt(w_ref[3 * hd:3 * hd + HEADS, :], xn) + bf_ref[...]

    logf = (jnp.minimum(f_t, 0.0) - jnp.log1p(jnp.exp(-jnp.abs(f_t)))) * LOG2E
    hi, mid, lo = _split3(logf)
    parts = jnp.concatenate([hi, mid, lo], axis=0).astype(BF16)
    src = lax.broadcasted_iota(jnp.int32, (tm, tm), 0)
    dst = lax.broadcasted_iota(jnp.int32, (tm, tm), 1)
    upper = jnp.where(src <= dst, 1.0, 0.0).astype(BF16)
    sums = jnp.dot(parts, upper, preferred_element_type=F32)
    local = sums[2 * HEADS:] + sums[HEADS:2 * HEADS] + sums[:HEADS]
    c = local + carry_ref[:, 0:1]
    carry_ref[...] = jnp.broadcast_to(c[:, tm - 1:tm], carry_ref.shape)
    c_ref[...] = c

    q_scale = (FOX_HEAD_DIM ** -0.5) * LOG2E
    gqn = gqn_ref[...] * q_scale
    gkn = gkn_ref[...]
    row = lax.broadcasted_iota(jnp.int32, (BF16_SUBLANES, tm), 0)
    ones_rows = _ones_row_group(tm).astype(BF16)
    zeros_tail = jnp.zeros((QK_PAD - FOX_HEAD_DIM - BF16_SUBLANES, tm), F32)

    def bias_rows(first, second):
        out = jnp.zeros((BF16_SUBLANES, tm), F32)
        for j in range(3):
            out = jnp.where(row == j, first[j], out)
            out = jnp.where(row == 8 + j, second[j], out)
        return out

    one3 = (1.0, 1.0, 1.0)
    for h in range(HEADS):
        sl = slice(h * FOX_HEAD_DIM, (h + 1) * FOX_HEAD_DIM)
        c3 = _split3(c[h:h + 1])
        qh = qt_all[sl]
        qn = qh * lax.rsqrt(jnp.mean(qh * qh, axis=0, keepdims=True) + EPS) * gqn
        qt_ref[h, 0:FOX_HEAD_DIM, :] = qn.astype(BF16)
        qt_ref[h, FOX_HEAD_DIM:FOX_HEAD_DIM + BF16_SUBLANES, :] = bias_rows(one3, c3).astype(BF16)
        qt_ref[h, FOX_HEAD_DIM + BF16_SUBLANES:QK_PAD, :] = zeros_tail.astype(BF16)

        kh = kt_all[sl]
        kn = kh * lax.rsqrt(jnp.mean(kh * kh, axis=0, keepdims=True) + EPS) * gkn
        neg3 = tuple(-p for p in c3)
        kt = jnp.concatenate([kn, bias_rows(neg3, one3), zeros_tail], axis=0)
        k_ref[h] = kt.T.astype(BF16)

        vt_ref[h, 0:HEAD_V, :] = vt_all[sl].astype(BF16)
        vt_ref[h, HEAD_V:V_ROWS, :] = ones_rows


def _fox_proj(h, g, w_t, b_f, gqn, gkn):
    b, s, _ = h.shape
    out_specs, out_shapes = _qkv_out(b, s)
    out_specs = out_specs + [pl.BlockSpec((None, HEADS, PROJ_TILE), lambda bi, i: (bi, 0, i))]
    out_shapes = out_shapes + [jax.ShapeDtypeStruct((b, HEADS, s), F32)]
    return pl.pallas_call(
        _fox_proj_kernel,
        grid=(b, s // PROJ_TILE),
        in_specs=[
            pl.BlockSpec((None, PROJ_TILE, D_MODEL), lambda bi, i: (bi, i, 0)),
            _resident(g.shape), _resident(w_t.shape), _resident(b_f.shape),
            _resident(gqn.shape), _resident(gkn.shape),
        ],
        out_specs=out_specs,
        out_shape=out_shapes,
        scratch_shapes=[pltpu.VMEM((HEADS, LANES), F32)],
        compiler_params=pltpu.CompilerParams(
            dimension_semantics=("arbitrary", "arbitrary"), vmem_limit_bytes=VMEM_LIMIT),
        name="fox_proj",
    )(h, g, w_t, b_f, gqn, gkn)


def _attn_kernel(cq_ref, cg_ref, thr_ref, qt_ref, k_ref, vt_ref, o_ref, acc_ref, m_ref, *bufs,
                 seq, tq, tk, cw):
    ncol = tq // cw
    nq = seq // tq
    all_chains = tuple(range(ncol))
    s_refs = bufs[:NSLOT]
    bm_refs = bufs[NSLOT:]
    bias_base = (pl.program_id(0) * pl.num_programs(1) + pl.program_id(1)) * nq

    def first_group(t):
        def dead(g):
            return jnp.logical_and(
                g < t - 1, cq_ref[bias_base + t] - cg_ref[bias_base + g] < -thr_ref[0])
        return lax.while_loop(dead, lambda g: g + 1, jnp.int32(0))

    pv_keys = min(PV_KEYS, tk)
    n_parts = tk // pv_keys
    full_plan = {c: tuple((r, False) for r in range(n_parts)) for c in all_chains}

    nd = tq // tk
    assert nd == NSLOT and MAIN_UNROLL % NSLOT == 0 and LAG < NSLOT

    def diag_plan(d):
        plan = {}
        for c in all_chains:
            parts = []
            for r in range(n_parts):
                key_lo = d * tk + r * pv_keys
                if key_lo <= (c + 1) * cw - 1:
                    parts.append((r, key_lo + pv_keys - 1 > c * cw))
            if parts:
                plan[c] = tuple(parts)
        return plan

    diag_plans = tuple(diag_plan(d) for d in range(nd))

    def stage_a_parts(q0, k0, slot, c, parts):
        cols = slice(c * cw, (c + 1) * cw)
        qt = qt_ref[:, pl.ds(q0 + c * cw, cw)]
        bm = None
        for r, masked in parts:
            rows = slice(r * pv_keys, (r + 1) * pv_keys)
            kb = k_ref[pl.ds(k0 + r * pv_keys, pv_keys), :]
            s = jnp.dot(kb, qt, preferred_element_type=F32)
            if masked:
                kpos = k0 + r * pv_keys + lax.broadcasted_iota(jnp.int32, (pv_keys, cw), 0)
                qpos = q0 + c * cw + lax.broadcasted_iota(jnp.int32, (pv_keys, cw), 1)
                s = jnp.where(kpos <= qpos, s, MASK_VALUE)
            s_refs[slot][c, rows, :] = s
            pm = jnp.max(s, axis=0, keepdims=True)
            bm = pm if bm is None else jnp.maximum(bm, pm)
            if r == parts[-1][0]:
                bm_refs[slot][:, cols] = bm
            yield

    def stage_b_parts(k0, slot, c, parts):
        cols = slice(c * cw, (c + 1) * cw)
        m_old = m_ref[:, cols]
        m_new = jnp.maximum(m_old, bm_refs[slot][:, cols])
        alpha = jnp.exp2(m_old - m_new)
        pv = None
        for r, _ in parts:
            rows = slice(r * pv_keys, (r + 1) * pv_keys)
            p = jnp.exp2(s_refs[slot][c, rows, :] - m_new).astype(BF16)
            vb = vt_ref[:, pl.ds(k0 + r * pv_keys, pv_keys)]
            t = jnp.dot(vb, p, preferred_element_type=F32)
            pv = t if pv is None else pv + t
            if r == parts[-1][0]:
                acc_ref[c] = alpha * acc_ref[c] + pv
                m_ref[:, cols] = m_new
            yield

    def step(q0, k0, rslot, nxt_q0, nxt_k0, a_plan=full_plan, b_plan=full_plan):
        for c in all_chains:
            gens = []
            if a_plan is not None and c in a_plan:
                gens.append(stage_a_parts(nxt_q0, nxt_k0, (rslot + LAG) % NSLOT, c, a_plan[c]))
            if b_plan is not None and c in b_plan:
                gens.append(stage_b_parts(k0, rslot, c, b_plan[c]))
            while gens:
                gens = [g for g in gens if next(g, gens) is not gens]

    def q_tile(qi, carry):
        q0 = pl.multiple_of(qi * tq, tq)
        acc_ref[...] = jnp.zeros_like(acc_ref)
        m_ref[...] = jnp.full_like(m_ref, M_INIT)

        def run_steps(k0, count):
            for j in range(count):
                step(q0, k0 + j * tk, j % NSLOT, q0, k0 + (j + LAG) * tk)

        g_first = first_group(qi)
        k_first = pl.multiple_of(g_first * tq, tq)
        n_main = NSLOT * (jnp.maximum(qi - 1, 0) - g_first)
        rem = n_main & (MAIN_UNROLL - 1)

        def main(t, carry2):
            run_steps(pl.multiple_of(k_first + t * MAIN_UNROLL * tk, tq), MAIN_UNROLL)
            return carry2

        lax.fori_loop(0, lax.shift_right_logical(n_main, MAIN_UNROLL.bit_length() - 1), main, 0)
        k_last = q0 - NSLOT * tk
        sub = MAIN_UNROLL // 2
        while sub >= NSLOT:
            k_sub = pl.multiple_of(k_last - ((rem & (sub - 1)) + sub) * tk, NSLOT * tk)

            @pl.when((rem & sub) != 0)
            def _(k_sub=k_sub, sub=sub):
                run_steps(k_sub, sub)

            sub //= 2

        nxt_qi = jnp.minimum(qi + 1, nq - 1)
        nxt_q0 = pl.multiple_of(nxt_qi * tq, tq)
        nxt_k_first = pl.multiple_of(first_group(nxt_qi) * tq, tq)

        def last_below():
            k0 = pl.multiple_of(k_last, NSLOT * tk)
            for j in range(NSLOT):
                tgt = j + LAG
                if tgt < NSLOT:
                    step(q0, k0 + j * tk, j, q0, k0 + tgt * tk)
                else:
                    step(q0, k0 + j * tk, j, q0, q0 + (tgt - NSLOT) * tk,
                         a_plan=diag_plans[tgt - NSLOT])

        def diagonal():
            for d in range(nd):
                tgt = d + LAG
                if tgt < nd:
                    step(q0, q0 + d * tk, d % NSLOT, q0, q0 + tgt * tk,
                         a_plan=diag_plans[tgt], b_plan=diag_plans[d])
                else:
                    step(q0, q0 + d * tk, d % NSLOT, nxt_q0, nxt_k_first + (tgt - nd) * tk,
                         b_plan=diag_plans[d])

        @pl.when(qi >= 1)
        def _():
            last_below()
            diagonal()

        @pl.when(qi == 0)
        def _():
            diagonal()

        for c in all_chains:
            acc = acc_ref[c]
            o = acc[:HEAD_V] / acc[HEAD_V:HEAD_V + 1]
            o_ref[:, pl.ds(q0 + c * cw, cw)] = o.astype(o_ref.dtype)
        return carry

    for j in range(LAG):
        step(0, 0, (j - LAG) % NSLOT, 0, j * tk, a_plan=diag_plans[j], b_plan=None)
    lax.fori_loop(0, nq, q_tile, 0)


def _attention(qt, k, vt, bias=None, bound=0.0, tile=ATT_TQ):
    b, heads, _, s = qt.shape
    tq = min(tile, s)
    tk = min(ATT_TK, tq // NSLOT)
    nq = s // tq
    if bias is None:
        cq = cg = jnp.zeros((b * heads * nq,), F32)
    else:
        cq = bias[:, :, ::tq].reshape(-1)
        cg = bias[:, :, tq - 1::tq].reshape(-1)
    thr = jnp.reshape(2.0 * bound + ZERO_WEIGHT_LOG2, (1,)).astype(F32)
    kern = functools.partial(_attn_kernel, seq=s, tq=tq, tk=tk, cw=ATT_CW)
    return pl.pallas_call(
        kern,
        grid_spec=pltpu.PrefetchScalarGridSpec(
            num_scalar_prefetch=3,
            grid=(b, heads),
            in_specs=[
                pl.BlockSpec((None, None, QK_PAD, s), lambda bi, hi, *_: (bi, hi, 0, 0)),
                pl.BlockSpec((None, None, s, QK_PAD), lambda bi, hi, *_: (bi, hi, 0, 0)),
                pl.BlockSpec((None, None, V_ROWS, s), lambda bi, hi, *_: (bi, hi, 0, 0)),
            ],
            out_specs=pl.BlockSpec((None, HEAD_V, s), lambda bi, hi, *_: (bi, hi, 0)),
            scratch_shapes=([pltpu.VMEM((tq // ATT_CW, V_ROWS, ATT_CW), F32),
                             pltpu.VMEM((1, tq), F32)]
                            + [pltpu.VMEM((tq // ATT_CW, tk, ATT_CW), F32)] * NSLOT
                            + [pltpu.VMEM((1, tq), F32)] * NSLOT),
        ),
        out_shape=jax.ShapeDtypeStruct((b, heads * HEAD_V, s), BF16),
        compiler_params=pltpu.CompilerParams(
            dimension_semantics=("parallel", "parallel"), vmem_limit_bytes=VMEM_LIMIT),
        name="attention",
    )(cq, cg, thr, qt, k, vt)


def kernel(x, p, positions, g_ffn1, g_mix, g_ffn2, g_ple, ffn1_w_in, ffn1_w_out, ffn2_w_in, ffn2_w_out, ple_w_proj, ple_w_gate, mla_w_in, mla_g_q_lat, mla_w_uq, mla_g_kv_lat, mla_w_ukv, mla_g_qn, mla_g_kn, mla_w_o, fox_w_in, fox_b_f, fox_g_qn, fox_g_kn, fox_w_o):
    b, s, d = x.shape
    depth = g_ffn1.shape[0]
    assert d == D_MODEL and s % PROJ_TILE == 0 and s % ROW_TILE == 0
    n = b * s
    bf = lambda w: w.astype(BF16)
    row = lambda g: g.reshape(1, -1)
    col = lambda g: g.reshape(-1, 1)

    pos3 = positions.reshape(b, 1, s)
    half = MLA_ROPE // 2
    inv_freq = (ROPE_THETA ** (-jnp.arange(0, MLA_ROPE, 2, dtype=F32) / MLA_ROPE)).reshape(half, 1)
    lat = MLA_Q_RANK + MLA_KV_RANK

    h = x
    for i in range(depth):
        j = i // N_MIXERS
        h = _ffn(h.reshape(n, d), row(g_ffn1[i]), bf(ffn1_w_in[i]), bf(ffn1_w_out[i])).reshape(b, s, d)
        if i % N_MIXERS == 0:
            qt, k, vt = _mla_proj(
                h, pos3, row(g_mix[i]), bf(mla_w_in[j][:, :lat]), bf(mla_w_in[j][:, lat:].T),
                row(mla_g_q_lat[j]), row(mla_g_kv_lat[j]), bf(mla_w_uq[j].T), bf(mla_w_ukv[j].T),
                col(mla_g_qn[j]), col(mla_g_kn[j]), inv_freq)
            w_o = mla_w_o[j]
            ot = _attention(qt, k, vt)
        else:
            qt, k, vt, c = _fox_proj(h, row(g_mix[i]), bf(fox_w_in[j].T), col(fox_b_f[j]),
                                     col(fox_g_qn[j]), col(fox_g_kn[j]))
            w_o = fox_w_o[j]
            bound = (jnp.max(jnp.abs(fox_g_qn[j])) * LOG2E
                     * jnp.max(jnp.abs(fox_g_kn[j])) * FOX_HEAD_DIM ** 0.5 * BF16_NORM_SLACK)
            ot = _attention(qt, k, vt, bias=c, bound=bound, tile=ATT_TQ_DECAY)
        h = _post(h.reshape(n, d), ot, bf(w_o), row(g_ffn2[i]), bf(ffn2_w_in[i]), bf(ffn2_w_out[i]),
                  p[i].reshape(n, PLE_DIM), row(g_ple[i]), bf(ple_w_gate[i]),
                  bf(ple_w_proj[i])).reshape(b, s, d)
    return h
```

```python
import functools
import math

import jax
import jax.numpy as jnp
from jax import lax
from jax.experimental import pallas as pl
from jax.experimental.pallas import tpu as pltpu

F32 = jnp.float32
BF16 = jnp.bfloat16

D_MODEL = 1024
D_FF = 2816
PLE_DIM = 256
N_MIXERS = 2
FFN_HALF = 0.5
HEADS = 16
MLA_Q_RANK = 512
MLA_KV_RANK = 256
MLA_NOPE = 64
MLA_ROPE = 32
MLA_QK = MLA_NOPE + MLA_ROPE
HEAD_V = 64
FOX_HEAD_DIM = 64
ROPE_THETA = 10000.0
EPS = 1e-6
LOG2E = math.log2(math.e)

LANES = 128
MXU_DIM = 256
BF16_SUBLANES = 16
QK_PAD = 128
V_ROWS = HEAD_V + BF16_SUBLANES
VMEM_LIMIT = 52 * 1024 * 1024

ROW_TILE = 512
PROJ_TILE = 512
HEAD_GROUP = 4
ATT_TQ = 1024
ATT_CW = MXU_DIM
ATT_TK = MXU_DIM
MAIN_UNROLL = 16
NSLOT = 4
LAG = 2
PV_KEYS = MXU_DIM
MASK_VALUE = -jnp.inf
M_INIT = -1e30
ZERO_WEIGHT_LOG2 = 170.0
BF16_NORM_SLACK = 1.02


def _rms(x, g):
    return x * lax.rsqrt(jnp.mean(x * x, axis=-1, keepdims=True) + EPS) * g


def _resident(shape):
    return pl.BlockSpec(shape, lambda *_: (0,) * len(shape), pipeline_mode=pl.Buffered(1))


def _swiglu_half_step(x, g, w_in_ref, w_out_ref):
    xn = _rms(x, g).astype(BF16)
    gu = jnp.dot(xn, w_in_ref[...], preferred_element_type=F32)
    gate = gu[:, :D_FF]
    up = gu[:, D_FF:]
    act = (gate * jax.nn.sigmoid(gate) * up).astype(BF16)
    return x + FFN_HALF * jnp.dot(act, w_out_ref[...], preferred_element_type=F32)


def _ffn_kernel(x_ref, g_ref, w_in_ref, w_out_ref, o_ref):
    o_ref[...] = _swiglu_half_step(x_ref[...], g_ref[...], w_in_ref, w_out_ref)


def _ffn(h2d, g, w_in, w_out):
    n = h2d.shape[0]
    return pl.pallas_call(
        _ffn_kernel,
        grid=(n // ROW_TILE,),
        in_specs=[
            pl.BlockSpec((ROW_TILE, D_MODEL), lambda i: (i, 0)),
            _resident((1, D_MODEL)),
            _resident((D_MODEL, 2 * D_FF)),
            _resident((D_FF, D_MODEL)),
        ],
        out_specs=pl.BlockSpec((ROW_TILE, D_MODEL), lambda i: (i, 0)),
        out_shape=jax.ShapeDtypeStruct((n, D_MODEL), F32),
        compiler_params=pltpu.CompilerParams(
            dimension_semantics=("parallel",), vmem_limit_bytes=VMEM_LIMIT),
        name="ffn",
    )(h2d, g, w_in, w_out)


def _post_kernel(x_ref, ot_ref, wo_ref, g2_ref, w_in_ref, w_out_ref,
                 p_ref, gp_ref, wg_ref, wp_ref, o_ref):
    x = x_ref[...] + lax.dot_general(ot_ref[...], wo_ref[...], (((0,), (0,)), ((), ())),
                                     preferred_element_type=F32)
    x = _swiglu_half_step(x, g2_ref[...], w_in_ref, w_out_ref)
    xn = _rms(x, gp_ref[...]).astype(BF16)
    gate = jax.nn.sigmoid(jnp.dot(xn, wg_ref[...], preferred_element_type=F32))
    pp = jnp.dot(p_ref[...].astype(BF16), wp_ref[...], preferred_element_type=F32)
    o_ref[...] = x + gate * pp


def _post(h2d, ot, w_o, g2, w_in, w_out, p3d, layer, g_ple, w_gate, w_proj):
    n = h2d.shape[0]
    hd, s = ot.shape[1], ot.shape[2]
    tiles = s // ROW_TILE
    rows = pl.BlockSpec((ROW_TILE, D_MODEL), lambda i: (i, 0))
    return pl.pallas_call(
        _post_kernel,
        grid=(n // ROW_TILE,),
        in_specs=[
            rows,
            pl.BlockSpec((None, hd, ROW_TILE), lambda i: (i // tiles, 0, i % tiles)),
            _resident((hd, D_MODEL)),
            _resident((1, D_MODEL)),
            _resident((D_MODEL, 2 * D_FF)),
            _resident((D_FF, D_MODEL)),
            pl.BlockSpec((None, ROW_TILE, PLE_DIM), lambda i: (layer, i, 0)),
            _resident((1, D_MODEL)),
            _resident((D_MODEL, D_MODEL)),
            _resident((PLE_DIM, D_MODEL)),
        ],
        out_specs=rows,
        out_shape=jax.ShapeDtypeStruct((n, D_MODEL), F32),
        compiler_params=pltpu.CompilerParams(
            dimension_semantics=("parallel",), vmem_limit_bytes=VMEM_LIMIT),
        name="post",
    )(h2d, ot, w_o, g2, w_in, w_out, p3d, g_ple, w_gate, w_proj)


def _dot_nt(w, x):
    return lax.dot_general(w, x, (((1,), (1,)), ((), ())), preferred_element_type=F32)


def _ones_row_group(width):
    row = lax.broadcasted_iota(jnp.int32, (BF16_SUBLANES, width), 0)
    return jnp.where(row == 0, 1.0, 0.0).astype(F32)


def _rope_rows(x1, x2, cos, sin):
    return x1 * cos - x2 * sin, x1 * sin + x2 * cos


def _mla_proj_kernel(x_ref, pos_ref, g_ref, w_in_ref, w_pe_ref, gq_lat_ref, gkv_lat_ref,
                     w_uq_ref, w_ukv_ref, gqn_ref, gkn_ref, inv_freq_ref,
                     qt_ref, k_ref, vt_ref):
    tm = x_ref.shape[0]
    xn = _rms(x_ref[...], g_ref[...]).astype(BF16)
    z = jnp.dot(xn, w_in_ref[...], preferred_element_type=F32)
    cq = _rms(z[:, :MLA_Q_RANK], gq_lat_ref[...]).astype(BF16)
    ckv = _rms(z[:, MLA_Q_RANK:], gkv_lat_ref[...]).astype(BF16)
    kpe_t = _dot_nt(w_pe_ref[...], xn)

    ang = pos_ref[...].astype(F32) * inv_freq_ref[...]
    cos = jnp.cos(ang)
    sin = jnp.sin(ang)
    half = MLA_ROPE // 2
    q_scale = (MLA_QK ** -0.5) * LOG2E
    gqn = gqn_ref[...] * q_scale
    gkn = gkn_ref[...]
    kpe_sq = jnp.sum(kpe_t * kpe_t, axis=0, keepdims=True)
    ones_rows = _ones_row_group(tm).astype(BF16)
    zeros_tail = jnp.zeros((QK_PAD - MLA_QK, tm), F32)

    kv_dim = MLA_NOPE + HEAD_V
    for h in range(HEADS):
        if h % HEAD_GROUP == 0:
            g0 = h
            qt_all = _dot_nt(w_uq_ref[g0 * MLA_QK:(g0 + HEAD_GROUP) * MLA_QK, :], cq)
            kvt_all = _dot_nt(w_ukv_ref[g0 * kv_dim:(g0 + HEAD_GROUP) * kv_dim, :], ckv)
        hl = h - g0
        qh = qt_all[hl * MLA_QK:(hl + 1) * MLA_QK]
        qn = qh * lax.rsqrt(jnp.mean(qh * qh, axis=0, keepdims=True) + EPS) * gqn
        q1, q2 = _rope_rows(qn[MLA_NOPE:MLA_NOPE + half], qn[MLA_NOPE + half:], cos, sin)
        qt_ref[h, 0:MLA_NOPE, :] = qn[:MLA_NOPE].astype(BF16)
        qt_ref[h, MLA_NOPE:MLA_NOPE + half, :] = q1.astype(BF16)
        qt_ref[h, MLA_NOPE + half:MLA_QK, :] = q2.astype(BF16)
        qt_ref[h, MLA_QK:QK_PAD, :] = zeros_tail.astype(BF16)

        base = hl * kv_dim
        kn = kvt_all[base:base + MLA_NOPE]
        ms = (jnp.sum(kn * kn, axis=0, keepdims=True) + kpe_sq) * (1.0 / MLA_QK)
        r = lax.rsqrt(ms + EPS)
        kn = kn * r * gkn[:MLA_NOPE]
        kp = kpe_t * r * gkn[MLA_NOPE:]
        k1, k2 = _rope_rows(kp[:half], kp[half:], cos, sin)
        kt = jnp.concatenate([kn, k1, k2, zeros_tail], axis=0)
        k_ref[h] = kt.T.astype(BF16)

        vt_ref[h, 0:HEAD_V, :] = kvt_all[base + MLA_NOPE:base + MLA_NOPE + HEAD_V].astype(BF16)
        vt_ref[h, HEAD_V:V_ROWS, :] = ones_rows


def _qkv_out(b, s):
    specs = [
        pl.BlockSpec((None, HEADS, QK_PAD, PROJ_TILE), lambda bi, i: (bi, 0, 0, i)),
        pl.BlockSpec((None, HEADS, PROJ_TILE, QK_PAD), lambda bi, i: (bi, 0, i, 0)),
        pl.BlockSpec((None, HEADS, V_ROWS, PROJ_TILE), lambda bi, i: (bi, 0, 0, i)),
    ]
    shapes = [
        jax.ShapeDtypeStruct((b, HEADS, QK_PAD, s), BF16),
        jax.ShapeDtypeStruct((b, HEADS, s, QK_PAD), BF16),
        jax.ShapeDtypeStruct((b, HEADS, V_ROWS, s), BF16),
    ]
    return specs, shapes


def _mla_proj(h, pos3, g, w_in, w_pe_t, gq_lat, gkv_lat, w_uq_t, w_ukv_t, gqn, gkn, inv_freq):
    b, s, _ = h.shape
    out_specs, out_shapes = _qkv_out(b, s)
    return pl.pallas_call(
        _mla_proj_kernel,
        grid=(b, s // PROJ_TILE),
        in_specs=[
            pl.BlockSpec((None, PROJ_TILE, D_MODEL), lambda bi, i: (bi, i, 0)),
            pl.BlockSpec((None, 1, PROJ_TILE), lambda bi, i: (bi, 0, i)),
            _resident(g.shape), _resident(w_in.shape), _resident(w_pe_t.shape),
            _resident(gq_lat.shape), _resident(gkv_lat.shape),
            _resident(w_uq_t.shape), _resident(w_ukv_t.shape),
            _resident(gqn.shape), _resident(gkn.shape), _resident(inv_freq.shape),
        ],
        out_specs=out_specs,
        out_shape=out_shapes,
        compiler_params=pltpu.CompilerParams(
            dimension_semantics=("parallel", "parallel"), vmem_limit_bytes=VMEM_LIMIT),
        name="mla_proj",
    )(h, pos3, g, w_in, w_pe_t, gq_lat, gkv_lat, w_uq_t, w_ukv_t, gqn, gkn, inv_freq)


def _split3(c):
    hi = c.astype(BF16).astype(F32)
    mid = (c - hi).astype(BF16).astype(F32)
    lo = (c - hi - mid).astype(BF16).astype(F32)
    return hi, mid, lo


def _fox_proj_kernel(x_ref, g_ref, w_ref, bf_ref, gqn_ref, gkn_ref,
                     qt_ref, k_ref, vt_ref, c_ref, carry_ref):
    tm = x_ref.shape[0]
    hd = HEADS * FOX_HEAD_DIM

    @pl.when(pl.program_id(1) == 0)
    def _():
        carry_ref[...] = jnp.zeros_like(carry_ref)

    xn = _rms(x_ref[...], g_ref[...]).astype(BF16)
    f_t = _dot_nt(w_ref[3 * hd:3 * hd + HEADS, :], xn) + bf_ref[...]

    logf = (jnp.minimum(f_t, 0.0) - jnp.log1p(jnp.exp(-jnp.abs(f_t)))) * LOG2E
    hi, mid, lo = _split3(logf)
    parts = jnp.concatenate([hi, mid, lo], axis=0).astype(BF16)
    src = lax.broadcasted_iota(jnp.int32, (tm, tm), 0)
    dst = lax.broadcasted_iota(jnp.int32, (tm, tm), 1)
    upper = jnp.where(src <= dst, 1.0, 0.0).astype(BF16)
    sums = jnp.dot(parts, upper, preferred_element_type=F32)
    local = sums[2 * HEADS:] + sums[HEADS:2 * HEADS] + sums[:HEADS]
    c = local + carry_ref[:, 0:1]
    carry_ref[...] = jnp.broadcast_to(c[:, tm - 1:tm], carry_ref.shape)
    c_ref[...] = c

    q_scale = (FOX_HEAD_DIM ** -0.5) * LOG2E
    gqn = gqn_ref[...] * q_scale
    gkn = gkn_ref[...]
    row = lax.broadcasted_iota(jnp.int32, (BF16_SUBLANES, tm), 0)
    ones_rows = _ones_row_group(tm).astype(BF16)
    zeros_tail = jnp.zeros((QK_PAD - FOX_HEAD_DIM - BF16_SUBLANES, tm), F32)

    def bias_rows(first, second):
        out = jnp.zeros((BF16_SUBLANES, tm), F32)
        for j in range(3):
            out = jnp.where(row == j, first[j], out)
            out = jnp.where(row == 8 + j, second[j], out)
        return out

    one3 = (1.0, 1.0, 1.0)
    gw = HEAD_GROUP * FOX_HEAD_DIM
    for h in range(HEADS):
        if h % HEAD_GROUP == 0:
            g0 = h
            r0 = g0 * FOX_HEAD_DIM
            qt_all = _dot_nt(w_ref[r0:r0 + gw, :], xn)
            kt_all = _dot_nt(w_ref[hd + r0:hd + r0 + gw, :], xn)
            vt_all = _dot_nt(w_ref[2 * hd + r0:2 * hd + r0 + gw, :], xn)
        sl = slice((h - g0) * FOX_HEAD_DIM, (h - g0 + 1) * FOX_HEAD_DIM)
        c3 = _split3(c[h:h + 1])
        qh = qt_all[sl]
        qn = qh * lax.rsqrt(jnp.mean(qh * qh, axis=0, keepdims=True) + EPS) * gqn
        qt_ref[h, 0:FOX_HEAD_DIM, :] = qn.astype(BF16)
        qt_ref[h, FOX_HEAD_DIM:FOX_HEAD_DIM + BF16_SUBLANES, :] = bias_rows(one3, c3).astype(BF16)
        qt_ref[h, FOX_HEAD_DIM + BF16_SUBLANES:QK_PAD, :] = zeros_tail.astype(BF16)

        kh = kt_all[sl]
        kn = kh * lax.rsqrt(jnp.mean(kh * kh, axis=0, keepdims=True) + EPS) * gkn
        neg3 = tuple(-p for p in c3)
        kt = jnp.concatenate([kn, bias_rows(neg3, one3), zeros_tail], axis=0)
        k_ref[h] = kt.T.astype(BF16)

        vt_ref[h, 0:HEAD_V, :] = vt_all[sl].astype(BF16)
        vt_ref[h, HEAD_V:V_ROWS, :] = ones_rows


def _fox_proj(h, g, w_t, b_f, gqn, gkn):
    b, s, _ = h.shape
    out_specs, out_shapes = _qkv_out(b, s)
    out_specs = out_specs + [pl.BlockSpec((None, HEADS, PROJ_TILE), lambda bi, i: (bi, 0, i))]
    out_shapes = out_shapes + [jax.ShapeDtypeStruct((b, HEADS, s), F32)]
    return pl.pallas_call(
        _fox_proj_kernel,
        grid=(b, s // PROJ_TILE),
        in_specs=[
            pl.BlockSpec((None, PROJ_TILE, D_MODEL), lambda bi, i: (bi, i, 0)),
            _resident(g.shape), _resident(w_t.shape), _resident(b_f.shape),
            _resident(gqn.shape), _resident(gkn.shape),
        ],
        out_specs=out_specs,
        out_shape=out_shapes,
        scratch_shapes=[pltpu.VMEM((HEADS, LANES), F32)],
        compiler_params=pltpu.CompilerParams(
            dimension_semantics=("arbitrary", "arbitrary"), vmem_limit_bytes=VMEM_LIMIT),
        name="fox_proj",
    )(h, g, w_t, b_f, gqn, gkn)


def _attn_kernel(cq_ref, cg_ref, thr_ref, qt_ref, k_ref, vt_ref, o_ref, acc_ref, m_ref, *bufs,
                 seq, tq, tk, cw):
    ncol = tq // cw
    nq = seq // tq
    all_chains = tuple(range(ncol))
    s_refs = bufs[:NSLOT]
    bm_refs = bufs[NSLOT:]
    bias_base = (pl.program_id(0) * pl.num_programs(1) + pl.program_id(1)) * nq

    def first_group(t):
        def dead(g):
            return jnp.logical_and(
                g < t - 1, cq_ref[bias_base + t] - cg_ref[bias_base + g] < -thr_ref[0])
        return lax.while_loop(dead, lambda g: g + 1, jnp.int32(0))

    pv_keys = min(PV_KEYS, tk)
    n_parts = tk // pv_keys
    full_plan = {c: tuple((r, False) for r in range(n_parts)) for c in all_chains}

    nd = tq // tk
    assert nd == NSLOT and MAIN_UNROLL % NSLOT == 0 and LAG < NSLOT

    def diag_plan(d):
        plan = {}
        for c in all_chains:
            parts = []
            for r in range(n_parts):
                key_lo = d * tk + r * pv_keys
                if key_lo <= (c + 1) * cw - 1:
                    parts.append((r, key_lo + pv_keys - 1 > c * cw))
            if parts:
                plan[c] = tuple(parts)
        return plan

    diag_plans = tuple(diag_plan(d) for d in range(nd))

    def stage_a_parts(q0, k0, slot, c, parts):
        cols = slice(c * cw, (c + 1) * cw)
        qt = qt_ref[:, pl.ds(q0 + c * cw, cw)]
        bm = None
        for r, masked in parts:
            rows = slice(r * pv_keys, (r + 1) * pv_keys)
            kb = k_ref[pl.ds(k0 + r * pv_keys, pv_keys), :]
            s = jnp.dot(kb, qt, preferred_element_type=F32)
            if masked:
                kpos = k0 + r * pv_keys + lax.broadcasted_iota(jnp.int32, (pv_keys, cw), 0)
                qpos = q0 + c * cw + lax.broadcasted_iota(jnp.int32, (pv_keys, cw), 1)
                s = jnp.where(kpos <= qpos, s, MASK_VALUE)
            s_refs[slot][c, rows, :] = s
            pm = jnp.max(s, axis=0, keepdims=True)
            bm = pm if bm is None else jnp.maximum(bm, pm)
            if r == parts[-1][0]:
                bm_refs[slot][:, cols] = bm
            yield

    def stage_b_parts(k0, slot, c, parts):
        cols = slice(c * cw, (c + 1) * cw)
        m_old = m_ref[:, cols]
        m_new = jnp.maximum(m_old, bm_refs[slot][:, cols])
        alpha = jnp.exp2(m_old - m_new)
        pv = None
        for r, _ in parts:
            rows = slice(r * pv_keys, (r + 1) * pv_keys)
            p = jnp.exp2(s_refs[slot][c, rows, :] - m_new).astype(BF16)
            vb = vt_ref[:, pl.ds(k0 + r * pv_keys, pv_keys)]
            t = jnp.dot(vb, p, preferred_element_type=F32)
            pv = t if pv is None else pv + t
            if r == parts[-1][0]:
                acc_ref[c] = alpha * acc_ref[c] + pv
                m_ref[:, cols] = m_new
            yield

    def step(q0, k0, rslot, nxt_q0, nxt_k0, a_plan=full_plan, b_plan=full_plan):
        for c in all_chains:
            gens = []
            if a_plan is not None and c in a_plan:
                gens.append(stage_a_parts(nxt_q0, nxt_k0, (rslot + LAG) % NSLOT, c, a_plan[c]))
            if b_plan is not None and c in b_plan:
                gens.append(stage_b_parts(k0, rslot, c, b_plan[c]))
            while gens:
                gens = [g for g in gens if next(g, gens) is not gens]

    def q_tile(qi, carry):
        q0 = pl.multiple_of(qi * tq, tq)
        acc_ref[...] = jnp.zeros_like(acc_ref)
        m_ref[...] = jnp.full_like(m_ref, M_INIT)

        def run_steps(k0, count):
            for j in range(count):
                step(q0, k0 + j * tk, j % NSLOT, q0, k0 + (j + LAG) * tk)

        g_first = first_group(qi)
        k_first = pl.multiple_of(g_first * tq, tq)
        n_main = NSLOT * (jnp.maximum(qi - 1, 0) - g_first)
        rem = n_main & (MAIN_UNROLL - 1)

        def main(t, carry2):
            run_steps(pl.multiple_of(k_first + t * MAIN_UNROLL * tk, tq), MAIN_UNROLL)
            return carry2

        lax.fori_loop(0, lax.shift_right_logical(n_main, MAIN_UNROLL.bit_length() - 1), main, 0)
        k_last = q0 - NSLOT * tk
        sub = MAIN_UNROLL // 2
        while sub >= NSLOT:
            k_sub = pl.multiple_of(k_last - ((rem & (sub - 1)) + sub) * tk, NSLOT * tk)

            @pl.when((rem & sub) != 0)
            def _(k_sub=k_sub, sub=sub):
                run_steps(k_sub, sub)

            sub //= 2

        nxt_qi = jnp.minimum(qi + 1, nq - 1)
        nxt_q0 = pl.multiple_of(nxt_qi * tq, tq)
        nxt_k_first = pl.multiple_of(first_group(nxt_qi) * tq, tq)

        def last_below():
            k0 = pl.multiple_of(k_last, NSLOT * tk)
            for j in range(NSLOT):
                tgt = j + LAG
                if tgt < NSLOT:
                    step(q0, k0 + j * tk, j, q0, k0 + tgt * tk)
                else:
                    step(q0, k0 + j * tk, j, q0, q0 + (tgt - NSLOT) * tk,
                         a_plan=diag_plans[tgt - NSLOT])

        def diagonal():
            for d in range(nd):
                tgt = d + LAG
                if tgt < nd:
                    step(q0, q0 + d * tk, d % NSLOT, q0, q0 + tgt * tk,
                         a_plan=diag_plans[tgt], b_plan=diag_plans[d])
                else:
                    step(q0, q0 + d * tk, d % NSLOT, nxt_q0, nxt_k_first + (tgt - nd) * tk,
                         b_plan=diag_plans[d])

        @pl.when(qi >= 1)
        def _():
            last_below()
            diagonal()

        @pl.when(qi == 0)
        def _():
            diagonal()

        for c in all_chains:
            acc = acc_ref[c]
            o = acc[:HEAD_V] / acc[HEAD_V:HEAD_V + 1]
            o_ref[:, pl.ds(q0 + c * cw, cw)] = o.astype(o_ref.dtype)
        return carry

    for j in range(LAG):
        step(0, 0, (j - LAG) % NSLOT, 0, j * tk, a_plan=diag_plans[j], b_plan=None)
    lax.fori_loop(0, nq, q_tile, 0)


def _attention(qt, k, vt, bias=None, bound=0.0):
    b, heads, _, s = qt.shape
    tq = min(ATT_TQ, s)
    tk = min(ATT_TK, tq // NSLOT)
    nq = s // tq
    if bias is None:
        cq = cg = jnp.zeros((b * heads * nq,), F32)
    else:
        cq = bias[:, :, ::tq].reshape(-1)
        cg = bias[:, :, tq - 1::tq].reshape(-1)
    thr = jnp.reshape(2.0 * bound + ZERO_WEIGHT_LOG2, (1,)).astype(F32)
    kern = functools.partial(_attn_kernel, seq=s, tq=tq, tk=tk, cw=ATT_CW)
    return pl.pallas_call(
        kern,
        grid_spec=pltpu.PrefetchScalarGridSpec(
            num_scalar_prefetch=3,
            grid=(b, heads),
            in_specs=[
                pl.BlockSpec((None, None, QK_PAD, s), lambda bi, hi, *_: (bi, hi, 0, 0)),
                pl.BlockSpec((None, None, s, QK_PAD), lambda bi, hi, *_: (bi, hi, 0, 0)),
                pl.BlockSpec((None, None, V_ROWS, s), lambda bi, hi, *_: (bi, hi, 0, 0)),
            ],
            out_specs=pl.BlockSpec((None, HEAD_V, s), lambda bi, hi, *_: (bi, hi, 0)),
            scratch_shapes=([pltpu.VMEM((tq // ATT_CW, V_ROWS, ATT_CW), F32),
                             pltpu.VMEM((1, tq), F32)]
                            + [pltpu.VMEM((tq // ATT_CW, tk, ATT_CW), F32)] * NSLOT
                            + [pltpu.VMEM((1, tq), F32)] * NSLOT),
        ),
        out_shape=jax.ShapeDtypeStruct((b, heads * HEAD_V, s), BF16),
        compiler_params=pltpu.CompilerParams(
            dimension_semantics=("parallel", "parallel"), vmem_limit_bytes=VMEM_LIMIT),
        name="attention",
    )(cq, cg, thr, qt, k, vt)


def kernel(x, p, positions, g_ffn1, g_mix, g_ffn2, g_ple, ffn1_w_in, ffn1_w_out, ffn2_w_in, ffn2_w_out, ple_w_proj, ple_w_gate, mla_w_in, mla_g_q_lat, mla_w_uq, mla_g_kv_lat, mla_w_ukv, mla_g_qn, mla_g_kn, mla_w_o, fox_w_in, fox_b_f, fox_g_qn, fox_g_kn, fox_w_o):
    b, s, d = x.shape
    depth = g_ffn1.shape[0]
    assert d == D_MODEL and s % PROJ_TILE == 0 and s % ROW_TILE == 0
    n = b * s
    bf = lambda w: w.astype(BF16)
    row = lambda g: g.reshape(1, -1)
    col = lambda g: g.reshape(-1, 1)

    pos3 = positions.reshape(b, 1, s)
    half = MLA_ROPE // 2
    inv_freq = (ROPE_THETA ** (-jnp.arange(0, MLA_ROPE, 2, dtype=F32) / MLA_ROPE)).reshape(half, 1)
    lat = MLA_Q_RANK + MLA_KV_RANK

    h = x
    for i in range(depth):
        j = i // N_MIXERS
        h = _ffn(h.reshape(n, d), row(g_ffn1[i]), bf(ffn1_w_in[i]), bf(ffn1_w_out[i])).reshape(b, s, d)
        if i % N_MIXERS == 0:
            qt, k, vt = _mla_proj(
                h, pos3, row(g_mix[i]), bf(mla_w_in[j][:, :lat]), bf(mla_w_in[j][:, lat:].T),
                row(mla_g_q_lat[j]), row(mla_g_kv_lat[j]), bf(mla_w_uq[j].T), bf(mla_w_ukv[j].T),
                col(mla_g_qn[j]), col(mla_g_kn[j]), inv_freq)
            w_o = mla_w_o[j]
            ot = _attention(qt, k, vt)
        else:
            qt, k, vt, c = _fox_proj(h, row(g_mix[i]), bf(fox_w_in[j].T), col(fox_b_f[j]),
                                     col(fox_g_qn[j]), col(fox_g_kn[j]))
            w_o = fox_w_o[j]
            bound = (jnp.max(jnp.abs(fox_g_qn[j])) * LOG2E
                     * jnp.max(jnp.abs(fox_g_kn[j])) * FOX_HEAD_DIM ** 0.5 * BF16_NORM_SLACK)
            ot = _attention(qt, k, vt, bias=c, bound=bound)
        h = _post(h.reshape(n, d), ot, bf(w_o), row(g_ffn2[i]), bf(ffn2_w_in[i]), bf(ffn2_w_out[i]),
                  p.reshape(depth, n, PLE_DIM), i, row(g_ple[i]), bf(ple_w_gate[i]),
                  bf(ple_w_proj[i])).reshape(b, s, d)
    return h
```

```python
import functools
import math

import jax
import jax.numpy as jnp
from jax import lax
from jax.experimental import pallas as pl
from jax.experimental.pallas import tpu as pltpu

F32 = jnp.float32
BF16 = jnp.bfloat16

D_MODEL = 1024
D_FF = 2816
PLE_DIM = 256
N_MIXERS = 2
FFN_HALF = 0.5
HEADS = 16
MLA_Q_RANK = 512
MLA_KV_RANK = 256
MLA_NOPE = 64
MLA_ROPE = 32
MLA_QK = MLA_NOPE + MLA_ROPE
HEAD_V = 64
FOX_HEAD_DIM = 64
ROPE_THETA = 10000.0
EPS = 1e-6
LOG2E = math.log2(math.e)

LANES = 128
MXU_DIM = 256
BF16_SUBLANES = 16
QK_PAD = 128
V_ROWS = HEAD_V + BF16_SUBLANES
VMEM_LIMIT = 52 * 1024 * 1024

ROW_TILE = 512
PROJ_TILE = 512
HEAD_GROUP = 4
ATT_TQ = 1024
ATT_CW = MXU_DIM
ATT_TK = MXU_DIM
MAIN_UNROLL = 16
NSLOT = 4
LAG = 2
PV_KEYS = MXU_DIM
MASK_VALUE = -jnp.inf
M_INIT = -1e30
ZERO_WEIGHT_LOG2 = 170.0
BF16_NORM_SLACK = 1.02
FIXED_REF_MAX_BOUND = 56.0


def _rms(x, g):
    return x * lax.rsqrt(jnp.mean(x * x, axis=-1, keepdims=True) + EPS) * g


def _resident(shape):
    return pl.BlockSpec(shape, lambda *_: (0,) * len(shape), pipeline_mode=pl.Buffered(1))


def _swiglu_half_step(x, g, w_in_ref, w_out_ref):
    xn = _rms(x, g).astype(BF16)
    gu = jnp.dot(xn, w_in_ref[...], preferred_element_type=F32)
    gate = gu[:, :D_FF]
    up = gu[:, D_FF:]
    act = (gate * jax.nn.sigmoid(gate) * up).astype(BF16)
    return x + FFN_HALF * jnp.dot(act, w_out_ref[...], preferred_element_type=F32)


def _ffn_kernel(x_ref, g_ref, w_in_ref, w_out_ref, o_ref):
    o_ref[...] = _swiglu_half_step(x_ref[...], g_ref[...], w_in_ref, w_out_ref)


def _ffn(h2d, g, w_in, w_out):
    n = h2d.shape[0]
    return pl.pallas_call(
        _ffn_kernel,
        grid=(n // ROW_TILE,),
        in_specs=[
            pl.BlockSpec((ROW_TILE, D_MODEL), lambda i: (i, 0)),
            _resident((1, D_MODEL)),
            _resident((D_MODEL, 2 * D_FF)),
            _resident((D_FF, D_MODEL)),
        ],
        out_specs=pl.BlockSpec((ROW_TILE, D_MODEL), lambda i: (i, 0)),
        out_shape=jax.ShapeDtypeStruct((n, D_MODEL), F32),
        compiler_params=pltpu.CompilerParams(
            dimension_semantics=("parallel",), vmem_limit_bytes=VMEM_LIMIT),
        name="ffn",
    )(h2d, g, w_in, w_out)


def _post_kernel(x_ref, ot_ref, wo_ref, g2_ref, w_in_ref, w_out_ref,
                 p_ref, gp_ref, wg_ref, wp_ref, o_ref):
    x = x_ref[...] + lax.dot_general(ot_ref[...], wo_ref[...], (((0,), (0,)), ((), ())),
                                     preferred_element_type=F32)
    x = _swiglu_half_step(x, g2_ref[...], w_in_ref, w_out_ref)
    xn = _rms(x, gp_ref[...]).astype(BF16)
    gate = jax.nn.sigmoid(jnp.dot(xn, wg_ref[...], preferred_element_type=F32))
    pp = jnp.dot(p_ref[...].astype(BF16), wp_ref[...], preferred_element_type=F32)
    o_ref[...] = x + gate * pp


def _post(h2d, ot, w_o, g2, w_in, w_out, p3d, layer, g_ple, w_gate, w_proj):
    n = h2d.shape[0]
    hd, s = ot.shape[1], ot.shape[2]
    tiles = s // ROW_TILE
    rows = pl.BlockSpec((ROW_TILE, D_MODEL), lambda i: (i, 0))
    return pl.pallas_call(
        _post_kernel,
        grid=(n // ROW_TILE,),
        in_specs=[
            rows,
            pl.BlockSpec((None, hd, ROW_TILE), lambda i: (i // tiles, 0, i % tiles)),
            _resident((hd, D_MODEL)),
            _resident((1, D_MODEL)),
            _resident((D_MODEL, 2 * D_FF)),
            _resident((D_FF, D_MODEL)),
            pl.BlockSpec((None, ROW_TILE, PLE_DIM), lambda i: (layer, i, 0)),
            _resident((1, D_MODEL)),
            _resident((D_MODEL, D_MODEL)),
            _resident((PLE_DIM, D_MODEL)),
        ],
        out_specs=rows,
        out_shape=jax.ShapeDtypeStruct((n, D_MODEL), F32),
        compiler_params=pltpu.CompilerParams(
            dimension_semantics=("parallel",), vmem_limit_bytes=VMEM_LIMIT),
        name="post",
    )(h2d, ot, w_o, g2, w_in, w_out, p3d, g_ple, w_gate, w_proj)


def _dot_nt(w, x):
    return lax.dot_general(w, x, (((1,), (1,)), ((), ())), preferred_element_type=F32)


def _ones_row_group(width):
    row = lax.broadcasted_iota(jnp.int32, (BF16_SUBLANES, width), 0)
    return jnp.where(row == 0, 1.0, 0.0).astype(F32)


def _rope_rows(x1, x2, cos, sin):
    return x1 * cos - x2 * sin, x1 * sin + x2 * cos


def _mla_proj_kernel(x_ref, pos_ref, g_ref, w_in_ref, w_pe_ref, gq_lat_ref, gkv_lat_ref,
                     w_uq_ref, w_ukv_ref, gqn_ref, gkn_ref, inv_freq_ref,
                     qt_ref, k_ref, vt_ref):
    tm = x_ref.shape[0]
    xn = _rms(x_ref[...], g_ref[...]).astype(BF16)
    z = jnp.dot(xn, w_in_ref[...], preferred_element_type=F32)
    cq = _rms(z[:, :MLA_Q_RANK], gq_lat_ref[...]).astype(BF16)
    ckv = _rms(z[:, MLA_Q_RANK:], gkv_lat_ref[...]).astype(BF16)
    kpe_t = _dot_nt(w_pe_ref[...], xn)

    ang = pos_ref[...].astype(F32) * inv_freq_ref[...]
    cos = jnp.cos(ang)
    sin = jnp.sin(ang)
    half = MLA_ROPE // 2
    q_scale = (MLA_QK ** -0.5) * LOG2E
    gqn = gqn_ref[...] * q_scale
    gkn = gkn_ref[...]
    kpe_sq = jnp.sum(kpe_t * kpe_t, axis=0, keepdims=True)
    ones_rows = _ones_row_group(tm).astype(BF16)
    zeros_tail = jnp.zeros((QK_PAD - MLA_QK, tm), F32)

    kv_dim = MLA_NOPE + HEAD_V
    for h in range(HEADS):
        if h % HEAD_GROUP == 0:
            g0 = h
            qt_all = _dot_nt(w_uq_ref[g0 * MLA_QK:(g0 + HEAD_GROUP) * MLA_QK, :], cq)
            kvt_all = _dot_nt(w_ukv_ref[g0 * kv_dim:(g0 + HEAD_GROUP) * kv_dim, :], ckv)
        hl = h - g0
        qh = qt_all[hl * MLA_QK:(hl + 1) * MLA_QK]
        qn = qh * lax.rsqrt(jnp.mean(qh * qh, axis=0, keepdims=True) + EPS) * gqn
        q1, q2 = _rope_rows(qn[MLA_NOPE:MLA_NOPE + half], qn[MLA_NOPE + half:], cos, sin)
        qt_ref[h, 0:MLA_NOPE, :] = qn[:MLA_NOPE].astype(BF16)
        qt_ref[h, MLA_NOPE:MLA_NOPE + half, :] = q1.astype(BF16)
        qt_ref[h, MLA_NOPE + half:MLA_QK, :] = q2.astype(BF16)
        qt_ref[h, MLA_QK:QK_PAD, :] = zeros_tail.astype(BF16)

        base = hl * kv_dim
        kn = kvt_all[base:base + MLA_NOPE]
        ms = (jnp.sum(kn * kn, axis=0, keepdims=True) + kpe_sq) * (1.0 / MLA_QK)
        r = lax.rsqrt(ms + EPS)
        kn = kn * r * gkn[:MLA_NOPE]
        kp = kpe_t * r * gkn[MLA_NOPE:]
        k1, k2 = _rope_rows(kp[:half], kp[half:], cos, sin)
        kt = jnp.concatenate([kn, k1, k2, zeros_tail], axis=0)
        k_ref[h] = kt.T.astype(BF16)

        vt_ref[h, 0:HEAD_V, :] = kvt_all[base + MLA_NOPE:base + MLA_NOPE + HEAD_V].astype(BF16)
        vt_ref[h, HEAD_V:V_ROWS, :] = ones_rows


def _qkv_out(b, s):
    specs = [
        pl.BlockSpec((None, HEADS, QK_PAD, PROJ_TILE), lambda bi, i: (bi, 0, 0, i)),
        pl.BlockSpec((None, HEADS, PROJ_TILE, QK_PAD), lambda bi, i: (bi, 0, i, 0)),
        pl.BlockSpec((None, HEADS, V_ROWS, PROJ_TILE), lambda bi, i: (bi, 0, 0, i)),
    ]
    shapes = [
        jax.ShapeDtypeStruct((b, HEADS, QK_PAD, s), BF16),
        jax.ShapeDtypeStruct((b, HEADS, s, QK_PAD), BF16),
        jax.ShapeDtypeStruct((b, HEADS, V_ROWS, s), BF16),
    ]
    return specs, shapes


def _mla_proj(h, pos3, g, w_in, w_pe_t, gq_lat, gkv_lat, w_uq_t, w_ukv_t, gqn, gkn, inv_freq):
    b, s, _ = h.shape
    out_specs, out_shapes = _qkv_out(b, s)
    return pl.pallas_call(
        _mla_proj_kernel,
        grid=(b, s // PROJ_TILE),
        in_specs=[
            pl.BlockSpec((None, PROJ_TILE, D_MODEL), lambda bi, i: (bi, i, 0)),
            pl.BlockSpec((None, 1, PROJ_TILE), lambda bi, i: (bi, 0, i)),
            _resident(g.shape), _resident(w_in.shape), _resident(w_pe_t.shape),
            _resident(gq_lat.shape), _resident(gkv_lat.shape),
            _resident(w_uq_t.shape), _resident(w_ukv_t.shape),
            _resident(gqn.shape), _resident(gkn.shape), _resident(inv_freq.shape),
        ],
        out_specs=out_specs,
        out_shape=out_shapes,
        compiler_params=pltpu.CompilerParams(
            dimension_semantics=("parallel", "parallel"), vmem_limit_bytes=VMEM_LIMIT),
        name="mla_proj",
    )(h, pos3, g, w_in, w_pe_t, gq_lat, gkv_lat, w_uq_t, w_ukv_t, gqn, gkn, inv_freq)


def _split3(c):
    hi = c.astype(BF16).astype(F32)
    mid = (c - hi).astype(BF16).astype(F32)
    lo = (c - hi - mid).astype(BF16).astype(F32)
    return hi, mid, lo


def _fox_proj_kernel(x_ref, g_ref, w_ref, bf_ref, gqn_ref, gkn_ref,
                     qt_ref, k_ref, vt_ref, c_ref, carry_ref):
    tm = x_ref.shape[0]
    hd = HEADS * FOX_HEAD_DIM

    @pl.when(pl.program_id(1) == 0)
    def _():
        carry_ref[...] = jnp.zeros_like(carry_ref)

    xn = _rms(x_ref[...], g_ref[...]).astype(BF16)
    f_t = _dot_nt(w_ref[3 * hd:3 * hd + HEADS, :], xn) + bf_ref[...]

    logf = (jnp.minimum(f_t, 0.0) - jnp.log1p(jnp.exp(-jnp.abs(f_t)))) * LOG2E
    hi, mid, lo = _split3(logf)
    parts = jnp.concatenate([hi, mid, lo], axis=0).astype(BF16)
    src = lax.broadcasted_iota(jnp.int32, (tm, tm), 0)
    dst = lax.broadcasted_iota(jnp.int32, (tm, tm), 1)
    upper = jnp.where(src <= dst, 1.0, 0.0).astype(BF16)
    sums = jnp.dot(parts, upper, preferred_element_type=F32)
    local = sums[2 * HEADS:] + sums[HEADS:2 * HEADS] + sums[:HEADS]
    c = local + carry_ref[:, 0:1]
    carry_ref[...] = jnp.broadcast_to(c[:, tm - 1:tm], carry_ref.shape)
    c_ref[...] = c

    q_scale = (FOX_HEAD_DIM ** -0.5) * LOG2E
    gqn = gqn_ref[...] * q_scale
    gkn = gkn_ref[...]
    row = lax.broadcasted_iota(jnp.int32, (BF16_SUBLANES, tm), 0)
    ones_rows = _ones_row_group(tm).astype(BF16)
    zeros_tail = jnp.zeros((QK_PAD - FOX_HEAD_DIM - BF16_SUBLANES, tm), F32)

    def bias_rows(first, second):
        out = jnp.zeros((BF16_SUBLANES, tm), F32)
        for j in range(3):
            out = jnp.where(row == j, first[j], out)
            out = jnp.where(row == 8 + j, second[j], out)
        return out

    one3 = (1.0, 1.0, 1.0)
    gw = HEAD_GROUP * FOX_HEAD_DIM
    for h in range(HEADS):
        if h % HEAD_GROUP == 0:
            g0 = h
            r0 = g0 * FOX_HEAD_DIM
            qt_all = _dot_nt(w_ref[r0:r0 + gw, :], xn)
            kt_all = _dot_nt(w_ref[hd + r0:hd + r0 + gw, :], xn)
            vt_all = _dot_nt(w_ref[2 * hd + r0:2 * hd + r0 + gw, :], xn)
        sl = slice((h - g0) * FOX_HEAD_DIM, (h - g0 + 1) * FOX_HEAD_DIM)
        c3 = _split3(c[h:h + 1])
        qh = qt_all[sl]
        qn = qh * lax.rsqrt(jnp.mean(qh * qh, axis=0, keepdims=True) + EPS) * gqn
        qt_ref[h, 0:FOX_HEAD_DIM, :] = qn.astype(BF16)
        qt_ref[h, FOX_HEAD_DIM:FOX_HEAD_DIM + BF16_SUBLANES, :] = bias_rows(one3, c3).astype(BF16)
        qt_ref[h, FOX_HEAD_DIM + BF16_SUBLANES:QK_PAD, :] = zeros_tail.astype(BF16)

        kh = kt_all[sl]
        kn = kh * lax.rsqrt(jnp.mean(kh * kh, axis=0, keepdims=True) + EPS) * gkn
        neg3 = tuple(-p for p in c3)
        kt = jnp.concatenate([kn, bias_rows(neg3, one3), zeros_tail], axis=0)
        k_ref[h] = kt.T.astype(BF16)

        vt_ref[h, 0:HEAD_V, :] = vt_all[sl].astype(BF16)
        vt_ref[h, HEAD_V:V_ROWS, :] = ones_rows


def _fox_proj(h, g, w_t, b_f, gqn, gkn):
    b, s, _ = h.shape
    out_specs, out_shapes = _qkv_out(b, s)
    out_specs = out_specs + [pl.BlockSpec((None, HEADS, PROJ_TILE), lambda bi, i: (bi, 0, i))]
    out_shapes = out_shapes + [jax.ShapeDtypeStruct((b, HEADS, s), F32)]
    return pl.pallas_call(
        _fox_proj_kernel,
        grid=(b, s // PROJ_TILE),
        in_specs=[
            pl.BlockSpec((None, PROJ_TILE, D_MODEL), lambda bi, i: (bi, i, 0)),
            _resident(g.shape), _resident(w_t.shape), _resident(b_f.shape),
            _resident(gqn.shape), _resident(gkn.shape),
        ],
        out_specs=out_specs,
        out_shape=out_shapes,
        scratch_shapes=[pltpu.VMEM((HEADS, LANES), F32)],
        compiler_params=pltpu.CompilerParams(
            dimension_semantics=("arbitrary", "arbitrary"), vmem_limit_bytes=VMEM_LIMIT),
        name="fox_proj",
    )(h, g, w_t, b_f, gqn, gkn)


def _attn_kernel(cq_ref, cg_ref, thr_ref, qt_ref, k_ref, vt_ref, o_ref, acc_ref, m_ref, *bufs,
                 seq, tq, tk, cw):
    ncol = tq // cw
    nq = seq // tq
    all_chains = tuple(range(ncol))
    s_refs = bufs[:NSLOT]
    bm_refs = bufs[NSLOT:]
    bias_base = (pl.program_id(0) * pl.num_programs(1) + pl.program_id(1)) * nq

    def first_group(t):
        def dead(g):
            return jnp.logical_and(
                g < t - 1, cq_ref[bias_base + t] - cg_ref[bias_base + g] < -thr_ref[0])
        return lax.while_loop(dead, lambda g: g + 1, jnp.int32(0))

    pv_keys = min(PV_KEYS, tk)
    n_parts = tk // pv_keys
    full_plan = {c: tuple((r, False) for r in range(n_parts)) for c in all_chains}

    nd = tq // tk
    assert nd == NSLOT and MAIN_UNROLL % NSLOT == 0 and LAG < NSLOT

    def diag_plan(d):
        plan = {}
        for c in all_chains:
            parts = []
            for r in range(n_parts):
                key_lo = d * tk + r * pv_keys
                if key_lo <= (c + 1) * cw - 1:
                    parts.append((r, key_lo + pv_keys - 1 > c * cw))
            if parts:
                plan[c] = tuple(parts)
        return plan

    diag_plans = tuple(diag_plan(d) for d in range(nd))

    def stage_a_parts(q0, k0, slot, c, parts):
        cols = slice(c * cw, (c + 1) * cw)
        qt = qt_ref[:, pl.ds(q0 + c * cw, cw)]
        bm = None
        for r, masked in parts:
            rows = slice(r * pv_keys, (r + 1) * pv_keys)
            kb = k_ref[pl.ds(k0 + r * pv_keys, pv_keys), :]
            s = jnp.dot(kb, qt, preferred_element_type=F32)
            if masked:
                kpos = k0 + r * pv_keys + lax.broadcasted_iota(jnp.int32, (pv_keys, cw), 0)
                qpos = q0 + c * cw + lax.broadcasted_iota(jnp.int32, (pv_keys, cw), 1)
                s = jnp.where(kpos <= qpos, s, MASK_VALUE)
            s_refs[slot][c, rows, :] = s
            pm = jnp.max(s, axis=0, keepdims=True)
            bm = pm if bm is None else jnp.maximum(bm, pm)
            if r == parts[-1][0]:
                bm_refs[slot][:, cols] = bm
            yield

    def stage_b_parts(k0, slot, c, parts):
        cols = slice(c * cw, (c + 1) * cw)
        m_old = m_ref[:, cols]
        m_new = jnp.maximum(m_old, bm_refs[slot][:, cols])
        alpha = jnp.exp2(m_old - m_new)
        pv = None
        for r, _ in parts:
            rows = slice(r * pv_keys, (r + 1) * pv_keys)
            p = jnp.exp2(s_refs[slot][c, rows, :] - m_new).astype(BF16)
            vb = vt_ref[:, pl.ds(k0 + r * pv_keys, pv_keys)]
            t = jnp.dot(vb, p, preferred_element_type=F32)
            pv = t if pv is None else pv + t
            if r == parts[-1][0]:
                acc_ref[c] = alpha * acc_ref[c] + pv
                m_ref[:, cols] = m_new
            yield

    def step(q0, k0, rslot, nxt_q0, nxt_k0, a_plan=full_plan, b_plan=full_plan):
        for c in all_chains:
            gens = []
            if a_plan is not None and c in a_plan:
                gens.append(stage_a_parts(nxt_q0, nxt_k0, (rslot + LAG) % NSLOT, c, a_plan[c]))
            if b_plan is not None and c in b_plan:
                gens.append(stage_b_parts(k0, rslot, c, b_plan[c]))
            while gens:
                gens = [g for g in gens if next(g, gens) is not gens]

    def q_tile(qi, carry):
        q0 = pl.multiple_of(qi * tq, tq)
        acc_ref[...] = jnp.zeros_like(acc_ref)
        m_ref[...] = jnp.full_like(m_ref, M_INIT)

        def run_steps(k0, count):
            for j in range(count):
                step(q0, k0 + j * tk, j % NSLOT, q0, k0 + (j + LAG) * tk)

        g_first = first_group(qi)
        k_first = pl.multiple_of(g_first * tq, tq)
        n_main = NSLOT * (jnp.maximum(qi - 1, 0) - g_first)
        rem = n_main & (MAIN_UNROLL - 1)

        def main(t, carry2):
            run_steps(pl.multiple_of(k_first + t * MAIN_UNROLL * tk, tq), MAIN_UNROLL)
            return carry2

        lax.fori_loop(0, lax.shift_right_logical(n_main, MAIN_UNROLL.bit_length() - 1), main, 0)
        k_last = q0 - NSLOT * tk
        sub = MAIN_UNROLL // 2
        while sub >= NSLOT:
            k_sub = pl.multiple_of(k_last - ((rem & (sub - 1)) + sub) * tk, NSLOT * tk)

            @pl.when((rem & sub) != 0)
            def _(k_sub=k_sub, sub=sub):
                run_steps(k_sub, sub)

            sub //= 2

        nxt_qi = jnp.minimum(qi + 1, nq - 1)
        nxt_q0 = pl.multiple_of(nxt_qi * tq, tq)
        nxt_k_first = pl.multiple_of(first_group(nxt_qi) * tq, tq)

        def last_below():
            k0 = pl.multiple_of(k_last, NSLOT * tk)
            for j in range(NSLOT):
                tgt = j + LAG
                if tgt < NSLOT:
                    step(q0, k0 + j * tk, j, q0, k0 + tgt * tk)
                else:
                    step(q0, k0 + j * tk, j, q0, q0 + (tgt - NSLOT) * tk,
                         a_plan=diag_plans[tgt - NSLOT])

        def diagonal():
            for d in range(nd):
                tgt = d + LAG
                if tgt < nd:
                    step(q0, q0 + d * tk, d % NSLOT, q0, q0 + tgt * tk,
                         a_plan=diag_plans[tgt], b_plan=diag_plans[d])
                else:
                    step(q0, q0 + d * tk, d % NSLOT, nxt_q0, nxt_k_first + (tgt - nd) * tk,
                         b_plan=diag_plans[d])

        @pl.when(qi >= 1)
        def _():
            last_below()
            diagonal()

        @pl.when(qi == 0)
        def _():
            diagonal()

        for c in all_chains:
            acc = acc_ref[c]
            o = acc[:HEAD_V] / acc[HEAD_V:HEAD_V + 1]
            o_ref[:, pl.ds(q0 + c * cw, cw)] = o.astype(o_ref.dtype)
        return carry

    for j in range(LAG):
        step(0, 0, (j - LAG) % NSLOT, 0, j * tk, a_plan=diag_plans[j], b_plan=None)
    lax.fori_loop(0, nq, q_tile, 0)


def _attn_fixed_kernel(cq_ref, cg_ref, thr_ref, qt_ref, k_ref, vt_ref, o_ref, acc_ref,
                       *, seq, tq, tk, cw):
    ncol = tq // cw
    nq = seq // tq
    nd = tq // tk
    all_chains = tuple(range(ncol))
    bias_base = (pl.program_id(0) * pl.num_programs(1) + pl.program_id(1)) * nq
    ref_max = thr_ref[1]

    def first_group(t):
        def dead(g):
            return jnp.logical_and(
                g < t - 1, cq_ref[bias_base + t] - cg_ref[bias_base + g] < -thr_ref[0])
        return lax.while_loop(dead, lambda g: g + 1, jnp.int32(0))

    def scores(q0, k0, c, masked):
        qt = qt_ref[:, pl.ds(q0 + c * cw, cw)]
        s = jnp.dot(k_ref[pl.ds(k0, tk), :], qt, preferred_element_type=F32)
        if masked:
            kpos = k0 + lax.broadcasted_iota(jnp.int32, (tk, cw), 0)
            qpos = q0 + c * cw + lax.broadcasted_iota(jnp.int32, (tk, cw), 1)
            s = jnp.where(kpos <= qpos, s, MASK_VALUE)
        return s

    def weighted_values(k0, s):
        p = jnp.exp2(s - ref_max).astype(BF16)
        return jnp.dot(vt_ref[:, pl.ds(k0, tk)], p, preferred_element_type=F32)

    def run_blocks(q0, k0, count):
        tot = [None] * ncol
        s_cur = [scores(q0, k0, c, False) for c in all_chains]
        for j in range(count):
            s_nxt = [None] * ncol
            for c in all_chains:
                if j + 1 < count:
                    s_nxt[c] = scores(q0, k0 + (j + 1) * tk, c, False)
                t = weighted_values(k0 + j * tk, s_cur[c])
                tot[c] = t if tot[c] is None else tot[c] + t
            s_cur = s_nxt
        for c in all_chains:
            acc_ref[c] = acc_ref[c] + tot[c]

    def q_tile(qi, carry):
        q0 = pl.multiple_of(qi * tq, tq)
        acc_ref[...] = jnp.zeros_like(acc_ref)
        g_first = first_group(qi)
        k_first = pl.multiple_of(g_first * tq, tq)
        n_blocks = nd * (qi - g_first)
        rem = n_blocks & (MAIN_UNROLL - 1)

        def main(t, carry2):
            run_blocks(q0, pl.multiple_of(k_first + t * MAIN_UNROLL * tk, tq), MAIN_UNROLL)
            return carry2

        lax.fori_loop(0, lax.shift_right_logical(n_blocks, MAIN_UNROLL.bit_length() - 1), main, 0)
        sub = MAIN_UNROLL // 2
        while sub >= nd:
            k_sub = pl.multiple_of(q0 - ((rem & (sub - 1)) + sub) * tk, tq)

            @pl.when((rem & sub) != 0)
            def _(k_sub=k_sub, sub=sub):
                run_blocks(q0, k_sub, sub)

            sub //= 2

        tot = [None] * ncol
        for d in range(nd):
            for c in range(d, ncol):
                t = weighted_values(q0 + d * tk, scores(q0, q0 + d * tk, c, c == d))
                tot[c] = t if tot[c] is None else tot[c] + t
        for c in all_chains:
            acc = acc_ref[c] + tot[c]
            o = acc[:HEAD_V] / acc[HEAD_V:HEAD_V + 1]
            o_ref[:, pl.ds(q0 + c * cw, cw)] = o.astype(o_ref.dtype)
        return carry

    lax.fori_loop(0, nq, q_tile, 0)


def _attention(qt, k, vt, bound, bias=None):
    b, heads, _, s = qt.shape
    tq = min(ATT_TQ, s)
    tk = min(ATT_TK, tq // NSLOT)
    assert tk == ATT_CW
    nq = s // tq
    if bias is None:
        cq = cg = jnp.zeros((b * heads * nq,), F32)
    else:
        cq = bias[:, :, ::tq].reshape(-1)
        cg = bias[:, :, tq - 1::tq].reshape(-1)
    bound = jnp.asarray(bound, F32)
    thr = jnp.stack([2.0 * bound + ZERO_WEIGHT_LOG2, bound])
    in_specs = [
        pl.BlockSpec((None, None, QK_PAD, s), lambda bi, hi, *_: (bi, hi, 0, 0)),
        pl.BlockSpec((None, None, s, QK_PAD), lambda bi, hi, *_: (bi, hi, 0, 0)),
        pl.BlockSpec((None, None, V_ROWS, s), lambda bi, hi, *_: (bi, hi, 0, 0)),
    ]
    out_specs = pl.BlockSpec((None, HEAD_V, s), lambda bi, hi, *_: (bi, hi, 0))
    acc_scratch = pltpu.VMEM((tq // ATT_CW, V_ROWS, ATT_CW), F32)
    common = dict(
        out_shape=jax.ShapeDtypeStruct((b, heads * HEAD_V, s), BF16),
        compiler_params=pltpu.CompilerParams(
            dimension_semantics=("parallel", "parallel"), vmem_limit_bytes=VMEM_LIMIT),
    )

    def online(*args):
        return pl.pallas_call(
            functools.partial(_attn_kernel, seq=s, tq=tq, tk=tk, cw=ATT_CW),
            grid_spec=pltpu.PrefetchScalarGridSpec(
                num_scalar_prefetch=3, grid=(b, heads), in_specs=in_specs, out_specs=out_specs,
                scratch_shapes=([acc_scratch, pltpu.VMEM((1, tq), F32)]
                                + [pltpu.VMEM((tq // ATT_CW, tk, ATT_CW), F32)] * NSLOT
                                + [pltpu.VMEM((1, tq), F32)] * NSLOT),
            ),
            name="attention", **common)(*args)

    def fixed(*args):
        return pl.pallas_call(
            functools.partial(_attn_fixed_kernel, seq=s, tq=tq, tk=tk, cw=ATT_CW),
            grid_spec=pltpu.PrefetchScalarGridSpec(
                num_scalar_prefetch=3, grid=(b, heads), in_specs=in_specs, out_specs=out_specs,
                scratch_shapes=[acc_scratch],
            ),
            name="attention_fixed", **common)(*args)

    return lax.cond(bound <= FIXED_REF_MAX_BOUND, fixed, online, cq, cg, thr, qt, k, vt)


def _score_bound(g_q, g_k, dk):
    return (jnp.max(jnp.abs(g_q)) * jnp.max(jnp.abs(g_k)) * (dk ** 0.5) * LOG2E
            * BF16_NORM_SLACK).astype(F32)


def kernel(x, p, positions, g_ffn1, g_mix, g_ffn2, g_ple, ffn1_w_in, ffn1_w_out, ffn2_w_in, ffn2_w_out, ple_w_proj, ple_w_gate, mla_w_in, mla_g_q_lat, mla_w_uq, mla_g_kv_lat, mla_w_ukv, mla_g_qn, mla_g_kn, mla_w_o, fox_w_in, fox_b_f, fox_g_qn, fox_g_kn, fox_w_o):
    b, s, d = x.shape
    depth = g_ffn1.shape[0]
    assert d == D_MODEL and s % PROJ_TILE == 0 and s % ROW_TILE == 0
    n = b * s
    bf = lambda w: w.astype(BF16)
    row = lambda g: g.reshape(1, -1)
    col = lambda g: g.reshape(-1, 1)

    pos3 = positions.reshape(b, 1, s)
    half = MLA_ROPE // 2
    inv_freq = (ROPE_THETA ** (-jnp.arange(0, MLA_ROPE, 2, dtype=F32) / MLA_ROPE)).reshape(half, 1)
    lat = MLA_Q_RANK + MLA_KV_RANK

    h = x
    for i in range(depth):
        j = i // N_MIXERS
        h = _ffn(h.reshape(n, d), row(g_ffn1[i]), bf(ffn1_w_in[i]), bf(ffn1_w_out[i])).reshape(b, s, d)
        if i % N_MIXERS == 0:
            qt, k, vt = _mla_proj(
                h, pos3, row(g_mix[i]), bf(mla_w_in[j][:, :lat]), bf(mla_w_in[j][:, lat:].T),
                row(mla_g_q_lat[j]), row(mla_g_kv_lat[j]), bf(mla_w_uq[j].T), bf(mla_w_ukv[j].T),
                col(mla_g_qn[j]), col(mla_g_kn[j]), inv_freq)
            w_o = mla_w_o[j]
            ot = _attention(qt, k, vt, _score_bound(mla_g_qn[j], mla_g_kn[j], MLA_QK))
        else:
            qt, k, vt, c = _fox_proj(h, row(g_mix[i]), bf(fox_w_in[j].T), col(fox_b_f[j]),
                                     col(fox_g_qn[j]), col(fox_g_kn[j]))
            w_o = fox_w_o[j]
            ot = _attention(qt, k, vt, _score_bound(fox_g_qn[j], fox_g_kn[j], FOX_HEAD_DIM), bias=c)
        h = _post(h.reshape(n, d), ot, bf(w_o), row(g_ffn2[i]), bf(ffn2_w_in[i]), bf(ffn2_w_out[i]),
                  p.reshape(depth, n, PLE_DIM), i, row(g_ple[i]), bf(ple_w_gate[i]),
                  bf(ple_w_proj[i])).reshape(b, s, d)
    return h
```

```python
import functools
import math

import jax
import jax.numpy as jnp
from jax import lax
from jax.experimental import pallas as pl
from jax.experimental.pallas import tpu as pltpu

F32 = jnp.float32
BF16 = jnp.bfloat16

D_MODEL = 1024
D_FF = 2816
PLE_DIM = 256
N_MIXERS = 2
FFN_HALF = 0.5
HEADS = 16
MLA_Q_RANK = 512
MLA_KV_RANK = 256
MLA_NOPE = 64
MLA_ROPE = 32
MLA_QK = MLA_NOPE + MLA_ROPE
HEAD_V = 64
FOX_HEAD_DIM = 64
ROPE_THETA = 10000.0
EPS = 1e-6
LOG2E = math.log2(math.e)

LANES = 128
MXU_DIM = 256
BF16_SUBLANES = 16
QK_PAD = 128
V_ROWS = HEAD_V + BF16_SUBLANES
VMEM_LIMIT = 52 * 1024 * 1024

ROW_TILE = 512
PROJ_TILE = 512
HEAD_GROUP = 4
ATT_TQ = 1024
ATT_CW = MXU_DIM
ATT_TK = MXU_DIM
MAIN_UNROLL = 16
NSLOT = 4
LAG = 2
PV_KEYS = MXU_DIM
MASK_VALUE = -jnp.inf
M_INIT = -1e30
ZERO_WEIGHT_LOG2 = 170.0
BF16_NORM_SLACK = 1.02
FIXED_REF_MAX_BOUND = 56.0


def _rms(x, g):
    return x * lax.rsqrt(jnp.mean(x * x, axis=-1, keepdims=True) + EPS) * g


def _resident(shape):
    return pl.BlockSpec(shape, lambda *_: (0,) * len(shape), pipeline_mode=pl.Buffered(1))


def _swiglu_half_step(x, g, w_in_ref, w_out_ref):
    xn = _rms(x, g).astype(BF16)
    gu = jnp.dot(xn, w_in_ref[...], preferred_element_type=F32)
    gate = gu[:, :D_FF]
    up = gu[:, D_FF:]
    act = (gate * jax.nn.sigmoid(gate) * up).astype(BF16)
    return x + FFN_HALF * jnp.dot(act, w_out_ref[...], preferred_element_type=F32)


def _ffn_kernel(x_ref, g_ref, w_in_ref, w_out_ref, o_ref):
    o_ref[...] = _swiglu_half_step(x_ref[...], g_ref[...], w_in_ref, w_out_ref)


def _ffn(h2d, g, w_in, w_out):
    n = h2d.shape[0]
    return pl.pallas_call(
        _ffn_kernel,
        grid=(n // ROW_TILE,),
        in_specs=[
            pl.BlockSpec((ROW_TILE, D_MODEL), lambda i: (i, 0)),
            _resident((1, D_MODEL)),
            _resident((D_MODEL, 2 * D_FF)),
            _resident((D_FF, D_MODEL)),
        ],
        out_specs=pl.BlockSpec((ROW_TILE, D_MODEL), lambda i: (i, 0)),
        out_shape=jax.ShapeDtypeStruct((n, D_MODEL), F32),
        compiler_params=pltpu.CompilerParams(
            dimension_semantics=("parallel",), vmem_limit_bytes=VMEM_LIMIT),
        name="ffn",
    )(h2d, g, w_in, w_out)


def _post_kernel(x_ref, ot_ref, wo_ref, g2_ref, w_in_ref, w_out_ref,
                 p_ref, gp_ref, wg_ref, wp_ref, o_ref):
    x = x_ref[...] + lax.dot_general(ot_ref[...], wo_ref[...], (((0,), (0,)), ((), ())),
                                     preferred_element_type=F32)
    x = _swiglu_half_step(x, g2_ref[...], w_in_ref, w_out_ref)
    xn = _rms(x, gp_ref[...]).astype(BF16)
    gate = jax.nn.sigmoid(jnp.dot(xn, wg_ref[...], preferred_element_type=F32))
    pp = jnp.dot(p_ref[...].astype(BF16), wp_ref[...], preferred_element_type=F32)
    o_ref[...] = x + gate * pp


def _post(h2d, ot, w_o, g2, w_in, w_out, p3d, layer, g_ple, w_gate, w_proj):
    n = h2d.shape[0]
    hd, s = ot.shape[1], ot.shape[2]
    tiles = s // ROW_TILE
    rows = pl.BlockSpec((ROW_TILE, D_MODEL), lambda i: (i, 0))
    return pl.pallas_call(
        _post_kernel,
        grid=(n // ROW_TILE,),
        in_specs=[
            rows,
            pl.BlockSpec((None, hd, ROW_TILE), lambda i: (i // tiles, 0, i % tiles)),
            _resident((hd, D_MODEL)),
            _resident((1, D_MODEL)),
            _resident((D_MODEL, 2 * D_FF)),
            _resident((D_FF, D_MODEL)),
            pl.BlockSpec((None, ROW_TILE, PLE_DIM), lambda i: (layer, i, 0)),
            _resident((1, D_MODEL)),
            _resident((D_MODEL, D_MODEL)),
            _resident((PLE_DIM, D_MODEL)),
        ],
        out_specs=rows,
        out_shape=jax.ShapeDtypeStruct((n, D_MODEL), F32),
        compiler_params=pltpu.CompilerParams(
            dimension_semantics=("parallel",), vmem_limit_bytes=VMEM_LIMIT),
        name="post",
    )(h2d, ot, w_o, g2, w_in, w_out, p3d, g_ple, w_gate, w_proj)


def _dot_nt(w, x):
    return lax.dot_general(w, x, (((1,), (1,)), ((), ())), preferred_element_type=F32)


def _ones_row_group(width):
    row = lax.broadcasted_iota(jnp.int32, (BF16_SUBLANES, width), 0)
    return jnp.where(row == 0, 1.0, 0.0).astype(F32)


def _rope_rows(x1, x2, cos, sin):
    return x1 * cos - x2 * sin, x1 * sin + x2 * cos


def _mla_proj_kernel(x_ref, pos_ref, g_ref, w_in_ref, w_pe_ref, gq_lat_ref, gkv_lat_ref,
                     w_uq_ref, w_ukv_ref, gqn_ref, gkn_ref, inv_freq_ref,
                     qt_ref, k_ref, vt_ref):
    tm = x_ref.shape[0]
    xn = _rms(x_ref[...], g_ref[...]).astype(BF16)
    z = jnp.dot(xn, w_in_ref[...], preferred_element_type=F32)
    cq = _rms(z[:, :MLA_Q_RANK], gq_lat_ref[...]).astype(BF16)
    ckv = _rms(z[:, MLA_Q_RANK:], gkv_lat_ref[...]).astype(BF16)
    kpe_t = _dot_nt(w_pe_ref[...], xn)

    ang = pos_ref[...].astype(F32) * inv_freq_ref[...]
    cos = jnp.cos(ang)
    sin = jnp.sin(ang)
    half = MLA_ROPE // 2
    q_scale = (MLA_QK ** -0.5) * LOG2E
    gqn = gqn_ref[...] * q_scale
    gkn = gkn_ref[...]
    kpe_sq = jnp.sum(kpe_t * kpe_t, axis=0, keepdims=True)
    ones_rows = _ones_row_group(tm).astype(BF16)
    zeros_tail = jnp.zeros((QK_PAD - MLA_QK, tm), F32)

    kv_dim = MLA_NOPE + HEAD_V
    for h in range(HEADS):
        if h % HEAD_GROUP == 0:
            g0 = h
            qt_all = _dot_nt(w_uq_ref[g0 * MLA_QK:(g0 + HEAD_GROUP) * MLA_QK, :], cq)
            kvt_all = _dot_nt(w_ukv_ref[g0 * kv_dim:(g0 + HEAD_GROUP) * kv_dim, :], ckv)
        hl = h - g0
        qh = qt_all[hl * MLA_QK:(hl + 1) * MLA_QK]
        qn = qh * lax.rsqrt(jnp.mean(qh * qh, axis=0, keepdims=True) + EPS) * gqn
        q1, q2 = _rope_rows(qn[MLA_NOPE:MLA_NOPE + half], qn[MLA_NOPE + half:], cos, sin)
        qt_ref[h, 0:MLA_NOPE, :] = qn[:MLA_NOPE].astype(BF16)
        qt_ref[h, MLA_NOPE:MLA_NOPE + half, :] = q1.astype(BF16)
        qt_ref[h, MLA_NOPE + half:MLA_QK, :] = q2.astype(BF16)
        qt_ref[h, MLA_QK:QK_PAD, :] = zeros_tail.astype(BF16)

        base = hl * kv_dim
        kn = kvt_all[base:base + MLA_NOPE]
        ms = (jnp.sum(kn * kn, axis=0, keepdims=True) + kpe_sq) * (1.0 / MLA_QK)
        r = lax.rsqrt(ms + EPS)
        kn = kn * r * gkn[:MLA_NOPE]
        kp = kpe_t * r * gkn[MLA_NOPE:]
        k1, k2 = _rope_rows(kp[:half], kp[half:], cos, sin)
        kt = jnp.concatenate([kn, k1, k2, zeros_tail], axis=0)
        k_ref[h] = kt.T.astype(BF16)

        vt_ref[h, 0:HEAD_V, :] = kvt_all[base + MLA_NOPE:base + MLA_NOPE + HEAD_V].astype(BF16)
        vt_ref[h, HEAD_V:V_ROWS, :] = ones_rows


def _qkv_out(b, s):
    specs = [
        pl.BlockSpec((None, HEADS, QK_PAD, PROJ_TILE), lambda bi, i: (bi, 0, 0, i)),
        pl.BlockSpec((None, HEADS, PROJ_TILE, QK_PAD), lambda bi, i: (bi, 0, i, 0)),
        pl.BlockSpec((None, HEADS, V_ROWS, PROJ_TILE), lambda bi, i: (bi, 0, 0, i)),
    ]
    shapes = [
        jax.ShapeDtypeStruct((b, HEADS, QK_PAD, s), BF16),
        jax.ShapeDtypeStruct((b, HEADS, s, QK_PAD), BF16),
        jax.ShapeDtypeStruct((b, HEADS, V_ROWS, s), BF16),
    ]
    return specs, shapes


def _mla_proj(h, pos3, g, w_in, w_pe_t, gq_lat, gkv_lat, w_uq_t, w_ukv_t, gqn, gkn, inv_freq):
    b, s, _ = h.shape
    out_specs, out_shapes = _qkv_out(b, s)
    return pl.pallas_call(
        _mla_proj_kernel,
        grid=(b, s // PROJ_TILE),
        in_specs=[
            pl.BlockSpec((None, PROJ_TILE, D_MODEL), lambda bi, i: (bi, i, 0)),
            pl.BlockSpec((None, 1, PROJ_TILE), lambda bi, i: (bi, 0, i)),
            _resident(g.shape), _resident(w_in.shape), _resident(w_pe_t.shape),
            _resident(gq_lat.shape), _resident(gkv_lat.shape),
            _resident(w_uq_t.shape), _resident(w_ukv_t.shape),
            _resident(gqn.shape), _resident(gkn.shape), _resident(inv_freq.shape),
        ],
        out_specs=out_specs,
        out_shape=out_shapes,
        compiler_params=pltpu.CompilerParams(
            dimension_semantics=("parallel", "parallel"), vmem_limit_bytes=VMEM_LIMIT),
        name="mla_proj",
    )(h, pos3, g, w_in, w_pe_t, gq_lat, gkv_lat, w_uq_t, w_ukv_t, gqn, gkn, inv_freq)


def _split3(c):
    hi = c.astype(BF16).astype(F32)
    mid = (c - hi).astype(BF16).astype(F32)
    lo = (c - hi - mid).astype(BF16).astype(F32)
    return hi, mid, lo


def _fox_proj_kernel(x_ref, g_ref, w_ref, bf_ref, gqn_ref, gkn_ref,
                     qt_ref, k_ref, vt_ref, c_ref, carry_ref):
    tm = x_ref.shape[0]
    hd = HEADS * FOX_HEAD_DIM

    @pl.when(pl.program_id(1) == 0)
    def _():
        carry_ref[...] = jnp.zeros_like(carry_ref)

    xn = _rms(x_ref[...], g_ref[...]).astype(BF16)
    f_t = _dot_nt(w_ref[3 * hd:3 * hd + HEADS, :], xn) + bf_ref[...]

    logf = (jnp.minimum(f_t, 0.0) - jnp.log1p(jnp.exp(-jnp.abs(f_t)))) * LOG2E
    hi, mid, lo = _split3(logf)
    parts = jnp.concatenate([hi, mid, lo], axis=0).astype(BF16)
    src = lax.broadcasted_iota(jnp.int32, (tm, tm), 0)
    dst = lax.broadcasted_iota(jnp.int32, (tm, tm), 1)
    upper = jnp.where(src <= dst, 1.0, 0.0).astype(BF16)
    sums = jnp.dot(parts, upper, preferred_element_type=F32)
    local = sums[2 * HEADS:] + sums[HEADS:2 * HEADS] + sums[:HEADS]
    c = local + carry_ref[:, 0:1]
    carry_ref[...] = jnp.broadcast_to(c[:, tm - 1:tm], carry_ref.shape)
    c_ref[...] = c

    q_scale = (FOX_HEAD_DIM ** -0.5) * LOG2E
    gqn = gqn_ref[...] * q_scale
    gkn = gkn_ref[...]
    row = lax.broadcasted_iota(jnp.int32, (BF16_SUBLANES, tm), 0)
    ones_rows = _ones_row_group(tm).astype(BF16)
    zeros_tail = jnp.zeros((QK_PAD - FOX_HEAD_DIM - BF16_SUBLANES, tm), F32)

    def bias_rows(first, second):
        out = jnp.zeros((BF16_SUBLANES, tm), F32)
        for j in range(3):
            out = jnp.where(row == j, first[j], out)
            out = jnp.where(row == 8 + j, second[j], out)
        return out

    one3 = (1.0, 1.0, 1.0)
    gw = HEAD_GROUP * FOX_HEAD_DIM
    for h in range(HEADS):
        if h % HEAD_GROUP == 0:
            g0 = h
            r0 = g0 * FOX_HEAD_DIM
            qt_all = _dot_nt(w_ref[r0:r0 + gw, :], xn)
            kt_all = _dot_nt(w_ref[hd + r0:hd + r0 + gw, :], xn)
            vt_all = _dot_nt(w_ref[2 * hd + r0:2 * hd + r0 + gw, :], xn)
        sl = slice((h - g0) * FOX_HEAD_DIM, (h - g0 + 1) * FOX_HEAD_DIM)
        c3 = _split3(c[h:h + 1])
        qh = qt_all[sl]
        qn = qh * lax.rsqrt(jnp.mean(qh * qh, axis=0, keepdims=True) + EPS) * gqn
        qt_ref[h, 0:FOX_HEAD_DIM, :] = qn.astype(BF16)
        qt_ref[h, FOX_HEAD_DIM:FOX_HEAD_DIM + BF16_SUBLANES, :] = bias_rows(one3, c3).astype(BF16)
        qt_ref[h, FOX_HEAD_DIM + BF16_SUBLANES:QK_PAD, :] = zeros_tail.astype(BF16)

        kh = kt_all[sl]
        kn = kh * lax.rsqrt(jnp.mean(kh * kh, axis=0, keepdims=True) + EPS) * gkn
        neg3 = tuple(-p for p in c3)
        kt = jnp.concatenate([kn, bias_rows(neg3, one3), zeros_tail], axis=0)
        k_ref[h] = kt.T.astype(BF16)

        vt_ref[h, 0:HEAD_V, :] = vt_all[sl].astype(BF16)
        vt_ref[h, HEAD_V:V_ROWS, :] = ones_rows


def _fox_proj(h, g, w_t, b_f, gqn, gkn):
    b, s, _ = h.shape
    out_specs, out_shapes = _qkv_out(b, s)
    out_specs = out_specs + [pl.BlockSpec((None, HEADS, PROJ_TILE), lambda bi, i: (bi, 0, i))]
    out_shapes = out_shapes + [jax.ShapeDtypeStruct((b, HEADS, s), F32)]
    return pl.pallas_call(
        _fox_proj_kernel,
        grid=(b, s // PROJ_TILE),
        in_specs=[
            pl.BlockSpec((None, PROJ_TILE, D_MODEL), lambda bi, i: (bi, i, 0)),
            _resident(g.shape), _resident(w_t.shape), _resident(b_f.shape),
            _resident(gqn.shape), _resident(gkn.shape),
        ],
        out_specs=out_specs,
        out_shape=out_shapes,
        scratch_shapes=[pltpu.VMEM((HEADS, LANES), F32)],
        compiler_params=pltpu.CompilerParams(
            dimension_semantics=("arbitrary", "arbitrary"), vmem_limit_bytes=VMEM_LIMIT),
        name="fox_proj",
    )(h, g, w_t, b_f, gqn, gkn)


def _attn_kernel(cq_ref, cg_ref, thr_ref, qt_ref, k_ref, vt_ref, o_ref, acc_ref, m_ref, *bufs,
                 seq, tq, tk, cw):
    ncol = tq // cw
    nq = seq // tq
    all_chains = tuple(range(ncol))
    s_refs = bufs[:NSLOT]
    bm_refs = bufs[NSLOT:]
    bias_base = (pl.program_id(0) * pl.num_programs(1) + pl.program_id(1)) * nq

    def first_group(t):
        def dead(g):
            return jnp.logical_and(
                g < t - 1, cq_ref[bias_base + t] - cg_ref[bias_base + g] < -thr_ref[0])
        return lax.while_loop(dead, lambda g: g + 1, jnp.int32(0))

    pv_keys = min(PV_KEYS, tk)
    n_parts = tk // pv_keys
    full_plan = {c: tuple((r, False) for r in range(n_parts)) for c in all_chains}

    nd = tq // tk
    assert nd == NSLOT and MAIN_UNROLL % NSLOT == 0 and LAG < NSLOT

    def diag_plan(d):
        plan = {}
        for c in all_chains:
            parts = []
            for r in range(n_parts):
                key_lo = d * tk + r * pv_keys
                if key_lo <= (c + 1) * cw - 1:
                    parts.append((r, key_lo + pv_keys - 1 > c * cw))
            if parts:
                plan[c] = tuple(parts)
        return plan

    diag_plans = tuple(diag_plan(d) for d in range(nd))

    def stage_a_parts(q0, k0, slot, c, parts):
        cols = slice(c * cw, (c + 1) * cw)
        qt = qt_ref[:, pl.ds(q0 + c * cw, cw)]
        bm = None
        for r, masked in parts:
            rows = slice(r * pv_keys, (r + 1) * pv_keys)
            kb = k_ref[pl.ds(k0 + r * pv_keys, pv_keys), :]
            s = jnp.dot(kb, qt, preferred_element_type=F32)
            if masked:
                kpos = k0 + r * pv_keys + lax.broadcasted_iota(jnp.int32, (pv_keys, cw), 0)
                qpos = q0 + c * cw + lax.broadcasted_iota(jnp.int32, (pv_keys, cw), 1)
                s = jnp.where(kpos <= qpos, s, MASK_VALUE)
            s_refs[slot][c, rows, :] = s
            pm = jnp.max(s, axis=0, keepdims=True)
            bm = pm if bm is None else jnp.maximum(bm, pm)
            if r == parts[-1][0]:
                bm_refs[slot][:, cols] = bm
            yield

    def stage_b_parts(k0, slot, c, parts):
        cols = slice(c * cw, (c + 1) * cw)
        m_old = m_ref[:, cols]
        m_new = jnp.maximum(m_old, bm_refs[slot][:, cols])
        alpha = jnp.exp2(m_old - m_new)
        pv = None
        for r, _ in parts:
            rows = slice(r * pv_keys, (r + 1) * pv_keys)
            p = jnp.exp2(s_refs[slot][c, rows, :] - m_new).astype(BF16)
            vb = vt_ref[:, pl.ds(k0 + r * pv_keys, pv_keys)]
            t = jnp.dot(vb, p, preferred_element_type=F32)
            pv = t if pv is None else pv + t
            if r == parts[-1][0]:
                acc_ref[c] = alpha * acc_ref[c] + pv
                m_ref[:, cols] = m_new
            yield

    def step(q0, k0, rslot, nxt_q0, nxt_k0, a_plan=full_plan, b_plan=full_plan):
        for c in all_chains:
            gens = []
            if a_plan is not None and c in a_plan:
                gens.append(stage_a_parts(nxt_q0, nxt_k0, (rslot + LAG) % NSLOT, c, a_plan[c]))
            if b_plan is not None and c in b_plan:
                gens.append(stage_b_parts(k0, rslot, c, b_plan[c]))
            while gens:
                gens = [g for g in gens if next(g, gens) is not gens]

    def q_tile(qi, carry):
        q0 = pl.multiple_of(qi * tq, tq)
        acc_ref[...] = jnp.zeros_like(acc_ref)
        m_ref[...] = jnp.full_like(m_ref, M_INIT)

        def run_steps(k0, count):
            for j in range(count):
                step(q0, k0 + j * tk, j % NSLOT, q0, k0 + (j + LAG) * tk)

        g_first = first_group(qi)
        k_first = pl.multiple_of(g_first * tq, tq)
        n_main = NSLOT * (jnp.maximum(qi - 1, 0) - g_first)
        rem = n_main & (MAIN_UNROLL - 1)

        def main(t, carry2):
            run_steps(pl.multiple_of(k_first + t * MAIN_UNROLL * tk, tq), MAIN_UNROLL)
            return carry2

        lax.fori_loop(0, lax.shift_right_logical(n_main, MAIN_UNROLL.bit_length() - 1), main, 0)
        k_last = q0 - NSLOT * tk
        sub = MAIN_UNROLL // 2
        while sub >= NSLOT:
            k_sub = pl.multiple_of(k_last - ((rem & (sub - 1)) + sub) * tk, NSLOT * tk)

            @pl.when((rem & sub) != 0)
            def _(k_sub=k_sub, sub=sub):
                run_steps(k_sub, sub)

            sub //= 2

        nxt_qi = jnp.minimum(qi + 1, nq - 1)
        nxt_q0 = pl.multiple_of(nxt_qi * tq, tq)
        nxt_k_first = pl.multiple_of(first_group(nxt_qi) * tq, tq)

        def last_below():
            k0 = pl.multiple_of(k_last, NSLOT * tk)
            for j in range(NSLOT):
                tgt = j + LAG
                if tgt < NSLOT:
                    step(q0, k0 + j * tk, j, q0, k0 + tgt * tk)
                else:
                    step(q0, k0 + j * tk, j, q0, q0 + (tgt - NSLOT) * tk,
                         a_plan=diag_plans[tgt - NSLOT])

        def diagonal():
            for d in range(nd):
                tgt = d + LAG
                if tgt < nd:
                    step(q0, q0 + d * tk, d % NSLOT, q0, q0 + tgt * tk,
                         a_plan=diag_plans[tgt], b_plan=diag_plans[d])
                else:
                    step(q0, q0 + d * tk, d % NSLOT, nxt_q0, nxt_k_first + (tgt - nd) * tk,
                         b_plan=diag_plans[d])

        @pl.when(qi >= 1)
        def _():
            last_below()
            diagonal()

        @pl.when(qi == 0)
        def _():
            diagonal()

        for c in all_chains:
            acc = acc_ref[c]
            o = acc[:HEAD_V] / acc[HEAD_V:HEAD_V + 1]
            o_ref[:, pl.ds(q0 + c * cw, cw)] = o.astype(o_ref.dtype)
        return carry

    for j in range(LAG):
        step(0, 0, (j - LAG) % NSLOT, 0, j * tk, a_plan=diag_plans[j], b_plan=None)
    lax.fori_loop(0, nq, q_tile, 0)


def _attn_fixed_kernel(cq_ref, cg_ref, thr_ref, qt_ref, k_ref, vt_ref, o_ref, acc_ref,
                       *, seq, tq, tk, cw):
    ncol = tq // cw
    nq = seq // tq
    nd = tq // tk
    all_chains = tuple(range(ncol))
    bias_base = (pl.program_id(0) * pl.num_programs(1) + pl.program_id(1)) * nq
    ref_max = thr_ref[1]

    def first_group(t):
        def dead(g):
            return jnp.logical_and(
                g < t - 1, cq_ref[bias_base + t] - cg_ref[bias_base + g] < -thr_ref[0])
        return lax.while_loop(dead, lambda g: g + 1, jnp.int32(0))

    def scores(q0, k0, c, masked):
        qt = qt_ref[:, pl.ds(q0 + c * cw, cw)]
        s = jnp.dot(k_ref[pl.ds(k0, tk), :], qt, preferred_element_type=F32)
        if masked:
            kpos = k0 + lax.broadcasted_iota(jnp.int32, (tk, cw), 0)
            qpos = q0 + c * cw + lax.broadcasted_iota(jnp.int32, (tk, cw), 1)
            s = jnp.where(kpos <= qpos, s, MASK_VALUE)
        return s

    def weighted_values(k0, s):
        p = jnp.exp2(s - ref_max).astype(BF16)
        return jnp.dot(vt_ref[:, pl.ds(k0, tk)], p, preferred_element_type=F32)

    def run_blocks(q0, k0, count):
        tot = [None] * ncol
        s_cur = [scores(q0, k0, c, False) for c in all_chains]
        for j in range(count):
            s_nxt = [None] * ncol
            for c in all_chains:
                if j + 1 < count:
                    s_nxt[c] = scores(q0, k0 + (j + 1) * tk, c, False)
                t = weighted_values(k0 + j * tk, s_cur[c])
                tot[c] = t if tot[c] is None else tot[c] + t
            s_cur = s_nxt
        for c in all_chains:
            acc_ref[c] = acc_ref[c] + tot[c]

    def q_tile(qi, carry):
        q0 = pl.multiple_of(qi * tq, tq)
        acc_ref[...] = jnp.zeros_like(acc_ref)
        g_first = first_group(qi)
        k_first = pl.multiple_of(g_first * tq, tq)
        n_blocks = nd * (qi - g_first)
        rem = n_blocks & (MAIN_UNROLL - 1)

        def main(t, carry2):
            run_blocks(q0, pl.multiple_of(k_first + t * MAIN_UNROLL * tk, tq), MAIN_UNROLL)
            return carry2

        lax.fori_loop(0, lax.shift_right_logical(n_blocks, MAIN_UNROLL.bit_length() - 1), main, 0)
        sub = MAIN_UNROLL // 2
        while sub >= nd:
            k_sub = pl.multiple_of(q0 - ((rem & (sub - 1)) + sub) * tk, tq)

            @pl.when((rem & sub) != 0)
            def _(k_sub=k_sub, sub=sub):
                run_blocks(q0, k_sub, sub)

            sub //= 2

        tot = [None] * ncol
        for d in range(nd):
            for c in range(d, ncol):
                t = weighted_values(q0 + d * tk, scores(q0, q0 + d * tk, c, c == d))
                tot[c] = t if tot[c] is None else tot[c] + t
        for c in all_chains:
            acc = acc_ref[c] + tot[c]
            o = acc[:HEAD_V] / acc[HEAD_V:HEAD_V + 1]
            o_ref[:, pl.ds(q0 + c * cw, cw)] = o.astype(o_ref.dtype)
        return carry

    lax.fori_loop(0, nq, q_tile, 0)


def _attention(qt, k, vt, bound, bias=None):
    b, heads, _, s = qt.shape
    tq = min(ATT_TQ, s)
    tk = min(ATT_TK, tq // NSLOT)
    assert tk == ATT_CW
    nq = s // tq
    if bias is None:
        cq = cg = jnp.zeros((b * heads * nq,), F32)
    else:
        cq = bias[:, :, ::tq].reshape(-1)
        cg = bias[:, :, tq - 1::tq].reshape(-1)
    bound = jnp.asarray(bound, F32)
    thr = jnp.stack([2.0 * bound + ZERO_WEIGHT_LOG2, bound])
    in_specs = [
        pl.BlockSpec((None, None, QK_PAD, s), lambda bi, hi, *_: (bi, hi, 0, 0)),
        pl.BlockSpec((None, None, s, QK_PAD), lambda bi, hi, *_: (bi, hi, 0, 0)),
        pl.BlockSpec((None, None, V_ROWS, s), lambda bi, hi, *_: (bi, hi, 0, 0)),
    ]
    out_specs = pl.BlockSpec((None, HEAD_V, s), lambda bi, hi, *_: (bi, hi, 0))
    acc_scratch = pltpu.VMEM((tq // ATT_CW, V_ROWS, ATT_CW), F32)
    common = dict(
        out_shape=jax.ShapeDtypeStruct((b, heads * HEAD_V, s), BF16),
        compiler_params=pltpu.CompilerParams(
            dimension_semantics=("parallel", "parallel"), vmem_limit_bytes=VMEM_LIMIT),
    )

    def online(*args):
        return pl.pallas_call(
            functools.partial(_attn_kernel, seq=s, tq=tq, tk=tk, cw=ATT_CW),
            grid_spec=pltpu.PrefetchScalarGridSpec(
                num_scalar_prefetch=3, grid=(b, heads), in_specs=in_specs, out_specs=out_specs,
                scratch_shapes=([acc_scratch, pltpu.VMEM((1, tq), F32)]
                                + [pltpu.VMEM((tq // ATT_CW, tk, ATT_CW), F32)] * NSLOT
                                + [pltpu.VMEM((1, tq), F32)] * NSLOT),
            ),
            name="attention", **common)(*args)

    def fixed(*args):
        return pl.pallas_call(
            functools.partial(_attn_fixed_kernel, seq=s, tq=tq, tk=tk, cw=ATT_CW),
            grid_spec=pltpu.PrefetchScalarGridSpec(
                num_scalar_prefetch=3, grid=(b, heads), in_specs=in_specs, out_specs=out_specs,
                scratch_shapes=[acc_scratch],
            ),
            name="attention_fixed", **common)(*args)

    if bias is not None:
        return online(cq, cg, thr, qt, k, vt)
    return lax.cond(bound <= FIXED_REF_MAX_BOUND, fixed, online, cq, cg, thr, qt, k, vt)


def _score_bound(g_q, g_k, dk):
    return (jnp.max(jnp.abs(g_q)) * jnp.max(jnp.abs(g_k)) * (dk ** 0.5) * LOG2E
            * BF16_NORM_SLACK).astype(F32)


def kernel(x, p, positions, g_ffn1, g_mix, g_ffn2, g_ple, ffn1_w_in, ffn1_w_out, ffn2_w_in, ffn2_w_out, ple_w_proj, ple_w_gate, mla_w_in, mla_g_q_lat, mla_w_uq, mla_g_kv_lat, mla_w_ukv, mla_g_qn, mla_g_kn, mla_w_o, fox_w_in, fox_b_f, fox_g_qn, fox_g_kn, fox_w_o):
    b, s, d = x.shape
    depth = g_ffn1.shape[0]
    assert d == D_MODEL and s % PROJ_TILE == 0 and s % ROW_TILE == 0
    n = b * s
    bf = lambda w: w.astype(BF16)
    row = lambda g: g.reshape(1, -1)
    col = lambda g: g.reshape(-1, 1)

    pos3 = positions.reshape(b, 1, s)
    half = MLA_ROPE // 2
    inv_freq = (ROPE_THETA ** (-jnp.arange(0, MLA_ROPE, 2, dtype=F32) / MLA_ROPE)).reshape(half, 1)
    lat = MLA_Q_RANK + MLA_KV_RANK

    h = x
    for i in range(depth):
        j = i // N_MIXERS
        h = _ffn(h.reshape(n, d), row(g_ffn1[i]), bf(ffn1_w_in[i]), bf(ffn1_w_out[i])).reshape(b, s, d)
        if i % N_MIXERS == 0:
            qt, k, vt = _mla_proj(
                h, pos3, row(g_mix[i]), bf(mla_w_in[j][:, :lat]), bf(mla_w_in[j][:, lat:].T),
                row(mla_g_q_lat[j]), row(mla_g_kv_lat[j]), bf(mla_w_uq[j].T), bf(mla_w_ukv[j].T),
                col(mla_g_qn[j]), col(mla_g_kn[j]), inv_freq)
            w_o = mla_w_o[j]
            ot = _attention(qt, k, vt, _score_bound(mla_g_qn[j], mla_g_kn[j], MLA_QK))
        else:
            qt, k, vt, c = _fox_proj(h, row(g_mix[i]), bf(fox_w_in[j].T), col(fox_b_f[j]),
                                     col(fox_g_qn[j]), col(fox_g_kn[j]))
            w_o = fox_w_o[j]
            ot = _attention(qt, k, vt, _score_bound(fox_g_qn[j], fox_g_kn[j], FOX_HEAD_DIM), bias=c)
        h = _post(h.reshape(n, d), ot, bf(w_o), row(g_ffn2[i]), bf(ffn2_w_in[i]), bf(ffn2_w_out[i]),
                  p.reshape(depth, n, PLE_DIM), i, row(g_ple[i]), bf(ple_w_gate[i]),
                  bf(ple_w_proj[i])).reshape(b, s, d)
    return h
```

```python
import functools
import math

import jax
import jax.numpy as jnp
from jax import lax
from jax.experimental import pallas as pl
from jax.experimental.pallas import tpu as pltpu

F32 = jnp.float32
BF16 = jnp.bfloat16

D_MODEL = 1024
D_FF = 2816
PLE_DIM = 256
N_MIXERS = 2
FFN_HALF = 0.5
HEADS = 16
MLA_Q_RANK = 512
MLA_KV_RANK = 256
MLA_NOPE = 64
MLA_ROPE = 32
MLA_QK = MLA_NOPE + MLA_ROPE
HEAD_V = 64
FOX_HEAD_DIM = 64
ROPE_THETA = 10000.0
EPS = 1e-6
LOG2E = math.log2(math.e)

LANES = 128
MXU_DIM = 256
BF16_SUBLANES = 16
QK_PAD = 128
V_ROWS = HEAD_V + BF16_SUBLANES
VMEM_LIMIT = 52 * 1024 * 1024

ROW_TILE = 512
PROJ_TILE = 512
HEAD_GROUP = 4
ATT_TQ = 1024
ATT_CW = MXU_DIM
ATT_TK = MXU_DIM
MAIN_UNROLL = 16
NSLOT = 4
LAG = 2
PV_KEYS = MXU_DIM
MASK_VALUE = -jnp.inf
M_INIT = -1e30
ZERO_WEIGHT_LOG2 = 170.0
BF16_NORM_SLACK = 1.02
FIXED_REF_MAX_BOUND = 56.0


def _rms(x, g):
    return x * lax.rsqrt(jnp.mean(x * x, axis=-1, keepdims=True) + EPS) * g


def _resident(shape):
    return pl.BlockSpec(shape, lambda *_: (0,) * len(shape), pipeline_mode=pl.Buffered(1))


def _swiglu_half_step(x, g, w_in_ref, w_out_ref):
    xn = _rms(x, g).astype(BF16)
    gu = jnp.dot(xn, w_in_ref[...], preferred_element_type=F32)
    gate = gu[:, :D_FF]
    up = gu[:, D_FF:]
    act = (gate * jax.nn.sigmoid(gate) * up).astype(BF16)
    return x + FFN_HALF * jnp.dot(act, w_out_ref[...], preferred_element_type=F32)


def _ffn_kernel(x_ref, g_ref, w_in_ref, w_out_ref, o_ref):
    o_ref[...] = _swiglu_half_step(x_ref[...], g_ref[...], w_in_ref, w_out_ref)


def _ffn(h2d, g, w_in, w_out):
    n = h2d.shape[0]
    return pl.pallas_call(
        _ffn_kernel,
        grid=(n // ROW_TILE,),
        in_specs=[
            pl.BlockSpec((ROW_TILE, D_MODEL), lambda i: (i, 0)),
            _resident((1, D_MODEL)),
            _resident((D_MODEL, 2 * D_FF)),
            _resident((D_FF, D_MODEL)),
        ],
        out_specs=pl.BlockSpec((ROW_TILE, D_MODEL), lambda i: (i, 0)),
        out_shape=jax.ShapeDtypeStruct((n, D_MODEL), F32),
        compiler_params=pltpu.CompilerParams(
            dimension_semantics=("parallel",), vmem_limit_bytes=VMEM_LIMIT),
        name="ffn",
    )(h2d, g, w_in, w_out)


def _post_kernel(x_ref, ot_ref, wo_ref, g2_ref, w_in_ref, w_out_ref,
                 p_ref, gp_ref, wg_ref, wp_ref, o_ref):
    x = x_ref[...] + lax.dot_general(ot_ref[...], wo_ref[...], (((0,), (0,)), ((), ())),
                                     preferred_element_type=F32)
    x = _swiglu_half_step(x, g2_ref[...], w_in_ref, w_out_ref)
    xn = _rms(x, gp_ref[...]).astype(BF16)
    gate = jax.nn.sigmoid(jnp.dot(xn, wg_ref[...], preferred_element_type=F32))
    pp = jnp.dot(p_ref[...].astype(BF16), wp_ref[...], preferred_element_type=F32)
    o_ref[...] = x + gate * pp


def _post(h2d, ot, w_o, g2, w_in, w_out, p3d, layer, g_ple, w_gate, w_proj):
    n = h2d.shape[0]
    hd, s = ot.shape[1], ot.shape[2]
    tiles = s // ROW_TILE
    rows = pl.BlockSpec((ROW_TILE, D_MODEL), lambda i: (i, 0))
    return pl.pallas_call(
        _post_kernel,
        grid=(n // ROW_TILE,),
        in_specs=[
            rows,
            pl.BlockSpec((None, hd, ROW_TILE), lambda i: (i // tiles, 0, i % tiles)),
            _resident((hd, D_MODEL)),
            _resident((1, D_MODEL)),
            _resident((D_MODEL, 2 * D_FF)),
            _resident((D_FF, D_MODEL)),
            pl.BlockSpec((None, ROW_TILE, PLE_DIM), lambda i: (layer, i, 0)),
            _resident((1, D_MODEL)),
            _resident((D_MODEL, D_MODEL)),
            _resident((PLE_DIM, D_MODEL)),
        ],
        out_specs=rows,
        out_shape=jax.ShapeDtypeStruct((n, D_MODEL), F32),
        compiler_params=pltpu.CompilerParams(
            dimension_semantics=("parallel",), vmem_limit_bytes=VMEM_LIMIT),
        name="post",
    )(h2d, ot, w_o, g2, w_in, w_out, p3d, g_ple, w_gate, w_proj)


def _dot_nt(w, x):
    return lax.dot_general(w, x, (((1,), (1,)), ((), ())), preferred_element_type=F32)


def _ones_row_group(width):
    row = lax.broadcasted_iota(jnp.int32, (BF16_SUBLANES, width), 0)
    return jnp.where(row == 0, 1.0, 0.0).astype(F32)


def _rope_rows(x1, x2, cos, sin):
    return x1 * cos - x2 * sin, x1 * sin + x2 * cos


def _mla_proj_kernel(x_ref, pos_ref, g_ref, w_in_ref, w_pe_ref, gq_lat_ref, gkv_lat_ref,
                     w_uq_ref, w_ukv_ref, gqn_ref, gkn_ref, inv_freq_ref,
                     qt_ref, k_ref, vt_ref):
    tm = x_ref.shape[0]
    xn = _rms(x_ref[...], g_ref[...]).astype(BF16)
    z = jnp.dot(xn, w_in_ref[...], preferred_element_type=F32)
    cq = _rms(z[:, :MLA_Q_RANK], gq_lat_ref[...]).astype(BF16)
    ckv = _rms(z[:, MLA_Q_RANK:], gkv_lat_ref[...]).astype(BF16)
    kpe_t = _dot_nt(w_pe_ref[...], xn)

    ang = pos_ref[...].astype(F32) * inv_freq_ref[...]
    cos = jnp.cos(ang)
    sin = jnp.sin(ang)
    half = MLA_ROPE // 2
    q_scale = (MLA_QK ** -0.5) * LOG2E
    gqn = gqn_ref[...] * q_scale
    gkn = gkn_ref[...]
    kpe_sq = jnp.sum(kpe_t * kpe_t, axis=0, keepdims=True)
    ones_rows = _ones_row_group(tm).astype(BF16)
    zeros_tail = jnp.zeros((QK_PAD - MLA_QK, tm), F32)

    kv_dim = MLA_NOPE + HEAD_V
    for h in range(HEADS):
        if h % HEAD_GROUP == 0:
            g0 = h
            qt_all = _dot_nt(w_uq_ref[g0 * MLA_QK:(g0 + HEAD_GROUP) * MLA_QK, :], cq)
            kvt_all = _dot_nt(w_ukv_ref[g0 * kv_dim:(g0 + HEAD_GROUP) * kv_dim, :], ckv)
        hl = h - g0
        qh = qt_all[hl * MLA_QK:(hl + 1) * MLA_QK]
        qn = qh * lax.rsqrt(jnp.mean(qh * qh, axis=0, keepdims=True) + EPS) * gqn
        q1, q2 = _rope_rows(qn[MLA_NOPE:MLA_NOPE + half], qn[MLA_NOPE + half:], cos, sin)
        qt_ref[h, 0:MLA_NOPE, :] = qn[:MLA_NOPE].astype(BF16)
        qt_ref[h, MLA_NOPE:MLA_NOPE + half, :] = q1.astype(BF16)
        qt_ref[h, MLA_NOPE + half:MLA_QK, :] = q2.astype(BF16)
        qt_ref[h, MLA_QK:QK_PAD, :] = zeros_tail.astype(BF16)

        base = hl * kv_dim
        kn = kvt_all[base:base + MLA_NOPE]
        ms = (jnp.sum(kn * kn, axis=0, keepdims=True) + kpe_sq) * (1.0 / MLA_QK)
        r = lax.rsqrt(ms + EPS)
        kn = kn * r * gkn[:MLA_NOPE]
        kp = kpe_t * r * gkn[MLA_NOPE:]
        k1, k2 = _rope_rows(kp[:half], kp[half:], cos, sin)
        kt = jnp.concatenate([kn, k1, k2, zeros_tail], axis=0)
        k_ref[h] = kt.T.astype(BF16)

        vt_ref[h, 0:HEAD_V, :] = kvt_all[base + MLA_NOPE:base + MLA_NOPE + HEAD_V].astype(BF16)
        vt_ref[h, HEAD_V:V_ROWS, :] = ones_rows


def _qkv_out(b, s):
    specs = [
        pl.BlockSpec((None, HEADS, QK_PAD, PROJ_TILE), lambda bi, i: (bi, 0, 0, i)),
        pl.BlockSpec((None, HEADS, PROJ_TILE, QK_PAD), lambda bi, i: (bi, 0, i, 0)),
        pl.BlockSpec((None, HEADS, V_ROWS, PROJ_TILE), lambda bi, i: (bi, 0, 0, i)),
    ]
    shapes = [
        jax.ShapeDtypeStruct((b, HEADS, QK_PAD, s), BF16),
        jax.ShapeDtypeStruct((b, HEADS, s, QK_PAD), BF16),
        jax.ShapeDtypeStruct((b, HEADS, V_ROWS, s), BF16),
    ]
    return specs, shapes


def _mla_proj(h, pos3, g, w_in, w_pe_t, gq_lat, gkv_lat, w_uq_t, w_ukv_t, gqn, gkn, inv_freq):
    b, s, _ = h.shape
    out_specs, out_shapes = _qkv_out(b, s)
    return pl.pallas_call(
        _mla_proj_kernel,
        grid=(b, s // PROJ_TILE),
        in_specs=[
            pl.BlockSpec((None, PROJ_TILE, D_MODEL), lambda bi, i: (bi, i, 0)),
            pl.BlockSpec((None, 1, PROJ_TILE), lambda bi, i: (bi, 0, i)),
            _resident(g.shape), _resident(w_in.shape), _resident(w_pe_t.shape),
            _resident(gq_lat.shape), _resident(gkv_lat.shape),
            _resident(w_uq_t.shape), _resident(w_ukv_t.shape),
            _resident(gqn.shape), _resident(gkn.shape), _resident(inv_freq.shape),
        ],
        out_specs=out_specs,
        out_shape=out_shapes,
        compiler_params=pltpu.CompilerParams(
            dimension_semantics=("parallel", "parallel"), vmem_limit_bytes=VMEM_LIMIT),
        name="mla_proj",
    )(h, pos3, g, w_in, w_pe_t, gq_lat, gkv_lat, w_uq_t, w_ukv_t, gqn, gkn, inv_freq)


def _split3(c):
    hi = c.astype(BF16).astype(F32)
    mid = (c - hi).astype(BF16).astype(F32)
    lo = (c - hi - mid).astype(BF16).astype(F32)
    return hi, mid, lo


def _fox_proj_kernel(x_ref, g_ref, w_ref, bf_ref, gqn_ref, gkn_ref,
                     qt_ref, k_ref, vt_ref, c_ref, carry_ref):
    tm = x_ref.shape[0]
    hd = HEADS * FOX_HEAD_DIM

    @pl.when(pl.program_id(1) == 0)
    def _():
        carry_ref[...] = jnp.zeros_like(carry_ref)

    xn = _rms(x_ref[...], g_ref[...]).astype(BF16)
    f_t = _dot_nt(w_ref[3 * hd:3 * hd + HEADS, :], xn) + bf_ref[...]

    logf = (jnp.minimum(f_t, 0.0) - jnp.log1p(jnp.exp(-jnp.abs(f_t)))) * LOG2E
    hi, mid, lo = _split3(logf)
    parts = jnp.concatenate([hi, mid, lo], axis=0).astype(BF16)
    src = lax.broadcasted_iota(jnp.int32, (tm, tm), 0)
    dst = lax.broadcasted_iota(jnp.int32, (tm, tm), 1)
    upper = jnp.where(src <= dst, 1.0, 0.0).astype(BF16)
    sums = jnp.dot(parts, upper, preferred_element_type=F32)
    local = sums[2 * HEADS:] + sums[HEADS:2 * HEADS] + sums[:HEADS]
    c = local + carry_ref[:, 0:1]
    carry_ref[...] = jnp.broadcast_to(c[:, tm - 1:tm], carry_ref.shape)
    c_ref[...] = c

    q_scale = (FOX_HEAD_DIM ** -0.5) * LOG2E
    gqn = gqn_ref[...] * q_scale
    gkn = gkn_ref[...]
    row = lax.broadcasted_iota(jnp.int32, (BF16_SUBLANES, tm), 0)
    ones_rows = _ones_row_group(tm).astype(BF16)
    zeros_tail = jnp.zeros((QK_PAD - FOX_HEAD_DIM - BF16_SUBLANES, tm), F32)

    def bias_rows(first, second):
        out = jnp.zeros((BF16_SUBLANES, tm), F32)
        for j in range(3):
            out = jnp.where(row == j, first[j], out)
            out = jnp.where(row == 8 + j, second[j], out)
        return out

    one3 = (1.0, 1.0, 1.0)
    gw = HEAD_GROUP * FOX_HEAD_DIM
    for h in range(HEADS):
        if h % HEAD_GROUP == 0:
            g0 = h
            r0 = g0 * FOX_HEAD_DIM
            qt_all = _dot_nt(w_ref[r0:r0 + gw, :], xn)
            kt_all = _dot_nt(w_ref[hd + r0:hd + r0 + gw, :], xn)
            vt_all = _dot_nt(w_ref[2 * hd + r0:2 * hd + r0 + gw, :], xn)
        sl = slice((h - g0) * FOX_HEAD_DIM, (h - g0 + 1) * FOX_HEAD_DIM)
        c3 = _split3(c[h:h + 1])
        qh = qt_all[sl]
        qn = qh * lax.rsqrt(jnp.mean(qh * qh, axis=0, keepdims=True) + EPS) * gqn
        qt_ref[h, 0:FOX_HEAD_DIM, :] = qn.astype(BF16)
        qt_ref[h, FOX_HEAD_DIM:FOX_HEAD_DIM + BF16_SUBLANES, :] = bias_rows(one3, c3).astype(BF16)
        qt_ref[h, FOX_HEAD_DIM + BF16_SUBLANES:QK_PAD, :] = zeros_tail.astype(BF16)

        kh = kt_all[sl]
        kn = kh * lax.rsqrt(jnp.mean(kh * kh, axis=0, keepdims=True) + EPS) * gkn
        neg3 = tuple(-p for p in c3)
        kt = jnp.concatenate([kn, bias_rows(neg3, one3), zeros_tail], axis=0)
        k_ref[h] = kt.T.astype(BF16)

        vt_ref[h, 0:HEAD_V, :] = vt_all[sl].astype(BF16)
        vt_ref[h, HEAD_V:V_ROWS, :] = ones_rows


def _fox_proj(h, g, w_t, b_f, gqn, gkn):
    b, s, _ = h.shape
    out_specs, out_shapes = _qkv_out(b, s)
    out_specs = out_specs + [pl.BlockSpec((None, HEADS, PROJ_TILE), lambda bi, i: (bi, 0, i))]
    out_shapes = out_shapes + [jax.ShapeDtypeStruct((b, HEADS, s), F32)]
    return pl.pallas_call(
        _fox_proj_kernel,
        grid=(b, s // PROJ_TILE),
        in_specs=[
            pl.BlockSpec((None, PROJ_TILE, D_MODEL), lambda bi, i: (bi, i, 0)),
            _resident(g.shape), _resident(w_t.shape), _resident(b_f.shape),
            _resident(gqn.shape), _resident(gkn.shape),
        ],
        out_specs=out_specs,
        out_shape=out_shapes,
        scratch_shapes=[pltpu.VMEM((HEADS, LANES), F32)],
        compiler_params=pltpu.CompilerParams(
            dimension_semantics=("arbitrary", "arbitrary"), vmem_limit_bytes=VMEM_LIMIT),
        name="fox_proj",
    )(h, g, w_t, b_f, gqn, gkn)


def _attn_kernel(cq_ref, cg_ref, thr_ref, qt_ref, k_ref, vt_ref, o_ref, acc_ref, m_ref, *bufs,
                 seq, tq, tk, cw):
    ncol = tq // cw
    nq = seq // tq
    all_chains = tuple(range(ncol))
    s_refs = bufs[:NSLOT]
    bm_refs = bufs[NSLOT:]
    bias_base = (pl.program_id(0) * pl.num_programs(1) + pl.program_id(1)) * nq

    def first_group(t):
        def dead(g):
            return jnp.logical_and(
                g < t - 1, cq_ref[bias_base + t] - cg_ref[bias_base + g] < -thr_ref[0])
        return lax.while_loop(dead, lambda g: g + 1, jnp.int32(0))

    pv_keys = min(PV_KEYS, tk)
    n_parts = tk // pv_keys
    full_plan = {c: tuple((r, False) for r in range(n_parts)) for c in all_chains}

    nd = tq // tk
    assert nd == NSLOT and MAIN_UNROLL % NSLOT == 0 and LAG < NSLOT

    def diag_plan(d):
        plan = {}
        for c in all_chains:
            parts = []
            for r in range(n_parts):
                key_lo = d * tk + r * pv_keys
                if key_lo <= (c + 1) * cw - 1:
                    parts.append((r, key_lo + pv_keys - 1 > c * cw))
            if parts:
                plan[c] = tuple(parts)
        return plan

    diag_plans = tuple(diag_plan(d) for d in range(nd))

    def stage_a_parts(q0, k0, slot, c, parts):
        cols = slice(c * cw, (c + 1) * cw)
        qt = qt_ref[:, pl.ds(q0 + c * cw, cw)]
        bm = None
        for r, masked in parts:
            rows = slice(r * pv_keys, (r + 1) * pv_keys)
            kb = k_ref[pl.ds(k0 + r * pv_keys, pv_keys), :]
            s = jnp.dot(kb, qt, preferred_element_type=F32)
            if masked:
                kpos = k0 + r * pv_keys + lax.broadcasted_iota(jnp.int32, (pv_keys, cw), 0)
                qpos = q0 + c * cw + lax.broadcasted_iota(jnp.int32, (pv_keys, cw), 1)
                s = jnp.where(kpos <= qpos, s, MASK_VALUE)
            s_refs[slot][c, rows, :] = s
            pm = jnp.max(s, axis=0, keepdims=True)
            bm = pm if bm is None else jnp.maximum(bm, pm)
            if r == parts[-1][0]:
                bm_refs[slot][:, cols] = bm
            yield

    def stage_b_parts(k0, slot, c, parts):
        cols = slice(c * cw, (c + 1) * cw)
        m_old = m_ref[:, cols]
        m_new = jnp.maximum(m_old, bm_refs[slot][:, cols])
        alpha = jnp.exp2(m_old - m_new)
        pv = None
        for r, _ in parts:
            rows = slice(r * pv_keys, (r + 1) * pv_keys)
            p = jnp.exp2(s_refs[slot][c, rows, :] - m_new).astype(BF16)
            vb = vt_ref[:, pl.ds(k0 + r * pv_keys, pv_keys)]
            t = jnp.dot(vb, p, preferred_element_type=F32)
            pv = t if pv is None else pv + t
            if r == parts[-1][0]:
                acc_ref[c] = alpha * acc_ref[c] + pv
                m_ref[:, cols] = m_new
            yield

    def step(q0, k0, rslot, nxt_q0, nxt_k0, a_plan=full_plan, b_plan=full_plan):
        for c in all_chains:
            gens = []
            if a_plan is not None and c in a_plan:
                gens.append(stage_a_parts(nxt_q0, nxt_k0, (rslot + LAG) % NSLOT, c, a_plan[c]))
            if b_plan is not None and c in b_plan:
                gens.append(stage_b_parts(k0, rslot, c, b_plan[c]))
            while gens:
                gens = [g for g in gens if next(g, gens) is not gens]

    def q_tile(qi, carry):
        q0 = pl.multiple_of(qi * tq, tq)
        acc_ref[...] = jnp.zeros_like(acc_ref)
        m_ref[...] = jnp.full_like(m_ref, M_INIT)

        def run_steps(k0, count):
            for j in range(count):
                step(q0, k0 + j * tk, j % NSLOT, q0, k0 + (j + LAG) * tk)

        g_first = first_group(qi)
        k_first = pl.multiple_of(g_first * tq, tq)
        n_main = NSLOT * (jnp.maximum(qi - 1, 0) - g_first)
        rem = n_main & (MAIN_UNROLL - 1)

        def main(t, carry2):
            run_steps(pl.multiple_of(k_first + t * MAIN_UNROLL * tk, tq), MAIN_UNROLL)
            return carry2

        lax.fori_loop(0, lax.shift_right_logical(n_main, MAIN_UNROLL.bit_length() - 1), main, 0)
        k_last = q0 - NSLOT * tk
        sub = MAIN_UNROLL // 2
        while sub >= NSLOT:
            k_sub = pl.multiple_of(k_last - ((rem & (sub - 1)) + sub) * tk, NSLOT * tk)

            @pl.when((rem & sub) != 0)
            def _(k_sub=k_sub, sub=sub):
                run_steps(k_sub, sub)

            sub //= 2

        nxt_qi = jnp.minimum(qi + 1, nq - 1)
        nxt_q0 = pl.multiple_of(nxt_qi * tq, tq)
        nxt_k_first = pl.multiple_of(first_group(nxt_qi) * tq, tq)

        def last_below():
            k0 = pl.multiple_of(k_last, NSLOT * tk)
            for j in range(NSLOT):
                tgt = j + LAG
                if tgt < NSLOT:
                    step(q0, k0 + j * tk, j, q0, k0 + tgt * tk)
                else:
                    step(q0, k0 + j * tk, j, q0, q0 + (tgt - NSLOT) * tk,
                         a_plan=diag_plans[tgt - NSLOT])

        def diagonal():
            for d in range(nd):
                tgt = d + LAG
                if tgt < nd:
                    step(q0, q0 + d * tk, d % NSLOT, q0, q0 + tgt * tk,
                         a_plan=diag_plans[tgt], b_plan=diag_plans[d])
                else:
                    step(q0, q0 + d * tk, d % NSLOT, nxt_q0, nxt_k_first + (tgt - nd) * tk,
                         b_plan=diag_plans[d])

        @pl.when(qi >= 1)
        def _():
            last_below()
            diagonal()

        @pl.when(qi == 0)
        def _():
            diagonal()

        for c in all_chains:
            acc = acc_ref[c]
            o = acc[:HEAD_V] / acc[HEAD_V:HEAD_V + 1]
            o_ref[:, pl.ds(q0 + c * cw, cw)] = o.astype(o_ref.dtype)
        return carry

    for j in range(LAG):
        step(0, 0, (j - LAG) % NSLOT, 0, j * tk, a_plan=diag_plans[j], b_plan=None)
    lax.fori_loop(0, nq, q_tile, 0)


def _attn_fixed_kernel(cq_ref, cg_ref, thr_ref, qt_ref, k_ref, vt_ref, o_ref, acc_ref,
                       *, seq, tq, tk, cw):
    ncol = tq // cw
    nq = seq // tq
    nd = tq // tk
    all_chains = tuple(range(ncol))
    bias_base = (pl.program_id(0) * pl.num_programs(1) + pl.program_id(1)) * nq
    ref_max = thr_ref[1]

    def first_group(t):
        def dead(g):
            return jnp.logical_and(
                g < t - 1, cq_ref[bias_base + t] - cg_ref[bias_base + g] < -thr_ref[0])
        return lax.while_loop(dead, lambda g: g + 1, jnp.int32(0))

    def scores(q0, k0, c, masked):
        qt = qt_ref[:, pl.ds(q0 + c * cw, cw)]
        s = jnp.dot(k_ref[pl.ds(k0, tk), :], qt, preferred_element_type=F32)
        if masked:
            kpos = k0 + lax.broadcasted_iota(jnp.int32, (tk, cw), 0)
            qpos = q0 + c * cw + lax.broadcasted_iota(jnp.int32, (tk, cw), 1)
            s = jnp.where(kpos <= qpos, s, MASK_VALUE)
        return s

    def weighted_values(k0, s):
        p = jnp.exp2(s - ref_max).astype(BF16)
        return jnp.dot(vt_ref[:, pl.ds(k0, tk)], p, preferred_element_type=F32)

    plain = {c: False for c in all_chains}

    def run_list(q0, blocks):
        tot = [None] * ncol
        s_cur = {c: scores(q0, blocks[0][0], c, m) for c, m in blocks[0][1].items()}
        for j, (k0, vis) in enumerate(blocks):
            nxt_k0, nxt_vis = blocks[j + 1] if j + 1 < len(blocks) else (None, {})
            s_nxt = {}
            for c in all_chains:
                if c in nxt_vis:
                    s_nxt[c] = scores(q0, nxt_k0, c, nxt_vis[c])
                if c in vis:
                    t = weighted_values(k0, s_cur[c])
                    tot[c] = t if tot[c] is None else tot[c] + t
            s_cur = s_nxt
        return tot

    def run_plain(q0, k0, count):
        tot = run_list(q0, [(k0 + j * tk, plain) for j in range(count)])
        for c in all_chains:
            acc_ref[c] = acc_ref[c] + tot[c]

    def q_tile(qi, carry):
        q0 = pl.multiple_of(qi * tq, tq)
        acc_ref[...] = jnp.zeros_like(acc_ref)
        g_first = first_group(qi)
        k_first = pl.multiple_of(g_first * tq, tq)
        n_main = nd * (jnp.maximum(qi - 1, 0) - g_first)
        rem = n_main & (MAIN_UNROLL - 1)

        def main(t, carry2):
            run_plain(q0, pl.multiple_of(k_first + t * MAIN_UNROLL * tk, tq), MAIN_UNROLL)
            return carry2

        lax.fori_loop(0, lax.shift_right_logical(n_main, MAIN_UNROLL.bit_length() - 1), main, 0)
        k_last = q0 - nd * tk
        sub = MAIN_UNROLL // 2
        while sub >= nd:
            k_sub = pl.multiple_of(k_last - ((rem & (sub - 1)) + sub) * tk, tq)

            @pl.when((rem & sub) != 0)
            def _(k_sub=k_sub, sub=sub):
                run_plain(q0, k_sub, sub)

            sub //= 2

        diagonal = [(q0 + d * tk, {c: c == d for c in range(d, ncol)}) for d in range(nd)]

        def finish(blocks):
            tot = run_list(q0, blocks)
            for c in all_chains:
                acc = acc_ref[c] + tot[c]
                o = acc[:HEAD_V] / acc[HEAD_V:HEAD_V + 1]
                o_ref[:, pl.ds(q0 + c * cw, cw)] = o.astype(o_ref.dtype)

        @pl.when(qi >= 1)
        def _():
            k0 = pl.multiple_of(k_last, tq)
            finish([(k0 + j * tk, plain) for j in range(nd)] + diagonal)

        @pl.when(qi == 0)
        def _():
            finish(diagonal)

        return carry

    lax.fori_loop(0, nq, q_tile, 0)


def _attention(qt, k, vt, bound, bias=None):
    b, heads, _, s = qt.shape
    tq = min(ATT_TQ, s)
    tk = min(ATT_TK, tq // NSLOT)
    assert tk == ATT_CW
    nq = s // tq
    if bias is None:
        cq = cg = jnp.zeros((b * heads * nq,), F32)
    else:
        cq = bias[:, :, ::tq].reshape(-1)
        cg = bias[:, :, tq - 1::tq].reshape(-1)
    bound = jnp.asarray(bound, F32)
    thr = jnp.stack([2.0 * bound + ZERO_WEIGHT_LOG2, bound])
    in_specs = [
        pl.BlockSpec((None, None, QK_PAD, s), lambda bi, hi, *_: (bi, hi, 0, 0)),
        pl.BlockSpec((None, None, s, QK_PAD), lambda bi, hi, *_: (bi, hi, 0, 0)),
        pl.BlockSpec((None, None, V_ROWS, s), lambda bi, hi, *_: (bi, hi, 0, 0)),
    ]
    out_specs = pl.BlockSpec((None, HEAD_V, s), lambda bi, hi, *_: (bi, hi, 0))
    acc_scratch = pltpu.VMEM((tq // ATT_CW, V_ROWS, ATT_CW), F32)
    common = dict(
        out_shape=jax.ShapeDtypeStruct((b, heads * HEAD_V, s), BF16),
        compiler_params=pltpu.CompilerParams(
            dimension_semantics=("parallel", "parallel"), vmem_limit_bytes=VMEM_LIMIT),
    )

    def online(*args):
        return pl.pallas_call(
            functools.partial(_attn_kernel, seq=s, tq=tq, tk=tk, cw=ATT_CW),
            grid_spec=pltpu.PrefetchScalarGridSpec(
                num_scalar_prefetch=3, grid=(b, heads), in_specs=in_specs, out_specs=out_specs,
                scratch_shapes=([acc_scratch, pltpu.VMEM((1, tq), F32)]
                                + [pltpu.VMEM((tq // ATT_CW, tk, ATT_CW), F32)] * NSLOT
                                + [pltpu.VMEM((1, tq), F32)] * NSLOT),
            ),
            name="attention", **common)(*args)

    def fixed(*args):
        return pl.pallas_call(
            functools.partial(_attn_fixed_kernel, seq=s, tq=tq, tk=tk, cw=ATT_CW),
            grid_spec=pltpu.PrefetchScalarGridSpec(
                num_scalar_prefetch=3, grid=(b, heads), in_specs=in_specs, out_specs=out_specs,
                scratch_shapes=[acc_scratch],
            ),
            name="attention_fixed", **common)(*args)

    if bias is not None:
        return online(cq, cg, thr, qt, k, vt)
    return lax.cond(bound <= FIXED_REF_MAX_BOUND, fixed, online, cq, cg, thr, qt, k, vt)


def _score_bound(g_q, g_k, dk):
    return (jnp.max(jnp.abs(g_q)) * jnp.max(jnp.abs(g_k)) * (dk ** 0.5) * LOG2E
            * BF16_NORM_SLACK).astype(F32)


def kernel(x, p, positions, g_ffn1, g_mix, g_ffn2, g_ple, ffn1_w_in, ffn1_w_out, ffn2_w_in, ffn2_w_out, ple_w_proj, ple_w_gate, mla_w_in, mla_g_q_lat, mla_w_uq, mla_g_kv_lat, mla_w_ukv, mla_g_qn, mla_g_kn, mla_w_o, fox_w_in, fox_b_f, fox_g_qn, fox_g_kn, fox_w_o):
    b, s, d = x.shape
    depth = g_ffn1.shape[0]
    assert d == D_MODEL and s % PROJ_TILE == 0 and s % ROW_TILE == 0
    n = b * s
    bf = lambda w: w.astype(BF16)
    row = lambda g: g.reshape(1, -1)
    col = lambda g: g.reshape(-1, 1)

    pos3 = positions.reshape(b, 1, s)
    half = MLA_ROPE // 2
    inv_freq = (ROPE_THETA ** (-jnp.arange(0, MLA_ROPE, 2, dtype=F32) / MLA_ROPE)).reshape(half, 1)
    lat = MLA_Q_RANK + MLA_KV_RANK

    h = x
    for i in range(depth):
        j = i // N_MIXERS
        h = _ffn(h.reshape(n, d), row(g_ffn1[i]), bf(ffn1_w_in[i]), bf(ffn1_w_out[i])).reshape(b, s, d)
        if i % N_MIXERS == 0:
            qt, k, vt = _mla_proj(
                h, pos3, row(g_mix[i]), bf(mla_w_in[j][:, :lat]), bf(mla_w_in[j][:, lat:].T),
                row(mla_g_q_lat[j]), row(mla_g_kv_lat[j]), bf(mla_w_uq[j].T), bf(mla_w_ukv[j].T),
                col(mla_g_qn[j]), col(mla_g_kn[j]), inv_freq)
            w_o = mla_w_o[j]
            ot = _attention(qt, k, vt, _score_bound(mla_g_qn[j], mla_g_kn[j], MLA_QK))
        else:
            qt, k, vt, c = _fox_proj(h, row(g_mix[i]), bf(fox_w_in[j].T), col(fox_b_f[j]),
                                     col(fox_g_qn[j]), col(fox_g_kn[j]))
            w_o = fox_w_o[j]
            ot = _attention(qt, k, vt, _score_bound(fox_g_qn[j], fox_g_kn[j], FOX_HEAD_DIM), bias=c)
        h = _post(h.reshape(n, d), ot, bf(w_o), row(g_ffn2[i]), bf(ffn2_w_in[i]), bf(ffn2_w_out[i]),
                  p.reshape(depth, n, PLE_DIM), i, row(g_ple[i]), bf(ple_w_gate[i]),
                  bf(ple_w_proj[i])).reshape(b, s, d)
    return h
```

```python
import functools
import math

import jax
import jax.numpy as jnp
from jax import lax
from jax.experimental import pallas as pl
from jax.experimental.pallas import tpu as pltpu

F32 = jnp.float32
BF16 = jnp.bfloat16

D_MODEL = 1024
D_FF = 2816
PLE_DIM = 256
N_MIXERS = 2
FFN_HALF = 0.5
HEADS = 16
MLA_Q_RANK = 512
MLA_KV_RANK = 256
MLA_NOPE = 64
MLA_ROPE = 32
MLA_QK = MLA_NOPE + MLA_ROPE
HEAD_V = 64
FOX_HEAD_DIM = 64
ROPE_THETA = 10000.0
EPS = 1e-6
LOG2E = math.log2(math.e)

LANES = 128
MXU_DIM = 256
BF16_SUBLANES = 16
QK_PAD = 128
V_ROWS = HEAD_V + BF16_SUBLANES
VMEM_LIMIT = 52 * 1024 * 1024

ROW_TILE = 512
PROJ_TILE = 512
HEAD_GROUP = 4
ATT_TQ = 1024
ATT_CW = MXU_DIM
ATT_TK = MXU_DIM
MAIN_UNROLL = 16
NSLOT = 4
LAG = 2
PV_KEYS = MXU_DIM
MASK_VALUE = -jnp.inf
M_INIT = -1e30
ZERO_WEIGHT_LOG2 = 170.0
BF16_NORM_SLACK = 1.02
FIXED_REF_MAX_BOUND = 56.0


def _rms(x, g):
    return x * lax.rsqrt(jnp.mean(x * x, axis=-1, keepdims=True) + EPS) * g


def _resident(shape):
    return pl.BlockSpec(shape, lambda *_: (0,) * len(shape), pipeline_mode=pl.Buffered(1))


def _swiglu_half_step(x, g, w_in_ref, w_out_ref):
    xn = _rms(x, g).astype(BF16)
    gu = jnp.dot(xn, w_in_ref[...], preferred_element_type=F32)
    gate = gu[:, :D_FF]
    up = gu[:, D_FF:]
    act = (gate * jax.nn.sigmoid(gate) * up).astype(BF16)
    return x + FFN_HALF * jnp.dot(act, w_out_ref[...], preferred_element_type=F32)


def _ffn_kernel(x_ref, g_ref, w_in_ref, w_out_ref, o_ref):
    o_ref[...] = _swiglu_half_step(x_ref[...], g_ref[...], w_in_ref, w_out_ref)


def _ffn(h2d, g, w_in, w_out):
    n = h2d.shape[0]
    return pl.pallas_call(
        _ffn_kernel,
        grid=(n // ROW_TILE,),
        in_specs=[
            pl.BlockSpec((ROW_TILE, D_MODEL), lambda i: (i, 0)),
            _resident((1, D_MODEL)),
            _resident((D_MODEL, 2 * D_FF)),
            _resident((D_FF, D_MODEL)),
        ],
        out_specs=pl.BlockSpec((ROW_TILE, D_MODEL), lambda i: (i, 0)),
        out_shape=jax.ShapeDtypeStruct((n, D_MODEL), F32),
        compiler_params=pltpu.CompilerParams(
            dimension_semantics=("parallel",), vmem_limit_bytes=VMEM_LIMIT),
        name="ffn",
    )(h2d, g, w_in, w_out)


def _post_kernel(x_ref, ot_ref, wo_ref, g2_ref, w_in_ref, w_out_ref,
                 p_ref, gp_ref, wg_ref, wp_ref, o_ref):
    x = x_ref[...] + lax.dot_general(ot_ref[...], wo_ref[...], (((0,), (0,)), ((), ())),
                                     preferred_element_type=F32)
    x = _swiglu_half_step(x, g2_ref[...], w_in_ref, w_out_ref)
    xn = _rms(x, gp_ref[...]).astype(BF16)
    gate = jax.nn.sigmoid(jnp.dot(xn, wg_ref[...], preferred_element_type=F32))
    pp = jnp.dot(p_ref[...].astype(BF16), wp_ref[...], preferred_element_type=F32)
    o_ref[...] = x + gate * pp


def _post(h2d, ot, w_o, g2, w_in, w_out, p3d, layer, g_ple, w_gate, w_proj):
    n = h2d.shape[0]
    hd, s = ot.shape[1], ot.shape[2]
    tiles = s // ROW_TILE
    rows = pl.BlockSpec((ROW_TILE, D_MODEL), lambda i: (i, 0))
    return pl.pallas_call(
        _post_kernel,
        grid=(n // ROW_TILE,),
        in_specs=[
            rows,
            pl.BlockSpec((None, hd, ROW_TILE), lambda i: (i // tiles, 0, i % tiles)),
            _resident((hd, D_MODEL)),
            _resident((1, D_MODEL)),
            _resident((D_MODEL, 2 * D_FF)),
            _resident((D_FF, D_MODEL)),
            pl.BlockSpec((None, ROW_TILE, PLE_DIM), lambda i: (layer, i, 0)),
            _resident((1, D_MODEL)),
            _resident((D_MODEL, D_MODEL)),
            _resident((PLE_DIM, D_MODEL)),
        ],
        out_specs=rows,
        out_shape=jax.ShapeDtypeStruct((n, D_MODEL), F32),
        compiler_params=pltpu.CompilerParams(
            dimension_semantics=("parallel",), vmem_limit_bytes=VMEM_LIMIT),
        name="post",
    )(h2d, ot, w_o, g2, w_in, w_out, p3d, g_ple, w_gate, w_proj)


def _dot_nt(w, x):
    return lax.dot_general(w, x, (((1,), (1,)), ((), ())), preferred_element_type=F32)


def _ones_row_group(width):
    row = lax.broadcasted_iota(jnp.int32, (BF16_SUBLANES, width), 0)
    return jnp.where(row == 0, 1.0, 0.0).astype(F32)


def _rope_rows(x1, x2, cos, sin):
    return x1 * cos - x2 * sin, x1 * sin + x2 * cos


def _mla_proj_kernel(x_ref, pos_ref, g_ref, w_in_ref, w_pe_ref, gq_lat_ref, gkv_lat_ref,
                     w_uq_ref, w_ukv_ref, gqn_ref, gkn_ref, inv_freq_ref,
                     qt_ref, k_ref, vt_ref):
    tm = x_ref.shape[0]
    xn = _rms(x_ref[...], g_ref[...]).astype(BF16)
    z = jnp.dot(xn, w_in_ref[...], preferred_element_type=F32)
    cq = _rms(z[:, :MLA_Q_RANK], gq_lat_ref[...]).astype(BF16)
    ckv = _rms(z[:, MLA_Q_RANK:], gkv_lat_ref[...]).astype(BF16)
    kpe_t = _dot_nt(w_pe_ref[...], xn)

    ang = pos_ref[...].astype(F32) * inv_freq_ref[...]
    cos = jnp.cos(ang)
    sin = jnp.sin(ang)
    half = MLA_ROPE // 2
    q_scale = (MLA_QK ** -0.5) * LOG2E
    gqn = gqn_ref[...] * q_scale
    gkn = gkn_ref[...]
    kpe_sq = jnp.sum(kpe_t * kpe_t, axis=0, keepdims=True)
    ones_rows = _ones_row_group(tm).astype(BF16)
    zeros_tail = jnp.zeros((QK_PAD - MLA_QK, tm), F32)

    kv_dim = MLA_NOPE + HEAD_V
    for h in range(HEADS):
        if h % HEAD_GROUP == 0:
            g0 = h
            qt_all = _dot_nt(w_uq_ref[g0 * MLA_QK:(g0 + HEAD_GROUP) * MLA_QK, :], cq)
            kvt_all = _dot_nt(w_ukv_ref[g0 * kv_dim:(g0 + HEAD_GROUP) * kv_dim, :], ckv)
        hl = h - g0
        qh = qt_all[hl * MLA_QK:(hl + 1) * MLA_QK]
        qn = qh * lax.rsqrt(jnp.mean(qh * qh, axis=0, keepdims=True) + EPS) * gqn
        q1, q2 = _rope_rows(qn[MLA_NOPE:MLA_NOPE + half], qn[MLA_NOPE + half:], cos, sin)
        qt_ref[h, 0:MLA_NOPE, :] = qn[:MLA_NOPE].astype(BF16)
        qt_ref[h, MLA_NOPE:MLA_NOPE + half, :] = q1.astype(BF16)
        qt_ref[h, MLA_NOPE + half:MLA_QK, :] = q2.astype(BF16)
        qt_ref[h, MLA_QK:QK_PAD, :] = zeros_tail.astype(BF16)

        base = hl * kv_dim
        kn = kvt_all[base:base + MLA_NOPE]
        ms = (jnp.sum(kn * kn, axis=0, keepdims=True) + kpe_sq) * (1.0 / MLA_QK)
        r = lax.rsqrt(ms + EPS)
        kn = kn * r * gkn[:MLA_NOPE]
        kp = kpe_t * r * gkn[MLA_NOPE:]
        k1, k2 = _rope_rows(kp[:half], kp[half:], cos, sin)
        kt = jnp.concatenate([kn, k1, k2, zeros_tail], axis=0)
        k_ref[h] = kt.T.astype(BF16)

        vt_ref[h, 0:HEAD_V, :] = kvt_all[base + MLA_NOPE:base + MLA_NOPE + HEAD_V].astype(BF16)
        vt_ref[h, HEAD_V:V_ROWS, :] = ones_rows


def _qkv_out(b, s):
    specs = [
        pl.BlockSpec((None, HEADS, QK_PAD, PROJ_TILE), lambda bi, i: (bi, 0, 0, i)),
        pl.BlockSpec((None, HEADS, PROJ_TILE, QK_PAD), lambda bi, i: (bi, 0, i, 0)),
        pl.BlockSpec((None, HEADS, V_ROWS, PROJ_TILE), lambda bi, i: (bi, 0, 0, i)),
    ]
    shapes = [
        jax.ShapeDtypeStruct((b, HEADS, QK_PAD, s), BF16),
        jax.ShapeDtypeStruct((b, HEADS, s, QK_PAD), BF16),
        jax.ShapeDtypeStruct((b, HEADS, V_ROWS, s), BF16),
    ]
    return specs, shapes


def _mla_proj(h, pos3, g, w_in, w_pe_t, gq_lat, gkv_lat, w_uq_t, w_ukv_t, gqn, gkn, inv_freq):
    b, s, _ = h.shape
    out_specs, out_shapes = _qkv_out(b, s)
    return pl.pallas_call(
        _mla_proj_kernel,
        grid=(b, s // PROJ_TILE),
        in_specs=[
            pl.BlockSpec((None, PROJ_TILE, D_MODEL), lambda bi, i: (bi, i, 0)),
            pl.BlockSpec((None, 1, PROJ_TILE), lambda bi, i: (bi, 0, i)),
            _resident(g.shape), _resident(w_in.shape), _resident(w_pe_t.shape),
            _resident(gq_lat.shape), _resident(gkv_lat.shape),
            _resident(w_uq_t.shape), _resident(w_ukv_t.shape),
            _resident(gqn.shape), _resident(gkn.shape), _resident(inv_freq.shape),
        ],
        out_specs=out_specs,
        out_shape=out_shapes,
        compiler_params=pltpu.CompilerParams(
            dimension_semantics=("parallel", "parallel"), vmem_limit_bytes=VMEM_LIMIT),
        name="mla_proj",
    )(h, pos3, g, w_in, w_pe_t, gq_lat, gkv_lat, w_uq_t, w_ukv_t, gqn, gkn, inv_freq)


def _split3(c):
    hi = c.astype(BF16).astype(F32)
    mid = (c - hi).astype(BF16).astype(F32)
    lo = (c - hi - mid).astype(BF16).astype(F32)
    return hi, mid, lo


def _fox_proj_kernel(x_ref, g_ref, w_ref, bf_ref, gqn_ref, gkn_ref,
                     qt_ref, k_ref, vt_ref, c_ref, carry_ref):
    tm = x_ref.shape[0]
    hd = HEADS * FOX_HEAD_DIM

    @pl.when(pl.program_id(1) == 0)
    def _():
        carry_ref[...] = jnp.zeros_like(carry_ref)

    xn = _rms(x_ref[...], g_ref[...]).astype(BF16)
    f_t = _dot_nt(w_ref[3 * hd:3 * hd + HEADS, :], xn) + bf_ref[...]

    logf = (jnp.minimum(f_t, 0.0) - jnp.log1p(jnp.exp(-jnp.abs(f_t)))) * LOG2E
    hi, mid, lo = _split3(logf)
    parts = jnp.concatenate([hi, mid, lo], axis=0).astype(BF16)
    src = lax.broadcasted_iota(jnp.int32, (tm, tm), 0)
    dst = lax.broadcasted_iota(jnp.int32, (tm, tm), 1)
    upper = jnp.where(src <= dst, 1.0, 0.0).astype(BF16)
    sums = jnp.dot(parts, upper, preferred_element_type=F32)
    local = sums[2 * HEADS:] + sums[HEADS:2 * HEADS] + sums[:HEADS]
    c = local + carry_ref[:, 0:1]
    carry_ref[...] = jnp.broadcast_to(c[:, tm - 1:tm], carry_ref.shape)
    c_ref[...] = c

    q_scale = (FOX_HEAD_DIM ** -0.5) * LOG2E
    gqn = gqn_ref[...] * q_scale
    gkn = gkn_ref[...]
    row = lax.broadcasted_iota(jnp.int32, (BF16_SUBLANES, tm), 0)
    ones_rows = _ones_row_group(tm).astype(BF16)
    zeros_tail = jnp.zeros((QK_PAD - FOX_HEAD_DIM - BF16_SUBLANES, tm), F32)

    def bias_rows(first, second):
        out = jnp.zeros((BF16_SUBLANES, tm), F32)
        for j in range(3):
            out = jnp.where(row == j, first[j], out)
            out = jnp.where(row == 8 + j, second[j], out)
        return out

    one3 = (1.0, 1.0, 1.0)
    gw = HEAD_GROUP * FOX_HEAD_DIM
    for h in range(HEADS):
        if h % HEAD_GROUP == 0:
            g0 = h
            r0 = g0 * FOX_HEAD_DIM
            qt_all = _dot_nt(w_ref[r0:r0 + gw, :], xn)
            kt_all = _dot_nt(w_ref[hd + r0:hd + r0 + gw, :], xn)
            vt_all = _dot_nt(w_ref[2 * hd + r0:2 * hd + r0 + gw, :], xn)
        sl = slice((h - g0) * FOX_HEAD_DIM, (h - g0 + 1) * FOX_HEAD_DIM)
        c3 = _split3(c[h:h + 1])
        qh = qt_all[sl]
        qn = qh * lax.rsqrt(jnp.mean(qh * qh, axis=0, keepdims=True) + EPS) * gqn
        qt_ref[h, 0:FOX_HEAD_DIM, :] = qn.astype(BF16)
        qt_ref[h, FOX_HEAD_DIM:FOX_HEAD_DIM + BF16_SUBLANES, :] = bias_rows(one3, c3).astype(BF16)
        qt_ref[h, FOX_HEAD_DIM + BF16_SUBLANES:QK_PAD, :] = zeros_tail.astype(BF16)

        kh = kt_all[sl]
        kn = kh * lax.rsqrt(jnp.mean(kh * kh, axis=0, keepdims=True) + EPS) * gkn
        neg3 = tuple(-p for p in c3)
        kt = jnp.concatenate([kn, bias_rows(neg3, one3), zeros_tail], axis=0)
        k_ref[h] = kt.T.astype(BF16)

        vt_ref[h, 0:HEAD_V, :] = vt_all[sl].astype(BF16)
        vt_ref[h, HEAD_V:V_ROWS, :] = ones_rows


def _fox_proj(h, g, w_t, b_f, gqn, gkn):
    b, s, _ = h.shape
    out_specs, out_shapes = _qkv_out(b, s)
    out_specs = out_specs + [pl.BlockSpec((None, HEADS, PROJ_TILE), lambda bi, i: (bi, 0, i))]
    out_shapes = out_shapes + [jax.ShapeDtypeStruct((b, HEADS, s), F32)]
    return pl.pallas_call(
        _fox_proj_kernel,
        grid=(b, s // PROJ_TILE),
        in_specs=[
            pl.BlockSpec((None, PROJ_TILE, D_MODEL), lambda bi, i: (bi, i, 0)),
            _resident(g.shape), _resident(w_t.shape), _resident(b_f.shape),
            _resident(gqn.shape), _resident(gkn.shape),
        ],
        out_specs=out_specs,
        out_shape=out_shapes,
        scratch_shapes=[pltpu.VMEM((HEADS, LANES), F32)],
        compiler_params=pltpu.CompilerParams(
            dimension_semantics=("arbitrary", "arbitrary"), vmem_limit_bytes=VMEM_LIMIT),
        name="fox_proj",
    )(h, g, w_t, b_f, gqn, gkn)


def _attn_kernel(cq_ref, cg_ref, thr_ref, qt_ref, k_ref, vt_ref, o_ref, acc_ref, m_ref, *bufs,
                 seq, tq, tk, cw):
    ncol = tq // cw
    nq = seq // tq
    all_chains = tuple(range(ncol))
    s_refs = bufs[:NSLOT]
    bm_refs = bufs[NSLOT:]
    bias_base = (pl.program_id(0) * pl.num_programs(1) + pl.program_id(1)) * nq

    def first_group(t):
        def dead(g):
            return jnp.logical_and(
                g < t - 1, cq_ref[bias_base + t] - cg_ref[bias_base + g] < -thr_ref[0])
        return lax.while_loop(dead, lambda g: g + 1, jnp.int32(0))

    pv_keys = min(PV_KEYS, tk)
    n_parts = tk // pv_keys
    full_plan = {c: tuple((r, False) for r in range(n_parts)) for c in all_chains}

    nd = tq // tk
    assert nd == NSLOT and MAIN_UNROLL % NSLOT == 0 and LAG < NSLOT

    def diag_plan(d):
        plan = {}
        for c in all_chains:
            parts = []
            for r in range(n_parts):
                key_lo = d * tk + r * pv_keys
                if key_lo <= (c + 1) * cw - 1:
                    parts.append((r, key_lo + pv_keys - 1 > c * cw))
            if parts:
                plan[c] = tuple(parts)
        return plan

    diag_plans = tuple(diag_plan(d) for d in range(nd))

    def stage_a_parts(q0, k0, slot, c, parts):
        cols = slice(c * cw, (c + 1) * cw)
        qt = qt_ref[:, pl.ds(q0 + c * cw, cw)]
        bm = None
        for r, masked in parts:
            rows = slice(r * pv_keys, (r + 1) * pv_keys)
            kb = k_ref[pl.ds(k0 + r * pv_keys, pv_keys), :]
            s = jnp.dot(kb, qt, preferred_element_type=F32)
            if masked:
                kpos = k0 + r * pv_keys + lax.broadcasted_iota(jnp.int32, (pv_keys, cw), 0)
                qpos = q0 + c * cw + lax.broadcasted_iota(jnp.int32, (pv_keys, cw), 1)
                s = jnp.where(kpos <= qpos, s, MASK_VALUE)
            s_refs[slot][c, rows, :] = s
            pm = jnp.max(s, axis=0, keepdims=True)
            bm = pm if bm is None else jnp.maximum(bm, pm)
            if r == parts[-1][0]:
                bm_refs[slot][:, cols] = bm
            yield

    def stage_b_parts(k0, slot, c, parts):
        cols = slice(c * cw, (c + 1) * cw)
        m_old = m_ref[:, cols]
        m_new = jnp.maximum(m_old, bm_refs[slot][:, cols])
        alpha = jnp.exp2(m_old - m_new)
        pv = None
        for r, _ in parts:
            rows = slice(r * pv_keys, (r + 1) * pv_keys)
            p = jnp.exp2(s_refs[slot][c, rows, :] - m_new).astype(BF16)
            vb = vt_ref[:, pl.ds(k0 + r * pv_keys, pv_keys)]
            t = jnp.dot(vb, p, preferred_element_type=F32)
            pv = t if pv is None else pv + t
            if r == parts[-1][0]:
                acc_ref[c] = alpha * acc_ref[c] + pv
                m_ref[:, cols] = m_new
            yield

    def step(q0, k0, rslot, nxt_q0, nxt_k0, a_plan=full_plan, b_plan=full_plan):
        for c in all_chains:
            gens = []
            if a_plan is not None and c in a_plan:
                gens.append(stage_a_parts(nxt_q0, nxt_k0, (rslot + LAG) % NSLOT, c, a_plan[c]))
            if b_plan is not None and c in b_plan:
                gens.append(stage_b_parts(k0, rslot, c, b_plan[c]))
            while gens:
                gens = [g for g in gens if next(g, gens) is not gens]

    def q_tile(qi, carry):
        q0 = pl.multiple_of(qi * tq, tq)
        acc_ref[...] = jnp.zeros_like(acc_ref)
        m_ref[...] = jnp.full_like(m_ref, M_INIT)

        def run_steps(k0, count):
            for j in range(count):
                step(q0, k0 + j * tk, j % NSLOT, q0, k0 + (j + LAG) * tk)

        g_first = first_group(qi)
        k_first = pl.multiple_of(g_first * tq, tq)
        n_main = NSLOT * (jnp.maximum(qi - 1, 0) - g_first)
        rem = n_main & (MAIN_UNROLL - 1)

        def main(t, carry2):
            run_steps(pl.multiple_of(k_first + t * MAIN_UNROLL * tk, tq), MAIN_UNROLL)
            return carry2

        lax.fori_loop(0, lax.shift_right_logical(n_main, MAIN_UNROLL.bit_length() - 1), main, 0)
        k_last = q0 - NSLOT * tk
        sub = MAIN_UNROLL // 2
        while sub >= NSLOT:
            k_sub = pl.multiple_of(k_last - ((rem & (sub - 1)) + sub) * tk, NSLOT * tk)

            @pl.when((rem & sub) != 0)
            def _(k_sub=k_sub, sub=sub):
                run_steps(k_sub, sub)

            sub //= 2

        nxt_qi = jnp.minimum(qi + 1, nq - 1)
        nxt_q0 = pl.multiple_of(nxt_qi * tq, tq)
        nxt_k_first = pl.multiple_of(first_group(nxt_qi) * tq, tq)

        def last_below():
            k0 = pl.multiple_of(k_last, NSLOT * tk)
            for j in range(NSLOT):
                tgt = j + LAG
                if tgt < NSLOT:
                    step(q0, k0 + j * tk, j, q0, k0 + tgt * tk)
                else:
                    step(q0, k0 + j * tk, j, q0, q0 + (tgt - NSLOT) * tk,
                         a_plan=diag_plans[tgt - NSLOT])

        def diagonal():
            for d in range(nd):
                tgt = d + LAG
                if tgt < nd:
                    step(q0, q0 + d * tk, d % NSLOT, q0, q0 + tgt * tk,
                         a_plan=diag_plans[tgt], b_plan=diag_plans[d])
                else:
                    step(q0, q0 + d * tk, d % NSLOT, nxt_q0, nxt_k_first + (tgt - nd) * tk,
                         b_plan=diag_plans[d])

        @pl.when(qi >= 1)
        def _():
            last_below()
            diagonal()

        @pl.when(qi == 0)
        def _():
            diagonal()

        for c in all_chains:
            acc = acc_ref[c]
            o = acc[:HEAD_V] / acc[HEAD_V:HEAD_V + 1]
            o_ref[:, pl.ds(q0 + c * cw, cw)] = o.astype(o_ref.dtype)
        return carry

    for j in range(LAG):
        step(0, 0, (j - LAG) % NSLOT, 0, j * tk, a_plan=diag_plans[j], b_plan=None)
    lax.fori_loop(0, nq, q_tile, 0)


def _attn_fixed_kernel(cq_ref, cg_ref, thr_ref, qt_ref, k_ref, vt_ref, o_ref, acc_ref,
                       *, seq, tq, tk, cw):
    ncol = tq // cw
    nq = seq // tq
    nd = tq // tk
    all_chains = tuple(range(ncol))
    bias_base = (pl.program_id(0) * pl.num_programs(1) + pl.program_id(1)) * nq
    ref_max = thr_ref[1]

    def first_group(t):
        def dead(g):
            return jnp.logical_and(
                g < t - 1, cq_ref[bias_base + t] - cg_ref[bias_base + g] < -thr_ref[0])
        return lax.while_loop(dead, lambda g: g + 1, jnp.int32(0))

    def scores(q0, k0, c, masked):
        qt = qt_ref[:, pl.ds(q0 + c * cw, cw)]
        s = jnp.dot(k_ref[pl.ds(k0, tk), :], qt, preferred_element_type=F32)
        if masked:
            kpos = k0 + lax.broadcasted_iota(jnp.int32, (tk, cw), 0)
            qpos = q0 + c * cw + lax.broadcasted_iota(jnp.int32, (tk, cw), 1)
            s = jnp.where(kpos <= qpos, s, MASK_VALUE)
        return s

    def weighted_values(k0, s):
        p = jnp.exp2(s - ref_max).astype(BF16)
        return jnp.dot(vt_ref[:, pl.ds(k0, tk)], p, preferred_element_type=F32)

    plain = {c: False for c in all_chains}

    def run_list(q0, blocks):
        tot = [None] * ncol
        s_cur = {c: scores(q0, blocks[0][0], c, m) for c, m in blocks[0][1].items()}
        for j, (k0, vis) in enumerate(blocks):
            nxt_k0, nxt_vis = blocks[j + 1] if j + 1 < len(blocks) else (None, {})
            s_nxt = {}
            for c in all_chains:
                if c in nxt_vis:
                    s_nxt[c] = scores(q0, nxt_k0, c, nxt_vis[c])
                if c in vis:
                    t = weighted_values(k0, s_cur[c])
                    tot[c] = t if tot[c] is None else tot[c] + t
            s_cur = s_nxt
        return tot

    def run_plain(q0, k0, count):
        tot = run_list(q0, [(k0 + j * tk, plain) for j in range(count)])
        for c in all_chains:
            acc_ref[c] = acc_ref[c] + tot[c]

    def q_tile(qi, carry):
        q0 = pl.multiple_of(qi * tq, tq)
        acc_ref[...] = jnp.zeros_like(acc_ref)
        g_first = first_group(qi)
        k_first = pl.multiple_of(g_first * tq, tq)
        n_main = nd * (jnp.maximum(qi - 1, 0) - g_first)
        rem = n_main & (MAIN_UNROLL - 1)

        def main(t, carry2):
            run_plain(q0, pl.multiple_of(k_first + t * MAIN_UNROLL * tk, tq), MAIN_UNROLL)
            return carry2

        lax.fori_loop(0, lax.shift_right_logical(n_main, MAIN_UNROLL.bit_length() - 1), main, 0)
        k_last = q0 - nd * tk
        sub = MAIN_UNROLL // 2
        while sub >= 2 * nd:
            k_sub = pl.multiple_of(k_last - ((rem & (sub - 1)) + sub) * tk, tq)

            @pl.when((rem & sub) != 0)
            def _(k_sub=k_sub, sub=sub):
                run_plain(q0, k_sub, sub)

            sub //= 2

        diagonal = [(q0 + d * tk, {c: c == d for c in range(d, ncol)}) for d in range(nd)]

        def finish(groups_below):
            k0 = pl.multiple_of(q0 - groups_below * nd * tk, tq)
            below = [(k0 + j * tk, plain) for j in range(groups_below * nd)]
            tot = run_list(q0, below + diagonal)
            for c in all_chains:
                acc = acc_ref[c] + tot[c]
                o = acc[:HEAD_V] / acc[HEAD_V:HEAD_V + 1]
                o_ref[:, pl.ds(q0 + c * cw, cw)] = o.astype(o_ref.dtype)

        two_groups = (rem & nd) != 0
        for groups_below, when in ((2, jnp.logical_and(qi >= 1, two_groups)),
                                   (1, jnp.logical_and(qi >= 1, jnp.logical_not(two_groups))),
                                   (0, qi == 0)):
            pl.when(when)(functools.partial(finish, groups_below))

        return carry

    lax.fori_loop(0, nq, q_tile, 0)


def _attention(qt, k, vt, bound, bias=None):
    b, heads, _, s = qt.shape
    tq = min(ATT_TQ, s)
    tk = min(ATT_TK, tq // NSLOT)
    assert tk == ATT_CW
    nq = s // tq
    if bias is None:
        cq = cg = jnp.zeros((b * heads * nq,), F32)
    else:
        cq = bias[:, :, ::tq].reshape(-1)
        cg = bias[:, :, tq - 1::tq].reshape(-1)
    bound = jnp.asarray(bound, F32)
    thr = jnp.stack([2.0 * bound + ZERO_WEIGHT_LOG2, bound])
    in_specs = [
        pl.BlockSpec((None, None, QK_PAD, s), lambda bi, hi, *_: (bi, hi, 0, 0)),
        pl.BlockSpec((None, None, s, QK_PAD), lambda bi, hi, *_: (bi, hi, 0, 0)),
        pl.BlockSpec((None, None, V_ROWS, s), lambda bi, hi, *_: (bi, hi, 0, 0)),
    ]
    out_specs = pl.BlockSpec((None, HEAD_V, s), lambda bi, hi, *_: (bi, hi, 0))
    acc_scratch = pltpu.VMEM((tq // ATT_CW, V_ROWS, ATT_CW), F32)
    common = dict(
        out_shape=jax.ShapeDtypeStruct((b, heads * HEAD_V, s), BF16),
        compiler_params=pltpu.CompilerParams(
            dimension_semantics=("parallel", "parallel"), vmem_limit_bytes=VMEM_LIMIT),
    )

    def online(*args):
        return pl.pallas_call(
            functools.partial(_attn_kernel, seq=s, tq=tq, tk=tk, cw=ATT_CW),
            grid_spec=pltpu.PrefetchScalarGridSpec(
                num_scalar_prefetch=3, grid=(b, heads), in_specs=in_specs, out_specs=out_specs,
                scratch_shapes=([acc_scratch, pltpu.VMEM((1, tq), F32)]
                                + [pltpu.VMEM((tq // ATT_CW, tk, ATT_CW), F32)] * NSLOT
                                + [pltpu.VMEM((1, tq), F32)] * NSLOT),
            ),
            name="attention", **common)(*args)

    def fixed(*args):
        return pl.pallas_call(
            functools.partial(_attn_fixed_kernel, seq=s, tq=tq, tk=tk, cw=ATT_CW),
            grid_spec=pltpu.PrefetchScalarGridSpec(
                num_scalar_prefetch=3, grid=(b, heads), in_specs=in_specs, out_specs=out_specs,
                scratch_shapes=[acc_scratch],
            ),
            name="attention_fixed", **common)(*args)

    if bias is not None:
        return online(cq, cg, thr, qt, k, vt)
    return lax.cond(bound <= FIXED_REF_MAX_BOUND, fixed, online, cq, cg, thr, qt, k, vt)


def _score_bound(g_q, g_k, dk):
    return (jnp.max(jnp.abs(g_q)) * jnp.max(jnp.abs(g_k)) * (dk ** 0.5) * LOG2E
            * BF16_NORM_SLACK).astype(F32)


def kernel(x, p, positions, g_ffn1, g_mix, g_ffn2, g_ple, ffn1_w_in, ffn1_w_out, ffn2_w_in, ffn2_w_out, ple_w_proj, ple_w_gate, mla_w_in, mla_g_q_lat, mla_w_uq, mla_g_kv_lat, mla_w_ukv, mla_g_qn, mla_g_kn, mla_w_o, fox_w_in, fox_b_f, fox_g_qn, fox_g_kn, fox_w_o):
    b, s, d = x.shape
    depth = g_ffn1.shape[0]
    assert d == D_MODEL and s % PROJ_TILE == 0 and s % ROW_TILE == 0
    n = b * s
    bf = lambda w: w.astype(BF16)
    row = lambda g: g.reshape(1, -1)
    col = lambda g: g.reshape(-1, 1)

    pos3 = positions.reshape(b, 1, s)
    half = MLA_ROPE // 2
    inv_freq = (ROPE_THETA ** (-jnp.arange(0, MLA_ROPE, 2, dtype=F32) / MLA_ROPE)).reshape(half, 1)
    lat = MLA_Q_RANK + MLA_KV_RANK

    h = x
    for i in range(depth):
        j = i // N_MIXERS
        h = _ffn(h.reshape(n, d), row(g_ffn1[i]), bf(ffn1_w_in[i]), bf(ffn1_w_out[i])).reshape(b, s, d)
        if i % N_MIXERS == 0:
            qt, k, vt = _mla_proj(
                h, pos3, row(g_mix[i]), bf(mla_w_in[j][:, :lat]), bf(mla_w_in[j][:, lat:].T),
                row(mla_g_q_lat[j]), row(mla_g_kv_lat[j]), bf(mla_w_uq[j].T), bf(mla_w_ukv[j].T),
                col(mla_g_qn[j]), col(mla_g_kn[j]), inv_freq)
            w_o = mla_w_o[j]
            ot = _attention(qt, k, vt, _score_bound(mla_g_qn[j], mla_g_kn[j], MLA_QK))
        else:
            qt, k, vt, c = _fox_proj(h, row(g_mix[i]), bf(fox_w_in[j].T), col(fox_b_f[j]),
                                     col(fox_g_qn[j]), col(fox_g_kn[j]))
            w_o = fox_w_o[j]
            ot = _attention(qt, k, vt, _score_bound(fox_g_qn[j], fox_g_kn[j], FOX_HEAD_DIM), bias=c)
        h = _post(h.reshape(n, d), ot, bf(w_o), row(g_ffn2[i]), bf(ffn2_w_in[i]), bf(ffn2_w_out[i]),
                  p.reshape(depth, n, PLE_DIM), i, row(g_ple[i]), bf(ple_w_gate[i]),
                  bf(ple_w_proj[i])).reshape(b, s, d)
    return h
```

```python
import functools
import math

import jax
import jax.numpy as jnp
from jax import lax
from jax.experimental import pallas as pl
from jax.experimental.pallas import tpu as pltpu

F32 = jnp.float32
BF16 = jnp.bfloat16

D_MODEL = 1024
D_FF = 2816
PLE_DIM = 256
N_MIXERS = 2
FFN_HALF = 0.5
HEADS = 16
MLA_Q_RANK = 512
MLA_KV_RANK = 256
MLA_NOPE = 64
MLA_ROPE = 32
MLA_QK = MLA_NOPE + MLA_ROPE
HEAD_V = 64
FOX_HEAD_DIM = 64
ROPE_THETA = 10000.0
EPS = 1e-6
LOG2E = math.log2(math.e)

LANES = 128
MXU_DIM = 256
BF16_SUBLANES = 16
QK_PAD = 128
V_ROWS = HEAD_V + BF16_SUBLANES
VMEM_LIMIT = 52 * 1024 * 1024

ROW_TILE = 512
PROJ_TILE = 512
HEAD_GROUP = 4
ATT_TQ = 1024
ATT_CW = MXU_DIM
ATT_TK = MXU_DIM
MAIN_UNROLL = 16
NSLOT = 4
LAG = 2
SCORES_AHEAD = 2
PV_KEYS = MXU_DIM
MASK_VALUE = -jnp.inf
M_INIT = -1e30
ZERO_WEIGHT_LOG2 = 170.0
BF16_NORM_SLACK = 1.02
FIXED_REF_MAX_BOUND = 56.0


def _rms(x, g):
    return x * lax.rsqrt(jnp.mean(x * x, axis=-1, keepdims=True) + EPS) * g


def _resident(shape):
    return pl.BlockSpec(shape, lambda *_: (0,) * len(shape), pipeline_mode=pl.Buffered(1))


def _swiglu_half_step(x, g, w_in_ref, w_out_ref):
    xn = _rms(x, g).astype(BF16)
    gu = jnp.dot(xn, w_in_ref[...], preferred_element_type=F32)
    gate = gu[:, :D_FF]
    up = gu[:, D_FF:]
    act = (gate * jax.nn.sigmoid(gate) * up).astype(BF16)
    return x + FFN_HALF * jnp.dot(act, w_out_ref[...], preferred_element_type=F32)


def _ffn_kernel(x_ref, g_ref, w_in_ref, w_out_ref, o_ref):
    o_ref[...] = _swiglu_half_step(x_ref[...], g_ref[...], w_in_ref, w_out_ref)


def _ffn(h2d, g, w_in, w_out):
    n = h2d.shape[0]
    return pl.pallas_call(
        _ffn_kernel,
        grid=(n // ROW_TILE,),
        in_specs=[
            pl.BlockSpec((ROW_TILE, D_MODEL), lambda i: (i, 0)),
            _resident((1, D_MODEL)),
            _resident((D_MODEL, 2 * D_FF)),
            _resident((D_FF, D_MODEL)),
        ],
        out_specs=pl.BlockSpec((ROW_TILE, D_MODEL), lambda i: (i, 0)),
        out_shape=jax.ShapeDtypeStruct((n, D_MODEL), F32),
        compiler_params=pltpu.CompilerParams(
            dimension_semantics=("parallel",), vmem_limit_bytes=VMEM_LIMIT),
        name="ffn",
    )(h2d, g, w_in, w_out)


def _post_kernel(x_ref, ot_ref, wo_ref, g2_ref, w_in_ref, w_out_ref,
                 p_ref, gp_ref, wg_ref, wp_ref, o_ref):
    x = x_ref[...] + lax.dot_general(ot_ref[...], wo_ref[...], (((0,), (0,)), ((), ())),
                                     preferred_element_type=F32)
    x = _swiglu_half_step(x, g2_ref[...], w_in_ref, w_out_ref)
    xn = _rms(x, gp_ref[...]).astype(BF16)
    gate = jax.nn.sigmoid(jnp.dot(xn, wg_ref[...], preferred_element_type=F32))
    pp = jnp.dot(p_ref[...].astype(BF16), wp_ref[...], preferred_element_type=F32)
    o_ref[...] = x + gate * pp


def _post(h2d, ot, w_o, g2, w_in, w_out, p3d, layer, g_ple, w_gate, w_proj):
    n = h2d.shape[0]
    hd, s = ot.shape[1], ot.shape[2]
    tiles = s // ROW_TILE
    rows = pl.BlockSpec((ROW_TILE, D_MODEL), lambda i: (i, 0))
    return pl.pallas_call(
        _post_kernel,
        grid=(n // ROW_TILE,),
        in_specs=[
            rows,
            pl.BlockSpec((None, hd, ROW_TILE), lambda i: (i // tiles, 0, i % tiles)),
            _resident((hd, D_MODEL)),
            _resident((1, D_MODEL)),
            _resident((D_MODEL, 2 * D_FF)),
            _resident((D_FF, D_MODEL)),
            pl.BlockSpec((None, ROW_TILE, PLE_DIM), lambda i: (layer, i, 0)),
            _resident((1, D_MODEL)),
            _resident((D_MODEL, D_MODEL)),
            _resident((PLE_DIM, D_MODEL)),
        ],
        out_specs=rows,
        out_shape=jax.ShapeDtypeStruct((n, D_MODEL), F32),
        compiler_params=pltpu.CompilerParams(
            dimension_semantics=("parallel",), vmem_limit_bytes=VMEM_LIMIT),
        name="post",
    )(h2d, ot, w_o, g2, w_in, w_out, p3d, g_ple, w_gate, w_proj)


def _dot_nt(w, x):
    return lax.dot_general(w, x, (((1,), (1,)), ((), ())), preferred_element_type=F32)


def _ones_row_group(width):
    row = lax.broadcasted_iota(jnp.int32, (BF16_SUBLANES, width), 0)
    return jnp.where(row == 0, 1.0, 0.0).astype(F32)


def _rope_rows(x1, x2, cos, sin):
    return x1 * cos - x2 * sin, x1 * sin + x2 * cos


def _mla_proj_kernel(x_ref, pos_ref, g_ref, w_in_ref, w_pe_ref, gq_lat_ref, gkv_lat_ref,
                     w_uq_ref, w_ukv_ref, gqn_ref, gkn_ref, inv_freq_ref,
                     qt_ref, k_ref, vt_ref):
    tm = x_ref.shape[0]
    xn = _rms(x_ref[...], g_ref[...]).astype(BF16)
    z = jnp.dot(xn, w_in_ref[...], preferred_element_type=F32)
    cq = _rms(z[:, :MLA_Q_RANK], gq_lat_ref[...]).astype(BF16)
    ckv = _rms(z[:, MLA_Q_RANK:], gkv_lat_ref[...]).astype(BF16)
    kpe_t = _dot_nt(w_pe_ref[...], xn)

    ang = pos_ref[...].astype(F32) * inv_freq_ref[...]
    cos = jnp.cos(ang)
    sin = jnp.sin(ang)
    half = MLA_ROPE // 2
    q_scale = (MLA_QK ** -0.5) * LOG2E
    gqn = gqn_ref[...] * q_scale
    gkn = gkn_ref[...]
    kpe_sq = jnp.sum(kpe_t * kpe_t, axis=0, keepdims=True)
    ones_rows = _ones_row_group(tm).astype(BF16)
    zeros_tail = jnp.zeros((QK_PAD - MLA_QK, tm), F32)

    kv_dim = MLA_NOPE + HEAD_V
    for h in range(HEADS):
        if h % HEAD_GROUP == 0:
            g0 = h
            qt_all = _dot_nt(w_uq_ref[g0 * MLA_QK:(g0 + HEAD_GROUP) * MLA_QK, :], cq)
            kvt_all = _dot_nt(w_ukv_ref[g0 * kv_dim:(g0 + HEAD_GROUP) * kv_dim, :], ckv)
        hl = h - g0
        qh = qt_all[hl * MLA_QK:(hl + 1) * MLA_QK]
        qn = qh * lax.rsqrt(jnp.mean(qh * qh, axis=0, keepdims=True) + EPS) * gqn
        q1, q2 = _rope_rows(qn[MLA_NOPE:MLA_NOPE + half], qn[MLA_NOPE + half:], cos, sin)
        qt_ref[h, 0:MLA_NOPE, :] = qn[:MLA_NOPE].astype(BF16)
        qt_ref[h, MLA_NOPE:MLA_NOPE + half, :] = q1.astype(BF16)
        qt_ref[h, MLA_NOPE + half:MLA_QK, :] = q2.astype(BF16)
        qt_ref[h, MLA_QK:QK_PAD, :] = zeros_tail.astype(BF16)

        base = hl * kv_dim
        kn = kvt_all[base:base + MLA_NOPE]
        ms = (jnp.sum(kn * kn, axis=0, keepdims=True) + kpe_sq) * (1.0 / MLA_QK)
        r = lax.rsqrt(ms + EPS)
        kn = kn * r * gkn[:MLA_NOPE]
        kp = kpe_t * r * gkn[MLA_NOPE:]
        k1, k2 = _rope_rows(kp[:half], kp[half:], cos, sin)
        kt = jnp.concatenate([kn, k1, k2, zeros_tail], axis=0)
        k_ref[h] = kt.T.astype(BF16)

        vt_ref[h, 0:HEAD_V, :] = kvt_all[base + MLA_NOPE:base + MLA_NOPE + HEAD_V].astype(BF16)
        vt_ref[h, HEAD_V:V_ROWS, :] = ones_rows


def _qkv_out(b, s):
    specs = [
        pl.BlockSpec((None, HEADS, QK_PAD, PROJ_TILE), lambda bi, i: (bi, 0, 0, i)),
        pl.BlockSpec((None, HEADS, PROJ_TILE, QK_PAD), lambda bi, i: (bi, 0, i, 0)),
        pl.BlockSpec((None, HEADS, V_ROWS, PROJ_TILE), lambda bi, i: (bi, 0, 0, i)),
    ]
    shapes = [
        jax.ShapeDtypeStruct((b, HEADS, QK_PAD, s), BF16),
        jax.ShapeDtypeStruct((b, HEADS, s, QK_PAD), BF16),
        jax.ShapeDtypeStruct((b, HEADS, V_ROWS, s), BF16),
    ]
    return specs, shapes


def _mla_proj(h, pos3, g, w_in, w_pe_t, gq_lat, gkv_lat, w_uq_t, w_ukv_t, gqn, gkn, inv_freq):
    b, s, _ = h.shape
    out_specs, out_shapes = _qkv_out(b, s)
    return pl.pallas_call(
        _mla_proj_kernel,
        grid=(b, s // PROJ_TILE),
        in_specs=[
            pl.BlockSpec((None, PROJ_TILE, D_MODEL), lambda bi, i: (bi, i, 0)),
            pl.BlockSpec((None, 1, PROJ_TILE), lambda bi, i: (bi, 0, i)),
            _resident(g.shape), _resident(w_in.shape), _resident(w_pe_t.shape),
            _resident(gq_lat.shape), _resident(gkv_lat.shape),
            _resident(w_uq_t.shape), _resident(w_ukv_t.shape),
            _resident(gqn.shape), _resident(gkn.shape), _resident(inv_freq.shape),
        ],
        out_specs=out_specs,
        out_shape=out_shapes,
        compiler_params=pltpu.CompilerParams(
            dimension_semantics=("parallel", "parallel"), vmem_limit_bytes=VMEM_LIMIT),
        name="mla_proj",
    )(h, pos3, g, w_in, w_pe_t, gq_lat, gkv_lat, w_uq_t, w_ukv_t, gqn, gkn, inv_freq)


def _split3(c):
    hi = c.astype(BF16).astype(F32)
    mid = (c - hi).astype(BF16).astype(F32)
    lo = (c - hi - mid).astype(BF16).astype(F32)
    return hi, mid, lo


def _fox_proj_kernel(x_ref, g_ref, w_ref, bf_ref, gqn_ref, gkn_ref,
                     qt_ref, k_ref, vt_ref, c_ref, carry_ref):
    tm = x_ref.shape[0]
    hd = HEADS * FOX_HEAD_DIM

    @pl.when(pl.program_id(1) == 0)
    def _():
        carry_ref[...] = jnp.zeros_like(carry_ref)

    xn = _rms(x_ref[...], g_ref[...]).astype(BF16)
    f_t = _dot_nt(w_ref[3 * hd:3 * hd + HEADS, :], xn) + bf_ref[...]

    logf = (jnp.minimum(f_t, 0.0) - jnp.log1p(jnp.exp(-jnp.abs(f_t)))) * LOG2E
    hi, mid, lo = _split3(logf)
    parts = jnp.concatenate([hi, mid, lo], axis=0).astype(BF16)
    src = lax.broadcasted_iota(jnp.int32, (tm, tm), 0)
    dst = lax.broadcasted_iota(jnp.int32, (tm, tm), 1)
    upper = jnp.where(src <= dst, 1.0, 0.0).astype(BF16)
    sums = jnp.dot(parts, upper, preferred_element_type=F32)
    local = sums[2 * HEADS:] + sums[HEADS:2 * HEADS] + sums[:HEADS]
    c = local + carry_ref[:, 0:1]
    carry_ref[...] = jnp.broadcast_to(c[:, tm - 1:tm], carry_ref.shape)
    c_ref[...] = c

    q_scale = (FOX_HEAD_DIM ** -0.5) * LOG2E
    gqn = gqn_ref[...] * q_scale
    gkn = gkn_ref[...]
    row = lax.broadcasted_iota(jnp.int32, (BF16_SUBLANES, tm), 0)
    ones_rows = _ones_row_group(tm).astype(BF16)
    zeros_tail = jnp.zeros((QK_PAD - FOX_HEAD_DIM - BF16_SUBLANES, tm), F32)

    def bias_rows(first, second):
        out = jnp.zeros((BF16_SUBLANES, tm), F32)
        for j in range(3):
            out = jnp.where(row == j, first[j], out)
            out = jnp.where(row == 8 + j, second[j], out)
        return out

    one3 = (1.0, 1.0, 1.0)
    gw = HEAD_GROUP * FOX_HEAD_DIM
    for h in range(HEADS):
        if h % HEAD_GROUP == 0:
            g0 = h
            r0 = g0 * FOX_HEAD_DIM
            qt_all = _dot_nt(w_ref[r0:r0 + gw, :], xn)
            kt_all = _dot_nt(w_ref[hd + r0:hd + r0 + gw, :], xn)
            vt_all = _dot_nt(w_ref[2 * hd + r0:2 * hd + r0 + gw, :], xn)
        sl = slice((h - g0) * FOX_HEAD_DIM, (h - g0 + 1) * FOX_HEAD_DIM)
        c3 = _split3(c[h:h + 1])
        qh = qt_all[sl]
        qn = qh * lax.rsqrt(jnp.mean(qh * qh, axis=0, keepdims=True) + EPS) * gqn
        qt_ref[h, 0:FOX_HEAD_DIM, :] = qn.astype(BF16)
        qt_ref[h, FOX_HEAD_DIM:FOX_HEAD_DIM + BF16_SUBLANES, :] = bias_rows(one3, c3).astype(BF16)
        qt_ref[h, FOX_HEAD_DIM + BF16_SUBLANES:QK_PAD, :] = zeros_tail.astype(BF16)

        kh = kt_all[sl]
        kn = kh * lax.rsqrt(jnp.mean(kh * kh, axis=0, keepdims=True) + EPS) * gkn
        neg3 = tuple(-p for p in c3)
        kt = jnp.concatenate([kn, bias_rows(neg3, one3), zeros_tail], axis=0)
        k_ref[h] = kt.T.astype(BF16)

        vt_ref[h, 0:HEAD_V, :] = vt_all[sl].astype(BF16)
        vt_ref[h, HEAD_V:V_ROWS, :] = ones_rows


def _fox_proj(h, g, w_t, b_f, gqn, gkn):
    b, s, _ = h.shape
    out_specs, out_shapes = _qkv_out(b, s)
    out_specs = out_specs + [pl.BlockSpec((None, HEADS, PROJ_TILE), lambda bi, i: (bi, 0, i))]
    out_shapes = out_shapes + [jax.ShapeDtypeStruct((b, HEADS, s), F32)]
    return pl.pallas_call(
        _fox_proj_kernel,
        grid=(b, s // PROJ_TILE),
        in_specs=[
            pl.BlockSpec((None, PROJ_TILE, D_MODEL), lambda bi, i: (bi, i, 0)),
            _resident(g.shape), _resident(w_t.shape), _resident(b_f.shape),
            _resident(gqn.shape), _resident(gkn.shape),
        ],
        out_specs=out_specs,
        out_shape=out_shapes,
        scratch_shapes=[pltpu.VMEM((HEADS, LANES), F32)],
        compiler_params=pltpu.CompilerParams(
            dimension_semantics=("arbitrary", "arbitrary"), vmem_limit_bytes=VMEM_LIMIT),
        name="fox_proj",
    )(h, g, w_t, b_f, gqn, gkn)


def _attn_kernel(cq_ref, cg_ref, thr_ref, qt_ref, k_ref, vt_ref, o_ref, acc_ref, m_ref, *bufs,
                 seq, tq, tk, cw):
    ncol = tq // cw
    nq = seq // tq
    all_chains = tuple(range(ncol))
    s_refs = bufs[:NSLOT]
    bm_refs = bufs[NSLOT:]
    bias_base = (pl.program_id(0) * pl.num_programs(1) + pl.program_id(1)) * nq

    def first_group(t):
        def dead(g):
            return jnp.logical_and(
                g < t - 1, cq_ref[bias_base + t] - cg_ref[bias_base + g] < -thr_ref[0])
        return lax.while_loop(dead, lambda g: g + 1, jnp.int32(0))

    pv_keys = min(PV_KEYS, tk)
    n_parts = tk // pv_keys
    full_plan = {c: tuple((r, False) for r in range(n_parts)) for c in all_chains}

    nd = tq // tk
    assert nd == NSLOT and MAIN_UNROLL % NSLOT == 0 and LAG < NSLOT

    def diag_plan(d):
        plan = {}
        for c in all_chains:
            parts = []
            for r in range(n_parts):
                key_lo = d * tk + r * pv_keys
                if key_lo <= (c + 1) * cw - 1:
                    parts.append((r, key_lo + pv_keys - 1 > c * cw))
            if parts:
                plan[c] = tuple(parts)
        return plan

    diag_plans = tuple(diag_plan(d) for d in range(nd))

    def stage_a_parts(q0, k0, slot, c, parts):
        cols = slice(c * cw, (c + 1) * cw)
        qt = qt_ref[:, pl.ds(q0 + c * cw, cw)]
        bm = None
        for r, masked in parts:
            rows = slice(r * pv_keys, (r + 1) * pv_keys)
            kb = k_ref[pl.ds(k0 + r * pv_keys, pv_keys), :]
            s = jnp.dot(kb, qt, preferred_element_type=F32)
            if masked:
                kpos = k0 + r * pv_keys + lax.broadcasted_iota(jnp.int32, (pv_keys, cw), 0)
                qpos = q0 + c * cw + lax.broadcasted_iota(jnp.int32, (pv_keys, cw), 1)
                s = jnp.where(kpos <= qpos, s, MASK_VALUE)
            s_refs[slot][c, rows, :] = s
            pm = jnp.max(s, axis=0, keepdims=True)
            bm = pm if bm is None else jnp.maximum(bm, pm)
            if r == parts[-1][0]:
                bm_refs[slot][:, cols] = bm
            yield

    def stage_b_parts(k0, slot, c, parts):
        cols = slice(c * cw, (c + 1) * cw)
        m_old = m_ref[:, cols]
        m_new = jnp.maximum(m_old, bm_refs[slot][:, cols])
        alpha = jnp.exp2(m_old - m_new)
        pv = None
        for r, _ in parts:
            rows = slice(r * pv_keys, (r + 1) * pv_keys)
            p = jnp.exp2(s_refs[slot][c, rows, :] - m_new).astype(BF16)
            vb = vt_ref[:, pl.ds(k0 + r * pv_keys, pv_keys)]
            t = jnp.dot(vb, p, preferred_element_type=F32)
            pv = t if pv is None else pv + t
            if r == parts[-1][0]:
                acc_ref[c] = alpha * acc_ref[c] + pv
                m_ref[:, cols] = m_new
            yield

    def step(q0, k0, rslot, nxt_q0, nxt_k0, a_plan=full_plan, b_plan=full_plan):
        for c in all_chains:
            gens = []
            if a_plan is not None and c in a_plan:
                gens.append(stage_a_parts(nxt_q0, nxt_k0, (rslot + LAG) % NSLOT, c, a_plan[c]))
            if b_plan is not None and c in b_plan:
                gens.append(stage_b_parts(k0, rslot, c, b_plan[c]))
            while gens:
                gens = [g for g in gens if next(g, gens) is not gens]

    def q_tile(qi, carry):
        q0 = pl.multiple_of(qi * tq, tq)
        acc_ref[...] = jnp.zeros_like(acc_ref)
        m_ref[...] = jnp.full_like(m_ref, M_INIT)

        def run_steps(k0, count):
            for j in range(count):
                step(q0, k0 + j * tk, j % NSLOT, q0, k0 + (j + LAG) * tk)

        g_first = first_group(qi)
        k_first = pl.multiple_of(g_first * tq, tq)
        n_main = NSLOT * (jnp.maximum(qi - 1, 0) - g_first)
        rem = n_main & (MAIN_UNROLL - 1)

        def main(t, carry2):
            run_steps(pl.multiple_of(k_first + t * MAIN_UNROLL * tk, tq), MAIN_UNROLL)
            return carry2

        lax.fori_loop(0, lax.shift_right_logical(n_main, MAIN_UNROLL.bit_length() - 1), main, 0)
        k_last = q0 - NSLOT * tk
        sub = MAIN_UNROLL // 2
        while sub >= NSLOT:
            k_sub = pl.multiple_of(k_last - ((rem & (sub - 1)) + sub) * tk, NSLOT * tk)

            @pl.when((rem & sub) != 0)
            def _(k_sub=k_sub, sub=sub):
                run_steps(k_sub, sub)

            sub //= 2

        nxt_qi = jnp.minimum(qi + 1, nq - 1)
        nxt_q0 = pl.multiple_of(nxt_qi * tq, tq)
        nxt_k_first = pl.multiple_of(first_group(nxt_qi) * tq, tq)

        def last_below():
            k0 = pl.multiple_of(k_last, NSLOT * tk)
            for j in range(NSLOT):
                tgt = j + LAG
                if tgt < NSLOT:
                    step(q0, k0 + j * tk, j, q0, k0 + tgt * tk)
                else:
                    step(q0, k0 + j * tk, j, q0, q0 + (tgt - NSLOT) * tk,
                         a_plan=diag_plans[tgt - NSLOT])

        def diagonal():
            for d in range(nd):
                tgt = d + LAG
                if tgt < nd:
                    step(q0, q0 + d * tk, d % NSLOT, q0, q0 + tgt * tk,
                         a_plan=diag_plans[tgt], b_plan=diag_plans[d])
                else:
                    step(q0, q0 + d * tk, d % NSLOT, nxt_q0, nxt_k_first + (tgt - nd) * tk,
                         b_plan=diag_plans[d])

        @pl.when(qi >= 1)
        def _():
            last_below()
            diagonal()

        @pl.when(qi == 0)
        def _():
            diagonal()

        for c in all_chains:
            acc = acc_ref[c]
            o = acc[:HEAD_V] / acc[HEAD_V:HEAD_V + 1]
            o_ref[:, pl.ds(q0 + c * cw, cw)] = o.astype(o_ref.dtype)
        return carry

    for j in range(LAG):
        step(0, 0, (j - LAG) % NSLOT, 0, j * tk, a_plan=diag_plans[j], b_plan=None)
    lax.fori_loop(0, nq, q_tile, 0)


def _attn_fixed_kernel(cq_ref, cg_ref, thr_ref, qt_ref, k_ref, vt_ref, o_ref, acc_ref,
                       *, seq, tq, tk, cw):
    ncol = tq // cw
    nq = seq // tq
    nd = tq // tk
    all_chains = tuple(range(ncol))
    bias_base = (pl.program_id(0) * pl.num_programs(1) + pl.program_id(1)) * nq
    ref_max = thr_ref[1]

    def first_group(t):
        def dead(g):
            return jnp.logical_and(
                g < t - 1, cq_ref[bias_base + t] - cg_ref[bias_base + g] < -thr_ref[0])
        return lax.while_loop(dead, lambda g: g + 1, jnp.int32(0))

    def scores(q0, k0, c, masked):
        qt = qt_ref[:, pl.ds(q0 + c * cw, cw)]
        s = jnp.dot(k_ref[pl.ds(k0, tk), :], qt, preferred_element_type=F32)
        if masked:
            kpos = k0 + lax.broadcasted_iota(jnp.int32, (tk, cw), 0)
            qpos = q0 + c * cw + lax.broadcasted_iota(jnp.int32, (tk, cw), 1)
            s = jnp.where(kpos <= qpos, s, MASK_VALUE)
        return s

    def weighted_values(k0, s):
        p = jnp.exp2(s - ref_max).astype(BF16)
        return jnp.dot(vt_ref[:, pl.ds(k0, tk)], p, preferred_element_type=F32)

    plain = {c: False for c in all_chains}

    def run_list(q0, blocks):
        tot = [None] * ncol
        pending = [{c: scores(q0, k0, c, m) for c, m in vis.items()}
                   for k0, vis in blocks[:SCORES_AHEAD]]
        for j, (k0, vis) in enumerate(blocks):
            nxt_k0, nxt_vis = (blocks[j + SCORES_AHEAD] if j + SCORES_AHEAD < len(blocks)
                               else (None, {}))
            s_cur = pending.pop(0)
            s_nxt = {}
            for c in all_chains:
                if c in nxt_vis:
                    s_nxt[c] = scores(q0, nxt_k0, c, nxt_vis[c])
                if c in vis:
                    t = weighted_values(k0, s_cur[c])
                    tot[c] = t if tot[c] is None else tot[c] + t
            pending.append(s_nxt)
        return tot

    def run_plain(q0, k0, count):
        tot = run_list(q0, [(k0 + j * tk, plain) for j in range(count)])
        for c in all_chains:
            acc_ref[c] = acc_ref[c] + tot[c]

    def q_tile(qi, carry):
        q0 = pl.multiple_of(qi * tq, tq)
        acc_ref[...] = jnp.zeros_like(acc_ref)
        g_first = first_group(qi)
        k_first = pl.multiple_of(g_first * tq, tq)
        n_main = nd * (jnp.maximum(qi - 1, 0) - g_first)
        rem = n_main & (MAIN_UNROLL - 1)

        def main(t, carry2):
            run_plain(q0, pl.multiple_of(k_first + t * MAIN_UNROLL * tk, tq), MAIN_UNROLL)
            return carry2

        lax.fori_loop(0, lax.shift_right_logical(n_main, MAIN_UNROLL.bit_length() - 1), main, 0)
        k_last = q0 - nd * tk
        sub = MAIN_UNROLL // 2
        while sub >= 2 * nd:
            k_sub = pl.multiple_of(k_last - ((rem & (sub - 1)) + sub) * tk, tq)

            @pl.when((rem & sub) != 0)
            def _(k_sub=k_sub, sub=sub):
                run_plain(q0, k_sub, sub)

            sub //= 2

        diagonal = [(q0 + d * tk, {c: c == d for c in range(d, ncol)}) for d in range(nd)]

        def finish(groups_below):
            k0 = pl.multiple_of(q0 - groups_below * nd * tk, tq)
            below = [(k0 + j * tk, plain) for j in range(groups_below * nd)]
            tot = run_list(q0, below + diagonal)
            for c in all_chains:
                acc = acc_ref[c] + tot[c]
                o = acc[:HEAD_V] / acc[HEAD_V:HEAD_V + 1]
                o_ref[:, pl.ds(q0 + c * cw, cw)] = o.astype(o_ref.dtype)

        two_groups = (rem & nd) != 0
        for groups_below, when in ((2, jnp.logical_and(qi >= 1, two_groups)),
                                   (1, jnp.logical_and(qi >= 1, jnp.logical_not(two_groups))),
                                   (0, qi == 0)):
            pl.when(when)(functools.partial(finish, groups_below))

        return carry

    lax.fori_loop(0, nq, q_tile, 0)


def _attention(qt, k, vt, bound, bias=None):
    b, heads, _, s = qt.shape
    tq = min(ATT_TQ, s)
    tk = min(ATT_TK, tq // NSLOT)
    assert tk == ATT_CW
    nq = s // tq
    if bias is None:
        cq = cg = jnp.zeros((b * heads * nq,), F32)
    else:
        cq = bias[:, :, ::tq].reshape(-1)
        cg = bias[:, :, tq - 1::tq].reshape(-1)
    bound = jnp.asarray(bound, F32)
    thr = jnp.stack([2.0 * bound + ZERO_WEIGHT_LOG2, bound])
    in_specs = [
        pl.BlockSpec((None, None, QK_PAD, s), lambda bi, hi, *_: (bi, hi, 0, 0)),
        pl.BlockSpec((None, None, s, QK_PAD), lambda bi, hi, *_: (bi, hi, 0, 0)),
        pl.BlockSpec((None, None, V_ROWS, s), lambda bi, hi, *_: (bi, hi, 0, 0)),
    ]
    out_specs = pl.BlockSpec((None, HEAD_V, s), lambda bi, hi, *_: (bi, hi, 0))
    acc_scratch = pltpu.VMEM((tq // ATT_CW, V_ROWS, ATT_CW), F32)
    common = dict(
        out_shape=jax.ShapeDtypeStruct((b, heads * HEAD_V, s), BF16),
        compiler_params=pltpu.CompilerParams(
            dimension_semantics=("parallel", "parallel"), vmem_limit_bytes=VMEM_LIMIT),
    )

    def online(*args):
        return pl.pallas_call(
            functools.partial(_attn_kernel, seq=s, tq=tq, tk=tk, cw=ATT_CW),
            grid_spec=pltpu.PrefetchScalarGridSpec(
                num_scalar_prefetch=3, grid=(b, heads), in_specs=in_specs, out_specs=out_specs,
                scratch_shapes=([acc_scratch, pltpu.VMEM((1, tq), F32)]
                                + [pltpu.VMEM((tq // ATT_CW, tk, ATT_CW), F32)] * NSLOT
                                + [pltpu.VMEM((1, tq), F32)] * NSLOT),
            ),
            name="attention", **common)(*args)

    def fixed(*args):
        return pl.pallas_call(
            functools.partial(_attn_fixed_kernel, seq=s, tq=tq, tk=tk, cw=ATT_CW),
            grid_spec=pltpu.PrefetchScalarGridSpec(
                num_scalar_prefetch=3, grid=(b, heads), in_specs=in_specs, out_specs=out_specs,
                scratch_shapes=[acc_scratch],
            ),
            name="attention_fixed", **common)(*args)

    if bias is not None:
        return online(cq, cg, thr, qt, k, vt)
    return lax.cond(bound <= FIXED_REF_MAX_BOUND, fixed, online, cq, cg, thr, qt, k, vt)


def _score_bound(g_q, g_k, dk):
    return (jnp.max(jnp.abs(g_q)) * jnp.max(jnp.abs(g_k)) * (dk ** 0.5) * LOG2E
            * BF16_NORM_SLACK).astype(F32)


def kernel(x, p, positions, g_ffn1, g_mix, g_ffn2, g_ple, ffn1_w_in, ffn1_w_out, ffn2_w_in, ffn2_w_out, ple_w_proj, ple_w_gate, mla_w_in, mla_g_q_lat, mla_w_uq, mla_g_kv_lat, mla_w_ukv, mla_g_qn, mla_g_kn, mla_w_o, fox_w_in, fox_b_f, fox_g_qn, fox_g_kn, fox_w_o):
    b, s, d = x.shape
    depth = g_ffn1.shape[0]
    assert d == D_MODEL and s % PROJ_TILE == 0 and s % ROW_TILE == 0
    n = b * s
    bf = lambda w: w.astype(BF16)
    row = lambda g: g.reshape(1, -1)
    col = lambda g: g.reshape(-1, 1)

    pos3 = positions.reshape(b, 1, s)
    half = MLA_ROPE // 2
    inv_freq = (ROPE_THETA ** (-jnp.arange(0, MLA_ROPE, 2, dtype=F32) / MLA_ROPE)).reshape(half, 1)
    lat = MLA_Q_RANK + MLA_KV_RANK

    h = x
    for i in range(depth):
        j = i // N_MIXERS
        h = _ffn(h.reshape(n, d), row(g_ffn1[i]), bf(ffn1_w_in[i]), bf(ffn1_w_out[i])).reshape(b, s, d)
        if i % N_MIXERS == 0:
            qt, k, vt = _mla_proj(
                h, pos3, row(g_mix[i]), bf(mla_w_in[j][:, :lat]), bf(mla_w_in[j][:, lat:].T),
                row(mla_g_q_lat[j]), row(mla_g_kv_lat[j]), bf(mla_w_uq[j].T), bf(mla_w_ukv[j].T),
                col(mla_g_qn[j]), col(mla_g_kn[j]), inv_freq)
            w_o = mla_w_o[j]
            ot = _attention(qt, k, vt, _score_bound(mla_g_qn[j], mla_g_kn[j], MLA_QK))
        else:
            qt, k, vt, c = _fox_proj(h, row(g_mix[i]), bf(fox_w_in[j].T), col(fox_b_f[j]),
                                     col(fox_g_qn[j]), col(fox_g_kn[j]))
            w_o = fox_w_o[j]
            ot = _attention(qt, k, vt, _score_bound(fox_g_qn[j], fox_g_kn[j], FOX_HEAD_DIM), bias=c)
        h = _post(h.reshape(n, d), ot, bf(w_o), row(g_ffn2[i]), bf(ffn2_w_in[i]), bf(ffn2_w_out[i]),
                  p.reshape(depth, n, PLE_DIM), i, row(g_ple[i]), bf(ple_w_gate[i]),
                  bf(ple_w_proj[i])).reshape(b, s, d)
    return h
```

```python
import functools
import math

import jax
import jax.numpy as jnp
from jax import lax
from jax.experimental import pallas as pl
from jax.experimental.pallas import tpu as pltpu

F32 = jnp.float32
BF16 = jnp.bfloat16

D_MODEL = 1024
D_FF = 2816
PLE_DIM = 256
N_MIXERS = 2
FFN_HALF = 0.5
HEADS = 16
MLA_Q_RANK = 512
MLA_KV_RANK = 256
MLA_NOPE = 64
MLA_ROPE = 32
MLA_QK = MLA_NOPE + MLA_ROPE
HEAD_V = 64
FOX_HEAD_DIM = 64
ROPE_THETA = 10000.0
EPS = 1e-6
LOG2E = math.log2(math.e)

LANES = 128
MXU_DIM = 256
BF16_SUBLANES = 16
QK_PAD = 128
V_ROWS = HEAD_V + BF16_SUBLANES
VMEM_LIMIT = 52 * 1024 * 1024

ROW_TILE = 512
PROJ_TILE = 512
HEAD_GROUP = 4
ATT_TQ = 1024
ATT_CW = MXU_DIM
ATT_TK = MXU_DIM
MAIN_UNROLL = 16
NSLOT = 4
LAG = 2
SCORES_AHEAD = 2
PV_KEYS = MXU_DIM
MASK_VALUE = -jnp.inf
M_INIT = -1e30
ZERO_WEIGHT_LOG2 = 170.0
BF16_NORM_SLACK = 1.02
FIXED_REF_MAX_BOUND = 56.0


def _rms(x, g):
    return x * lax.rsqrt(jnp.mean(x * x, axis=-1, keepdims=True) + EPS) * g


def _resident(shape):
    return pl.BlockSpec(shape, lambda *_: (0,) * len(shape), pipeline_mode=pl.Buffered(1))


def _swiglu_half_step(x, g, w_in_ref, w_out_ref):
    xn = _rms(x, g).astype(BF16)
    gu = jnp.dot(xn, w_in_ref[...], preferred_element_type=F32)
    gate = gu[:, :D_FF]
    up = gu[:, D_FF:]
    act = (gate * jax.nn.sigmoid(gate) * up).astype(BF16)
    return x + FFN_HALF * jnp.dot(act, w_out_ref[...], preferred_element_type=F32)


def _ffn_kernel(x_ref, g_ref, w_in_ref, w_out_ref, o_ref):
    o_ref[...] = _swiglu_half_step(x_ref[...], g_ref[...], w_in_ref, w_out_ref)


def _ffn(h2d, g, w_in, w_out):
    n = h2d.shape[0]
    return pl.pallas_call(
        _ffn_kernel,
        grid=(n // ROW_TILE,),
        in_specs=[
            pl.BlockSpec((ROW_TILE, D_MODEL), lambda i: (i, 0)),
            _resident((1, D_MODEL)),
            _resident((D_MODEL, 2 * D_FF)),
            _resident((D_FF, D_MODEL)),
        ],
        out_specs=pl.BlockSpec((ROW_TILE, D_MODEL), lambda i: (i, 0)),
        out_shape=jax.ShapeDtypeStruct((n, D_MODEL), F32),
        compiler_params=pltpu.CompilerParams(
            dimension_semantics=("parallel",), vmem_limit_bytes=VMEM_LIMIT),
        name="ffn",
    )(h2d, g, w_in, w_out)


def _post_kernel(x_ref, ot_ref, wo_ref, g2_ref, w_in_ref, w_out_ref,
                 p_ref, gp_ref, wg_ref, wp_ref, o_ref):
    x = x_ref[...] + lax.dot_general(ot_ref[...], wo_ref[...], (((0,), (0,)), ((), ())),
                                     preferred_element_type=F32)
    x = _swiglu_half_step(x, g2_ref[...], w_in_ref, w_out_ref)
    xn = _rms(x, gp_ref[...]).astype(BF16)
    gate = jax.nn.sigmoid(jnp.dot(xn, wg_ref[...], preferred_element_type=F32))
    pp = jnp.dot(p_ref[...].astype(BF16), wp_ref[...], preferred_element_type=F32)
    o_ref[...] = x + gate * pp


def _post(h2d, ot, w_o, g2, w_in, w_out, p3d, layer, g_ple, w_gate, w_proj):
    n = h2d.shape[0]
    hd, s = ot.shape[1], ot.shape[2]
    tiles = s // ROW_TILE
    rows = pl.BlockSpec((ROW_TILE, D_MODEL), lambda i: (i, 0))
    return pl.pallas_call(
        _post_kernel,
        grid=(n // ROW_TILE,),
        in_specs=[
            rows,
            pl.BlockSpec((None, hd, ROW_TILE), lambda i: (i // tiles, 0, i % tiles)),
            _resident((hd, D_MODEL)),
            _resident((1, D_MODEL)),
            _resident((D_MODEL, 2 * D_FF)),
            _resident((D_FF, D_MODEL)),
            pl.BlockSpec((None, ROW_TILE, PLE_DIM), lambda i: (layer, i, 0)),
            _resident((1, D_MODEL)),
            _resident((D_MODEL, D_MODEL)),
            _resident((PLE_DIM, D_MODEL)),
        ],
        out_specs=rows,
        out_shape=jax.ShapeDtypeStruct((n, D_MODEL), F32),
        compiler_params=pltpu.CompilerParams(
            dimension_semantics=("parallel",), vmem_limit_bytes=VMEM_LIMIT),
        name="post",
    )(h2d, ot, w_o, g2, w_in, w_out, p3d, g_ple, w_gate, w_proj)


def _dot_nt(w, x):
    return lax.dot_general(w, x, (((1,), (1,)), ((), ())), preferred_element_type=F32)


def _ones_row_group(width):
    row = lax.broadcasted_iota(jnp.int32, (BF16_SUBLANES, width), 0)
    return jnp.where(row == 0, 1.0, 0.0).astype(F32)


def _rope_rows(x1, x2, cos, sin):
    return x1 * cos - x2 * sin, x1 * sin + x2 * cos


def _mla_proj_kernel(x_ref, pos_ref, g_ref, w_in_ref, w_pe_ref, gq_lat_ref, gkv_lat_ref,
                     w_uq_ref, w_ukv_ref, gqn_ref, gkn_ref, inv_freq_ref,
                     qt_ref, k_ref, vt_ref):
    tm = x_ref.shape[0]
    xn = _rms(x_ref[...], g_ref[...]).astype(BF16)
    z = jnp.dot(xn, w_in_ref[...], preferred_element_type=F32)
    cq = _rms(z[:, :MLA_Q_RANK], gq_lat_ref[...]).astype(BF16)
    ckv = _rms(z[:, MLA_Q_RANK:], gkv_lat_ref[...]).astype(BF16)
    kpe_t = _dot_nt(w_pe_ref[...], xn)

    ang = pos_ref[...].astype(F32) * inv_freq_ref[...]
    cos = jnp.cos(ang)
    sin = jnp.sin(ang)
    half = MLA_ROPE // 2
    q_scale = (MLA_QK ** -0.5) * LOG2E
    gqn = gqn_ref[...] * q_scale
    gkn = gkn_ref[...]
    kpe_sq = jnp.sum(kpe_t * kpe_t, axis=0, keepdims=True)
    ones_rows = _ones_row_group(tm).astype(BF16)
    zeros_tail = jnp.zeros((QK_PAD - MLA_QK, tm), F32)

    kv_dim = MLA_NOPE + HEAD_V
    for h in range(HEADS):
        if h % HEAD_GROUP == 0:
            g0 = h
            qt_all = _dot_nt(w_uq_ref[g0 * MLA_QK:(g0 + HEAD_GROUP) * MLA_QK, :], cq)
            kvt_all = _dot_nt(w_ukv_ref[g0 * kv_dim:(g0 + HEAD_GROUP) * kv_dim, :], ckv)
        hl = h - g0
        qh = qt_all[hl * MLA_QK:(hl + 1) * MLA_QK]
        qn = qh * lax.rsqrt(jnp.mean(qh * qh, axis=0, keepdims=True) + EPS) * gqn
        q1, q2 = _rope_rows(qn[MLA_NOPE:MLA_NOPE + half], qn[MLA_NOPE + half:], cos, sin)
        qt_ref[h, 0:MLA_NOPE, :] = qn[:MLA_NOPE].astype(BF16)
        qt_ref[h, MLA_NOPE:MLA_NOPE + half, :] = q1.astype(BF16)
        qt_ref[h, MLA_NOPE + half:MLA_QK, :] = q2.astype(BF16)
        qt_ref[h, MLA_QK:QK_PAD, :] = zeros_tail.astype(BF16)

        base = hl * kv_dim
        kn = kvt_all[base:base + MLA_NOPE]
        ms = (jnp.sum(kn * kn, axis=0, keepdims=True) + kpe_sq) * (1.0 / MLA_QK)
        r = lax.rsqrt(ms + EPS)
        kn = kn * r * gkn[:MLA_NOPE]
        kp = kpe_t * r * gkn[MLA_NOPE:]
        k1, k2 = _rope_rows(kp[:half], kp[half:], cos, sin)
        kt = jnp.concatenate([kn, k1, k2, zeros_tail], axis=0)
        k_ref[h] = kt.T.astype(BF16)

        vt_ref[h, 0:HEAD_V, :] = kvt_all[base + MLA_NOPE:base + MLA_NOPE + HEAD_V].astype(BF16)
        vt_ref[h, HEAD_V:V_ROWS, :] = ones_rows


def _qkv_out(b, s):
    specs = [
        pl.BlockSpec((None, HEADS, QK_PAD, PROJ_TILE), lambda bi, i: (bi, 0, 0, i)),
        pl.BlockSpec((None, HEADS, PROJ_TILE, QK_PAD), lambda bi, i: (bi, 0, i, 0)),
        pl.BlockSpec((None, HEADS, V_ROWS, PROJ_TILE), lambda bi, i: (bi, 0, 0, i)),
    ]
    shapes = [
        jax.ShapeDtypeStruct((b, HEADS, QK_PAD, s), BF16),
        jax.ShapeDtypeStruct((b, HEADS, s, QK_PAD), BF16),
        jax.ShapeDtypeStruct((b, HEADS, V_ROWS, s), BF16),
    ]
    return specs, shapes


def _mla_proj(h, pos3, g, w_in, w_pe_t, gq_lat, gkv_lat, w_uq_t, w_ukv_t, gqn, gkn, inv_freq):
    b, s, _ = h.shape
    out_specs, out_shapes = _qkv_out(b, s)
    return pl.pallas_call(
        _mla_proj_kernel,
        grid=(b, s // PROJ_TILE),
        in_specs=[
            pl.BlockSpec((None, PROJ_TILE, D_MODEL), lambda bi, i: (bi, i, 0)),
            pl.BlockSpec((None, 1, PROJ_TILE), lambda bi, i: (bi, 0, i)),
            _resident(g.shape), _resident(w_in.shape), _resident(w_pe_t.shape),
            _resident(gq_lat.shape), _resident(gkv_lat.shape),
            _resident(w_uq_t.shape), _resident(w_ukv_t.shape),
            _resident(gqn.shape), _resident(gkn.shape), _resident(inv_freq.shape),
        ],
        out_specs=out_specs,
        out_shape=out_shapes,
        compiler_params=pltpu.CompilerParams(
            dimension_semantics=("parallel", "parallel"), vmem_limit_bytes=VMEM_LIMIT),
        name="mla_proj",
    )(h, pos3, g, w_in, w_pe_t, gq_lat, gkv_lat, w_uq_t, w_ukv_t, gqn, gkn, inv_freq)


def _split3(c):
    hi = c.astype(BF16).astype(F32)
    mid = (c - hi).astype(BF16).astype(F32)
    lo = (c - hi - mid).astype(BF16).astype(F32)
    return hi, mid, lo


def _fox_proj_kernel(x_ref, g_ref, w_ref, bf_ref, gqn_ref, gkn_ref,
                     qt_ref, k_ref, vt_ref, c_ref, carry_ref):
    tm = x_ref.shape[0]
    hd = HEADS * FOX_HEAD_DIM

    @pl.when(pl.program_id(1) == 0)
    def _():
        carry_ref[...] = jnp.zeros_like(carry_ref)

    xn = _rms(x_ref[...], g_ref[...]).astype(BF16)
    f_t = _dot_nt(w_ref[3 * hd:3 * hd + HEADS, :], xn) + bf_ref[...]

    logf = (jnp.minimum(f_t, 0.0) - jnp.log1p(jnp.exp(-jnp.abs(f_t)))) * LOG2E
    hi, mid, lo = _split3(logf)
    parts = jnp.concatenate([hi, mid, lo], axis=0).astype(BF16)
    src = lax.broadcasted_iota(jnp.int32, (tm, tm), 0)
    dst = lax.broadcasted_iota(jnp.int32, (tm, tm), 1)
    upper = jnp.where(src <= dst, 1.0, 0.0).astype(BF16)
    sums = jnp.dot(parts, upper, preferred_element_type=F32)
    local = sums[2 * HEADS:] + sums[HEADS:2 * HEADS] + sums[:HEADS]
    c = local + carry_ref[:, 0:1]
    carry_ref[...] = jnp.broadcast_to(c[:, tm - 1:tm], carry_ref.shape)
    c_ref[...] = c

    q_scale = (FOX_HEAD_DIM ** -0.5) * LOG2E
    gqn = gqn_ref[...] * q_scale
    gkn = gkn_ref[...]
    row = lax.broadcasted_iota(jnp.int32, (BF16_SUBLANES, tm), 0)
    ones_rows = _ones_row_group(tm).astype(BF16)
    zeros_tail = jnp.zeros((QK_PAD - FOX_HEAD_DIM - BF16_SUBLANES, tm), F32)

    def bias_rows(first, second):
        out = jnp.zeros((BF16_SUBLANES, tm), F32)
        for j in range(3):
            out = jnp.where(row == j, first[j], out)
            out = jnp.where(row == 8 + j, second[j], out)
        return out

    one3 = (1.0, 1.0, 1.0)
    gw = HEAD_GROUP * FOX_HEAD_DIM
    for h in range(HEADS):
        if h % HEAD_GROUP == 0:
            g0 = h
            r0 = g0 * FOX_HEAD_DIM
            qt_all = _dot_nt(w_ref[r0:r0 + gw, :], xn)
            kt_all = _dot_nt(w_ref[hd + r0:hd + r0 + gw, :], xn)
            vt_all = _dot_nt(w_ref[2 * hd + r0:2 * hd + r0 + gw, :], xn)
        sl = slice((h - g0) * FOX_HEAD_DIM, (h - g0 + 1) * FOX_HEAD_DIM)
        c3 = _split3(c[h:h + 1])
        qh = qt_all[sl]
        qn = qh * lax.rsqrt(jnp.mean(qh * qh, axis=0, keepdims=True) + EPS) * gqn
        qt_ref[h, 0:FOX_HEAD_DIM, :] = qn.astype(BF16)
        qt_ref[h, FOX_HEAD_DIM:FOX_HEAD_DIM + BF16_SUBLANES, :] = bias_rows(one3, c3).astype(BF16)
        qt_ref[h, FOX_HEAD_DIM + BF16_SUBLANES:QK_PAD, :] = zeros_tail.astype(BF16)

        kh = kt_all[sl]
        kn = kh * lax.rsqrt(jnp.mean(kh * kh, axis=0, keepdims=True) + EPS) * gkn
        neg3 = tuple(-p for p in c3)
        kt = jnp.concatenate([kn, bias_rows(neg3, one3), zeros_tail], axis=0)
        k_ref[h] = kt.T.astype(BF16)

        vt_ref[h, 0:HEAD_V, :] = vt_all[sl].astype(BF16)
        vt_ref[h, HEAD_V:V_ROWS, :] = ones_rows


def _fox_proj(h, g, w_t, b_f, gqn, gkn):
    b, s, _ = h.shape
    out_specs, out_shapes = _qkv_out(b, s)
    out_specs = out_specs + [pl.BlockSpec((None, HEADS, PROJ_TILE), lambda bi, i: (bi, 0, i))]
    out_shapes = out_shapes + [jax.ShapeDtypeStruct((b, HEADS, s), F32)]
    return pl.pallas_call(
        _fox_proj_kernel,
        grid=(b, s // PROJ_TILE),
        in_specs=[
            pl.BlockSpec((None, PROJ_TILE, D_MODEL), lambda bi, i: (bi, i, 0)),
            _resident(g.shape), _resident(w_t.shape), _resident(b_f.shape),
            _resident(gqn.shape), _resident(gkn.shape),
        ],
        out_specs=out_specs,
        out_shape=out_shapes,
        scratch_shapes=[pltpu.VMEM((HEADS, LANES), F32)],
        compiler_params=pltpu.CompilerParams(
            dimension_semantics=("arbitrary", "arbitrary"), vmem_limit_bytes=VMEM_LIMIT),
        name="fox_proj",
    )(h, g, w_t, b_f, gqn, gkn)


def _attn_kernel(cq_ref, cg_ref, thr_ref, qt_ref, k_ref, vt_ref, o_ref, acc_ref, m_ref, *bufs,
                 seq, tq, tk, cw):
    ncol = tq // cw
    nq = seq // tq
    all_chains = tuple(range(ncol))
    s_refs = bufs[:NSLOT]
    bm_refs = bufs[NSLOT:]
    bias_base = (pl.program_id(0) * pl.num_programs(1) + pl.program_id(1)) * nq

    def first_group(t):
        def dead(g):
            return jnp.logical_and(
                g < t - 1, cq_ref[bias_base + t] - cg_ref[bias_base + g] < -thr_ref[0])
        return lax.while_loop(dead, lambda g: g + 1, jnp.int32(0))

    pv_keys = min(PV_KEYS, tk)
    n_parts = tk // pv_keys
    full_plan = {c: tuple((r, False) for r in range(n_parts)) for c in all_chains}

    nd = tq // tk
    assert nd == NSLOT and MAIN_UNROLL % NSLOT == 0 and LAG < NSLOT

    def diag_plan(d):
        plan = {}
        for c in all_chains:
            parts = []
            for r in range(n_parts):
                key_lo = d * tk + r * pv_keys
                if key_lo <= (c + 1) * cw - 1:
                    parts.append((r, key_lo + pv_keys - 1 > c * cw))
            if parts:
                plan[c] = tuple(parts)
        return plan

    diag_plans = tuple(diag_plan(d) for d in range(nd))

    def stage_a_parts(q0, k0, slot, c, parts):
        cols = slice(c * cw, (c + 1) * cw)
        qt = qt_ref[:, pl.ds(q0 + c * cw, cw)]
        bm = None
        for r, masked in parts:
            rows = slice(r * pv_keys, (r + 1) * pv_keys)
            kb = k_ref[pl.ds(k0 + r * pv_keys, pv_keys), :]
            s = jnp.dot(kb, qt, preferred_element_type=F32)
            if masked:
                kpos = k0 + r * pv_keys + lax.broadcasted_iota(jnp.int32, (pv_keys, cw), 0)
                qpos = q0 + c * cw + lax.broadcasted_iota(jnp.int32, (pv_keys, cw), 1)
                s = jnp.where(kpos <= qpos, s, MASK_VALUE)
            s_refs[slot][c, rows, :] = s
            pm = jnp.max(s, axis=0, keepdims=True)
            bm = pm if bm is None else jnp.maximum(bm, pm)
            if r == parts[-1][0]:
                bm_refs[slot][:, cols] = bm
            yield

    def stage_b_parts(k0, slot, c, parts):
        cols = slice(c * cw, (c + 1) * cw)
        m_old = m_ref[:, cols]
        m_new = jnp.maximum(m_old, bm_refs[slot][:, cols])
        alpha = jnp.exp2(m_old - m_new)
        pv = None
        for r, _ in parts:
            rows = slice(r * pv_keys, (r + 1) * pv_keys)
            p = jnp.exp2(s_refs[slot][c, rows, :] - m_new).astype(BF16)
            vb = vt_ref[:, pl.ds(k0 + r * pv_keys, pv_keys)]
            t = jnp.dot(vb, p, preferred_element_type=F32)
            pv = t if pv is None else pv + t
            if r == parts[-1][0]:
                acc_ref[c] = alpha * acc_ref[c] + pv
                m_ref[:, cols] = m_new
            yield

    def step(q0, k0, rslot, nxt_q0, nxt_k0, a_plan=full_plan, b_plan=full_plan):
        for c in all_chains:
            gens = []
            if a_plan is not None and c in a_plan:
                gens.append(stage_a_parts(nxt_q0, nxt_k0, (rslot + LAG) % NSLOT, c, a_plan[c]))
            if b_plan is not None and c in b_plan:
                gens.append(stage_b_parts(k0, rslot, c, b_plan[c]))
            while gens:
                gens = [g for g in gens if next(g, gens) is not gens]

    def q_tile(qi, carry):
        q0 = pl.multiple_of(qi * tq, tq)
        acc_ref[...] = jnp.zeros_like(acc_ref)
        m_ref[...] = jnp.full_like(m_ref, M_INIT)

        def run_steps(k0, count):
            for j in range(count):
                step(q0, k0 + j * tk, j % NSLOT, q0, k0 + (j + LAG) * tk)

        g_first = first_group(qi)
        k_first = pl.multiple_of(g_first * tq, tq)
        n_main = NSLOT * (jnp.maximum(qi - 1, 0) - g_first)
        rem = n_main & (MAIN_UNROLL - 1)

        def main(t, carry2):
            run_steps(pl.multiple_of(k_first + t * MAIN_UNROLL * tk, tq), MAIN_UNROLL)
            return carry2

        lax.fori_loop(0, lax.shift_right_logical(n_main, MAIN_UNROLL.bit_length() - 1), main, 0)
        k_last = q0 - NSLOT * tk
        sub = MAIN_UNROLL // 2
        while sub >= NSLOT:
            k_sub = pl.multiple_of(k_last - ((rem & (sub - 1)) + sub) * tk, NSLOT * tk)

            @pl.when((rem & sub) != 0)
            def _(k_sub=k_sub, sub=sub):
                run_steps(k_sub, sub)

            sub //= 2

        nxt_qi = jnp.minimum(qi + 1, nq - 1)
        nxt_q0 = pl.multiple_of(nxt_qi * tq, tq)
        nxt_k_first = pl.multiple_of(first_group(nxt_qi) * tq, tq)

        def last_below():
            k0 = pl.multiple_of(k_last, NSLOT * tk)
            for j in range(NSLOT):
                tgt = j + LAG
                if tgt < NSLOT:
                    step(q0, k0 + j * tk, j, q0, k0 + tgt * tk)
                else:
                    step(q0, k0 + j * tk, j, q0, q0 + (tgt - NSLOT) * tk,
                         a_plan=diag_plans[tgt - NSLOT])

        def diagonal():
            for d in range(nd):
                tgt = d + LAG
                if tgt < nd:
                    step(q0, q0 + d * tk, d % NSLOT, q0, q0 + tgt * tk,
                         a_plan=diag_plans[tgt], b_plan=diag_plans[d])
                else:
                    step(q0, q0 + d * tk, d % NSLOT, nxt_q0, nxt_k_first + (tgt - nd) * tk,
                         b_plan=diag_plans[d])

        @pl.when(qi >= 1)
        def _():
            last_below()
            diagonal()

        @pl.when(qi == 0)
        def _():
            diagonal()

        for c in all_chains:
            acc = acc_ref[c]
            o = acc[:HEAD_V] / acc[HEAD_V:HEAD_V + 1]
            o_ref[:, pl.ds(q0 + c * cw, cw)] = o.astype(o_ref.dtype)
        return carry

    for j in range(LAG):
        step(0, 0, (j - LAG) % NSLOT, 0, j * tk, a_plan=diag_plans[j], b_plan=None)
    lax.fori_loop(0, nq, q_tile, 0)


def _attn_fixed_kernel(cq_ref, cg_ref, thr_ref, qt_ref, k_ref, vt_ref, o_ref, acc_ref,
                       *, seq, tq, tk, cw):
    ncol = tq // cw
    nq = seq // tq
    nd = tq // tk
    all_chains = tuple(range(ncol))
    bias_base = (pl.program_id(0) * pl.num_programs(1) + pl.program_id(1)) * nq
    ref_max = thr_ref[1]

    def first_group(t):
        def dead(g):
            return jnp.logical_and(
                g < t - 1, cq_ref[bias_base + t] - cg_ref[bias_base + g] < -thr_ref[0])
        return lax.while_loop(dead, lambda g: g + 1, jnp.int32(0))

    def scores(q0, k0, c, masked):
        qt = qt_ref[:, pl.ds(q0 + c * cw, cw)]
        s = jnp.dot(k_ref[pl.ds(k0, tk), :], qt, preferred_element_type=F32)
        if masked:
            kpos = k0 + lax.broadcasted_iota(jnp.int32, (tk, cw), 0)
            qpos = q0 + c * cw + lax.broadcasted_iota(jnp.int32, (tk, cw), 1)
            s = jnp.where(kpos <= qpos, s, MASK_VALUE)
        return s

    def weighted_values(k0, s):
        p = jnp.exp2(s - ref_max).astype(BF16)
        return jnp.dot(vt_ref[:, pl.ds(k0, tk)], p, preferred_element_type=F32)

    plain = {c: False for c in all_chains}

    def run_list(q0, blocks):
        tot = [None] * ncol
        pending = [{c: scores(q0, k0, c, m) for c, m in vis.items()}
                   for k0, vis in blocks[:SCORES_AHEAD]]
        for j, (k0, vis) in enumerate(blocks):
            nxt_k0, nxt_vis = (blocks[j + SCORES_AHEAD] if j + SCORES_AHEAD < len(blocks)
                               else (None, {}))
            s_cur = pending.pop(0)
            s_nxt = {}
            for c in all_chains:
                if c in nxt_vis:
                    s_nxt[c] = scores(q0, nxt_k0, c, nxt_vis[c])
                if c in vis:
                    t = weighted_values(k0, s_cur[c])
                    tot[c] = t if tot[c] is None else tot[c] + t
            pending.append(s_nxt)
        return tot

    def run_plain(q0, k0, count):
        tot = run_list(q0, [(k0 + j * tk, plain) for j in range(count)])
        for c in all_chains:
            acc_ref[c] = acc_ref[c] + tot[c]

    def q_tile(qi, carry):
        q0 = pl.multiple_of(qi * tq, tq)
        acc_ref[...] = jnp.zeros_like(acc_ref)
        g_first = first_group(qi)
        k_first = pl.multiple_of(g_first * tq, tq)
        n_main = nd * (jnp.maximum(qi - 1, 0) - g_first)
        rem = n_main & (MAIN_UNROLL - 1)

        def main(t, carry2):
            run_plain(q0, pl.multiple_of(k_first + t * MAIN_UNROLL * tk, tq), MAIN_UNROLL)
            return carry2

        lax.fori_loop(0, lax.shift_right_logical(n_main, MAIN_UNROLL.bit_length() - 1), main, 0)
        k_last = q0 - nd * tk
        sub = MAIN_UNROLL // 2
        while sub >= 2 * nd:
            k_sub = pl.multiple_of(k_last - ((rem & (sub - 1)) + sub) * tk, tq)

            @pl.when((rem & sub) != 0)
            def _(k_sub=k_sub, sub=sub):
                run_plain(q0, k_sub, sub)

            sub //= 2

        diagonal = [(q0 + d * tk, {c: c == d for c in range(d, ncol)}) for d in range(nd)]

        def finish(groups_below):
            k0 = pl.multiple_of(q0 - groups_below * nd * tk, tq)
            below = [(k0 + j * tk, plain) for j in range(groups_below * nd)]
            tot = run_list(q0, below + diagonal)
            for c in all_chains:
                acc = acc_ref[c] + tot[c]
                o = acc[:HEAD_V] / acc[HEAD_V:HEAD_V + 1]
                o_ref[:, pl.ds(q0 + c * cw, cw)] = o.astype(o_ref.dtype)

        two_groups = (rem & nd) != 0
        for groups_below, when in ((2, jnp.logical_and(qi >= 1, two_groups)),
                                   (1, jnp.logical_and(qi >= 1, jnp.logical_not(two_groups))),
                                   (0, qi == 0)):
            pl.when(when)(functools.partial(finish, groups_below))

        return carry

    lax.fori_loop(0, nq, q_tile, 0)


def _attention(qt, k, vt, bound, bias=None):
    b, heads, _, s = qt.shape
    tq = min(ATT_TQ, s)
    tk = min(ATT_TK, tq // NSLOT)
    assert tk == ATT_CW
    nq = s // tq
    if bias is None:
        cq = cg = jnp.zeros((b * heads * nq,), F32)
    else:
        cq = bias[:, :, ::tq].reshape(-1)
        cg = bias[:, :, tq - 1::tq].reshape(-1)
    bound = jnp.asarray(bound, F32)
    thr = jnp.stack([2.0 * bound + ZERO_WEIGHT_LOG2, bound])
    in_specs = [
        pl.BlockSpec((None, None, QK_PAD, s), lambda bi, hi, *_: (bi, hi, 0, 0)),
        pl.BlockSpec((None, None, s, QK_PAD), lambda bi, hi, *_: (bi, hi, 0, 0)),
        pl.BlockSpec((None, None, V_ROWS, s), lambda bi, hi, *_: (bi, hi, 0, 0)),
    ]
    out_specs = pl.BlockSpec((None, HEAD_V, s), lambda bi, hi, *_: (bi, hi, 0))
    acc_scratch = pltpu.VMEM((tq // ATT_CW, V_ROWS, ATT_CW), F32)
    common = dict(
        out_shape=jax.ShapeDtypeStruct((b, heads * HEAD_V, s), BF16),
        compiler_params=pltpu.CompilerParams(
            dimension_semantics=("parallel", "parallel"), vmem_limit_bytes=VMEM_LIMIT),
    )

    def online(*args):
        return pl.pallas_call(
            functools.partial(_attn_kernel, seq=s, tq=tq, tk=tk, cw=ATT_CW),
            grid_spec=pltpu.PrefetchScalarGridSpec(
                num_scalar_prefetch=3, grid=(b, heads), in_specs=in_specs, out_specs=out_specs,
                scratch_shapes=([acc_scratch, pltpu.VMEM((1, tq), F32)]
                                + [pltpu.VMEM((tq // ATT_CW, tk, ATT_CW), F32)] * NSLOT
                                + [pltpu.VMEM((1, tq), F32)] * NSLOT),
            ),
            name="attention", **common)(*args)

    def fixed(*args):
        return pl.pallas_call(
            functools.partial(_attn_fixed_kernel, seq=s, tq=tq, tk=tk, cw=ATT_CW),
            grid_spec=pltpu.PrefetchScalarGridSpec(
                num_scalar_prefetch=3, grid=(b, heads), in_specs=in_specs, out_specs=out_specs,
                scratch_shapes=[acc_scratch],
            ),
            name="attention_fixed", **common)(*args)

    return lax.cond(bound <= FIXED_REF_MAX_BOUND, fixed, online, cq, cg, thr, qt, k, vt)


def _score_bound(g_q, g_k, dk):
    return (jnp.max(jnp.abs(g_q)) * jnp.max(jnp.abs(g_k)) * (dk ** 0.5) * LOG2E
            * BF16_NORM_SLACK).astype(F32)


def kernel(x, p, positions, g_ffn1, g_mix, g_ffn2, g_ple, ffn1_w_in, ffn1_w_out, ffn2_w_in, ffn2_w_out, ple_w_proj, ple_w_gate, mla_w_in, mla_g_q_lat, mla_w_uq, mla_g_kv_lat, mla_w_ukv, mla_g_qn, mla_g_kn, mla_w_o, fox_w_in, fox_b_f, fox_g_qn, fox_g_kn, fox_w_o):
    b, s, d = x.shape
    depth = g_ffn1.shape[0]
    assert d == D_MODEL and s % PROJ_TILE == 0 and s % ROW_TILE == 0
    n = b * s
    bf = lambda w: w.astype(BF16)
    row = lambda g: g.reshape(1, -1)
    col = lambda g: g.reshape(-1, 1)

    pos3 = positions.reshape(b, 1, s)
    half = MLA_ROPE // 2
    inv_freq = (ROPE_THETA ** (-jnp.arange(0, MLA_ROPE, 2, dtype=F32) / MLA_ROPE)).reshape(half, 1)
    lat = MLA_Q_RANK + MLA_KV_RANK

    h = x
    for i in range(depth):
        j = i // N_MIXERS
        h = _ffn(h.reshape(n, d), row(g_ffn1[i]), bf(ffn1_w_in[i]), bf(ffn1_w_out[i])).reshape(b, s, d)
        if i % N_MIXERS == 0:
            qt, k, vt = _mla_proj(
                h, pos3, row(g_mix[i]), bf(mla_w_in[j][:, :lat]), bf(mla_w_in[j][:, lat:].T),
                row(mla_g_q_lat[j]), row(mla_g_kv_lat[j]), bf(mla_w_uq[j].T), bf(mla_w_ukv[j].T),
                col(mla_g_qn[j]), col(mla_g_kn[j]), inv_freq)
            w_o = mla_w_o[j]
            ot = _attention(qt, k, vt, _score_bound(mla_g_qn[j], mla_g_kn[j], MLA_QK))
        else:
            qt, k, vt, c = _fox_proj(h, row(g_mix[i]), bf(fox_w_in[j].T), col(fox_b_f[j]),
                                     col(fox_g_qn[j]), col(fox_g_kn[j]))
            w_o = fox_w_o[j]
            ot = _attention(qt, k, vt, _score_bound(fox_g_qn[j], fox_g_kn[j], FOX_HEAD_DIM), bias=c)
        h = _post(h.reshape(n, d), ot, bf(w_o), row(g_ffn2[i]), bf(ffn2_w_in[i]), bf(ffn2_w_out[i]),
                  p.reshape(depth, n, PLE_DIM), i, row(g_ple[i]), bf(ple_w_gate[i]),
                  bf(ple_w_proj[i])).reshape(b, s, d)
    return h
```

```python
import functools
import math

import jax
import jax.numpy as jnp
from jax import lax
from jax.experimental import pallas as pl
from jax.experimental.pallas import tpu as pltpu

F32 = jnp.float32
BF16 = jnp.bfloat16

D_MODEL = 1024
D_FF = 2816
PLE_DIM = 256
N_MIXERS = 2
FFN_HALF = 0.5
HEADS = 16
MLA_Q_RANK = 512
MLA_KV_RANK = 256
MLA_NOPE = 64
MLA_ROPE = 32
MLA_QK = MLA_NOPE + MLA_ROPE
HEAD_V = 64
FOX_HEAD_DIM = 64
ROPE_THETA = 10000.0
EPS = 1e-6
LOG2E = math.log2(math.e)

LANES = 128
MXU_DIM = 256
BF16_SUBLANES = 16
QK_PAD = 128
V_ROWS = HEAD_V + BF16_SUBLANES
VMEM_LIMIT = 52 * 1024 * 1024

ROW_TILE = 512
PROJ_TILE = 512
HEAD_GROUP = 4
ATT_TQ = 1024
ATT_CW = MXU_DIM
ATT_TK = MXU_DIM
MAIN_UNROLL = 16
NSLOT = 4
LAG = 2
SCORES_AHEAD = 2
PV_KEYS = MXU_DIM
MASK_VALUE = -jnp.inf
M_INIT = -1e30
ZERO_WEIGHT_LOG2 = 170.0
BF16_NORM_SLACK = 1.02
FIXED_REF_MAX_BOUND = 56.0


def _rms(x, g):
    return x * lax.rsqrt(jnp.mean(x * x, axis=-1, keepdims=True) + EPS) * g


def _resident(shape):
    return pl.BlockSpec(shape, lambda *_: (0,) * len(shape), pipeline_mode=pl.Buffered(1))


def _swiglu_half_step(x, g, w_in_ref, w_out_ref):
    xn = _rms(x, g).astype(BF16)
    gu = jnp.dot(xn, w_in_ref[...], preferred_element_type=F32)
    gate = gu[:, :D_FF]
    up = gu[:, D_FF:]
    act = (gate * jax.nn.sigmoid(gate) * up).astype(BF16)
    return x + FFN_HALF * jnp.dot(act, w_out_ref[...], preferred_element_type=F32)


def _ffn_kernel(x_ref, g_ref, w_in_ref, w_out_ref, o_ref):
    o_ref[...] = _swiglu_half_step(x_ref[...], g_ref[...], w_in_ref, w_out_ref)


def _ffn(h2d, g, w_in, w_out):
    n = h2d.shape[0]
    return pl.pallas_call(
        _ffn_kernel,
        grid=(n // ROW_TILE,),
        in_specs=[
            pl.BlockSpec((ROW_TILE, D_MODEL), lambda i: (i, 0)),
            _resident((1, D_MODEL)),
            _resident((D_MODEL, 2 * D_FF)),
            _resident((D_FF, D_MODEL)),
        ],
        out_specs=pl.BlockSpec((ROW_TILE, D_MODEL), lambda i: (i, 0)),
        out_shape=jax.ShapeDtypeStruct((n, D_MODEL), F32),
        compiler_params=pltpu.CompilerParams(
            dimension_semantics=("parallel",), vmem_limit_bytes=VMEM_LIMIT),
        name="ffn",
    )(h2d, g, w_in, w_out)


def _post_kernel(x_ref, ot_ref, wo_ref, g2_ref, w_in_ref, w_out_ref,
                 p_ref, gp_ref, wg_ref, wp_ref, o_ref):
    x = x_ref[...] + lax.dot_general(ot_ref[...], wo_ref[...], (((0,), (0,)), ((), ())),
                                     preferred_element_type=F32)
    x = _swiglu_half_step(x, g2_ref[...], w_in_ref, w_out_ref)
    xn = _rms(x, gp_ref[...]).astype(BF16)
    gate = jax.nn.sigmoid(jnp.dot(xn, wg_ref[...], preferred_element_type=F32))
    pp = jnp.dot(p_ref[...].astype(BF16), wp_ref[...], preferred_element_type=F32)
    o_ref[...] = x + gate * pp


def _post(h2d, ot, w_o, g2, w_in, w_out, p3d, layer, g_ple, w_gate, w_proj):
    n = h2d.shape[0]
    hd, s = ot.shape[1], ot.shape[2]
    tiles = s // ROW_TILE
    rows = pl.BlockSpec((ROW_TILE, D_MODEL), lambda i: (i, 0))
    return pl.pallas_call(
        _post_kernel,
        grid=(n // ROW_TILE,),
        in_specs=[
            rows,
            pl.BlockSpec((None, hd, ROW_TILE), lambda i: (i // tiles, 0, i % tiles)),
            _resident((hd, D_MODEL)),
            _resident((1, D_MODEL)),
            _resident((D_MODEL, 2 * D_FF)),
            _resident((D_FF, D_MODEL)),
            pl.BlockSpec((None, ROW_TILE, PLE_DIM), lambda i: (layer, i, 0)),
            _resident((1, D_MODEL)),
            _resident((D_MODEL, D_MODEL)),
            _resident((PLE_DIM, D_MODEL)),
        ],
        out_specs=rows,
        out_shape=jax.ShapeDtypeStruct((n, D_MODEL), F32),
        compiler_params=pltpu.CompilerParams(
            dimension_semantics=("parallel",), vmem_limit_bytes=VMEM_LIMIT),
        name="post",
    )(h2d, ot, w_o, g2, w_in, w_out, p3d, g_ple, w_gate, w_proj)


def _dot_nt(w, x):
    return lax.dot_general(w, x, (((1,), (1,)), ((), ())), preferred_element_type=F32)


def _ones_row_group(width):
    row = lax.broadcasted_iota(jnp.int32, (BF16_SUBLANES, width), 0)
    return jnp.where(row == 0, 1.0, 0.0).astype(F32)


def _rope_rows(x1, x2, cos, sin):
    return x1 * cos - x2 * sin, x1 * sin + x2 * cos


def _mla_proj_kernel(x_ref, pos_ref, g_ref, w_in_ref, w_pe_ref, gq_lat_ref, gkv_lat_ref,
                     w_uq_ref, w_ukv_ref, gqn_ref, gkn_ref, inv_freq_ref,
                     qt_ref, k_ref, vt_ref):
    tm = x_ref.shape[0]
    xn = _rms(x_ref[...], g_ref[...]).astype(BF16)
    z = jnp.dot(xn, w_in_ref[...], preferred_element_type=F32)
    cq = _rms(z[:, :MLA_Q_RANK], gq_lat_ref[...]).astype(BF16)
    ckv = _rms(z[:, MLA_Q_RANK:], gkv_lat_ref[...]).astype(BF16)
    kpe_t = _dot_nt(w_pe_ref[...], xn)

    ang = pos_ref[...].astype(F32) * inv_freq_ref[...]
    cos = jnp.cos(ang)
    sin = jnp.sin(ang)
    half = MLA_ROPE // 2
    q_scale = (MLA_QK ** -0.5) * LOG2E
    gqn = gqn_ref[...] * q_scale
    gkn = gkn_ref[...]
    kpe_sq = jnp.sum(kpe_t * kpe_t, axis=0, keepdims=True)
    ones_rows = _ones_row_group(tm).astype(BF16)
    zeros_tail = jnp.zeros((QK_PAD - MLA_QK, tm), F32)

    kv_dim = MLA_NOPE + HEAD_V
    for h in range(HEADS):
        if h % HEAD_GROUP == 0:
            g0 = h
            qt_all = _dot_nt(w_uq_ref[g0 * MLA_QK:(g0 + HEAD_GROUP) * MLA_QK, :], cq)
            kvt_all = _dot_nt(w_ukv_ref[g0 * kv_dim:(g0 + HEAD_GROUP) * kv_dim, :], ckv)
        hl = h - g0
        qh = qt_all[hl * MLA_QK:(hl + 1) * MLA_QK]
        qn = qh * lax.rsqrt(jnp.mean(qh * qh, axis=0, keepdims=True) + EPS) * gqn
        q1, q2 = _rope_rows(qn[MLA_NOPE:MLA_NOPE + half], qn[MLA_NOPE + half:], cos, sin)
        qt_ref[h, 0:MLA_NOPE, :] = qn[:MLA_NOPE].astype(BF16)
        qt_ref[h, MLA_NOPE:MLA_NOPE + half, :] = q1.astype(BF16)
        qt_ref[h, MLA_NOPE + half:MLA_QK, :] = q2.astype(BF16)
        qt_ref[h, MLA_QK:QK_PAD, :] = zeros_tail.astype(BF16)

        base = hl * kv_dim
        kn = kvt_all[base:base + MLA_NOPE]
        ms = (jnp.sum(kn * kn, axis=0, keepdims=True) + kpe_sq) * (1.0 / MLA_QK)
        r = lax.rsqrt(ms + EPS)
        kn = kn * r * gkn[:MLA_NOPE]
        kp = kpe_t * r * gkn[MLA_NOPE:]
        k1, k2 = _rope_rows(kp[:half], kp[half:], cos, sin)
        kt = jnp.concatenate([kn, k1, k2, zeros_tail], axis=0)
        k_ref[h] = kt.T.astype(BF16)

        vt_ref[h, 0:HEAD_V, :] = kvt_all[base + MLA_NOPE:base + MLA_NOPE + HEAD_V].astype(BF16)
        vt_ref[h, HEAD_V:V_ROWS, :] = ones_rows


def _qkv_out(b, s):
    specs = [
        pl.BlockSpec((None, HEADS, QK_PAD, PROJ_TILE), lambda bi, i: (bi, 0, 0, i)),
        pl.BlockSpec((None, HEADS, PROJ_TILE, QK_PAD), lambda bi, i: (bi, 0, i, 0)),
        pl.BlockSpec((None, HEADS, V_ROWS, PROJ_TILE), lambda bi, i: (bi, 0, 0, i)),
    ]
    shapes = [
        jax.ShapeDtypeStruct((b, HEADS, QK_PAD, s), BF16),
        jax.ShapeDtypeStruct((b, HEADS, s, QK_PAD), BF16),
        jax.ShapeDtypeStruct((b, HEADS, V_ROWS, s), BF16),
    ]
    return specs, shapes


def _mla_proj(h, pos3, g, w_in, w_pe_t, gq_lat, gkv_lat, w_uq_t, w_ukv_t, gqn, gkn, inv_freq):
    b, s, _ = h.shape
    out_specs, out_shapes = _qkv_out(b, s)
    return pl.pallas_call(
        _mla_proj_kernel,
        grid=(b, s // PROJ_TILE),
        in_specs=[
            pl.BlockSpec((None, PROJ_TILE, D_MODEL), lambda bi, i: (bi, i, 0)),
            pl.BlockSpec((None, 1, PROJ_TILE), lambda bi, i: (bi, 0, i)),
            _resident(g.shape), _resident(w_in.shape), _resident(w_pe_t.shape),
            _resident(gq_lat.shape), _resident(gkv_lat.shape),
            _resident(w_uq_t.shape), _resident(w_ukv_t.shape),
            _resident(gqn.shape), _resident(gkn.shape), _resident(inv_freq.shape),
        ],
        out_specs=out_specs,
        out_shape=out_shapes,
        compiler_params=pltpu.CompilerParams(
            dimension_semantics=("parallel", "parallel"), vmem_limit_bytes=VMEM_LIMIT),
        name="mla_proj",
    )(h, pos3, g, w_in, w_pe_t, gq_lat, gkv_lat, w_uq_t, w_ukv_t, gqn, gkn, inv_freq)


def _split3(c):
    hi = c.astype(BF16).astype(F32)
    mid = (c - hi).astype(BF16).astype(F32)
    lo = (c - hi - mid).astype(BF16).astype(F32)
    return hi, mid, lo


def _fox_proj_kernel(x_ref, g_ref, w_ref, bf_ref, gqn_ref, gkn_ref,
                     qt_ref, k_ref, vt_ref, c_ref, carry_ref):
    tm = x_ref.shape[0]
    hd = HEADS * FOX_HEAD_DIM

    @pl.when(pl.program_id(1) == 0)
    def _():
        carry_ref[...] = jnp.zeros_like(carry_ref)

    xn = _rms(x_ref[...], g_ref[...]).astype(BF16)
    f_t = _dot_nt(w_ref[3 * hd:3 * hd + HEADS, :], xn) + bf_ref[...]

    logf = (jnp.minimum(f_t, 0.0) - jnp.log1p(jnp.exp(-jnp.abs(f_t)))) * LOG2E
    hi, mid, lo = _split3(logf)
    parts = jnp.concatenate([hi, mid, lo], axis=0).astype(BF16)
    src = lax.broadcasted_iota(jnp.int32, (tm, tm), 0)
    dst = lax.broadcasted_iota(jnp.int32, (tm, tm), 1)
    upper = jnp.where(src <= dst, 1.0, 0.0).astype(BF16)
    sums = jnp.dot(parts, upper, preferred_element_type=F32)
    local = sums[2 * HEADS:] + sums[HEADS:2 * HEADS] + sums[:HEADS]
    c = local + carry_ref[:, 0:1]
    carry_ref[...] = jnp.broadcast_to(c[:, tm - 1:tm], carry_ref.shape)
    c_ref[...] = c

    q_scale = (FOX_HEAD_DIM ** -0.5) * LOG2E
    gqn = gqn_ref[...] * q_scale
    gkn = gkn_ref[...]
    row = lax.broadcasted_iota(jnp.int32, (BF16_SUBLANES, tm), 0)
    ones_rows = _ones_row_group(tm).astype(BF16)
    zeros_tail = jnp.zeros((QK_PAD - FOX_HEAD_DIM - BF16_SUBLANES, tm), F32)

    def bias_rows(first, second):
        out = jnp.zeros((BF16_SUBLANES, tm), F32)
        for j in range(3):
            out = jnp.where(row == j, first[j], out)
            out = jnp.where(row == 8 + j, second[j], out)
        return out

    one3 = (1.0, 1.0, 1.0)
    gw = HEAD_GROUP * FOX_HEAD_DIM
    for h in range(HEADS):
        if h % HEAD_GROUP == 0:
            g0 = h
            r0 = g0 * FOX_HEAD_DIM
            qt_all = _dot_nt(w_ref[r0:r0 + gw, :], xn)
            kt_all = _dot_nt(w_ref[hd + r0:hd + r0 + gw, :], xn)
            vt_all = _dot_nt(w_ref[2 * hd + r0:2 * hd + r0 + gw, :], xn)
        sl = slice((h - g0) * FOX_HEAD_DIM, (h - g0 + 1) * FOX_HEAD_DIM)
        c3 = _split3(c[h:h + 1])
        qh = qt_all[sl]
        qn = qh * lax.rsqrt(jnp.mean(qh * qh, axis=0, keepdims=True) + EPS) * gqn
        qt_ref[h, 0:FOX_HEAD_DIM, :] = qn.astype(BF16)
        qt_ref[h, FOX_HEAD_DIM:FOX_HEAD_DIM + BF16_SUBLANES, :] = bias_rows(one3, c3).astype(BF16)
        qt_ref[h, FOX_HEAD_DIM + BF16_SUBLANES:QK_PAD, :] = zeros_tail.astype(BF16)

        kh = kt_all[sl]
        kn = kh * lax.rsqrt(jnp.mean(kh * kh, axis=0, keepdims=True) + EPS) * gkn
        neg3 = tuple(-p for p in c3)
        kt = jnp.concatenate([kn, bias_rows(neg3, one3), zeros_tail], axis=0)
        k_ref[h] = kt.T.astype(BF16)

        vt_ref[h, 0:HEAD_V, :] = vt_all[sl].astype(BF16)
        vt_ref[h, HEAD_V:V_ROWS, :] = ones_rows


def _fox_proj(h, g, w_t, b_f, gqn, gkn):
    b, s, _ = h.shape
    out_specs, out_shapes = _qkv_out(b, s)
    out_specs = out_specs + [pl.BlockSpec((None, HEADS, PROJ_TILE), lambda bi, i: (bi, 0, i))]
    out_shapes = out_shapes + [jax.ShapeDtypeStruct((b, HEADS, s), F32)]
    return pl.pallas_call(
        _fox_proj_kernel,
        grid=(b, s // PROJ_TILE),
        in_specs=[
            pl.BlockSpec((None, PROJ_TILE, D_MODEL), lambda bi, i: (bi, i, 0)),
            _resident(g.shape), _resident(w_t.shape), _resident(b_f.shape),
            _resident(gqn.shape), _resident(gkn.shape),
        ],
        out_specs=out_specs,
        out_shape=out_shapes,
        scratch_shapes=[pltpu.VMEM((HEADS, LANES), F32)],
        compiler_params=pltpu.CompilerParams(
            dimension_semantics=("arbitrary", "arbitrary"), vmem_limit_bytes=VMEM_LIMIT),
        name="fox_proj",
    )(h, g, w_t, b_f, gqn, gkn)


def _attn_kernel(cq_ref, cg_ref, thr_ref, qt_ref, k_ref, vt_ref, o_ref, acc_ref, m_ref, *bufs,
                 seq, tq, tk, cw):
    ncol = tq // cw
    nq = seq // tq
    all_chains = tuple(range(ncol))
    s_refs = bufs[:NSLOT]
    bm_refs = bufs[NSLOT:]
    bias_base = (pl.program_id(0) * pl.num_programs(1) + pl.program_id(1)) * nq

    def first_group(t):
        def dead(g):
            return jnp.logical_and(
                g < t - 1, cq_ref[bias_base + t] - cg_ref[bias_base + g] < -thr_ref[0])
        return lax.while_loop(dead, lambda g: g + 1, jnp.int32(0))

    pv_keys = min(PV_KEYS, tk)
    n_parts = tk // pv_keys
    full_plan = {c: tuple((r, False) for r in range(n_parts)) for c in all_chains}

    nd = tq // tk
    assert nd == NSLOT and MAIN_UNROLL % NSLOT == 0 and LAG < NSLOT

    def diag_plan(d):
        plan = {}
        for c in all_chains:
            parts = []
            for r in range(n_parts):
                key_lo = d * tk + r * pv_keys
                if key_lo <= (c + 1) * cw - 1:
                    parts.append((r, key_lo + pv_keys - 1 > c * cw))
            if parts:
                plan[c] = tuple(parts)
        return plan

    diag_plans = tuple(diag_plan(d) for d in range(nd))

    def stage_a_parts(q0, k0, slot, c, parts):
        cols = slice(c * cw, (c + 1) * cw)
        qt = qt_ref[:, pl.ds(q0 + c * cw, cw)]
        bm = None
        for r, masked in parts:
            rows = slice(r * pv_keys, (r + 1) * pv_keys)
            kb = k_ref[pl.ds(k0 + r * pv_keys, pv_keys), :]
            s = jnp.dot(kb, qt, preferred_element_type=F32)
            if masked:
                kpos = k0 + r * pv_keys + lax.broadcasted_iota(jnp.int32, (pv_keys, cw), 0)
                qpos = q0 + c * cw + lax.broadcasted_iota(jnp.int32, (pv_keys, cw), 1)
                s = jnp.where(kpos <= qpos, s, MASK_VALUE)
            s_refs[slot][c, rows, :] = s
            pm = jnp.max(s, axis=0, keepdims=True)
            bm = pm if bm is None else jnp.maximum(bm, pm)
            if r == parts[-1][0]:
                bm_refs[slot][:, cols] = bm
            yield

    def stage_b_parts(k0, slot, c, parts):
        cols = slice(c * cw, (c + 1) * cw)
        m_old = m_ref[:, cols]
        m_new = jnp.maximum(m_old, bm_refs[slot][:, cols])
        alpha = jnp.exp2(m_old - m_new)
        pv = None
        for r, _ in parts:
            rows = slice(r * pv_keys, (r + 1) * pv_keys)
            p = jnp.exp2(s_refs[slot][c, rows, :] - m_new).astype(BF16)
            vb = vt_ref[:, pl.ds(k0 + r * pv_keys, pv_keys)]
            t = jnp.dot(vb, p, preferred_element_type=F32)
            pv = t if pv is None else pv + t
            if r == parts[-1][0]:
                acc_ref[c] = alpha * acc_ref[c] + pv
                m_ref[:, cols] = m_new
            yield

    def step(q0, k0, rslot, nxt_q0, nxt_k0, a_plan=full_plan, b_plan=full_plan):
        for c in all_chains:
            gens = []
            if a_plan is not None and c in a_plan:
                gens.append(stage_a_parts(nxt_q0, nxt_k0, (rslot + LAG) % NSLOT, c, a_plan[c]))
            if b_plan is not None and c in b_plan:
                gens.append(stage_b_parts(k0, rslot, c, b_plan[c]))
            while gens:
                gens = [g for g in gens if next(g, gens) is not gens]

    def q_tile(qi, carry):
        q0 = pl.multiple_of(qi * tq, tq)
        acc_ref[...] = jnp.zeros_like(acc_ref)
        m_ref[...] = jnp.full_like(m_ref, M_INIT)

        def run_steps(k0, count):
            for j in range(count):
                step(q0, k0 + j * tk, j % NSLOT, q0, k0 + (j + LAG) * tk)

        g_first = first_group(qi)
        k_first = pl.multiple_of(g_first * tq, tq)
        n_main = NSLOT * (jnp.maximum(qi - 1, 0) - g_first)
        rem = n_main & (MAIN_UNROLL - 1)

        def main(t, carry2):
            run_steps(pl.multiple_of(k_first + t * MAIN_UNROLL * tk, tq), MAIN_UNROLL)
            return carry2

        lax.fori_loop(0, lax.shift_right_logical(n_main, MAIN_UNROLL.bit_length() - 1), main, 0)
        k_last = q0 - NSLOT * tk
        sub = MAIN_UNROLL // 2
        while sub >= NSLOT:
            k_sub = pl.multiple_of(k_last - ((rem & (sub - 1)) + sub) * tk, NSLOT * tk)

            @pl.when((rem & sub) != 0)
            def _(k_sub=k_sub, sub=sub):
                run_steps(k_sub, sub)

            sub //= 2

        nxt_qi = jnp.minimum(qi + 1, nq - 1)
        nxt_q0 = pl.multiple_of(nxt_qi * tq, tq)
        nxt_k_first = pl.multiple_of(first_group(nxt_qi) * tq, tq)

        def last_below():
            k0 = pl.multiple_of(k_last, NSLOT * tk)
            for j in range(NSLOT):
                tgt = j + LAG
                if tgt < NSLOT:
                    step(q0, k0 + j * tk, j, q0, k0 + tgt * tk)
                else:
                    step(q0, k0 + j * tk, j, q0, q0 + (tgt - NSLOT) * tk,
                         a_plan=diag_plans[tgt - NSLOT])

        def diagonal():
            for d in range(nd):
                tgt = d + LAG
                if tgt < nd:
                    step(q0, q0 + d * tk, d % NSLOT, q0, q0 + tgt * tk,
                         a_plan=diag_plans[tgt], b_plan=diag_plans[d])
                else:
                    step(q0, q0 + d * tk, d % NSLOT, nxt_q0, nxt_k_first + (tgt - nd) * tk,
                         b_plan=diag_plans[d])

        @pl.when(qi >= 1)
        def _():
            last_below()
            diagonal()

        @pl.when(qi == 0)
        def _():
            diagonal()

        for c in all_chains:
            acc = acc_ref[c]
            o = acc[:HEAD_V] / acc[HEAD_V:HEAD_V + 1]
            o_ref[:, pl.ds(q0 + c * cw, cw)] = o.astype(o_ref.dtype)
        return carry

    for j in range(LAG):
        step(0, 0, (j - LAG) % NSLOT, 0, j * tk, a_plan=diag_plans[j], b_plan=None)
    lax.fori_loop(0, nq, q_tile, 0)


def _attn_fixed_kernel(cq_ref, cg_ref, thr_ref, qt_ref, k_ref, vt_ref, o_ref, acc_ref,
                       *, seq, tq, tk, cw):
    ncol = tq // cw
    nq = seq // tq
    nd = tq // tk
    all_chains = tuple(range(ncol))
    bias_base = (pl.program_id(0) * pl.num_programs(1) + pl.program_id(1)) * nq
    ref_max = thr_ref[1]

    def first_group(t):
        def dead(g):
            return jnp.logical_and(
                g < t - 1, cq_ref[bias_base + t] - cg_ref[bias_base + g] < -thr_ref[0])
        return lax.while_loop(dead, lambda g: g + 1, jnp.int32(0))

    def scores(q0, k0, c, masked):
        qt = qt_ref[:, pl.ds(q0 + c * cw, cw)]
        s = jnp.dot(k_ref[pl.ds(k0, tk), :], qt, preferred_element_type=F32)
        if masked:
            kpos = k0 + lax.broadcasted_iota(jnp.int32, (tk, cw), 0)
            qpos = q0 + c * cw + lax.broadcasted_iota(jnp.int32, (tk, cw), 1)
            s = jnp.where(kpos <= qpos, s, MASK_VALUE)
        return s

    def weighted_values(k0, s):
        p = jnp.exp2(s - ref_max)
        pv = jnp.dot(vt_ref[0:HEAD_V, pl.ds(k0, tk)], p.astype(BF16), preferred_element_type=F32)
        return pv, jnp.sum(p, axis=0, keepdims=True)

    plain = {c: False for c in all_chains}

    def run_list(q0, blocks):
        tot = [None] * ncol
        pending = [{c: scores(q0, k0, c, m) for c, m in vis.items()}
                   for k0, vis in blocks[:SCORES_AHEAD]]
        for j, (k0, vis) in enumerate(blocks):
            nxt_k0, nxt_vis = (blocks[j + SCORES_AHEAD] if j + SCORES_AHEAD < len(blocks)
                               else (None, {}))
            s_cur = pending.pop(0)
            s_nxt = {}
            for c in all_chains:
                if c in nxt_vis:
                    s_nxt[c] = scores(q0, nxt_k0, c, nxt_vis[c])
                if c in vis:
                    t, l = weighted_values(k0, s_cur[c])
                    tot[c] = (t, l) if tot[c] is None else (tot[c][0] + t, tot[c][1] + l)
            pending.append(s_nxt)
        return tot

    def run_plain(q0, k0, count):
        tot = run_list(q0, [(k0 + j * tk, plain) for j in range(count)])
        for c in all_chains:
            acc_ref[c, 0:HEAD_V] = acc_ref[c, 0:HEAD_V] + tot[c][0]
            acc_ref[c, HEAD_V:HEAD_V + 1] = acc_ref[c, HEAD_V:HEAD_V + 1] + tot[c][1]

    def q_tile(qi, carry):
        q0 = pl.multiple_of(qi * tq, tq)
        acc_ref[...] = jnp.zeros_like(acc_ref)
        g_first = first_group(qi)
        k_first = pl.multiple_of(g_first * tq, tq)
        n_main = nd * (jnp.maximum(qi - 1, 0) - g_first)
        rem = n_main & (MAIN_UNROLL - 1)

        def main(t, carry2):
            run_plain(q0, pl.multiple_of(k_first + t * MAIN_UNROLL * tk, tq), MAIN_UNROLL)
            return carry2

        lax.fori_loop(0, lax.shift_right_logical(n_main, MAIN_UNROLL.bit_length() - 1), main, 0)
        k_last = q0 - nd * tk
        sub = MAIN_UNROLL // 2
        while sub >= 2 * nd:
            k_sub = pl.multiple_of(k_last - ((rem & (sub - 1)) + sub) * tk, tq)

            @pl.when((rem & sub) != 0)
            def _(k_sub=k_sub, sub=sub):
                run_plain(q0, k_sub, sub)

            sub //= 2

        diagonal = [(q0 + d * tk, {c: c == d for c in range(d, ncol)}) for d in range(nd)]

        def finish(groups_below):
            k0 = pl.multiple_of(q0 - groups_below * nd * tk, tq)
            below = [(k0 + j * tk, plain) for j in range(groups_below * nd)]
            tot = run_list(q0, below + diagonal)
            for c in all_chains:
                o = ((acc_ref[c, 0:HEAD_V] + tot[c][0])
                     / (acc_ref[c, HEAD_V:HEAD_V + 1] + tot[c][1]))
                o_ref[:, pl.ds(q0 + c * cw, cw)] = o.astype(o_ref.dtype)

        two_groups = (rem & nd) != 0
        for groups_below, when in ((2, jnp.logical_and(qi >= 1, two_groups)),
                                   (1, jnp.logical_and(qi >= 1, jnp.logical_not(two_groups))),
                                   (0, qi == 0)):
            pl.when(when)(functools.partial(finish, groups_below))

        return carry

    lax.fori_loop(0, nq, q_tile, 0)


def _attention(qt, k, vt, bound, bias=None):
    b, heads, _, s = qt.shape
    tq = min(ATT_TQ, s)
    tk = min(ATT_TK, tq // NSLOT)
    assert tk == ATT_CW
    nq = s // tq
    if bias is None:
        cq = cg = jnp.zeros((b * heads * nq,), F32)
    else:
        cq = bias[:, :, ::tq].reshape(-1)
        cg = bias[:, :, tq - 1::tq].reshape(-1)
    bound = jnp.asarray(bound, F32)
    thr = jnp.stack([2.0 * bound + ZERO_WEIGHT_LOG2, bound])
    in_specs = [
        pl.BlockSpec((None, None, QK_PAD, s), lambda bi, hi, *_: (bi, hi, 0, 0)),
        pl.BlockSpec((None, None, s, QK_PAD), lambda bi, hi, *_: (bi, hi, 0, 0)),
        pl.BlockSpec((None, None, V_ROWS, s), lambda bi, hi, *_: (bi, hi, 0, 0)),
    ]
    out_specs = pl.BlockSpec((None, HEAD_V, s), lambda bi, hi, *_: (bi, hi, 0))
    acc_scratch = pltpu.VMEM((tq // ATT_CW, V_ROWS, ATT_CW), F32)
    common = dict(
        out_shape=jax.ShapeDtypeStruct((b, heads * HEAD_V, s), BF16),
        compiler_params=pltpu.CompilerParams(
            dimension_semantics=("parallel", "parallel"), vmem_limit_bytes=VMEM_LIMIT),
    )

    def online(*args):
        return pl.pallas_call(
            functools.partial(_attn_kernel, seq=s, tq=tq, tk=tk, cw=ATT_CW),
            grid_spec=pltpu.PrefetchScalarGridSpec(
                num_scalar_prefetch=3, grid=(b, heads), in_specs=in_specs, out_specs=out_specs,
                scratch_shapes=([acc_scratch, pltpu.VMEM((1, tq), F32)]
                                + [pltpu.VMEM((tq // ATT_CW, tk, ATT_CW), F32)] * NSLOT
                                + [pltpu.VMEM((1, tq), F32)] * NSLOT),
            ),
            name="attention", **common)(*args)

    def fixed(*args):
        return pl.pallas_call(
            functools.partial(_attn_fixed_kernel, seq=s, tq=tq, tk=tk, cw=ATT_CW),
            grid_spec=pltpu.PrefetchScalarGridSpec(
                num_scalar_prefetch=3, grid=(b, heads), in_specs=in_specs, out_specs=out_specs,
                scratch_shapes=[acc_scratch],
            ),
            name="attention_fixed", **common)(*args)

    return lax.cond(bound <= FIXED_REF_MAX_BOUND, fixed, online, cq, cg, thr, qt, k, vt)


def _score_bound(g_q, g_k, dk):
    return (jnp.max(jnp.abs(g_q)) * jnp.max(jnp.abs(g_k)) * (dk ** 0.5) * LOG2E
            * BF16_NORM_SLACK).astype(F32)


def kernel(x, p, positions, g_ffn1, g_mix, g_ffn2, g_ple, ffn1_w_in, ffn1_w_out, ffn2_w_in, ffn2_w_out, ple_w_proj, ple_w_gate, mla_w_in, mla_g_q_lat, mla_w_uq, mla_g_kv_lat, mla_w_ukv, mla_g_qn, mla_g_kn, mla_w_o, fox_w_in, fox_b_f, fox_g_qn, fox_g_kn, fox_w_o):
    b, s, d = x.shape
    depth = g_ffn1.shape[0]
    assert d == D_MODEL and s % PROJ_TILE == 0 and s % ROW_TILE == 0
    n = b * s
    bf = lambda w: w.astype(BF16)
    row = lambda g: g.reshape(1, -1)
    col = lambda g: g.reshape(-1, 1)

    pos3 = positions.reshape(b, 1, s)
    half = MLA_ROPE // 2
    inv_freq = (ROPE_THETA ** (-jnp.arange(0, MLA_ROPE, 2, dtype=F32) / MLA_ROPE)).reshape(half, 1)
    lat = MLA_Q_RANK + MLA_KV_RANK

    h = x
    for i in range(depth):
        j = i // N_MIXERS
        h = _ffn(h.reshape(n, d), row(g_ffn1[i]), bf(ffn1_w_in[i]), bf(ffn1_w_out[i])).reshape(b, s, d)
        if i % N_MIXERS == 0:
            qt, k, vt = _mla_proj(
                h, pos3, row(g_mix[i]), bf(mla_w_in[j][:, :lat]), bf(mla_w_in[j][:, lat:].T),
                row(mla_g_q_lat[j]), row(mla_g_kv_lat[j]), bf(mla_w_uq[j].T), bf(mla_w_ukv[j].T),
                col(mla_g_qn[j]), col(mla_g_kn[j]), inv_freq)
            w_o = mla_w_o[j]
            ot = _attention(qt, k, vt, _score_bound(mla_g_qn[j], mla_g_kn[j], MLA_QK))
        else:
            qt, k, vt, c = _fox_proj(h, row(g_mix[i]), bf(fox_w_in[j].T), col(fox_b_f[j]),
                                     col(fox_g_qn[j]), col(fox_g_kn[j]))
            w_o = fox_w_o[j]
            ot = _attention(qt, k, vt, _score_bound(fox_g_qn[j], fox_g_kn[j], FOX_HEAD_DIM), bias=c)
        h = _post(h.reshape(n, d), ot, bf(w_o), row(g_ffn2[i]), bf(ffn2_w_in[i]), bf(ffn2_w_out[i]),
                  p.reshape(depth, n, PLE_DIM), i, row(g_ple[i]), bf(ple_w_gate[i]),
                  bf(ple_w_proj[i])).reshape(b, s, d)
    return h
```

```python
import functools
import math

import jax
import jax.numpy as jnp
from jax import lax
from jax.experimental import pallas as pl
from jax.experimental.pallas import tpu as pltpu

F32 = jnp.float32
BF16 = jnp.bfloat16

D_MODEL = 1024
D_FF = 2816
PLE_DIM = 256
N_MIXERS = 2
FFN_HALF = 0.5
HEADS = 16
MLA_Q_RANK = 512
MLA_KV_RANK = 256
MLA_NOPE = 64
MLA_ROPE = 32
MLA_QK = MLA_NOPE + MLA_ROPE
HEAD_V = 64
FOX_HEAD_DIM = 64
ROPE_THETA = 10000.0
EPS = 1e-6
LOG2E = math.log2(math.e)

LANES = 128
MXU_DIM = 256
BF16_SUBLANES = 16
QK_PAD = 128
V_ROWS = HEAD_V + BF16_SUBLANES
VMEM_LIMIT = 52 * 1024 * 1024

ROW_TILE = 512
PROJ_TILE = 512
HEAD_GROUP = 4
ATT_TQ = 1024
ATT_CW = MXU_DIM
ATT_TK = MXU_DIM
MAIN_UNROLL = 16
NSLOT = 4
LAG = 2
SCORES_AHEAD = 2
PV_KEYS = MXU_DIM
MASK_VALUE = -jnp.inf
M_INIT = -1e30
ZERO_WEIGHT_LOG2 = 170.0
BF16_NORM_SLACK = 1.02
FIXED_REF_MAX_BOUND = 56.0


def _rms(x, g):
    return x * lax.rsqrt(jnp.mean(x * x, axis=-1, keepdims=True) + EPS) * g


def _resident(shape):
    return pl.BlockSpec(shape, lambda *_: (0,) * len(shape), pipeline_mode=pl.Buffered(1))


def _swiglu_half_step(x, g, w_in_ref, w_out_ref):
    xn = _rms(x, g).astype(BF16)
    gu = jnp.dot(xn, w_in_ref[...], preferred_element_type=F32)
    gate = gu[:, :D_FF]
    up = gu[:, D_FF:]
    act = (gate * jax.nn.sigmoid(gate) * up).astype(BF16)
    return x + FFN_HALF * jnp.dot(act, w_out_ref[...], preferred_element_type=F32)


def _ffn_kernel(x_ref, g_ref, w_in_ref, w_out_ref, o_ref):
    o_ref[...] = _swiglu_half_step(x_ref[...], g_ref[...], w_in_ref, w_out_ref)


def _ffn(h2d, g, w_in, w_out):
    n = h2d.shape[0]
    return pl.pallas_call(
        _ffn_kernel,
        grid=(n // ROW_TILE,),
        in_specs=[
            pl.BlockSpec((ROW_TILE, D_MODEL), lambda i: (i, 0)),
            _resident((1, D_MODEL)),
            _resident((D_MODEL, 2 * D_FF)),
            _resident((D_FF, D_MODEL)),
        ],
        out_specs=pl.BlockSpec((ROW_TILE, D_MODEL), lambda i: (i, 0)),
        out_shape=jax.ShapeDtypeStruct((n, D_MODEL), F32),
        compiler_params=pltpu.CompilerParams(
            dimension_semantics=("parallel",), vmem_limit_bytes=VMEM_LIMIT),
        name="ffn",
    )(h2d, g, w_in, w_out)


def _post_kernel(x_ref, ot_ref, wo_ref, g2_ref, w_in_ref, w_out_ref,
                 p_ref, gp_ref, wg_ref, wp_ref, o_ref):
    x = x_ref[...] + lax.dot_general(ot_ref[...], wo_ref[...], (((0,), (0,)), ((), ())),
                                     preferred_element_type=F32)
    x = _swiglu_half_step(x, g2_ref[...], w_in_ref, w_out_ref)
    xn = _rms(x, gp_ref[...]).astype(BF16)
    gate = jax.nn.sigmoid(jnp.dot(xn, wg_ref[...], preferred_element_type=F32))
    pp = jnp.dot(p_ref[...].astype(BF16), wp_ref[...], preferred_element_type=F32)
    o_ref[...] = x + gate * pp


def _post(h2d, ot, w_o, g2, w_in, w_out, p3d, layer, g_ple, w_gate, w_proj):
    n = h2d.shape[0]
    hd, s = ot.shape[1], ot.shape[2]
    tiles = s // ROW_TILE
    rows = pl.BlockSpec((ROW_TILE, D_MODEL), lambda i: (i, 0))
    return pl.pallas_call(
        _post_kernel,
        grid=(n // ROW_TILE,),
        in_specs=[
            rows,
            pl.BlockSpec((None, hd, ROW_TILE), lambda i: (i // tiles, 0, i % tiles)),
            _resident((hd, D_MODEL)),
            _resident((1, D_MODEL)),
            _resident((D_MODEL, 2 * D_FF)),
            _resident((D_FF, D_MODEL)),
            pl.BlockSpec((None, ROW_TILE, PLE_DIM), lambda i: (layer, i, 0)),
            _resident((1, D_MODEL)),
            _resident((D_MODEL, D_MODEL)),
            _resident((PLE_DIM, D_MODEL)),
        ],
        out_specs=rows,
        out_shape=jax.ShapeDtypeStruct((n, D_MODEL), F32),
        compiler_params=pltpu.CompilerParams(
            dimension_semantics=("parallel",), vmem_limit_bytes=VMEM_LIMIT),
        name="post",
    )(h2d, ot, w_o, g2, w_in, w_out, p3d, g_ple, w_gate, w_proj)


def _dot_nt(w, x):
    return lax.dot_general(w, x, (((1,), (1,)), ((), ())), preferred_element_type=F32)


def _ones_row_group(width):
    row = lax.broadcasted_iota(jnp.int32, (BF16_SUBLANES, width), 0)
    return jnp.where(row == 0, 1.0, 0.0).astype(F32)


def _rope_rows(x1, x2, cos, sin):
    return x1 * cos - x2 * sin, x1 * sin + x2 * cos


def _mla_proj_kernel(x_ref, pos_ref, g_ref, w_in_ref, w_pe_ref, gq_lat_ref, gkv_lat_ref,
                     w_uq_ref, w_ukv_ref, gqn_ref, gkn_ref, inv_freq_ref,
                     qt_ref, k_ref, vt_ref):
    tm = x_ref.shape[0]
    xn = _rms(x_ref[...], g_ref[...]).astype(BF16)
    z = jnp.dot(xn, w_in_ref[...], preferred_element_type=F32)
    cq = _rms(z[:, :MLA_Q_RANK], gq_lat_ref[...]).astype(BF16)
    ckv = _rms(z[:, MLA_Q_RANK:], gkv_lat_ref[...]).astype(BF16)
    kpe_t = _dot_nt(w_pe_ref[...], xn)

    ang = pos_ref[...].astype(F32) * inv_freq_ref[...]
    cos = jnp.cos(ang)
    sin = jnp.sin(ang)
    half = MLA_ROPE // 2
    q_scale = (MLA_QK ** -0.5) * LOG2E
    gqn = gqn_ref[...] * q_scale
    gkn = gkn_ref[...]
    kpe_sq = jnp.sum(kpe_t * kpe_t, axis=0, keepdims=True)
    ones_rows = _ones_row_group(tm).astype(BF16)
    zeros_tail = jnp.zeros((QK_PAD - MLA_QK, tm), F32)

    kv_dim = MLA_NOPE + HEAD_V
    for h in range(HEADS):
        if h % HEAD_GROUP == 0:
            g0 = h
            qt_all = _dot_nt(w_uq_ref[g0 * MLA_QK:(g0 + HEAD_GROUP) * MLA_QK, :], cq)
            kvt_all = _dot_nt(w_ukv_ref[g0 * kv_dim:(g0 + HEAD_GROUP) * kv_dim, :], ckv)
        hl = h - g0
        qh = qt_all[hl * MLA_QK:(hl + 1) * MLA_QK]
        qn = qh * lax.rsqrt(jnp.mean(qh * qh, axis=0, keepdims=True) + EPS) * gqn
        q1, q2 = _rope_rows(qn[MLA_NOPE:MLA_NOPE + half], qn[MLA_NOPE + half:], cos, sin)
        qt_ref[h, 0:MLA_NOPE, :] = qn[:MLA_NOPE].astype(BF16)
        qt_ref[h, MLA_NOPE:MLA_NOPE + half, :] = q1.astype(BF16)
        qt_ref[h, MLA_NOPE + half:MLA_QK, :] = q2.astype(BF16)
        qt_ref[h, MLA_QK:QK_PAD, :] = zeros_tail.astype(BF16)

        base = hl * kv_dim
        kn = kvt_all[base:base + MLA_NOPE]
        ms = (jnp.sum(kn * kn, axis=0, keepdims=True) + kpe_sq) * (1.0 / MLA_QK)
        r = lax.rsqrt(ms + EPS)
        kn = kn * r * gkn[:MLA_NOPE]
        kp = kpe_t * r * gkn[MLA_NOPE:]
        k1, k2 = _rope_rows(kp[:half], kp[half:], cos, sin)
        kt = jnp.concatenate([kn, k1, k2, zeros_tail], axis=0)
        k_ref[h] = kt.T.astype(BF16)

        vt_ref[h, 0:HEAD_V, :] = kvt_all[base + MLA_NOPE:base + MLA_NOPE + HEAD_V].astype(BF16)
        vt_ref[h, HEAD_V:V_ROWS, :] = ones_rows


def _qkv_out(b, s):
    specs = [
        pl.BlockSpec((None, HEADS, QK_PAD, PROJ_TILE), lambda bi, i: (bi, 0, 0, i)),
        pl.BlockSpec((None, HEADS, PROJ_TILE, QK_PAD), lambda bi, i: (bi, 0, i, 0)),
        pl.BlockSpec((None, HEADS, V_ROWS, PROJ_TILE), lambda bi, i: (bi, 0, 0, i)),
    ]
    shapes = [
        jax.ShapeDtypeStruct((b, HEADS, QK_PAD, s), BF16),
        jax.ShapeDtypeStruct((b, HEADS, s, QK_PAD), BF16),
        jax.ShapeDtypeStruct((b, HEADS, V_ROWS, s), BF16),
    ]
    return specs, shapes


def _mla_proj(h, pos3, g, w_in, w_pe_t, gq_lat, gkv_lat, w_uq_t, w_ukv_t, gqn, gkn, inv_freq):
    b, s, _ = h.shape
    out_specs, out_shapes = _qkv_out(b, s)
    return pl.pallas_call(
        _mla_proj_kernel,
        grid=(b, s // PROJ_TILE),
        in_specs=[
            pl.BlockSpec((None, PROJ_TILE, D_MODEL), lambda bi, i: (bi, i, 0)),
            pl.BlockSpec((None, 1, PROJ_TILE), lambda bi, i: (bi, 0, i)),
            _resident(g.shape), _resident(w_in.shape), _resident(w_pe_t.shape),
            _resident(gq_lat.shape), _resident(gkv_lat.shape),
            _resident(w_uq_t.shape), _resident(w_ukv_t.shape),
            _resident(gqn.shape), _resident(gkn.shape), _resident(inv_freq.shape),
        ],
        out_specs=out_specs,
        out_shape=out_shapes,
        compiler_params=pltpu.CompilerParams(
            dimension_semantics=("parallel", "parallel"), vmem_limit_bytes=VMEM_LIMIT),
        name="mla_proj",
    )(h, pos3, g, w_in, w_pe_t, gq_lat, gkv_lat, w_uq_t, w_ukv_t, gqn, gkn, inv_freq)


def _split3(c):
    hi = c.astype(BF16).astype(F32)
    mid = (c - hi).astype(BF16).astype(F32)
    lo = (c - hi - mid).astype(BF16).astype(F32)
    return hi, mid, lo


def _fox_proj_kernel(x_ref, g_ref, w_ref, bf_ref, gqn_ref, gkn_ref,
                     qt_ref, k_ref, vt_ref, c_ref, carry_ref):
    tm = x_ref.shape[0]
    hd = HEADS * FOX_HEAD_DIM

    @pl.when(pl.program_id(1) == 0)
    def _():
        carry_ref[...] = jnp.zeros_like(carry_ref)

    xn = _rms(x_ref[...], g_ref[...]).astype(BF16)
    f_t = _dot_nt(w_ref[3 * hd:3 * hd + HEADS, :], xn) + bf_ref[...]

    logf = (jnp.minimum(f_t, 0.0) - jnp.log1p(jnp.exp(-jnp.abs(f_t)))) * LOG2E
    hi, mid, lo = _split3(logf)
    parts = jnp.concatenate([hi, mid, lo], axis=0).astype(BF16)
    src = lax.broadcasted_iota(jnp.int32, (tm, tm), 0)
    dst = lax.broadcasted_iota(jnp.int32, (tm, tm), 1)
    upper = jnp.where(src <= dst, 1.0, 0.0).astype(BF16)
    sums = jnp.dot(parts, upper, preferred_element_type=F32)
    local = sums[2 * HEADS:] + sums[HEADS:2 * HEADS] + sums[:HEADS]
    c = local + carry_ref[:, 0:1]
    carry_ref[...] = jnp.broadcast_to(c[:, tm - 1:tm], carry_ref.shape)
    c_ref[...] = c

    q_scale = (FOX_HEAD_DIM ** -0.5) * LOG2E
    gqn = gqn_ref[...] * q_scale
    gkn = gkn_ref[...]
    row = lax.broadcasted_iota(jnp.int32, (BF16_SUBLANES, tm), 0)
    ones_rows = _ones_row_group(tm).astype(BF16)
    zeros_tail = jnp.zeros((QK_PAD - FOX_HEAD_DIM - BF16_SUBLANES, tm), F32)

    def bias_rows(first, second):
        out = jnp.zeros((BF16_SUBLANES, tm), F32)
        for j in range(3):
            out = jnp.where(row == j, first[j], out)
            out = jnp.where(row == 8 + j, second[j], out)
        return out

    one3 = (1.0, 1.0, 1.0)
    gw = HEAD_GROUP * FOX_HEAD_DIM
    for h in range(HEADS):
        if h % HEAD_GROUP == 0:
            g0 = h
            r0 = g0 * FOX_HEAD_DIM
            qt_all = _dot_nt(w_ref[r0:r0 + gw, :], xn)
            kt_all = _dot_nt(w_ref[hd + r0:hd + r0 + gw, :], xn)
            vt_all = _dot_nt(w_ref[2 * hd + r0:2 * hd + r0 + gw, :], xn)
        sl = slice((h - g0) * FOX_HEAD_DIM, (h - g0 + 1) * FOX_HEAD_DIM)
        c3 = _split3(c[h:h + 1])
        qh = qt_all[sl]
        qn = qh * lax.rsqrt(jnp.mean(qh * qh, axis=0, keepdims=True) + EPS) * gqn
        qt_ref[h, 0:FOX_HEAD_DIM, :] = qn.astype(BF16)
        qt_ref[h, FOX_HEAD_DIM:FOX_HEAD_DIM + BF16_SUBLANES, :] = bias_rows(one3, c3).astype(BF16)
        qt_ref[h, FOX_HEAD_DIM + BF16_SUBLANES:QK_PAD, :] = zeros_tail.astype(BF16)

        kh = kt_all[sl]
        kn = kh * lax.rsqrt(jnp.mean(kh * kh, axis=0, keepdims=True) + EPS) * gkn
        neg3 = tuple(-p for p in c3)
        kt = jnp.concatenate([kn, bias_rows(neg3, one3), zeros_tail], axis=0)
        k_ref[h] = kt.T.astype(BF16)

        vt_ref[h, 0:HEAD_V, :] = vt_all[sl].astype(BF16)
        vt_ref[h, HEAD_V:V_ROWS, :] = ones_rows


def _fox_proj(h, g, w_t, b_f, gqn, gkn):
    b, s, _ = h.shape
    out_specs, out_shapes = _qkv_out(b, s)
    out_specs = out_specs + [pl.BlockSpec((None, HEADS, PROJ_TILE), lambda bi, i: (bi, 0, i))]
    out_shapes = out_shapes + [jax.ShapeDtypeStruct((b, HEADS, s), F32)]
    return pl.pallas_call(
        _fox_proj_kernel,
        grid=(b, s // PROJ_TILE),
        in_specs=[
            pl.BlockSpec((None, PROJ_TILE, D_MODEL), lambda bi, i: (bi, i, 0)),
            _resident(g.shape), _resident(w_t.shape), _resident(b_f.shape),
            _resident(gqn.shape), _resident(gkn.shape),
        ],
        out_specs=out_specs,
        out_shape=out_shapes,
        scratch_shapes=[pltpu.VMEM((HEADS, LANES), F32)],
        compiler_params=pltpu.CompilerParams(
            dimension_semantics=("arbitrary", "arbitrary"), vmem_limit_bytes=VMEM_LIMIT),
        name="fox_proj",
    )(h, g, w_t, b_f, gqn, gkn)


def _attn_kernel(cq_ref, cg_ref, thr_ref, qt_ref, k_ref, vt_ref, o_ref, acc_ref, m_ref, *bufs,
                 seq, tq, tk, cw):
    ncol = tq // cw
    nq = seq // tq
    all_chains = tuple(range(ncol))
    s_refs = bufs[:NSLOT]
    bm_refs = bufs[NSLOT:]
    bias_base = (pl.program_id(0) * pl.num_programs(1) + pl.program_id(1)) * nq

    def first_group(t):
        def dead(g):
            return jnp.logical_and(
                g < t - 1, cq_ref[bias_base + t] - cg_ref[bias_base + g] < -thr_ref[0])
        return lax.while_loop(dead, lambda g: g + 1, jnp.int32(0))

    pv_keys = min(PV_KEYS, tk)
    n_parts = tk // pv_keys
    full_plan = {c: tuple((r, False) for r in range(n_parts)) for c in all_chains}

    nd = tq // tk
    assert nd == NSLOT and MAIN_UNROLL % NSLOT == 0 and LAG < NSLOT

    def diag_plan(d):
        plan = {}
        for c in all_chains:
            parts = []
            for r in range(n_parts):
                key_lo = d * tk + r * pv_keys
                if key_lo <= (c + 1) * cw - 1:
                    parts.append((r, key_lo + pv_keys - 1 > c * cw))
            if parts:
                plan[c] = tuple(parts)
        return plan

    diag_plans = tuple(diag_plan(d) for d in range(nd))

    def stage_a_parts(q0, k0, slot, c, parts):
        cols = slice(c * cw, (c + 1) * cw)
        qt = qt_ref[:, pl.ds(q0 + c * cw, cw)]
        bm = None
        for r, masked in parts:
            rows = slice(r * pv_keys, (r + 1) * pv_keys)
            kb = k_ref[pl.ds(k0 + r * pv_keys, pv_keys), :]
            s = jnp.dot(kb, qt, preferred_element_type=F32)
            if masked:
                kpos = k0 + r * pv_keys + lax.broadcasted_iota(jnp.int32, (pv_keys, cw), 0)
                qpos = q0 + c * cw + lax.broadcasted_iota(jnp.int32, (pv_keys, cw), 1)
                s = jnp.where(kpos <= qpos, s, MASK_VALUE)
            s_refs[slot][c, rows, :] = s
            pm = jnp.max(s, axis=0, keepdims=True)
            bm = pm if bm is None else jnp.maximum(bm, pm)
            if r == parts[-1][0]:
                bm_refs[slot][:, cols] = bm
            yield

    def stage_b_parts(k0, slot, c, parts):
        cols = slice(c * cw, (c + 1) * cw)
        m_old = m_ref[:, cols]
        m_new = jnp.maximum(m_old, bm_refs[slot][:, cols])
        alpha = jnp.exp2(m_old - m_new)
        pv = None
        for r, _ in parts:
            rows = slice(r * pv_keys, (r + 1) * pv_keys)
            p = jnp.exp2(s_refs[slot][c, rows, :] - m_new).astype(BF16)
            vb = vt_ref[:, pl.ds(k0 + r * pv_keys, pv_keys)]
            t = jnp.dot(vb, p, preferred_element_type=F32)
            pv = t if pv is None else pv + t
            if r == parts[-1][0]:
                acc_ref[c] = alpha * acc_ref[c] + pv
                m_ref[:, cols] = m_new
            yield

    def step(q0, k0, rslot, nxt_q0, nxt_k0, a_plan=full_plan, b_plan=full_plan):
        for c in all_chains:
            gens = []
            if a_plan is not None and c in a_plan:
                gens.append(stage_a_parts(nxt_q0, nxt_k0, (rslot + LAG) % NSLOT, c, a_plan[c]))
            if b_plan is not None and c in b_plan:
                gens.append(stage_b_parts(k0, rslot, c, b_plan[c]))
            while gens:
                gens = [g for g in gens if next(g, gens) is not gens]

    def q_tile(qi, carry):
        q0 = pl.multiple_of(qi * tq, tq)
        acc_ref[...] = jnp.zeros_like(acc_ref)
        m_ref[...] = jnp.full_like(m_ref, M_INIT)

        def run_steps(k0, count):
            for j in range(count):
                step(q0, k0 + j * tk, j % NSLOT, q0, k0 + (j + LAG) * tk)

        g_first = first_group(qi)
        k_first = pl.multiple_of(g_first * tq, tq)
        n_main = NSLOT * (jnp.maximum(qi - 1, 0) - g_first)
        rem = n_main & (MAIN_UNROLL - 1)

        def main(t, carry2):
            run_steps(pl.multiple_of(k_first + t * MAIN_UNROLL * tk, tq), MAIN_UNROLL)
            return carry2

        lax.fori_loop(0, lax.shift_right_logical(n_main, MAIN_UNROLL.bit_length() - 1), main, 0)
        k_last = q0 - NSLOT * tk
        sub = MAIN_UNROLL // 2
        while sub >= NSLOT:
            k_sub = pl.multiple_of(k_last - ((rem & (sub - 1)) + sub) * tk, NSLOT * tk)

            @pl.when((rem & sub) != 0)
            def _(k_sub=k_sub, sub=sub):
                run_steps(k_sub, sub)

            sub //= 2

        nxt_qi = jnp.minimum(qi + 1, nq - 1)
        nxt_q0 = pl.multiple_of(nxt_qi * tq, tq)
        nxt_k_first = pl.multiple_of(first_group(nxt_qi) * tq, tq)

        def last_below():
            k0 = pl.multiple_of(k_last, NSLOT * tk)
            for j in range(NSLOT):
                tgt = j + LAG
                if tgt < NSLOT:
                    step(q0, k0 + j * tk, j, q0, k0 + tgt * tk)
                else:
                    step(q0, k0 + j * tk, j, q0, q0 + (tgt - NSLOT) * tk,
                         a_plan=diag_plans[tgt - NSLOT])

        def diagonal():
            for d in range(nd):
                tgt = d + LAG
                if tgt < nd:
                    step(q0, q0 + d * tk, d % NSLOT, q0, q0 + tgt * tk,
                         a_plan=diag_plans[tgt], b_plan=diag_plans[d])
                else:
                    step(q0, q0 + d * tk, d % NSLOT, nxt_q0, nxt_k_first + (tgt - nd) * tk,
                         b_plan=diag_plans[d])

        @pl.when(qi >= 1)
        def _():
            last_below()
            diagonal()

        @pl.when(qi == 0)
        def _():
            diagonal()

        for c in all_chains:
            acc = acc_ref[c]
            o = acc[:HEAD_V] / acc[HEAD_V:HEAD_V + 1]
            o_ref[:, pl.ds(q0 + c * cw, cw)] = o.astype(o_ref.dtype)
        return carry

    for j in range(LAG):
        step(0, 0, (j - LAG) % NSLOT, 0, j * tk, a_plan=diag_plans[j], b_plan=None)
    lax.fori_loop(0, nq, q_tile, 0)


def _attn_fixed_kernel(cq_ref, cg_ref, thr_ref, qt_ref, k_ref, vt_ref, o_ref, acc_ref,
                       *, seq, tq, tk, cw):
    ncol = tq // cw
    nq = seq // tq
    nd = tq // tk
    all_chains = tuple(range(ncol))
    bias_base = (pl.program_id(0) * pl.num_programs(1) + pl.program_id(1)) * nq
    ref_max = thr_ref[1]

    def first_group(t):
        def dead(g):
            return jnp.logical_and(
                g < t - 1, cq_ref[bias_base + t] - cg_ref[bias_base + g] < -thr_ref[0])
        return lax.while_loop(dead, lambda g: g + 1, jnp.int32(0))

    def scores(q0, k0, c, masked):
        qt = qt_ref[:, pl.ds(q0 + c * cw, cw)]
        s = jnp.dot(k_ref[pl.ds(k0, tk), :], qt, preferred_element_type=F32)
        if masked:
            kpos = k0 + lax.broadcasted_iota(jnp.int32, (tk, cw), 0)
            qpos = q0 + c * cw + lax.broadcasted_iota(jnp.int32, (tk, cw), 1)
            s = jnp.where(kpos <= qpos, s, MASK_VALUE)
        return s

    def weighted_values(k0, s):
        p = jnp.exp2(s - ref_max)
        pv = jnp.dot(vt_ref[0:HEAD_V, pl.ds(k0, tk)], p.astype(BF16), preferred_element_type=F32)
        return pv, jnp.sum(p, axis=0, keepdims=True)

    plain = {c: False for c in all_chains}

    def run_list(q0, blocks):
        tot = [None] * ncol
        pending = [{c: scores(q0, k0, c, m) for c, m in vis.items()}
                   for k0, vis in blocks[:SCORES_AHEAD]]
        for j, (k0, vis) in enumerate(blocks):
            nxt_k0, nxt_vis = (blocks[j + SCORES_AHEAD] if j + SCORES_AHEAD < len(blocks)
                               else (None, {}))
            s_cur = pending.pop(0)
            s_nxt = {}
            for c in all_chains:
                if c in nxt_vis:
                    s_nxt[c] = scores(q0, nxt_k0, c, nxt_vis[c])
                if c in vis:
                    t, l = weighted_values(k0, s_cur[c])
                    tot[c] = (t, l) if tot[c] is None else (tot[c][0] + t, tot[c][1] + l)
            pending.append(s_nxt)
        return tot

    def run_plain(q0, k0, count):
        tot = run_list(q0, [(k0 + j * tk, plain) for j in range(count)])
        for c in all_chains:
            acc_ref[c, 0:HEAD_V] = acc_ref[c, 0:HEAD_V] + tot[c][0]
            acc_ref[c, HEAD_V:HEAD_V + 1] = acc_ref[c, HEAD_V:HEAD_V + 1] + tot[c][1]

    def q_tile(qi, carry):
        q0 = pl.multiple_of(qi * tq, tq)
        acc_ref[...] = jnp.zeros_like(acc_ref)
        g_first = first_group(qi)
        k_first = pl.multiple_of(g_first * tq, tq)
        n_main = nd * (jnp.maximum(qi - 1, 0) - g_first)
        rem = n_main & (MAIN_UNROLL - 1)

        def main(t, carry2):
            run_plain(q0, pl.multiple_of(k_first + t * MAIN_UNROLL * tk, tq), MAIN_UNROLL)
            return carry2

        lax.fori_loop(0, lax.shift_right_logical(n_main, MAIN_UNROLL.bit_length() - 1), main, 0)
        diagonal = [(q0 + d * tk, {c: c == d for c in range(d, ncol)}) for d in range(nd)]

        def finish(groups_below):
            k0 = pl.multiple_of(q0 - groups_below * nd * tk, tq)
            below = [(k0 + j * tk, plain) for j in range(groups_below * nd)]
            tot = run_list(q0, below + diagonal)
            for c in all_chains:
                o = ((acc_ref[c, 0:HEAD_V] + tot[c][0])
                     / (acc_ref[c, HEAD_V:HEAD_V + 1] + tot[c][1]))
                o_ref[:, pl.ds(q0 + c * cw, cw)] = o.astype(o_ref.dtype)

        groups_left = lax.shift_right_logical(rem, nd.bit_length() - 1) + 1
        for groups_below in range(1, MAIN_UNROLL // nd + 1):
            pl.when(jnp.logical_and(qi >= 1, groups_left == groups_below))(
                functools.partial(finish, groups_below))
        pl.when(qi == 0)(functools.partial(finish, 0))

        return carry

    lax.fori_loop(0, nq, q_tile, 0)


def _attention(qt, k, vt, bound, bias=None):
    b, heads, _, s = qt.shape
    tq = min(ATT_TQ, s)
    tk = min(ATT_TK, tq // NSLOT)
    assert tk == ATT_CW
    nq = s // tq
    if bias is None:
        cq = cg = jnp.zeros((b * heads * nq,), F32)
    else:
        cq = bias[:, :, ::tq].reshape(-1)
        cg = bias[:, :, tq - 1::tq].reshape(-1)
    bound = jnp.asarray(bound, F32)
    thr = jnp.stack([2.0 * bound + ZERO_WEIGHT_LOG2, bound])
    in_specs = [
        pl.BlockSpec((None, None, QK_PAD, s), lambda bi, hi, *_: (bi, hi, 0, 0)),
        pl.BlockSpec((None, None, s, QK_PAD), lambda bi, hi, *_: (bi, hi, 0, 0)),
        pl.BlockSpec((None, None, V_ROWS, s), lambda bi, hi, *_: (bi, hi, 0, 0)),
    ]
    out_specs = pl.BlockSpec((None, HEAD_V, s), lambda bi, hi, *_: (bi, hi, 0))
    acc_scratch = pltpu.VMEM((tq // ATT_CW, V_ROWS, ATT_CW), F32)
    common = dict(
        out_shape=jax.ShapeDtypeStruct((b, heads * HEAD_V, s), BF16),
        compiler_params=pltpu.CompilerParams(
            dimension_semantics=("parallel", "parallel"), vmem_limit_bytes=VMEM_LIMIT),
    )

    def online(*args):
        return pl.pallas_call(
            functools.partial(_attn_kernel, seq=s, tq=tq, tk=tk, cw=ATT_CW),
            grid_spec=pltpu.PrefetchScalarGridSpec(
                num_scalar_prefetch=3, grid=(b, heads), in_specs=in_specs, out_specs=out_specs,
                scratch_shapes=([acc_scratch, pltpu.VMEM((1, tq), F32)]
                                + [pltpu.VMEM((tq // ATT_CW, tk, ATT_CW), F32)] * NSLOT
                                + [pltpu.VMEM((1, tq), F32)] * NSLOT),
            ),
            name="attention", **common)(*args)

    def fixed(*args):
        return pl.pallas_call(
            functools.partial(_attn_fixed_kernel, seq=s, tq=tq, tk=tk, cw=ATT_CW),
            grid_spec=pltpu.PrefetchScalarGridSpec(
                num_scalar_prefetch=3, grid=(b, heads), in_specs=in_specs, out_specs=out_specs,
                scratch_shapes=[acc_scratch],
            ),
            name="attention_fixed", **common)(*args)

    return lax.cond(bound <= FIXED_REF_MAX_BOUND, fixed, online, cq, cg, thr, qt, k, vt)


def _score_bound(g_q, g_k, dk):
    return (jnp.max(jnp.abs(g_q)) * jnp.max(jnp.abs(g_k)) * (dk ** 0.5) * LOG2E
            * BF16_NORM_SLACK).astype(F32)


def kernel(x, p, positions, g_ffn1, g_mix, g_ffn2, g_ple, ffn1_w_in, ffn1_w_out, ffn2_w_in, ffn2_w_out, ple_w_proj, ple_w_gate, mla_w_in, mla_g_q_lat, mla_w_uq, mla_g_kv_lat, mla_w_ukv, mla_g_qn, mla_g_kn, mla_w_o, fox_w_in, fox_b_f, fox_g_qn, fox_g_kn, fox_w_o):
    b, s, d = x.shape
    depth = g_ffn1.shape[0]
    assert d == D_MODEL and s % PROJ_TILE == 0 and s % ROW_TILE == 0
    n = b * s
    bf = lambda w: w.astype(BF16)
    row = lambda g: g.reshape(1, -1)
    col = lambda g: g.reshape(-1, 1)

    pos3 = positions.reshape(b, 1, s)
    half = MLA_ROPE // 2
    inv_freq = (ROPE_THETA ** (-jnp.arange(0, MLA_ROPE, 2, dtype=F32) / MLA_ROPE)).reshape(half, 1)
    lat = MLA_Q_RANK + MLA_KV_RANK

    h = x
    for i in range(depth):
        j = i // N_MIXERS
        h = _ffn(h.reshape(n, d), row(g_ffn1[i]), bf(ffn1_w_in[i]), bf(ffn1_w_out[i])).reshape(b, s, d)
        if i % N_MIXERS == 0:
            qt, k, vt = _mla_proj(
                h, pos3, row(g_mix[i]), bf(mla_w_in[j][:, :lat]), bf(mla_w_in[j][:, lat:].T),
                row(mla_g_q_lat[j]), row(mla_g_kv_lat[j]), bf(mla_w_uq[j].T), bf(mla_w_ukv[j].T),
                col(mla_g_qn[j]), col(mla_g_kn[j]), inv_freq)
            w_o = mla_w_o[j]
            ot = _attention(qt, k, vt, _score_bound(mla_g_qn[j], mla_g_kn[j], MLA_QK))
        else:
            qt, k, vt, c = _fox_proj(h, row(g_mix[i]), bf(fox_w_in[j].T), col(fox_b_f[j]),
                                     col(fox_g_qn[j]), col(fox_g_kn[j]))
            w_o = fox_w_o[j]
            ot = _attention(qt, k, vt, _score_bound(fox_g_qn[j], fox_g_kn[j], FOX_HEAD_DIM), bias=c)
        h = _post(h.reshape(n, d), ot, bf(w_o), row(g_ffn2[i]), bf(ffn2_w_in[i]), bf(ffn2_w_out[i]),
                  p.reshape(depth, n, PLE_DIM), i, row(g_ple[i]), bf(ple_w_gate[i]),
                  bf(ple_w_proj[i])).reshape(b, s, d)
    return h
```

```python
import functools
import math

import jax
import jax.numpy as jnp
from jax import lax
from jax.experimental import pallas as pl
from jax.experimental.pallas import tpu as pltpu

F32 = jnp.float32
BF16 = jnp.bfloat16

D_MODEL = 1024
D_FF = 2816
PLE_DIM = 256
N_MIXERS = 2
FFN_HALF = 0.5
HEADS = 16
MLA_Q_RANK = 512
MLA_KV_RANK = 256
MLA_NOPE = 64
MLA_ROPE = 32
MLA_QK = MLA_NOPE + MLA_ROPE
HEAD_V = 64
FOX_HEAD_DIM = 64
ROPE_THETA = 10000.0
EPS = 1e-6
LOG2E = math.log2(math.e)

LANES = 128
MXU_DIM = 256
BF16_SUBLANES = 16
QK_PAD = 128
V_ROWS = HEAD_V + BF16_SUBLANES
VMEM_LIMIT = 52 * 1024 * 1024

ROW_TILE = 512
PROJ_TILE = 512
HEAD_GROUP = 4
ATT_TQ = 1024
ATT_CW = MXU_DIM
ATT_TK = MXU_DIM
MAIN_UNROLL = 16
NSLOT = 4
LAG = 2
SCORES_AHEAD = 2
PV_KEYS = MXU_DIM
MASK_VALUE = -jnp.inf
M_INIT = -1e30
ZERO_WEIGHT_LOG2 = 170.0
BF16_NORM_SLACK = 1.02
FIXED_REF_MAX_BOUND = 56.0


def _rms(x, g):
    return x * lax.rsqrt(jnp.mean(x * x, axis=-1, keepdims=True) + EPS) * g


def _resident(shape):
    return pl.BlockSpec(shape, lambda *_: (0,) * len(shape), pipeline_mode=pl.Buffered(1))


def _swiglu_half_step(x, g, w_in_ref, w_out_ref):
    xn = _rms(x, g).astype(BF16)
    gu = jnp.dot(xn, w_in_ref[...], preferred_element_type=F32)
    gate = gu[:, :D_FF]
    up = gu[:, D_FF:]
    act = (gate * jax.nn.sigmoid(gate) * up).astype(BF16)
    return x + FFN_HALF * jnp.dot(act, w_out_ref[...], preferred_element_type=F32)


def _ffn_kernel(x_ref, g_ref, w_in_ref, w_out_ref, o_ref):
    o_ref[...] = _swiglu_half_step(x_ref[...], g_ref[...], w_in_ref, w_out_ref)


def _ffn(h2d, g, w_in, w_out):
    n = h2d.shape[0]
    return pl.pallas_call(
        _ffn_kernel,
        grid=(n // ROW_TILE,),
        in_specs=[
            pl.BlockSpec((ROW_TILE, D_MODEL), lambda i: (i, 0)),
            _resident((1, D_MODEL)),
            _resident((D_MODEL, 2 * D_FF)),
            _resident((D_FF, D_MODEL)),
        ],
        out_specs=pl.BlockSpec((ROW_TILE, D_MODEL), lambda i: (i, 0)),
        out_shape=jax.ShapeDtypeStruct((n, D_MODEL), F32),
        compiler_params=pltpu.CompilerParams(
            dimension_semantics=("parallel",), vmem_limit_bytes=VMEM_LIMIT),
        name="ffn",
    )(h2d, g, w_in, w_out)


def _post_kernel(x_ref, ot_ref, wo_ref, g2_ref, w_in_ref, w_out_ref,
                 p_ref, gp_ref, wg_ref, wp_ref, o_ref):
    x = x_ref[...] + lax.dot_general(ot_ref[...], wo_ref[...], (((0,), (0,)), ((), ())),
                                     preferred_element_type=F32)
    x = _swiglu_half_step(x, g2_ref[...], w_in_ref, w_out_ref)
    xn = _rms(x, gp_ref[...]).astype(BF16)
    gate = jax.nn.sigmoid(jnp.dot(xn, wg_ref[...], preferred_element_type=F32))
    pp = jnp.dot(p_ref[...].astype(BF16), wp_ref[...], preferred_element_type=F32)
    o_ref[...] = x + gate * pp


def _post(h2d, ot, w_o, g2, w_in, w_out, p3d, layer, g_ple, w_gate, w_proj):
    n = h2d.shape[0]
    hd, s = ot.shape[1], ot.shape[2]
    tiles = s // ROW_TILE
    rows = pl.BlockSpec((ROW_TILE, D_MODEL), lambda i: (i, 0))
    return pl.pallas_call(
        _post_kernel,
        grid=(n // ROW_TILE,),
        in_specs=[
            rows,
            pl.BlockSpec((None, hd, ROW_TILE), lambda i: (i // tiles, 0, i % tiles)),
            _resident((hd, D_MODEL)),
            _resident((1, D_MODEL)),
            _resident((D_MODEL, 2 * D_FF)),
            _resident((D_FF, D_MODEL)),
            pl.BlockSpec((None, ROW_TILE, PLE_DIM), lambda i: (layer, i, 0)),
            _resident((1, D_MODEL)),
            _resident((D_MODEL, D_MODEL)),
            _resident((PLE_DIM, D_MODEL)),
        ],
        out_specs=rows,
        out_shape=jax.ShapeDtypeStruct((n, D_MODEL), F32),
        compiler_params=pltpu.CompilerParams(
            dimension_semantics=("parallel",), vmem_limit_bytes=VMEM_LIMIT),
        name="post",
    )(h2d, ot, w_o, g2, w_in, w_out, p3d, g_ple, w_gate, w_proj)


def _dot_nt(w, x):
    return lax.dot_general(w, x, (((1,), (1,)), ((), ())), preferred_element_type=F32)


def _ones_row_group(width):
    row = lax.broadcasted_iota(jnp.int32, (BF16_SUBLANES, width), 0)
    return jnp.where(row == 0, 1.0, 0.0).astype(F32)


def _rope_rows(x1, x2, cos, sin):
    return x1 * cos - x2 * sin, x1 * sin + x2 * cos


def _mla_proj_kernel(x_ref, pos_ref, g_ref, w_in_ref, w_pe_ref, gq_lat_ref, gkv_lat_ref,
                     w_uq_ref, w_ukv_ref, gqn_ref, gkn_ref, inv_freq_ref,
                     qt_ref, k_ref, vt_ref):
    tm = x_ref.shape[0]
    xn = _rms(x_ref[...], g_ref[...]).astype(BF16)
    z = jnp.dot(xn, w_in_ref[...], preferred_element_type=F32)
    cq = _rms(z[:, :MLA_Q_RANK], gq_lat_ref[...]).astype(BF16)
    ckv = _rms(z[:, MLA_Q_RANK:], gkv_lat_ref[...]).astype(BF16)
    kpe_t = _dot_nt(w_pe_ref[...], xn)

    ang = pos_ref[...].astype(F32) * inv_freq_ref[...]
    cos = jnp.cos(ang)
    sin = jnp.sin(ang)
    half = MLA_ROPE // 2
    q_scale = (MLA_QK ** -0.5) * LOG2E
    gqn = gqn_ref[...] * q_scale
    gkn = gkn_ref[...]
    kpe_sq = jnp.sum(kpe_t * kpe_t, axis=0, keepdims=True)
    ones_rows = _ones_row_group(tm).astype(BF16)
    zeros_tail = jnp.zeros((QK_PAD - MLA_QK, tm), F32)

    kv_dim = MLA_NOPE + HEAD_V
    for h in range(HEADS):
        if h % HEAD_GROUP == 0:
            g0 = h
            qt_all = _dot_nt(w_uq_ref[g0 * MLA_QK:(g0 + HEAD_GROUP) * MLA_QK, :], cq)
            kvt_all = _dot_nt(w_ukv_ref[g0 * kv_dim:(g0 + HEAD_GROUP) * kv_dim, :], ckv)
        hl = h - g0
        qh = qt_all[hl * MLA_QK:(hl + 1) * MLA_QK]
        qn = qh * lax.rsqrt(jnp.mean(qh * qh, axis=0, keepdims=True) + EPS) * gqn
        q1, q2 = _rope_rows(qn[MLA_NOPE:MLA_NOPE + half], qn[MLA_NOPE + half:], cos, sin)
        qt_ref[h, 0:MLA_NOPE, :] = qn[:MLA_NOPE].astype(BF16)
        qt_ref[h, MLA_NOPE:MLA_NOPE + half, :] = q1.astype(BF16)
        qt_ref[h, MLA_NOPE + half:MLA_QK, :] = q2.astype(BF16)
        qt_ref[h, MLA_QK:QK_PAD, :] = zeros_tail.astype(BF16)

        base = hl * kv_dim
        kn = kvt_all[base:base + MLA_NOPE]
        ms = (jnp.sum(kn * kn, axis=0, keepdims=True) + kpe_sq) * (1.0 / MLA_QK)
        r = lax.rsqrt(ms + EPS)
        kn = kn * r * gkn[:MLA_NOPE]
        kp = kpe_t * r * gkn[MLA_NOPE:]
        k1, k2 = _rope_rows(kp[:half], kp[half:], cos, sin)
        kt = jnp.concatenate([kn, k1, k2, zeros_tail], axis=0)
        k_ref[h] = kt.T.astype(BF16)

        vt_ref[h, 0:HEAD_V, :] = kvt_all[base + MLA_NOPE:base + MLA_NOPE + HEAD_V].astype(BF16)
        vt_ref[h, HEAD_V:V_ROWS, :] = ones_rows


def _qkv_out(b, s):
    specs = [
        pl.BlockSpec((None, HEADS, QK_PAD, PROJ_TILE), lambda bi, i: (bi, 0, 0, i)),
        pl.BlockSpec((None, HEADS, PROJ_TILE, QK_PAD), lambda bi, i: (bi, 0, i, 0)),
        pl.BlockSpec((None, HEADS, V_ROWS, PROJ_TILE), lambda bi, i: (bi, 0, 0, i)),
    ]
    shapes = [
        jax.ShapeDtypeStruct((b, HEADS, QK_PAD, s), BF16),
        jax.ShapeDtypeStruct((b, HEADS, s, QK_PAD), BF16),
        jax.ShapeDtypeStruct((b, HEADS, V_ROWS, s), BF16),
    ]
    return specs, shapes


def _mla_proj(h, pos3, g, w_in, w_pe_t, gq_lat, gkv_lat, w_uq_t, w_ukv_t, gqn, gkn, inv_freq):
    b, s, _ = h.shape
    out_specs, out_shapes = _qkv_out(b, s)
    return pl.pallas_call(
        _mla_proj_kernel,
        grid=(b, s // PROJ_TILE),
        in_specs=[
            pl.BlockSpec((None, PROJ_TILE, D_MODEL), lambda bi, i: (bi, i, 0)),
            pl.BlockSpec((None, 1, PROJ_TILE), lambda bi, i: (bi, 0, i)),
            _resident(g.shape), _resident(w_in.shape), _resident(w_pe_t.shape),
            _resident(gq_lat.shape), _resident(gkv_lat.shape),
            _resident(w_uq_t.shape), _resident(w_ukv_t.shape),
            _resident(gqn.shape), _resident(gkn.shape), _resident(inv_freq.shape),
        ],
        out_specs=out_specs,
        out_shape=out_shapes,
        compiler_params=pltpu.CompilerParams(
            dimension_semantics=("parallel", "parallel"), vmem_limit_bytes=VMEM_LIMIT),
        name="mla_proj",
    )(h, pos3, g, w_in, w_pe_t, gq_lat, gkv_lat, w_uq_t, w_ukv_t, gqn, gkn, inv_freq)


def _split3(c):
    hi = c.astype(BF16).astype(F32)
    mid = (c - hi).astype(BF16).astype(F32)
    lo = (c - hi - mid).astype(BF16).astype(F32)
    return hi, mid, lo


def _fox_proj_kernel(x_ref, g_ref, w_ref, bf_ref, gqn_ref, gkn_ref,
                     qt_ref, k_ref, vt_ref, c_ref, carry_ref):
    tm = x_ref.shape[0]
    hd = HEADS * FOX_HEAD_DIM

    @pl.when(pl.program_id(1) == 0)
    def _():
        carry_ref[...] = jnp.zeros_like(carry_ref)

    xn = _rms(x_ref[...], g_ref[...]).astype(BF16)
    f_t = _dot_nt(w_ref[3 * hd:3 * hd + HEADS, :], xn) + bf_ref[...]

    logf = (jnp.minimum(f_t, 0.0) - jnp.log1p(jnp.exp(-jnp.abs(f_t)))) * LOG2E
    hi, mid, lo = _split3(logf)
    parts = jnp.concatenate([hi, mid, lo], axis=0).astype(BF16)
    src = lax.broadcasted_iota(jnp.int32, (tm, tm), 0)
    dst = lax.broadcasted_iota(jnp.int32, (tm, tm), 1)
    upper = jnp.where(src <= dst, 1.0, 0.0).astype(BF16)
    sums = jnp.dot(parts, upper, preferred_element_type=F32)
    local = sums[2 * HEADS:] + sums[HEADS:2 * HEADS] + sums[:HEADS]
    c = local + carry_ref[:, 0:1]
    carry_ref[...] = jnp.broadcast_to(c[:, tm - 1:tm], carry_ref.shape)
    c_ref[...] = c

    q_scale = (FOX_HEAD_DIM ** -0.5) * LOG2E
    gqn = gqn_ref[...] * q_scale
    gkn = gkn_ref[...]
    row = lax.broadcasted_iota(jnp.int32, (BF16_SUBLANES, tm), 0)
    ones_rows = _ones_row_group(tm).astype(BF16)
    zeros_tail = jnp.zeros((QK_PAD - FOX_HEAD_DIM - BF16_SUBLANES, tm), F32)

    def bias_rows(first, second):
        out = jnp.zeros((BF16_SUBLANES, tm), F32)
        for j in range(3):
            out = jnp.where(row == j, first[j], out)
            out = jnp.where(row == 8 + j, second[j], out)
        return out

    one3 = (1.0, 1.0, 1.0)
    gw = HEAD_GROUP * FOX_HEAD_DIM
    for h in range(HEADS):
        if h % HEAD_GROUP == 0:
            g0 = h
            r0 = g0 * FOX_HEAD_DIM
            qt_all = _dot_nt(w_ref[r0:r0 + gw, :], xn)
            kt_all = _dot_nt(w_ref[hd + r0:hd + r0 + gw, :], xn)
            vt_all = _dot_nt(w_ref[2 * hd + r0:2 * hd + r0 + gw, :], xn)
        sl = slice((h - g0) * FOX_HEAD_DIM, (h - g0 + 1) * FOX_HEAD_DIM)
        c3 = _split3(c[h:h + 1])
        qh = qt_all[sl]
        qn = qh * lax.rsqrt(jnp.mean(qh * qh, axis=0, keepdims=True) + EPS) * gqn
        qt_ref[h, 0:FOX_HEAD_DIM, :] = qn.astype(BF16)
        qt_ref[h, FOX_HEAD_DIM:FOX_HEAD_DIM + BF16_SUBLANES, :] = bias_rows(one3, c3).astype(BF16)
        qt_ref[h, FOX_HEAD_DIM + BF16_SUBLANES:QK_PAD, :] = zeros_tail.astype(BF16)

        kh = kt_all[sl]
        kn = kh * lax.rsqrt(jnp.mean(kh * kh, axis=0, keepdims=True) + EPS) * gkn
        neg3 = tuple(-p for p in c3)
        kt = jnp.concatenate([kn, bias_rows(neg3, one3), zeros_tail], axis=0)
        k_ref[h] = kt.T.astype(BF16)

        vt_ref[h, 0:HEAD_V, :] = vt_all[sl].astype(BF16)
        vt_ref[h, HEAD_V:V_ROWS, :] = ones_rows


def _fox_proj(h, g, w_t, b_f, gqn, gkn):
    b, s, _ = h.shape
    out_specs, out_shapes = _qkv_out(b, s)
    out_specs = out_specs + [pl.BlockSpec((None, HEADS, PROJ_TILE), lambda bi, i: (bi, 0, i))]
    out_shapes = out_shapes + [jax.ShapeDtypeStruct((b, HEADS, s), F32)]
    return pl.pallas_call(
        _fox_proj_kernel,
        grid=(b, s // PROJ_TILE),
        in_specs=[
            pl.BlockSpec((None, PROJ_TILE, D_MODEL), lambda bi, i: (bi, i, 0)),
            _resident(g.shape), _resident(w_t.shape), _resident(b_f.shape),
            _resident(gqn.shape), _resident(gkn.shape),
        ],
        out_specs=out_specs,
        out_shape=out_shapes,
        scratch_shapes=[pltpu.VMEM((HEADS, LANES), F32)],
        compiler_params=pltpu.CompilerParams(
            dimension_semantics=("arbitrary", "arbitrary"), vmem_limit_bytes=VMEM_LIMIT),
        name="fox_proj",
    )(h, g, w_t, b_f, gqn, gkn)


def _attn_kernel(cq_ref, cg_ref, thr_ref, qt_ref, k_ref, vt_ref, o_ref, acc_ref, m_ref, *bufs,
                 seq, tq, tk, cw):
    ncol = tq // cw
    nq = seq // tq
    all_chains = tuple(range(ncol))
    s_refs = bufs[:NSLOT]
    bm_refs = bufs[NSLOT:]
    bias_base = (pl.program_id(0) * pl.num_programs(1) + pl.program_id(1)) * nq

    def first_group(t):
        def dead(g):
            return jnp.logical_and(
                g < t - 1, cq_ref[bias_base + t] - cg_ref[bias_base + g] < -thr_ref[0])
        return lax.while_loop(dead, lambda g: g + 1, jnp.int32(0))

    pv_keys = min(PV_KEYS, tk)
    n_parts = tk // pv_keys
    full_plan = {c: tuple((r, False) for r in range(n_parts)) for c in all_chains}

    nd = tq // tk
    assert nd == NSLOT and MAIN_UNROLL % NSLOT == 0 and LAG < NSLOT

    def diag_plan(d):
        plan = {}
        for c in all_chains:
            parts = []
            for r in range(n_parts):
                key_lo = d * tk + r * pv_keys
                if key_lo <= (c + 1) * cw - 1:
                    parts.append((r, key_lo + pv_keys - 1 > c * cw))
            if parts:
                plan[c] = tuple(parts)
        return plan

    diag_plans = tuple(diag_plan(d) for d in range(nd))

    def stage_a_parts(q0, k0, slot, c, parts):
        cols = slice(c * cw, (c + 1) * cw)
        qt = qt_ref[:, pl.ds(q0 + c * cw, cw)]
        bm = None
        for r, masked in parts:
            rows = slice(r * pv_keys, (r + 1) * pv_keys)
            kb = k_ref[pl.ds(k0 + r * pv_keys, pv_keys), :]
            s = jnp.dot(kb, qt, preferred_element_type=F32)
            if masked:
                kpos = k0 + r * pv_keys + lax.broadcasted_iota(jnp.int32, (pv_keys, cw), 0)
                qpos = q0 + c * cw + lax.broadcasted_iota(jnp.int32, (pv_keys, cw), 1)
                s = jnp.where(kpos <= qpos, s, MASK_VALUE)
            s_refs[slot][c, rows, :] = s
            pm = jnp.max(s, axis=0, keepdims=True)
            bm = pm if bm is None else jnp.maximum(bm, pm)
            if r == parts[-1][0]:
                bm_refs[slot][:, cols] = bm
            yield

    def stage_b_parts(k0, slot, c, parts):
        cols = slice(c * cw, (c + 1) * cw)
        m_old = m_ref[:, cols]
        m_new = jnp.maximum(m_old, bm_refs[slot][:, cols])
        alpha = jnp.exp2(m_old - m_new)
        pv = None
        for r, _ in parts:
            rows = slice(r * pv_keys, (r + 1) * pv_keys)
            p = jnp.exp2(s_refs[slot][c, rows, :] - m_new).astype(BF16)
            vb = vt_ref[:, pl.ds(k0 + r * pv_keys, pv_keys)]
            t = jnp.dot(vb, p, preferred_element_type=F32)
            pv = t if pv is None else pv + t
            if r == parts[-1][0]:
                acc_ref[c] = alpha * acc_ref[c] + pv
                m_ref[:, cols] = m_new
            yield

    def step(q0, k0, rslot, nxt_q0, nxt_k0, a_plan=full_plan, b_plan=full_plan):
        for c in all_chains:
            gens = []
            if a_plan is not None and c in a_plan:
                gens.append(stage_a_parts(nxt_q0, nxt_k0, (rslot + LAG) % NSLOT, c, a_plan[c]))
            if b_plan is not None and c in b_plan:
                gens.append(stage_b_parts(k0, rslot, c, b_plan[c]))
            while gens:
                gens = [g for g in gens if next(g, gens) is not gens]

    def q_tile(qi, carry):
        q0 = pl.multiple_of(qi * tq, tq)
        acc_ref[...] = jnp.zeros_like(acc_ref)
        m_ref[...] = jnp.full_like(m_ref, M_INIT)

        def run_steps(k0, count):
            for j in range(count):
                step(q0, k0 + j * tk, j % NSLOT, q0, k0 + (j + LAG) * tk)

        g_first = first_group(qi)
        k_first = pl.multiple_of(g_first * tq, tq)
        n_main = NSLOT * (jnp.maximum(qi - 1, 0) - g_first)
        rem = n_main & (MAIN_UNROLL - 1)

        def main(t, carry2):
            run_steps(pl.multiple_of(k_first + t * MAIN_UNROLL * tk, tq), MAIN_UNROLL)
            return carry2

        lax.fori_loop(0, lax.shift_right_logical(n_main, MAIN_UNROLL.bit_length() - 1), main, 0)
        k_last = q0 - NSLOT * tk
        sub = MAIN_UNROLL // 2
        while sub >= NSLOT:
            k_sub = pl.multiple_of(k_last - ((rem & (sub - 1)) + sub) * tk, NSLOT * tk)

            @pl.when((rem & sub) != 0)
            def _(k_sub=k_sub, sub=sub):
                run_steps(k_sub, sub)

            sub //= 2

        nxt_qi = jnp.minimum(qi + 1, nq - 1)
        nxt_q0 = pl.multiple_of(nxt_qi * tq, tq)
        nxt_k_first = pl.multiple_of(first_group(nxt_qi) * tq, tq)

        def last_below():
            k0 = pl.multiple_of(k_last, NSLOT * tk)
            for j in range(NSLOT):
                tgt = j + LAG
                if tgt < NSLOT:
                    step(q0, k0 + j * tk, j, q0, k0 + tgt * tk)
                else:
                    step(q0, k0 + j * tk, j, q0, q0 + (tgt - NSLOT) * tk,
                         a_plan=diag_plans[tgt - NSLOT])

        def diagonal():
            for d in range(nd):
                tgt = d + LAG
                if tgt < nd:
                    step(q0, q0 + d * tk, d % NSLOT, q0, q0 + tgt * tk,
                         a_plan=diag_plans[tgt], b_plan=diag_plans[d])
                else:
                    step(q0, q0 + d * tk, d % NSLOT, nxt_q0, nxt_k_first + (tgt - nd) * tk,
                         b_plan=diag_plans[d])

        @pl.when(qi >= 1)
        def _():
            last_below()
            diagonal()

        @pl.when(qi == 0)
        def _():
            diagonal()

        for c in all_chains:
            acc = acc_ref[c]
            o = acc[:HEAD_V] / acc[HEAD_V:HEAD_V + 1]
            o_ref[:, pl.ds(q0 + c * cw, cw)] = o.astype(o_ref.dtype)
        return carry

    for j in range(LAG):
        step(0, 0, (j - LAG) % NSLOT, 0, j * tk, a_plan=diag_plans[j], b_plan=None)
    lax.fori_loop(0, nq, q_tile, 0)


def _attn_fixed_kernel(cq_ref, cg_ref, thr_ref, qt_ref, k_ref, vt_ref, o_ref, acc_ref,
                       *, seq, tq, tk, cw):
    ncol = tq // cw
    nq = seq // tq
    nd = tq // tk
    all_chains = tuple(range(ncol))
    bias_base = (pl.program_id(0) * pl.num_programs(1) + pl.program_id(1)) * nq
    ref_max = thr_ref[1]

    def first_group(t):
        def dead(g):
            return jnp.logical_and(
                g < t - 1, cq_ref[bias_base + t] - cg_ref[bias_base + g] < -thr_ref[0])
        return lax.while_loop(dead, lambda g: g + 1, jnp.int32(0))

    def scores(q0, k0, c, masked):
        qt = qt_ref[:, pl.ds(q0 + c * cw, cw)]
        s = jnp.dot(k_ref[pl.ds(k0, tk), :], qt, preferred_element_type=F32)
        if masked:
            kpos = k0 + lax.broadcasted_iota(jnp.int32, (tk, cw), 0)
            qpos = q0 + c * cw + lax.broadcasted_iota(jnp.int32, (tk, cw), 1)
            s = jnp.where(kpos <= qpos, s, MASK_VALUE)
        return s

    def weighted_values(k0, s):
        p = jnp.exp2(s - ref_max)
        pv = jnp.dot(vt_ref[0:HEAD_V, pl.ds(k0, tk)], p.astype(BF16), preferred_element_type=F32)
        return pv, jnp.sum(p, axis=0, keepdims=True)

    plain = {c: False for c in all_chains}

    def run_list(q0, blocks):
        tot = [None] * ncol
        pending = [{c: scores(q0, k0, c, m) for c, m in vis.items()}
                   for k0, vis in blocks[:SCORES_AHEAD]]
        for j, (k0, vis) in enumerate(blocks):
            nxt_k0, nxt_vis = (blocks[j + SCORES_AHEAD] if j + SCORES_AHEAD < len(blocks)
                               else (None, {}))
            s_cur = pending.pop(0)
            s_nxt = {}
            for pair in range(0, ncol, 2):
                for c in all_chains[pair:pair + 2]:
                    if c in nxt_vis:
                        s_nxt[c] = scores(q0, nxt_k0, c, nxt_vis[c])
                for c in all_chains[pair:pair + 2]:
                    if c in vis:
                        t, l = weighted_values(k0, s_cur[c])
                        tot[c] = (t, l) if tot[c] is None else (tot[c][0] + t, tot[c][1] + l)
            pending.append(s_nxt)
        return tot

    def run_plain(q0, k0, count):
        tot = run_list(q0, [(k0 + j * tk, plain) for j in range(count)])
        for c in all_chains:
            acc_ref[c, 0:HEAD_V] = acc_ref[c, 0:HEAD_V] + tot[c][0]
            acc_ref[c, HEAD_V:HEAD_V + 1] = acc_ref[c, HEAD_V:HEAD_V + 1] + tot[c][1]

    def q_tile(qi, carry):
        q0 = pl.multiple_of(qi * tq, tq)
        acc_ref[...] = jnp.zeros_like(acc_ref)
        g_first = first_group(qi)
        k_first = pl.multiple_of(g_first * tq, tq)
        n_main = nd * (jnp.maximum(qi - 1, 0) - g_first)
        rem = n_main & (MAIN_UNROLL - 1)

        def main(t, carry2):
            run_plain(q0, pl.multiple_of(k_first + t * MAIN_UNROLL * tk, tq), MAIN_UNROLL)
            return carry2

        lax.fori_loop(0, lax.shift_right_logical(n_main, MAIN_UNROLL.bit_length() - 1), main, 0)
        diagonal = [(q0 + d * tk, {c: c == d for c in range(d, ncol)}) for d in range(nd)]

        def finish(groups_below):
            k0 = pl.multiple_of(q0 - groups_below * nd * tk, tq)
            below = [(k0 + j * tk, plain) for j in range(groups_below * nd)]
            tot = run_list(q0, below + diagonal)
            for c in all_chains:
                o = ((acc_ref[c, 0:HEAD_V] + tot[c][0])
                     / (acc_ref[c, HEAD_V:HEAD_V + 1] + tot[c][1]))
                o_ref[:, pl.ds(q0 + c * cw, cw)] = o.astype(o_ref.dtype)

        groups_left = lax.shift_right_logical(rem, nd.bit_length() - 1) + 1
        for groups_below in range(1, MAIN_UNROLL // nd + 1):
            pl.when(jnp.logical_and(qi >= 1, groups_left == groups_below))(
                functools.partial(finish, groups_below))
        pl.when(qi == 0)(functools.partial(finish, 0))

        return carry

    lax.fori_loop(0, nq, q_tile, 0)


def _attention(qt, k, vt, bound, bias=None):
    b, heads, _, s = qt.shape
    tq = min(ATT_TQ, s)
    tk = min(ATT_TK, tq // NSLOT)
    assert tk == ATT_CW
    nq = s // tq
    if bias is None:
        cq = cg = jnp.zeros((b * heads * nq,), F32)
    else:
        cq = bias[:, :, ::tq].reshape(-1)
        cg = bias[:, :, tq - 1::tq].reshape(-1)
    bound = jnp.asarray(bound, F32)
    thr = jnp.stack([2.0 * bound + ZERO_WEIGHT_LOG2, bound])
    in_specs = [
        pl.BlockSpec((None, None, QK_PAD, s), lambda bi, hi, *_: (bi, hi, 0, 0)),
        pl.BlockSpec((None, None, s, QK_PAD), lambda bi, hi, *_: (bi, hi, 0, 0)),
        pl.BlockSpec((None, None, V_ROWS, s), lambda bi, hi, *_: (bi, hi, 0, 0)),
    ]
    out_specs = pl.BlockSpec((None, HEAD_V, s), lambda bi, hi, *_: (bi, hi, 0))
    acc_scratch = pltpu.VMEM((tq // ATT_CW, V_ROWS, ATT_CW), F32)
    common = dict(
        out_shape=jax.ShapeDtypeStruct((b, heads * HEAD_V, s), BF16),
        compiler_params=pltpu.CompilerParams(
            dimension_semantics=("parallel", "parallel"), vmem_limit_bytes=VMEM_LIMIT),
    )

    def online(*args):
        return pl.pallas_call(
            functools.partial(_attn_kernel, seq=s, tq=tq, tk=tk, cw=ATT_CW),
            grid_spec=pltpu.PrefetchScalarGridSpec(
                num_scalar_prefetch=3, grid=(b, heads), in_specs=in_specs, out_specs=out_specs,
                scratch_shapes=([acc_scratch, pltpu.VMEM((1, tq), F32)]
                                + [pltpu.VMEM((tq // ATT_CW, tk, ATT_CW), F32)] * NSLOT
                                + [pltpu.VMEM((1, tq), F32)] * NSLOT),
            ),
            name="attention", **common)(*args)

    def fixed(*args):
        return pl.pallas_call(
            functools.partial(_attn_fixed_kernel, seq=s, tq=tq, tk=tk, cw=ATT_CW),
            grid_spec=pltpu.PrefetchScalarGridSpec(
                num_scalar_prefetch=3, grid=(b, heads), in_specs=in_specs, out_specs=out_specs,
                scratch_shapes=[acc_scratch],
            ),
            name="attention_fixed", **common)(*args)

    return lax.cond(bound <= FIXED_REF_MAX_BOUND, fixed, online, cq, cg, thr, qt, k, vt)


def _score_bound(g_q, g_k, dk):
    return (jnp.max(jnp.abs(g_q)) * jnp.max(jnp.abs(g_k)) * (dk ** 0.5) * LOG2E
            * BF16_NORM_SLACK).astype(F32)


def kernel(x, p, positions, g_ffn1, g_mix, g_ffn2, g_ple, ffn1_w_in, ffn1_w_out, ffn2_w_in, ffn2_w_out, ple_w_proj, ple_w_gate, mla_w_in, mla_g_q_lat, mla_w_uq, mla_g_kv_lat, mla_w_ukv, mla_g_qn, mla_g_kn, mla_w_o, fox_w_in, fox_b_f, fox_g_qn, fox_g_kn, fox_w_o):
    b, s, d = x.shape
    depth = g_ffn1.shape[0]
    assert d == D_MODEL and s % PROJ_TILE == 0 and s % ROW_TILE == 0
    n = b * s
    bf = lambda w: w.astype(BF16)
    row = lambda g: g.reshape(1, -1)
    col = lambda g: g.reshape(-1, 1)

    pos3 = positions.reshape(b, 1, s)
    half = MLA_ROPE // 2
    inv_freq = (ROPE_THETA ** (-jnp.arange(0, MLA_ROPE, 2, dtype=F32) / MLA_ROPE)).reshape(half, 1)
    lat = MLA_Q_RANK + MLA_KV_RANK

    h = x
    for i in range(depth):
        j = i // N_MIXERS
        h = _ffn(h.reshape(n, d), row(g_ffn1[i]), bf(ffn1_w_in[i]), bf(ffn1_w_out[i])).reshape(b, s, d)
        if i % N_MIXERS == 0:
            qt, k, vt = _mla_proj(
                h, pos3, row(g_mix[i]), bf(mla_w_in[j][:, :lat]), bf(mla_w_in[j][:, lat:].T),
                row(mla_g_q_lat[j]), row(mla_g_kv_lat[j]), bf(mla_w_uq[j].T), bf(mla_w_ukv[j].T),
                col(mla_g_qn[j]), col(mla_g_kn[j]), inv_freq)
            w_o = mla_w_o[j]
            ot = _attention(qt, k, vt, _score_bound(mla_g_qn[j], mla_g_kn[j], MLA_QK))
        else:
            qt, k, vt, c = _fox_proj(h, row(g_mix[i]), bf(fox_w_in[j].T), col(fox_b_f[j]),
                                     col(fox_g_qn[j]), col(fox_g_kn[j]))
            w_o = fox_w_o[j]
            ot = _attention(qt, k, vt, _score_bound(fox_g_qn[j], fox_g_kn[j], FOX_HEAD_DIM), bias=c)
        h = _post(h.reshape(n, d), ot, bf(w_o), row(g_ffn2[i]), bf(ffn2_w_in[i]), bf(ffn2_w_out[i]),
                  p.reshape(depth, n, PLE_DIM), i, row(g_ple[i]), bf(ple_w_gate[i]),
                  bf(ple_w_proj[i])).reshape(b, s, d)
    return h
```

```python
import functools
import math

import jax
import jax.numpy as jnp
from jax import lax
from jax.experimental import pallas as pl
from jax.experimental.pallas import tpu as pltpu

F32 = jnp.float32
BF16 = jnp.bfloat16

D_MODEL = 1024
D_FF = 2816
PLE_DIM = 256
N_MIXERS = 2
FFN_HALF = 0.5
HEADS = 16
MLA_Q_RANK = 512
MLA_KV_RANK = 256
MLA_NOPE = 64
MLA_ROPE = 32
MLA_QK = MLA_NOPE + MLA_ROPE
HEAD_V = 64
FOX_HEAD_DIM = 64
ROPE_THETA = 10000.0
EPS = 1e-6
LOG2E = math.log2(math.e)

LANES = 128
MXU_DIM = 256
BF16_SUBLANES = 16
QK_PAD = 128
V_ROWS = HEAD_V + BF16_SUBLANES
VMEM_LIMIT = 52 * 1024 * 1024

ROW_TILE = 512
PROJ_TILE = 512
HEAD_GROUP = 4
ATT_TQ = 1024
ATT_CW = MXU_DIM
ATT_TK = MXU_DIM
MAIN_UNROLL = 16
NSLOT = 4
LAG = 2
SCORES_AHEAD = 2
PV_KEYS = MXU_DIM
MASK_VALUE = -jnp.inf
M_INIT = -1e30
ZERO_WEIGHT_LOG2 = 170.0
BF16_NORM_SLACK = 1.02
FIXED_REF_MAX_BOUND = 56.0


def _rms(x, g):
    return x * lax.rsqrt(jnp.mean(x * x, axis=-1, keepdims=True) + EPS) * g


def _resident(shape):
    return pl.BlockSpec(shape, lambda *_: (0,) * len(shape), pipeline_mode=pl.Buffered(1))


def _swiglu_half_step(x, g, w_in_ref, w_out_ref):
    xn = _rms(x, g).astype(BF16)
    gu = jnp.dot(xn, w_in_ref[...], preferred_element_type=F32)
    gate = gu[:, :D_FF]
    up = gu[:, D_FF:]
    act = (gate * jax.nn.sigmoid(gate) * up).astype(BF16)
    return x + FFN_HALF * jnp.dot(act, w_out_ref[...], preferred_element_type=F32)


def _ffn_kernel(x_ref, g_ref, w_in_ref, w_out_ref, o_ref):
    o_ref[...] = _swiglu_half_step(x_ref[...], g_ref[...], w_in_ref, w_out_ref)


def _ffn(h2d, g, w_in, w_out):
    n = h2d.shape[0]
    return pl.pallas_call(
        _ffn_kernel,
        grid=(n // ROW_TILE,),
        in_specs=[
            pl.BlockSpec((ROW_TILE, D_MODEL), lambda i: (i, 0)),
            _resident((1, D_MODEL)),
            _resident((D_MODEL, 2 * D_FF)),
            _resident((D_FF, D_MODEL)),
        ],
        out_specs=pl.BlockSpec((ROW_TILE, D_MODEL), lambda i: (i, 0)),
        out_shape=jax.ShapeDtypeStruct((n, D_MODEL), F32),
        compiler_params=pltpu.CompilerParams(
            dimension_semantics=("parallel",), vmem_limit_bytes=VMEM_LIMIT),
        name="ffn",
    )(h2d, g, w_in, w_out)


def _post_kernel(x_ref, ot_ref, wo_ref, g2_ref, w_in_ref, w_out_ref,
                 p_ref, gp_ref, wg_ref, wp_ref, o_ref):
    x = x_ref[...] + lax.dot_general(ot_ref[...], wo_ref[...], (((0,), (0,)), ((), ())),
                                     preferred_element_type=F32)
    x = _swiglu_half_step(x, g2_ref[...], w_in_ref, w_out_ref)
    xn = _rms(x, gp_ref[...]).astype(BF16)
    gate = jax.nn.sigmoid(jnp.dot(xn, wg_ref[...], preferred_element_type=F32))
    pp = jnp.dot(p_ref[...].astype(BF16), wp_ref[...], preferred_element_type=F32)
    o_ref[...] = x + gate * pp


def _post(h2d, ot, w_o, g2, w_in, w_out, p3d, layer, g_ple, w_gate, w_proj):
    n = h2d.shape[0]
    hd, s = ot.shape[1], ot.shape[2]
    tiles = s // ROW_TILE
    rows = pl.BlockSpec((ROW_TILE, D_MODEL), lambda i: (i, 0))
    return pl.pallas_call(
        _post_kernel,
        grid=(n // ROW_TILE,),
        in_specs=[
            rows,
            pl.BlockSpec((None, hd, ROW_TILE), lambda i: (i // tiles, 0, i % tiles)),
            _resident((hd, D_MODEL)),
            _resident((1, D_MODEL)),
            _resident((D_MODEL, 2 * D_FF)),
            _resident((D_FF, D_MODEL)),
            pl.BlockSpec((None, ROW_TILE, PLE_DIM), lambda i: (layer, i, 0)),
            _resident((1, D_MODEL)),
            _resident((D_MODEL, D_MODEL)),
            _resident((PLE_DIM, D_MODEL)),
        ],
        out_specs=rows,
        out_shape=jax.ShapeDtypeStruct((n, D_MODEL), F32),
        compiler_params=pltpu.CompilerParams(
            dimension_semantics=("parallel",), vmem_limit_bytes=VMEM_LIMIT),
        name="post",
    )(h2d, ot, w_o, g2, w_in, w_out, p3d, g_ple, w_gate, w_proj)


def _dot_nt(w, x):
    return lax.dot_general(w, x, (((1,), (1,)), ((), ())), preferred_element_type=F32)


def _ones_row_group(width):
    row = lax.broadcasted_iota(jnp.int32, (BF16_SUBLANES, width), 0)
    return jnp.where(row == 0, 1.0, 0.0).astype(F32)


def _rope_rows(x1, x2, cos, sin):
    return x1 * cos - x2 * sin, x1 * sin + x2 * cos


def _mla_proj_kernel(x_ref, pos_ref, g_ref, w_in_ref, w_pe_ref, gq_lat_ref, gkv_lat_ref,
                     w_uq_ref, w_ukv_ref, gqn_ref, gkn_ref, inv_freq_ref,
                     qt_ref, k_ref, vt_ref):
    tm = x_ref.shape[0]
    xn = _rms(x_ref[...], g_ref[...]).astype(BF16)
    z = jnp.dot(xn, w_in_ref[...], preferred_element_type=F32)
    cq = _rms(z[:, :MLA_Q_RANK], gq_lat_ref[...]).astype(BF16)
    ckv = _rms(z[:, MLA_Q_RANK:], gkv_lat_ref[...]).astype(BF16)
    kpe_t = _dot_nt(w_pe_ref[...], xn)

    ang = pos_ref[...].astype(F32) * inv_freq_ref[...]
    cos = jnp.cos(ang)
    sin = jnp.sin(ang)
    half = MLA_ROPE // 2
    q_scale = (MLA_QK ** -0.5) * LOG2E
    gqn = gqn_ref[...] * q_scale
    gkn = gkn_ref[...]
    kpe_sq = jnp.sum(kpe_t * kpe_t, axis=0, keepdims=True)
    ones_rows = _ones_row_group(tm).astype(BF16)
    zeros_tail = jnp.zeros((QK_PAD - MLA_QK, tm), F32)

    kv_dim = MLA_NOPE + HEAD_V
    for h in range(HEADS):
        if h % HEAD_GROUP == 0:
            g0 = h
            qt_all = _dot_nt(w_uq_ref[g0 * MLA_QK:(g0 + HEAD_GROUP) * MLA_QK, :], cq)
            kvt_all = _dot_nt(w_ukv_ref[g0 * kv_dim:(g0 + HEAD_GROUP) * kv_dim, :], ckv)
        hl = h - g0
        qh = qt_all[hl * MLA_QK:(hl + 1) * MLA_QK]
        qn = qh * lax.rsqrt(jnp.mean(qh * qh, axis=0, keepdims=True) + EPS) * gqn
        q1, q2 = _rope_rows(qn[MLA_NOPE:MLA_NOPE + half], qn[MLA_NOPE + half:], cos, sin)
        qt_ref[h, 0:MLA_NOPE, :] = qn[:MLA_NOPE].astype(BF16)
        qt_ref[h, MLA_NOPE:MLA_NOPE + half, :] = q1.astype(BF16)
        qt_ref[h, MLA_NOPE + half:MLA_QK, :] = q2.astype(BF16)
        qt_ref[h, MLA_QK:QK_PAD, :] = zeros_tail.astype(BF16)

        base = hl * kv_dim
        kn = kvt_all[base:base + MLA_NOPE]
        ms = (jnp.sum(kn * kn, axis=0, keepdims=True) + kpe_sq) * (1.0 / MLA_QK)
        r = lax.rsqrt(ms + EPS)
        kn = kn * r * gkn[:MLA_NOPE]
        kp = kpe_t * r * gkn[MLA_NOPE:]
        k1, k2 = _rope_rows(kp[:half], kp[half:], cos, sin)
        kt = jnp.concatenate([kn, k1, k2, zeros_tail], axis=0)
        k_ref[h] = kt.T.astype(BF16)

        vt_ref[h, 0:HEAD_V, :] = kvt_all[base + MLA_NOPE:base + MLA_NOPE + HEAD_V].astype(BF16)
        vt_ref[h, HEAD_V:V_ROWS, :] = ones_rows


def _qkv_out(b, s):
    specs = [
        pl.BlockSpec((None, HEADS, QK_PAD, PROJ_TILE), lambda bi, i: (bi, 0, 0, i)),
        pl.BlockSpec((None, HEADS, PROJ_TILE, QK_PAD), lambda bi, i: (bi, 0, i, 0)),
        pl.BlockSpec((None, HEADS, V_ROWS, PROJ_TILE), lambda bi, i: (bi, 0, 0, i)),
    ]
    shapes = [
        jax.ShapeDtypeStruct((b, HEADS, QK_PAD, s), BF16),
        jax.ShapeDtypeStruct((b, HEADS, s, QK_PAD), BF16),
        jax.ShapeDtypeStruct((b, HEADS, V_ROWS, s), BF16),
    ]
    return specs, shapes


def _mla_proj(h, pos3, g, w_in, w_pe_t, gq_lat, gkv_lat, w_uq_t, w_ukv_t, gqn, gkn, inv_freq):
    b, s, _ = h.shape
    out_specs, out_shapes = _qkv_out(b, s)
    return pl.pallas_call(
        _mla_proj_kernel,
        grid=(b, s // PROJ_TILE),
        in_specs=[
            pl.BlockSpec((None, PROJ_TILE, D_MODEL), lambda bi, i: (bi, i, 0)),
            pl.BlockSpec((None, 1, PROJ_TILE), lambda bi, i: (bi, 0, i)),
            _resident(g.shape), _resident(w_in.shape), _resident(w_pe_t.shape),
            _resident(gq_lat.shape), _resident(gkv_lat.shape),
            _resident(w_uq_t.shape), _resident(w_ukv_t.shape),
            _resident(gqn.shape), _resident(gkn.shape), _resident(inv_freq.shape),
        ],
        out_specs=out_specs,
        out_shape=out_shapes,
        compiler_params=pltpu.CompilerParams(
            dimension_semantics=("parallel", "parallel"), vmem_limit_bytes=VMEM_LIMIT),
        name="mla_proj",
    )(h, pos3, g, w_in, w_pe_t, gq_lat, gkv_lat, w_uq_t, w_ukv_t, gqn, gkn, inv_freq)


def _split3(c):
    hi = c.astype(BF16).astype(F32)
    mid = (c - hi).astype(BF16).astype(F32)
    lo = (c - hi - mid).astype(BF16).astype(F32)
    return hi, mid, lo


def _fox_proj_kernel(x_ref, g_ref, w_ref, bf_ref, gqn_ref, gkn_ref,
                     qt_ref, k_ref, vt_ref, c_ref, carry_ref):
    tm = x_ref.shape[0]
    hd = HEADS * FOX_HEAD_DIM

    @pl.when(pl.program_id(1) == 0)
    def _():
        carry_ref[...] = jnp.zeros_like(carry_ref)

    xn = _rms(x_ref[...], g_ref[...]).astype(BF16)
    f_t = _dot_nt(w_ref[3 * hd:3 * hd + HEADS, :], xn) + bf_ref[...]

    logf = (jnp.minimum(f_t, 0.0) - jnp.log1p(jnp.exp(-jnp.abs(f_t)))) * LOG2E
    hi, mid, lo = _split3(logf)
    parts = jnp.concatenate([hi, mid, lo], axis=0).astype(BF16)
    src = lax.broadcasted_iota(jnp.int32, (tm, tm), 0)
    dst = lax.broadcasted_iota(jnp.int32, (tm, tm), 1)
    upper = jnp.where(src <= dst, 1.0, 0.0).astype(BF16)
    sums = jnp.dot(parts, upper, preferred_element_type=F32)
    local = sums[2 * HEADS:] + sums[HEADS:2 * HEADS] + sums[:HEADS]
    c = local + carry_ref[:, 0:1]
    carry_ref[...] = jnp.broadcast_to(c[:, tm - 1:tm], carry_ref.shape)
    c_ref[...] = c

    q_scale = (FOX_HEAD_DIM ** -0.5) * LOG2E
    gqn = gqn_ref[...] * q_scale
    gkn = gkn_ref[...]
    row = lax.broadcasted_iota(jnp.int32, (BF16_SUBLANES, tm), 0)
    ones_rows = _ones_row_group(tm).astype(BF16)
    zeros_tail = jnp.zeros((QK_PAD - FOX_HEAD_DIM - BF16_SUBLANES, tm), F32)

    def bias_rows(first, second):
        out = jnp.zeros((BF16_SUBLANES, tm), F32)
        for j in range(3):
            out = jnp.where(row == j, first[j], out)
            out = jnp.where(row == 8 + j, second[j], out)
        return out

    one3 = (1.0, 1.0, 1.0)
    gw = HEAD_GROUP * FOX_HEAD_DIM
    for h in range(HEADS):
        if h % HEAD_GROUP == 0:
            g0 = h
            r0 = g0 * FOX_HEAD_DIM
            qt_all = _dot_nt(w_ref[r0:r0 + gw, :], xn)
            kt_all = _dot_nt(w_ref[hd + r0:hd + r0 + gw, :], xn)
            vt_all = _dot_nt(w_ref[2 * hd + r0:2 * hd + r0 + gw, :], xn)
        sl = slice((h - g0) * FOX_HEAD_DIM, (h - g0 + 1) * FOX_HEAD_DIM)
        c3 = _split3(c[h:h + 1])
        qh = qt_all[sl]
        qn = qh * lax.rsqrt(jnp.mean(qh * qh, axis=0, keepdims=True) + EPS) * gqn
        qt_ref[h, 0:FOX_HEAD_DIM, :] = qn.astype(BF16)
        qt_ref[h, FOX_HEAD_DIM:FOX_HEAD_DIM + BF16_SUBLANES, :] = bias_rows(one3, c3).astype(BF16)
        qt_ref[h, FOX_HEAD_DIM + BF16_SUBLANES:QK_PAD, :] = zeros_tail.astype(BF16)

        kh = kt_all[sl]
        kn = kh * lax.rsqrt(jnp.mean(kh * kh, axis=0, keepdims=True) + EPS) * gkn
        neg3 = tuple(-p for p in c3)
        kt = jnp.concatenate([kn, bias_rows(neg3, one3), zeros_tail], axis=0)
        k_ref[h] = kt.T.astype(BF16)

        vt_ref[h, 0:HEAD_V, :] = vt_all[sl].astype(BF16)
        vt_ref[h, HEAD_V:V_ROWS, :] = ones_rows


def _fox_proj(h, g, w_t, b_f, gqn, gkn):
    b, s, _ = h.shape
    out_specs, out_shapes = _qkv_out(b, s)
    out_specs = out_specs + [pl.BlockSpec((None, HEADS, PROJ_TILE), lambda bi, i: (bi, 0, i))]
    out_shapes = out_shapes + [jax.ShapeDtypeStruct((b, HEADS, s), F32)]
    return pl.pallas_call(
        _fox_proj_kernel,
        grid=(b, s // PROJ_TILE),
        in_specs=[
            pl.BlockSpec((None, PROJ_TILE, D_MODEL), lambda bi, i: (bi, i, 0)),
            _resident(g.shape), _resident(w_t.shape), _resident(b_f.shape),
            _resident(gqn.shape), _resident(gkn.shape),
        ],
        out_specs=out_specs,
        out_shape=out_shapes,
        scratch_shapes=[pltpu.VMEM((HEADS, LANES), F32)],
        compiler_params=pltpu.CompilerParams(
            dimension_semantics=("arbitrary", "arbitrary"), vmem_limit_bytes=VMEM_LIMIT),
        name="fox_proj",
    )(h, g, w_t, b_f, gqn, gkn)


def _attn_kernel(cq_ref, cg_ref, thr_ref, qt_ref, k_ref, vt_ref, o_ref, acc_ref, m_ref, *bufs,
                 seq, tq, tk, cw):
    ncol = tq // cw
    nq = seq // tq
    all_chains = tuple(range(ncol))
    s_refs = bufs[:NSLOT]
    bm_refs = bufs[NSLOT:]
    bias_base = (pl.program_id(0) * pl.num_programs(1) + pl.program_id(1)) * nq

    def first_group(t):
        def dead(g):
            return jnp.logical_and(
                g < t - 1, cq_ref[bias_base + t] - cg_ref[bias_base + g] < -thr_ref[0])
        return lax.while_loop(dead, lambda g: g + 1, jnp.int32(0))

    pv_keys = min(PV_KEYS, tk)
    n_parts = tk // pv_keys
    full_plan = {c: tuple((r, False) for r in range(n_parts)) for c in all_chains}

    nd = tq // tk
    assert nd == NSLOT and MAIN_UNROLL % NSLOT == 0 and LAG < NSLOT

    def diag_plan(d):
        plan = {}
        for c in all_chains:
            parts = []
            for r in range(n_parts):
                key_lo = d * tk + r * pv_keys
                if key_lo <= (c + 1) * cw - 1:
                    parts.append((r, key_lo + pv_keys - 1 > c * cw))
            if parts:
                plan[c] = tuple(parts)
        return plan

    diag_plans = tuple(diag_plan(d) for d in range(nd))

    def stage_a_parts(q0, k0, slot, c, parts):
        cols = slice(c * cw, (c + 1) * cw)
        qt = qt_ref[:, pl.ds(q0 + c * cw, cw)]
        bm = None
        for r, masked in parts:
            rows = slice(r * pv_keys, (r + 1) * pv_keys)
            kb = k_ref[pl.ds(k0 + r * pv_keys, pv_keys), :]
            s = jnp.dot(kb, qt, preferred_element_type=F32)
            if masked:
                kpos = k0 + r * pv_keys + lax.broadcasted_iota(jnp.int32, (pv_keys, cw), 0)
                qpos = q0 + c * cw + lax.broadcasted_iota(jnp.int32, (pv_keys, cw), 1)
                s = jnp.where(kpos <= qpos, s, MASK_VALUE)
            s_refs[slot][c, rows, :] = s
            pm = jnp.max(s, axis=0, keepdims=True)
            bm = pm if bm is None else jnp.maximum(bm, pm)
            if r == parts[-1][0]:
                bm_refs[slot][:, cols] = bm
            yield

    def stage_b_parts(k0, slot, c, parts):
        cols = slice(c * cw, (c + 1) * cw)
        m_old = m_ref[:, cols]
        m_new = jnp.maximum(m_old, bm_refs[slot][:, cols])
        alpha = jnp.exp2(m_old - m_new)
        pv = None
        for r, _ in parts:
            rows = slice(r * pv_keys, (r + 1) * pv_keys)
            p = jnp.exp2(s_refs[slot][c, rows, :] - m_new).astype(BF16)
            vb = vt_ref[:, pl.ds(k0 + r * pv_keys, pv_keys)]
            t = jnp.dot(vb, p, preferred_element_type=F32)
            pv = t if pv is None else pv + t
            if r == parts[-1][0]:
                acc_ref[c] = alpha * acc_ref[c] + pv
                m_ref[:, cols] = m_new
            yield

    def step(q0, k0, rslot, nxt_q0, nxt_k0, a_plan=full_plan, b_plan=full_plan):
        for c in all_chains:
            gens = []
            if a_plan is not None and c in a_plan:
                gens.append(stage_a_parts(nxt_q0, nxt_k0, (rslot + LAG) % NSLOT, c, a_plan[c]))
            if b_plan is not None and c in b_plan:
                gens.append(stage_b_parts(k0, rslot, c, b_plan[c]))
            while gens:
                gens = [g for g in gens if next(g, gens) is not gens]

    def q_tile(qi, carry):
        q0 = pl.multiple_of(qi * tq, tq)
        acc_ref[...] = jnp.zeros_like(acc_ref)
        m_ref[...] = jnp.full_like(m_ref, M_INIT)

        def run_steps(k0, count):
            for j in range(count):
                step(q0, k0 + j * tk, j % NSLOT, q0, k0 + (j + LAG) * tk)

        g_first = first_group(qi)
        k_first = pl.multiple_of(g_first * tq, tq)
        n_main = NSLOT * (jnp.maximum(qi - 1, 0) - g_first)
        rem = n_main & (MAIN_UNROLL - 1)

        def main(t, carry2):
            run_steps(pl.multiple_of(k_first + t * MAIN_UNROLL * tk, tq), MAIN_UNROLL)
            return carry2

        lax.fori_loop(0, lax.shift_right_logical(n_main, MAIN_UNROLL.bit_length() - 1), main, 0)
        k_last = q0 - NSLOT * tk
        sub = MAIN_UNROLL // 2
        while sub >= NSLOT:
            k_sub = pl.multiple_of(k_last - ((rem & (sub - 1)) + sub) * tk, NSLOT * tk)

            @pl.when((rem & sub) != 0)
            def _(k_sub=k_sub, sub=sub):
                run_steps(k_sub, sub)

            sub //= 2

        nxt_qi = jnp.minimum(qi + 1, nq - 1)
        nxt_q0 = pl.multiple_of(nxt_qi * tq, tq)
        nxt_k_first = pl.multiple_of(first_group(nxt_qi) * tq, tq)

        def last_below():
            k0 = pl.multiple_of(k_last, NSLOT * tk)
            for j in range(NSLOT):
                tgt = j + LAG
                if tgt < NSLOT:
                    step(q0, k0 + j * tk, j, q0, k0 + tgt * tk)
                else:
                    step(q0, k0 + j * tk, j, q0, q0 + (tgt - NSLOT) * tk,
                         a_plan=diag_plans[tgt - NSLOT])

        def diagonal():
            for d in range(nd):
                tgt = d + LAG
                if tgt < nd:
                    step(q0, q0 + d * tk, d % NSLOT, q0, q0 + tgt * tk,
                         a_plan=diag_plans[tgt], b_plan=diag_plans[d])
                else:
                    step(q0, q0 + d * tk, d % NSLOT, nxt_q0, nxt_k_first + (tgt - nd) * tk,
                         b_plan=diag_plans[d])

        @pl.when(qi >= 1)
        def _():
            last_below()
            diagonal()

        @pl.when(qi == 0)
        def _():
            diagonal()

        for c in all_chains:
            acc = acc_ref[c]
            o = acc[:HEAD_V] / acc[HEAD_V:HEAD_V + 1]
            o_ref[:, pl.ds(q0 + c * cw, cw)] = o.astype(o_ref.dtype)
        return carry

    for j in range(LAG):
        step(0, 0, (j - LAG) % NSLOT, 0, j * tk, a_plan=diag_plans[j], b_plan=None)
    lax.fori_loop(0, nq, q_tile, 0)


def _attn_fixed_kernel(cq_ref, cg_ref, thr_ref, qt_ref, k_ref, vt_ref, o_ref, acc_ref,
                       *, seq, tq, tk, cw):
    ncol = tq // cw
    nq = seq // tq
    nd = tq // tk
    all_chains = tuple(range(ncol))
    bias_base = (pl.program_id(0) * pl.num_programs(1) + pl.program_id(1)) * nq
    ref_max = thr_ref[1]

    def first_group(t):
        def dead(g):
            return jnp.logical_and(
                g < t - 1, cq_ref[bias_base + t] - cg_ref[bias_base + g] < -thr_ref[0])
        return lax.while_loop(dead, lambda g: g + 1, jnp.int32(0))

    def scores(q0, k0, c, masked):
        qt = qt_ref[:, pl.ds(q0 + c * cw, cw)]
        s = jnp.dot(k_ref[pl.ds(k0, tk), :], qt, preferred_element_type=F32)
        if masked:
            kpos = k0 + lax.broadcasted_iota(jnp.int32, (tk, cw), 0)
            qpos = q0 + c * cw + lax.broadcasted_iota(jnp.int32, (tk, cw), 1)
            s = jnp.where(kpos <= qpos, s, MASK_VALUE)
        return s

    def weighted_values(k0, s):
        p = jnp.exp2(s - ref_max)
        pv = jnp.dot(vt_ref[0:HEAD_V, pl.ds(k0, tk)], p.astype(BF16), preferred_element_type=F32)
        return pv, jnp.sum(p, axis=0, keepdims=True)

    plain = {c: False for c in all_chains}

    def run_list(q0, blocks):
        tot = [None] * ncol
        pending = [{c: scores(q0, k0, c, m) for c, m in vis.items()}
                   for k0, vis in blocks[:SCORES_AHEAD]]
        for j, (k0, vis) in enumerate(blocks):
            nxt_k0, nxt_vis = (blocks[j + SCORES_AHEAD] if j + SCORES_AHEAD < len(blocks)
                               else (None, {}))
            s_cur = pending.pop(0)
            s_nxt = {}
            for c in all_chains:
                if c in nxt_vis:
                    s_nxt[c] = scores(q0, nxt_k0, c, nxt_vis[c])
                if c in vis:
                    t, l = weighted_values(k0, s_cur[c])
                    tot[c] = (t, l) if tot[c] is None else (tot[c][0] + t, tot[c][1] + l)
            pending.append(s_nxt)
        return tot

    def run_plain(q0, k0, count):
        tot = run_list(q0, [(k0 + j * tk, plain) for j in range(count)])
        for c in all_chains:
            acc_ref[c, 0:HEAD_V] = acc_ref[c, 0:HEAD_V] + tot[c][0]
            acc_ref[c, HEAD_V:HEAD_V + 1] = acc_ref[c, HEAD_V:HEAD_V + 1] + tot[c][1]

    def q_tile(qi, carry):
        q0 = pl.multiple_of(qi * tq, tq)
        acc_ref[...] = jnp.zeros_like(acc_ref)
        g_first = first_group(qi)
        k_first = pl.multiple_of(g_first * tq, tq)
        n_main = nd * (jnp.maximum(qi - 1, 0) - g_first)
        rem = n_main & (MAIN_UNROLL - 1)

        def main(t, carry2):
            run_plain(q0, pl.multiple_of(k_first + t * MAIN_UNROLL * tk, tq), MAIN_UNROLL)
            return carry2

        lax.fori_loop(0, lax.shift_right_logical(n_main, MAIN_UNROLL.bit_length() - 1), main, 0)
        diagonal = [(q0 + d * tk, {c: c == d for c in range(d, ncol)}) for d in range(nd)]

        def finish(groups_below):
            k0 = pl.multiple_of(q0 - groups_below * nd * tk, tq)
            below = [(k0 + j * tk, plain) for j in range(groups_below * nd)]
            tot = run_list(q0, below + diagonal)
            for c in all_chains:
                o = ((acc_ref[c, 0:HEAD_V] + tot[c][0])
                     / (acc_ref[c, HEAD_V:HEAD_V + 1] + tot[c][1]))
                o_ref[:, pl.ds(q0 + c * cw, cw)] = o.astype(o_ref.dtype)

        groups_left = lax.shift_right_logical(rem, nd.bit_length() - 1) + 1
        for groups_below in range(1, MAIN_UNROLL // nd + 1):
            pl.when(jnp.logical_and(qi >= 1, groups_left == groups_below))(
                functools.partial(finish, groups_below))
        pl.when(qi == 0)(functools.partial(finish, 0))

        return carry

    lax.fori_loop(0, nq, q_tile, 0)


def _attention(qt, k, vt, bound, bias=None):
    b, heads, _, s = qt.shape
    tq = min(ATT_TQ, s)
    tk = min(ATT_TK, tq // NSLOT)
    assert tk == ATT_CW
    nq = s // tq
    if bias is None:
        cq = cg = jnp.zeros((b * heads * nq,), F32)
    else:
        cq = bias[:, :, ::tq].reshape(-1)
        cg = bias[:, :, tq - 1::tq].reshape(-1)
    bound = jnp.asarray(bound, F32)
    thr = jnp.stack([2.0 * bound + ZERO_WEIGHT_LOG2, bound])
    in_specs = [
        pl.BlockSpec((None, None, QK_PAD, s), lambda bi, hi, *_: (bi, hi, 0, 0)),
        pl.BlockSpec((None, None, s, QK_PAD), lambda bi, hi, *_: (bi, hi, 0, 0)),
        pl.BlockSpec((None, None, V_ROWS, s), lambda bi, hi, *_: (bi, hi, 0, 0)),
    ]
    out_specs = pl.BlockSpec((None, HEAD_V, s), lambda bi, hi, *_: (bi, hi, 0))
    acc_scratch = pltpu.VMEM((tq // ATT_CW, V_ROWS, ATT_CW), F32)
    common = dict(
        out_shape=jax.ShapeDtypeStruct((b, heads * HEAD_V, s), BF16),
        compiler_params=pltpu.CompilerParams(
            dimension_semantics=("parallel", "parallel"), vmem_limit_bytes=VMEM_LIMIT),
    )

    def online(*args):
        return pl.pallas_call(
            functools.partial(_attn_kernel, seq=s, tq=tq, tk=tk, cw=ATT_CW),
            grid_spec=pltpu.PrefetchScalarGridSpec(
                num_scalar_prefetch=3, grid=(b, heads), in_specs=in_specs, out_specs=out_specs,
                scratch_shapes=([acc_scratch, pltpu.VMEM((1, tq), F32)]
                                + [pltpu.VMEM((tq // ATT_CW, tk, ATT_CW), F32)] * NSLOT
                                + [pltpu.VMEM((1, tq), F32)] * NSLOT),
            ),
            name="attention", **common)(*args)

    tq_fixed = 2 * tq if bias is None and s % (2 * tq) == 0 else tq

    def fixed(*args):
        return pl.pallas_call(
            functools.partial(_attn_fixed_kernel, seq=s, tq=tq_fixed, tk=tk, cw=ATT_CW),
            grid_spec=pltpu.PrefetchScalarGridSpec(
                num_scalar_prefetch=3, grid=(b, heads), in_specs=in_specs, out_specs=out_specs,
                scratch_shapes=[pltpu.VMEM((tq_fixed // ATT_CW, V_ROWS, ATT_CW), F32)],
            ),
            name="attention_fixed", **common)(*args)

    return lax.cond(bound <= FIXED_REF_MAX_BOUND, fixed, online, cq, cg, thr, qt, k, vt)


def _score_bound(g_q, g_k, dk):
    return (jnp.max(jnp.abs(g_q)) * jnp.max(jnp.abs(g_k)) * (dk ** 0.5) * LOG2E
            * BF16_NORM_SLACK).astype(F32)


def kernel(x, p, positions, g_ffn1, g_mix, g_ffn2, g_ple, ffn1_w_in, ffn1_w_out, ffn2_w_in, ffn2_w_out, ple_w_proj, ple_w_gate, mla_w_in, mla_g_q_lat, mla_w_uq, mla_g_kv_lat, mla_w_ukv, mla_g_qn, mla_g_kn, mla_w_o, fox_w_in, fox_b_f, fox_g_qn, fox_g_kn, fox_w_o):
    b, s, d = x.shape
    depth = g_ffn1.shape[0]
    assert d == D_MODEL and s % PROJ_TILE == 0 and s % ROW_TILE == 0
    n = b * s
    bf = lambda w: w.astype(BF16)
    row = lambda g: g.reshape(1, -1)
    col = lambda g: g.reshape(-1, 1)

    pos3 = positions.reshape(b, 1, s)
    half = MLA_ROPE // 2
    inv_freq = (ROPE_THETA ** (-jnp.arange(0, MLA_ROPE, 2, dtype=F32) / MLA_ROPE)).reshape(half, 1)
    lat = MLA_Q_RANK + MLA_KV_RANK

    h = x
    for i in range(depth):
        j = i // N_MIXERS
        h = _ffn(h.reshape(n, d), row(g_ffn1[i]), bf(ffn1_w_in[i]), bf(ffn1_w_out[i])).reshape(b, s, d)
        if i % N_MIXERS == 0:
            qt, k, vt = _mla_proj(
                h, pos3, row(g_mix[i]), bf(mla_w_in[j][:, :lat]), bf(mla_w_in[j][:, lat:].T),
                row(mla_g_q_lat[j]), row(mla_g_kv_lat[j]), bf(mla_w_uq[j].T), bf(mla_w_ukv[j].T),
                col(mla_g_qn[j]), col(mla_g_kn[j]), inv_freq)
            w_o = mla_w_o[j]
            ot = _attention(qt, k, vt, _score_bound(mla_g_qn[j], mla_g_kn[j], MLA_QK))
        else:
            qt, k, vt, c = _fox_proj(h, row(g_mix[i]), bf(fox_w_in[j].T), col(fox_b_f[j]),
                                     col(fox_g_qn[j]), col(fox_g_kn[j]))
            w_o = fox_w_o[j]
            ot = _attention(qt, k, vt, _score_bound(fox_g_qn[j], fox_g_kn[j], FOX_HEAD_DIM), bias=c)
        h = _post(h.reshape(n, d), ot, bf(w_o), row(g_ffn2[i]), bf(ffn2_w_in[i]), bf(ffn2_w_out[i]),
                  p.reshape(depth, n, PLE_DIM), i, row(g_ple[i]), bf(ple_w_gate[i]),
                  bf(ple_w_proj[i])).reshape(b, s, d)
    return h
```
